```python
import jax, jax.numpy as jnp
from jax import lax
import numpy as np

D_MODEL = 1024
BATCH = 4
SEQ = 8192
DEPTH = 1

GLA_HEADS = 4
GLA_DK = 64
GLA_DV = 128
GLA_GATE_RANK = 16
GLA_GATE_TEMP = 16.0
GLA_CHUNK = 64
SWA_HEADS = 8
SWA_KV_HEADS = 2
SWA_HEAD_DIM = 64
SWA_WINDOW = 128
SWA_BLOCK = 128
PEER_HEADS = 8
PEER_N_KEYS = 128
PEER_N_EXPERTS = PEER_N_KEYS * PEER_N_KEYS
PEER_QUERY_DIM = 256
PEER_TOPK = 16
PEER_TOKEN_BLOCK = 128
RMS_EPS = 1e-6

GLA_QK_WIDTH = GLA_HEADS * GLA_DK
GLA_WIDTH = GLA_HEADS * GLA_DV
SWA_WIDTH = SWA_HEADS * SWA_HEAD_DIM
SWA_KV_WIDTH = SWA_KV_HEADS * SWA_HEAD_DIM
MIX_WIDTH = GLA_WIDTH + SWA_WIDTH
IN_WIDTH = 2 * GLA_QK_WIDTH + GLA_WIDTH + GLA_GATE_RANK + GLA_WIDTH + SWA_WIDTH + 2 * SWA_KV_WIDTH

kernel_name = "hymba_gla_swa_peer_adaln"


def rms_norm(t, g):
    tf = t.astype(jnp.float32)
    tf = tf * lax.rsqrt(jnp.mean(tf * tf, axis=-1, keepdims=True) + RMS_EPS)
    return (tf * g.astype(jnp.float32)).astype(t.dtype)


def modulate(t, g, shift, scale):
    return rms_norm(t, g) * (1 + scale[:, None, :]) + shift[:, None, :]


def gla_mixer(q, k, v, log_a):
    bsz, s = q.shape[0], q.shape[1]
    nc = s // GLA_CHUNK

    def to_chunks(t):
        return t.astype(jnp.float32).reshape(bsz, nc, GLA_CHUNK, GLA_HEADS, -1).transpose(1, 0, 3, 2, 4)

    causal = jnp.tril(jnp.ones((GLA_CHUNK, GLA_CHUNK), dtype=bool))

    def step(state, inp):
        qc, kc, vc, ac = inp
        b = jnp.cumsum(ac, axis=2)
        o_inter = jnp.einsum('bhtk,bhkv->bhtv', qc * jnp.exp(b), state)
        diff = b[:, :, :, None, :] - b[:, :, None, :, :]
        decay = jnp.exp(jnp.where(causal[None, None, :, :, None], diff, -jnp.inf))
        scores = jnp.einsum('bhtk,bhsk,bhtsk->bhts', qc, kc, decay)
        o_intra = jnp.einsum('bhts,bhsv->bhtv', scores, vc)
        b_last = b[:, :, -1:, :]
        state = jnp.exp(b_last[:, :, 0, :, None]) * state + jnp.einsum(
            'bhsk,bhsv->bhkv', kc * jnp.exp(b_last - b), vc)
        return state, o_inter + o_intra

    init = jnp.zeros((bsz, GLA_HEADS, GLA_DK, GLA_DV), jnp.float32)
    xs = (to_chunks(q * GLA_DK ** -0.5), to_chunks(k), to_chunks(v), to_chunks(log_a))
    _, o = lax.scan(step, init, xs)
    return o.transpose(1, 0, 3, 2, 4).reshape(bsz, s, GLA_HEADS, GLA_DV).astype(v.dtype)


def swa_mixer(q, k, v, sinks):
    bsz, s = q.shape[0], q.shape[1]
    nb = s // SWA_BLOCK
    grp = SWA_HEADS // SWA_KV_HEADS
    qb = q.reshape(bsz, nb, SWA_BLOCK, SWA_KV_HEADS, grp, SWA_HEAD_DIM) * SWA_HEAD_DIM ** -0.5

    def with_prev(t):
        tb = t.reshape(bsz, nb, SWA_BLOCK, SWA_KV_HEADS, SWA_HEAD_DIM)
        prev = jnp.pad(tb, ((0, 0), (1, 0), (0, 0), (0, 0), (0, 0)))[:, :-1]
        return jnp.concatenate([prev, tb], axis=2)

    kb, vb = with_prev(k), with_prev(v)
    sc = jnp.einsum('bnqhgd,bnkhd->bnhgqk', qb, kb).astype(jnp.float32)
    qi = jnp.arange(SWA_BLOCK)[:, None]
    kj = jnp.arange(2 * SWA_BLOCK)[None, :]
    dist = qi + SWA_BLOCK - kj
    key_pos = jnp.arange(nb)[:, None] * SWA_BLOCK - SWA_BLOCK + kj[0][None, :]
    valid = ((dist >= 0) & (dist < SWA_WINDOW))[None] & (key_pos >= 0)[:, None, :]
    slopes = (2.0 ** (-8.0 * (jnp.arange(SWA_HEADS) + 1) / SWA_HEADS)).astype(jnp.float32)
    slopes = slopes.reshape(SWA_KV_HEADS, grp)[None, None, :, :, None, None]
    sc = sc - slopes * dist.astype(jnp.float32)
    sc = jnp.where(valid[None, :, None, None], sc, -jnp.inf)
    sink = sinks.astype(jnp.float32).reshape(SWA_KV_HEADS, grp)[None, None, :, :, None, None]
    m = jnp.maximum(jnp.max(sc, axis=-1, keepdims=True), sink)
    p = jnp.exp(sc - m)
    p = p / (jnp.sum(p, axis=-1, keepdims=True) + jnp.exp(sink - m))
    o = jnp.einsum('bnhgqk,bnkhd->bnqhgd', p.astype(vb.dtype), vb)
    return o.reshape(bsz, s, SWA_WIDTH)


def peer_mixer(h, w_q, keys_1, keys_2, u_tab, v_tab):
    bsz, s, d = h.shape
    t = bsz * s
    hf = h.reshape(t, d)
    q = (hf @ w_q).reshape(t, PEER_HEADS, PEER_QUERY_DIM)
    half = PEER_QUERY_DIM // 2
    s1 = jnp.einsum('thd,nd->thn', q[..., :half], keys_1).astype(jnp.float32)
    s2 = jnp.einsum('thd,nd->thn', q[..., half:], keys_2).astype(jnp.float32)
    v1, i1 = lax.top_k(s1, PEER_TOPK)
    v2, i2 = lax.top_k(s2, PEER_TOPK)
    cand = (v1[..., :, None] + v2[..., None, :]).reshape(t, PEER_HEADS, PEER_TOPK * PEER_TOPK)
    sc, ci = lax.top_k(cand, PEER_TOPK)
    e1 = jnp.take_along_axis(i1, ci // PEER_TOPK, axis=-1)
    e2 = jnp.take_along_axis(i2, ci % PEER_TOPK, axis=-1)
    experts = (e1 * PEER_N_KEYS + e2).reshape(t, PEER_HEADS * PEER_TOPK)
    gates = jax.nn.softmax(sc, axis=-1).reshape(t, PEER_HEADS * PEER_TOPK)
    nblk = t // PEER_TOKEN_BLOCK

    def expert_block(args):
        hb, eb, gb = args
        a = jax.nn.gelu(jnp.einsum('td,ted->te', hb, u_tab[eb]).astype(jnp.float32), approximate=False)
        w = (gb * a).astype(hb.dtype)
        return jnp.einsum('te,ted->td', w, v_tab[eb])

    out = lax.map(expert_block, (hf.reshape(nblk, PEER_TOKEN_BLOCK, d),
                                 experts.reshape(nblk, PEER_TOKEN_BLOCK, -1),
                                 gates.reshape(nblk, PEER_TOKEN_BLOCK, -1)))
    return out.reshape(bsz, s, d)


def setup_inputs(seed: int = 0) -> dict:
    key = jax.random.key(seed)
    ks = jax.random.split(key, 24)
    f32 = jnp.float32
    n = lambda k, shape, sc: (jax.random.normal(k, shape, f32) * sc)
    gain = lambda k, shape: 1.0 + 0.02 * jax.random.normal(k, shape, f32)
    half = PEER_QUERY_DIM // 2
    return {
        "x": n(ks[0], (BATCH, SEQ, D_MODEL), 1.0),
        "c": n(ks[1], (BATCH, D_MODEL), 1.0),
        "w_ada": n(ks[2], (DEPTH, D_MODEL, 6 * D_MODEL), 0.5 * D_MODEL ** -0.5),
        "b_ada": n(ks[3], (DEPTH, 6 * D_MODEL), 0.02),
        "norm1_g": gain(ks[4], (DEPTH, D_MODEL)),
        "norm2_g": gain(ks[5], (DEPTH, D_MODEL)),
        "w_in": n(ks[6], (DEPTH, D_MODEL, IN_WIDTH), D_MODEL ** -0.5),
        "w_gla_alpha": n(ks[7], (DEPTH, GLA_GATE_RANK, GLA_QK_WIDTH), GLA_GATE_RANK ** -0.5),
        "b_gla_alpha": n(ks[8], (DEPTH, GLA_QK_WIDTH), 0.1),
        "gla_norm_g": gain(ks[9], (DEPTH, GLA_WIDTH)),
        "swa_sinks": n(ks[10], (DEPTH, SWA_HEADS), 0.5),
        "swa_norm_g": gain(ks[11], (DEPTH, SWA_WIDTH)),
        "w_out": n(ks[12], (DEPTH, MIX_WIDTH, D_MODEL), MIX_WIDTH ** -0.5),
        "w_peer_q": n(ks[13], (DEPTH, D_MODEL, PEER_HEADS * PEER_QUERY_DIM), D_MODEL ** -0.5),
        "peer_keys_1": n(ks[14], (DEPTH, PEER_N_KEYS, half), half ** -0.5),
        "peer_keys_2": n(ks[15], (DEPTH, PEER_N_KEYS, half), half ** -0.5),
        "peer_u": n(ks[16], (DEPTH, PEER_N_EXPERTS, D_MODEL), D_MODEL ** -0.5),
        "peer_v": n(ks[17], (DEPTH, PEER_N_EXPERTS, D_MODEL), 0.2),
        "final_g": gain(ks[18], (D_MODEL,)),
    }


def reference(x, c, w_ada, b_ada, norm1_g, norm2_g, w_in, w_gla_alpha, b_gla_alpha, gla_norm_g,
              swa_sinks, swa_norm_g, w_out, w_peer_q, peer_keys_1, peer_keys_2, peer_u, peer_v, final_g):
    bsz, s, _ = x.shape
    widths = (GLA_QK_WIDTH, GLA_QK_WIDTH, GLA_WIDTH, GLA_GATE_RANK, GLA_WIDTH, SWA_WIDTH, SWA_KV_WIDTH, SWA_KV_WIDTH)
    split_at = [int(v) for v in np.cumsum(widths)[:-1]]
    c_act = jax.nn.silu(c)
    for l in range(DEPTH):
        mod = c_act @ w_ada[l] + b_ada[l]
        shift1, scale1, gate1, shift2, scale2, gate2 = jnp.split(mod, 6, axis=-1)

        h = modulate(x, norm1_g[l], shift1, scale1)
        proj = h @ w_in[l]
        q_g, k_g, v_g, a_lr, r_g, q_s, k_s, v_s = jnp.split(proj, split_at, axis=-1)
        log_a = jax.nn.log_sigmoid((a_lr @ w_gla_alpha[l] + b_gla_alpha[l]).astype(jnp.float32)) / GLA_GATE_TEMP
        o_gla = gla_mixer(q_g.reshape(bsz, s, GLA_HEADS, GLA_DK),
                          k_g.reshape(bsz, s, GLA_HEADS, GLA_DK),
                          v_g.reshape(bsz, s, GLA_HEADS, GLA_DV),
                          log_a.reshape(bsz, s, GLA_HEADS, GLA_DK))
        o_gla = rms_norm(o_gla, gla_norm_g[l].reshape(GLA_HEADS, GLA_DV)).reshape(bsz, s, GLA_WIDTH)
        o_gla = o_gla * jax.nn.silu(r_g)
        o_swa = swa_mixer(q_s.reshape(bsz, s, SWA_HEADS, SWA_HEAD_DIM),
                          k_s.reshape(bsz, s, SWA_KV_HEADS, SWA_HEAD_DIM),
                          v_s.reshape(bsz, s, SWA_KV_HEADS, SWA_HEAD_DIM),
                          swa_sinks[l])
        o_swa = rms_norm(o_swa, swa_norm_g[l])
        y = jnp.concatenate([o_gla, o_swa], axis=-1) @ w_out[l]
        x = x + gate1[:, None, :] * y

        h2 = modulate(x, norm2_g[l], shift2, scale2)
        x = x + gate2[:, None, :] * peer_mixer(h2, w_peer_q[l], peer_keys_1[l], peer_keys_2[l], peer_u[l], peer_v[l])
    return rms_norm(x, final_g)
```

```python
import jax, jax.numpy as jnp
from jax import lax
import numpy as np
from jax.experimental import pallas as pl
from jax.experimental.pallas import tpu as pltpu

D_MODEL = 1024
BATCH = 4
SEQ = 8192
DEPTH = 1

GLA_HEADS = 4
GLA_DK = 64
GLA_DV = 128
GLA_GATE_RANK = 16
GLA_GATE_TEMP = 16.0
GLA_CHUNK = 64
SWA_HEADS = 8
SWA_KV_HEADS = 2
SWA_HEAD_DIM = 64
SWA_WINDOW = 128
SWA_BLOCK = 128
PEER_HEADS = 8
PEER_N_KEYS = 128
PEER_N_EXPERTS = PEER_N_KEYS * PEER_N_KEYS
PEER_QUERY_DIM = 256
PEER_TOPK = 16
PEER_TOKEN_BLOCK = 128
RMS_EPS = 1e-6

GLA_QK_WIDTH = GLA_HEADS * GLA_DK
GLA_WIDTH = GLA_HEADS * GLA_DV
SWA_WIDTH = SWA_HEADS * SWA_HEAD_DIM
SWA_KV_WIDTH = SWA_KV_HEADS * SWA_HEAD_DIM
MIX_WIDTH = GLA_WIDTH + SWA_WIDTH
IN_WIDTH = 2 * GLA_QK_WIDTH + GLA_WIDTH + GLA_GATE_RANK + GLA_WIDTH + SWA_WIDTH + 2 * SWA_KV_WIDTH


def rms_norm(t, g):
    tf = t.astype(jnp.float32)
    tf = tf * lax.rsqrt(jnp.mean(tf * tf, axis=-1, keepdims=True) + RMS_EPS)
    return (tf * g.astype(jnp.float32)).astype(t.dtype)


def modulate(t, g, shift, scale):
    return rms_norm(t, g) * (1 + scale[:, None, :]) + shift[:, None, :]


def gla_mixer(q, k, v, log_a):
    bsz, s = q.shape[0], q.shape[1]
    nc = s // GLA_CHUNK

    def to_chunks(t):
        return t.astype(jnp.float32).reshape(bsz, nc, GLA_CHUNK, GLA_HEADS, -1).transpose(1, 0, 3, 2, 4)

    causal = jnp.tril(jnp.ones((GLA_CHUNK, GLA_CHUNK), dtype=bool))

    def step(state, inp):
        qc, kc, vc, ac = inp
        b = jnp.cumsum(ac, axis=2)
        o_inter = jnp.einsum('bhtk,bhkv->bhtv', qc * jnp.exp(b), state)
        diff = b[:, :, :, None, :] - b[:, :, None, :, :]
        decay = jnp.exp(jnp.where(causal[None, None, :, :, None], diff, -jnp.inf))
        scores = jnp.einsum('bhtk,bhsk,bhtsk->bhts', qc, kc, decay)
        o_intra = jnp.einsum('bhts,bhsv->bhtv', scores, vc)
        b_last = b[:, :, -1:, :]
        state = jnp.exp(b_last[:, :, 0, :, None]) * state + jnp.einsum(
            'bhsk,bhsv->bhkv', kc * jnp.exp(b_last - b), vc)
        return state, o_inter + o_intra

    init = jnp.zeros((bsz, GLA_HEADS, GLA_DK, GLA_DV), jnp.float32)
    xs = (to_chunks(q * GLA_DK ** -0.5), to_chunks(k), to_chunks(v), to_chunks(log_a))
    _, o = lax.scan(step, init, xs)
    return o.transpose(1, 0, 3, 2, 4).reshape(bsz, s, GLA_HEADS, GLA_DV).astype(v.dtype)


def swa_mixer(q, k, v, sinks):
    bsz, s = q.shape[0], q.shape[1]
    nb = s // SWA_BLOCK
    grp = SWA_HEADS // SWA_KV_HEADS
    qb = q.reshape(bsz, nb, SWA_BLOCK, SWA_KV_HEADS, grp, SWA_HEAD_DIM) * SWA_HEAD_DIM ** -0.5

    def with_prev(t):
        tb = t.reshape(bsz, nb, SWA_BLOCK, SWA_KV_HEADS, SWA_HEAD_DIM)
        prev = jnp.pad(tb, ((0, 0), (1, 0), (0, 0), (0, 0), (0, 0)))[:, :-1]
        return jnp.concatenate([prev, tb], axis=2)

    kb, vb = with_prev(k), with_prev(v)
    sc = jnp.einsum('bnqhgd,bnkhd->bnhgqk', qb, kb).astype(jnp.float32)
    qi = jnp.arange(SWA_BLOCK)[:, None]
    kj = jnp.arange(2 * SWA_BLOCK)[None, :]
    dist = qi + SWA_BLOCK - kj
    key_pos = jnp.arange(nb)[:, None] * SWA_BLOCK - SWA_BLOCK + kj[0][None, :]
    valid = ((dist >= 0) & (dist < SWA_WINDOW))[None] & (key_pos >= 0)[:, None, :]
    slopes = (2.0 ** (-8.0 * (jnp.arange(SWA_HEADS) + 1) / SWA_HEADS)).astype(jnp.float32)
    slopes = slopes.reshape(SWA_KV_HEADS, grp)[None, None, :, :, None, None]
    sc = sc - slopes * dist.astype(jnp.float32)
    sc = jnp.where(valid[None, :, None, None], sc, -jnp.inf)
    sink = sinks.astype(jnp.float32).reshape(SWA_KV_HEADS, grp)[None, None, :, :, None, None]
    m = jnp.maximum(jnp.max(sc, axis=-1, keepdims=True), sink)
    p = jnp.exp(sc - m)
    p = p / (jnp.sum(p, axis=-1, keepdims=True) + jnp.exp(sink - m))
    o = jnp.einsum('bnhgqk,bnkhd->bnqhgd', p.astype(vb.dtype), vb)
    return o.reshape(bsz, s, SWA_WIDTH)


def peer_mixer(h, w_q, keys_1, keys_2, u_tab, v_tab):
    bsz, s, d = h.shape
    t = bsz * s
    hf = h.reshape(t, d)
    q = (hf @ w_q).reshape(t, PEER_HEADS, PEER_QUERY_DIM)
    half = PEER_QUERY_DIM // 2
    s1 = jnp.einsum('thd,nd->thn', q[..., :half], keys_1).astype(jnp.float32)
    s2 = jnp.einsum('thd,nd->thn', q[..., half:], keys_2).astype(jnp.float32)
    v1, i1 = lax.top_k(s1, PEER_TOPK)
    v2, i2 = lax.top_k(s2, PEER_TOPK)
    cand = (v1[..., :, None] + v2[..., None, :]).reshape(t, PEER_HEADS, PEER_TOPK * PEER_TOPK)
    sc, ci = lax.top_k(cand, PEER_TOPK)
    e1 = jnp.take_along_axis(i1, ci // PEER_TOPK, axis=-1)
    e2 = jnp.take_along_axis(i2, ci % PEER_TOPK, axis=-1)
    experts = (e1 * PEER_N_KEYS + e2).reshape(t, PEER_HEADS * PEER_TOPK)
    gates = jax.nn.softmax(sc, axis=-1).reshape(t, PEER_HEADS * PEER_TOPK)
    nblk = t // PEER_TOKEN_BLOCK

    def expert_block(args):
        hb, eb, gb = args
        a = jax.nn.gelu(jnp.einsum('td,ted->te', hb, u_tab[eb]).astype(jnp.float32), approximate=False)
        w = (gb * a).astype(hb.dtype)
        return jnp.einsum('te,ted->td', w, v_tab[eb])

    out = lax.map(expert_block, (hf.reshape(nblk, PEER_TOKEN_BLOCK, d),
                                 experts.reshape(nblk, PEER_TOKEN_BLOCK, -1),
                                 gates.reshape(nblk, PEER_TOKEN_BLOCK, -1)))
    return out.reshape(bsz, s, d)


def _final_norm_kernel(x_ref, g_ref, o_ref):
    xf = x_ref[...]
    ms = jnp.mean(xf * xf, axis=-1, keepdims=True)
    o_ref[...] = xf * lax.rsqrt(ms + RMS_EPS) * g_ref[...]


def _final_norm(x, g):
    b, s, d = x.shape
    xf = x.reshape(b * s, d)
    tm = 1024
    out = pl.pallas_call(
        _final_norm_kernel,
        grid=(b * s // tm,),
        in_specs=[pl.BlockSpec((tm, d), lambda i: (i, 0)), pl.BlockSpec((1, d), lambda i: (0, 0))],
        out_specs=pl.BlockSpec((tm, d), lambda i: (i, 0)),
        out_shape=jax.ShapeDtypeStruct((b * s, d), x.dtype),
    )(xf, g.reshape(1, d))
    return out.reshape(b, s, d)


def kernel(x, c, w_ada, b_ada, norm1_g, norm2_g, w_in, w_gla_alpha, b_gla_alpha, gla_norm_g,
           swa_sinks, swa_norm_g, w_out, w_peer_q, peer_keys_1, peer_keys_2, peer_u, peer_v, final_g):
    bsz, s, _ = x.shape
    widths = (GLA_QK_WIDTH, GLA_QK_WIDTH, GLA_WIDTH, GLA_GATE_RANK, GLA_WIDTH, SWA_WIDTH, SWA_KV_WIDTH, SWA_KV_WIDTH)
    split_at = [int(v) for v in np.cumsum(widths)[:-1]]
    c_act = jax.nn.silu(c)
    for l in range(DEPTH):
        mod = c_act @ w_ada[l] + b_ada[l]
        shift1, scale1, gate1, shift2, scale2, gate2 = jnp.split(mod, 6, axis=-1)
        h = modulate(x, norm1_g[l], shift1, scale1)
        proj = h @ w_in[l]
        q_g, k_g, v_g, a_lr, r_g, q_s, k_s, v_s = jnp.split(proj, split_at, axis=-1)
        log_a = jax.nn.log_sigmoid((a_lr @ w_gla_alpha[l] + b_gla_alpha[l]).astype(jnp.float32)) / GLA_GATE_TEMP
        o_gla = gla_mixer(q_g.reshape(bsz, s, GLA_HEADS, GLA_DK),
                          k_g.reshape(bsz, s, GLA_HEADS, GLA_DK),
                          v_g.reshape(bsz, s, GLA_HEADS, GLA_DV),
                          log_a.reshape(bsz, s, GLA_HEADS, GLA_DK))
        o_gla = rms_norm(o_gla, gla_norm_g[l].reshape(GLA_HEADS, GLA_DV)).reshape(bsz, s, GLA_WIDTH)
        o_gla = o_gla * jax.nn.silu(r_g)
        o_swa = swa_mixer(q_s.reshape(bsz, s, SWA_HEADS, SWA_HEAD_DIM),
                          k_s.reshape(bsz, s, SWA_KV_HEADS, SWA_HEAD_DIM),
                          v_s.reshape(bsz, s, SWA_KV_HEADS, SWA_HEAD_DIM),
                          swa_sinks[l])
        o_swa = rms_norm(o_swa, swa_norm_g[l])
        y = jnp.concatenate([o_gla, o_swa], axis=-1) @ w_out[l]
        x = x + gate1[:, None, :] * y
        h2 = modulate(x, norm2_g[l], shift2, scale2)
        x = x + gate2[:, None, :] * peer_mixer(h2, w_peer_q[l], peer_keys_1[l], peer_keys_2[l], peer_u[l], peer_v[l])
    return _final_norm(x, final_g)
```

```python
import jax, jax.numpy as jnp
from jax import lax
import numpy as np
from jax.experimental import pallas as pl
from jax.experimental.pallas import tpu as pltpu

D_MODEL = 1024
BATCH = 4
SEQ = 8192
DEPTH = 1

GLA_HEADS = 4
GLA_DK = 64
GLA_DV = 128
GLA_GATE_RANK = 16
GLA_GATE_TEMP = 16.0
GLA_CHUNK = 64
SWA_HEADS = 8
SWA_KV_HEADS = 2
SWA_HEAD_DIM = 64
SWA_WINDOW = 128
SWA_BLOCK = 128
PEER_HEADS = 8
PEER_N_KEYS = 128
PEER_N_EXPERTS = PEER_N_KEYS * PEER_N_KEYS
PEER_QUERY_DIM = 256
PEER_TOPK = 16
PEER_TOKEN_BLOCK = 128
RMS_EPS = 1e-6

GLA_QK_WIDTH = GLA_HEADS * GLA_DK
GLA_WIDTH = GLA_HEADS * GLA_DV
SWA_WIDTH = SWA_HEADS * SWA_HEAD_DIM
SWA_KV_WIDTH = SWA_KV_HEADS * SWA_HEAD_DIM
MIX_WIDTH = GLA_WIDTH + SWA_WIDTH
IN_WIDTH = 2 * GLA_QK_WIDTH + GLA_WIDTH + GLA_GATE_RANK + GLA_WIDTH + SWA_WIDTH + 2 * SWA_KV_WIDTH


def rms_norm(t, g):
    tf = t.astype(jnp.float32)
    tf = tf * lax.rsqrt(jnp.mean(tf * tf, axis=-1, keepdims=True) + RMS_EPS)
    return (tf * g.astype(jnp.float32)).astype(t.dtype)


def modulate(t, g, shift, scale):
    return rms_norm(t, g) * (1 + scale[:, None, :]) + shift[:, None, :]


def gla_mixer(q, k, v, log_a):
    bsz, s = q.shape[0], q.shape[1]
    nc = s // GLA_CHUNK

    def to_chunks(t):
        return t.astype(jnp.float32).reshape(bsz, nc, GLA_CHUNK, GLA_HEADS, -1).transpose(1, 0, 3, 2, 4)

    causal = jnp.tril(jnp.ones((GLA_CHUNK, GLA_CHUNK), dtype=bool))

    def step(state, inp):
        qc, kc, vc, ac = inp
        b = jnp.cumsum(ac, axis=2)
        o_inter = jnp.einsum('bhtk,bhkv->bhtv', qc * jnp.exp(b), state)
        diff = b[:, :, :, None, :] - b[:, :, None, :, :]
        decay = jnp.exp(jnp.where(causal[None, None, :, :, None], diff, -jnp.inf))
        scores = jnp.einsum('bhtk,bhsk,bhtsk->bhts', qc, kc, decay)
        o_intra = jnp.einsum('bhts,bhsv->bhtv', scores, vc)
        b_last = b[:, :, -1:, :]
        state = jnp.exp(b_last[:, :, 0, :, None]) * state + jnp.einsum(
            'bhsk,bhsv->bhkv', kc * jnp.exp(b_last - b), vc)
        return state, o_inter + o_intra

    init = jnp.zeros((bsz, GLA_HEADS, GLA_DK, GLA_DV), jnp.float32)
    xs = (to_chunks(q * GLA_DK ** -0.5), to_chunks(k), to_chunks(v), to_chunks(log_a))
    _, o = lax.scan(step, init, xs)
    return o.transpose(1, 0, 3, 2, 4).reshape(bsz, s, GLA_HEADS, GLA_DV).astype(v.dtype)


def swa_mixer(q, k, v, sinks):
    bsz, s = q.shape[0], q.shape[1]
    nb = s // SWA_BLOCK
    grp = SWA_HEADS // SWA_KV_HEADS
    qb = q.reshape(bsz, nb, SWA_BLOCK, SWA_KV_HEADS, grp, SWA_HEAD_DIM) * SWA_HEAD_DIM ** -0.5

    def with_prev(t):
        tb = t.reshape(bsz, nb, SWA_BLOCK, SWA_KV_HEADS, SWA_HEAD_DIM)
        prev = jnp.pad(tb, ((0, 0), (1, 0), (0, 0), (0, 0), (0, 0)))[:, :-1]
        return jnp.concatenate([prev, tb], axis=2)

    kb, vb = with_prev(k), with_prev(v)
    sc = jnp.einsum('bnqhgd,bnkhd->bnhgqk', qb, kb).astype(jnp.float32)
    qi = jnp.arange(SWA_BLOCK)[:, None]
    kj = jnp.arange(2 * SWA_BLOCK)[None, :]
    dist = qi + SWA_BLOCK - kj
    key_pos = jnp.arange(nb)[:, None] * SWA_BLOCK - SWA_BLOCK + kj[0][None, :]
    valid = ((dist >= 0) & (dist < SWA_WINDOW))[None] & (key_pos >= 0)[:, None, :]
    slopes = (2.0 ** (-8.0 * (jnp.arange(SWA_HEADS) + 1) / SWA_HEADS)).astype(jnp.float32)
    slopes = slopes.reshape(SWA_KV_HEADS, grp)[None, None, :, :, None, None]
    sc = sc - slopes * dist.astype(jnp.float32)
    sc = jnp.where(valid[None, :, None, None], sc, -jnp.inf)
    sink = sinks.astype(jnp.float32).reshape(SWA_KV_HEADS, grp)[None, None, :, :, None, None]
    m = jnp.maximum(jnp.max(sc, axis=-1, keepdims=True), sink)
    p = jnp.exp(sc - m)
    p = p / (jnp.sum(p, axis=-1, keepdims=True) + jnp.exp(sink - m))
    o = jnp.einsum('bnhgqk,bnkhd->bnqhgd', p.astype(vb.dtype), vb)
    return o.reshape(bsz, s, SWA_WIDTH)


def peer_mixer(h, w_q, keys_1, keys_2, u_tab, v_tab):
    bsz, s, d = h.shape
    t = bsz * s
    hf = h.reshape(t, d)
    q = (hf @ w_q).reshape(t, PEER_HEADS, PEER_QUERY_DIM)
    half = PEER_QUERY_DIM // 2
    s1 = jnp.einsum('thd,nd->thn', q[..., :half], keys_1).astype(jnp.float32)
    s2 = jnp.einsum('thd,nd->thn', q[..., half:], keys_2).astype(jnp.float32)
    v1, i1 = lax.top_k(s1, PEER_TOPK)
    v2, i2 = lax.top_k(s2, PEER_TOPK)
    cand = (v1[..., :, None] + v2[..., None, :]).reshape(t, PEER_HEADS, PEER_TOPK * PEER_TOPK)
    sc, ci = lax.top_k(cand, PEER_TOPK)
    e1 = jnp.take_along_axis(i1, ci // PEER_TOPK, axis=-1)
    e2 = jnp.take_along_axis(i2, ci % PEER_TOPK, axis=-1)
    experts = (e1 * PEER_N_KEYS + e2).reshape(t, PEER_HEADS * PEER_TOPK)
    gates = jax.nn.softmax(sc, axis=-1).reshape(t, PEER_HEADS * PEER_TOPK)
    return _peer_experts(hf, experts, gates, u_tab, v_tab).reshape(bsz, s, d)


PEER_PAIRS = PEER_HEADS * PEER_TOPK
PEER_HALF = PEER_N_EXPERTS // 2
PEER_TB = 64
ROW_SUB, ROW_LANE = 8, D_MODEL // 8
HI_MASK = -65536


def _pack_table(tab):
    b = lax.bitcast_convert_type(tab.astype(jnp.bfloat16), jnp.uint16).astype(jnp.uint32)
    word = b[:PEER_HALF] | (b[PEER_HALF:] << 16)
    return lax.bitcast_convert_type(word, jnp.int32).reshape(PEER_HALF, ROW_SUB, ROW_LANE)


def _unpack_row(tab_ref, row, sh):
    w = tab_ref[row]
    return lax.bitcast_convert_type(jnp.left_shift(w, sh) & HI_MASK, jnp.float32)


_ORDER = (6, 2, 4, 0, 7, 3, 5, 1)


def _sublane_sums(ps, sub):
    m4 = sub < 4
    q = []
    for i in range(4):
        a, b = ps[2 * i], ps[2 * i + 1]
        q.append(jnp.where(m4, a, b) + pltpu.roll(jnp.where(m4, b, a), 4, 0))
    b1 = (sub & 2) != 0
    w = []
    for i in range(2):
        x, y = q[2 * i], q[2 * i + 1]
        w.append(jnp.where(b1, x + pltpu.roll(x, 2, 0), y + pltpu.roll(y, 6, 0)))
    b0 = (sub & 1) != 0
    x, y = w
    return jnp.where(b0, x + pltpu.roll(x, 1, 0), y + pltpu.roll(y, 7, 0))


def _peer_u_kernel(rows_ref, shs_ref, h_ref, tab_ref, a_ref):
    sub = lax.broadcasted_iota(jnp.int32, (ROW_SUB, ROW_LANE), 0)
    lane = lax.broadcasted_iota(jnp.int32, (ROW_SUB, ROW_LANE), 1)

    def group_body(tg, carry):
        def tok_body(tl, acc):
            t = tg * 8 + tl
            h = h_ref[t]
            base = t * PEER_PAIRS
            for g in range(PEER_PAIRS // 8):
                ps = []
                for s in range(8):
                    p = base + g * 8 + s
                    ps.append(_unpack_row(tab_ref, rows_ref[p], shs_ref[p]) * h)
                col = jnp.sum(_sublane_sums(ps, sub), axis=1, keepdims=True)
                acc = jnp.where(lane == tl * 16 + g, col, acc)
            return acc

        a_ref[tg] = lax.fori_loop(0, 8, tok_body, jnp.zeros((ROW_SUB, ROW_LANE), jnp.float32))
        return carry

    lax.fori_loop(0, PEER_TB // 8, group_body, 0)


def _peer_v_kernel(rows_ref, shs_ref, w_ref, tab_ref, o_ref):
    def tok_body(t, carry):
        base = t * PEER_PAIRS
        accs = [jnp.zeros((ROW_SUB, ROW_LANE), jnp.float32) for _ in range(4)]
        for p in range(PEER_PAIRS):
            v = _unpack_row(tab_ref, rows_ref[base + p], shs_ref[base + p])
            accs[p % 4] = accs[p % 4] + v * w_ref[base + p]
        o_ref[t] = (accs[0] + accs[1]) + (accs[2] + accs[3])
        return carry

    lax.fori_loop(0, PEER_TB, tok_body, 0)


def _gate_kernel(g_ref, a_ref, w_ref):
    a = a_ref[...]
    w_ref[...] = g_ref[...] * (0.5 * a * (1.0 + lax.erf(a * (2.0 ** -0.5))))


_PEER_VMEM_LIMIT = 56 * 1024 * 1024


def _peer_experts(hf, experts, gates, u_tab, v_tab):
    t, d = hf.shape
    nblk = t // PEER_TB
    smem_spec = pl.BlockSpec((PEER_TB * PEER_PAIRS,), lambda i: (i,), memory_space=pltpu.SMEM)
    tab_spec = pl.BlockSpec(memory_space=pltpu.VMEM)
    params = pltpu.CompilerParams(dimension_semantics=("arbitrary",), vmem_limit_bytes=_PEER_VMEM_LIMIT)
    rows = (experts % PEER_HALF).astype(jnp.int32).reshape(-1)
    shs = (16 - 16 * (experts // PEER_HALF)).astype(jnp.int32).reshape(-1)
    a_tiles = pl.pallas_call(
        _peer_u_kernel,
        grid=(nblk,),
        in_specs=[smem_spec, smem_spec,
                  pl.BlockSpec((PEER_TB, ROW_SUB, ROW_LANE), lambda i: (i, 0, 0)), tab_spec],
        out_specs=pl.BlockSpec((PEER_TB // 8, ROW_SUB, ROW_LANE), lambda i: (i, 0, 0)),
        out_shape=jax.ShapeDtypeStruct((t // 8, ROW_SUB, ROW_LANE), jnp.float32),
        compiler_params=params, name="peer_u",
    )(rows, shs, hf.reshape(t, ROW_SUB, ROW_LANE), _pack_table(u_tab))
    inv = np.argsort(np.array(_ORDER))
    a = a_tiles.reshape(t // 8, 8, 8, PEER_PAIRS // 8)[:, inv].transpose(0, 2, 3, 1).reshape(t, PEER_PAIRS)
    rb = 2048
    w = pl.pallas_call(
        _gate_kernel,
        grid=(t // rb,),
        in_specs=[pl.BlockSpec((rb, PEER_PAIRS), lambda i: (i, 0))] * 2,
        out_specs=pl.BlockSpec((rb, PEER_PAIRS), lambda i: (i, 0)),
        out_shape=jax.ShapeDtypeStruct((t, PEER_PAIRS), jnp.float32), name="peer_gate",
    )(gates, a)
    out = pl.pallas_call(
        _peer_v_kernel,
        grid=(nblk,),
        in_specs=[smem_spec, smem_spec, smem_spec, tab_spec],
        out_specs=pl.BlockSpec((PEER_TB, ROW_SUB, ROW_LANE), lambda i: (i, 0, 0)),
        out_shape=jax.ShapeDtypeStruct((t, ROW_SUB, ROW_LANE), jnp.float32),
        compiler_params=params, name="peer_v",
    )(rows, shs, w.reshape(-1), _pack_table(v_tab))
    return out.reshape(t, d)


def _final_norm_kernel(x_ref, g_ref, o_ref):
    xf = x_ref[...]
    ms = jnp.mean(xf * xf, axis=-1, keepdims=True)
    o_ref[...] = xf * lax.rsqrt(ms + RMS_EPS) * g_ref[...]


def _final_norm(x, g):
    b, s, d = x.shape
    xf = x.reshape(b * s, d)
    tm = 1024
    out = pl.pallas_call(
        _final_norm_kernel,
        grid=(b * s // tm,),
        in_specs=[pl.BlockSpec((tm, d), lambda i: (i, 0)), pl.BlockSpec((1, d), lambda i: (0, 0))],
        out_specs=pl.BlockSpec((tm, d), lambda i: (i, 0)),
        out_shape=jax.ShapeDtypeStruct((b * s, d), x.dtype),
    )(xf, g.reshape(1, d))
    return out.reshape(b, s, d)


def kernel(x, c, w_ada, b_ada, norm1_g, norm2_g, w_in, w_gla_alpha, b_gla_alpha, gla_norm_g,
           swa_sinks, swa_norm_g, w_out, w_peer_q, peer_keys_1, peer_keys_2, peer_u, peer_v, final_g):
    bsz, s, _ = x.shape
    widths = (GLA_QK_WIDTH, GLA_QK_WIDTH, GLA_WIDTH, GLA_GATE_RANK, GLA_WIDTH, SWA_WIDTH, SWA_KV_WIDTH, SWA_KV_WIDTH)
    split_at = [int(v) for v in np.cumsum(widths)[:-1]]
    c_act = jax.nn.silu(c)
    for l in range(DEPTH):
        mod = c_act @ w_ada[l] + b_ada[l]
        shift1, scale1, gate1, shift2, scale2, gate2 = jnp.split(mod, 6, axis=-1)
        h = modulate(x, norm1_g[l], shift1, scale1)
        proj = h @ w_in[l]
        q_g, k_g, v_g, a_lr, r_g, q_s, k_s, v_s = jnp.split(proj, split_at, axis=-1)
        log_a = jax.nn.log_sigmoid((a_lr @ w_gla_alpha[l] + b_gla_alpha[l]).astype(jnp.float32)) / GLA_GATE_TEMP
        o_gla = gla_mixer(q_g.reshape(bsz, s, GLA_HEADS, GLA_DK),
                          k_g.reshape(bsz, s, GLA_HEADS, GLA_DK),
                          v_g.reshape(bsz, s, GLA_HEADS, GLA_DV),
                          log_a.reshape(bsz, s, GLA_HEADS, GLA_DK))
        o_gla = rms_norm(o_gla, gla_norm_g[l].reshape(GLA_HEADS, GLA_DV)).reshape(bsz, s, GLA_WIDTH)
        o_gla = o_gla * jax.nn.silu(r_g)
        o_swa = swa_mixer(q_s.reshape(bsz, s, SWA_HEADS, SWA_HEAD_DIM),
                          k_s.reshape(bsz, s, SWA_KV_HEADS, SWA_HEAD_DIM),
                          v_s.reshape(bsz, s, SWA_KV_HEADS, SWA_HEAD_DIM),
                          swa_sinks[l])
        o_swa = rms_norm(o_swa, swa_norm_g[l])
        y = jnp.concatenate([o_gla, o_swa], axis=-1) @ w_out[l]
        x = x + gate1[:, None, :] * y
        h2 = modulate(x, norm2_g[l], shift2, scale2)
        x = x + gate2[:, None, :] * peer_mixer(h2, w_peer_q[l], peer_keys_1[l], peer_keys_2[l], peer_u[l], peer_v[l])
    return _final_norm(x, final_g)
```

```python
import jax, jax.numpy as jnp
from jax import lax
import numpy as np
from jax.experimental import pallas as pl
from jax.experimental.pallas import tpu as pltpu

D_MODEL = 1024
BATCH = 4
SEQ = 8192
DEPTH = 1

GLA_HEADS = 4
GLA_DK = 64
GLA_DV = 128
GLA_GATE_RANK = 16
GLA_GATE_TEMP = 16.0
GLA_CHUNK = 64
SWA_HEADS = 8
SWA_KV_HEADS = 2
SWA_HEAD_DIM = 64
SWA_WINDOW = 128
SWA_BLOCK = 128
PEER_HEADS = 8
PEER_N_KEYS = 128
PEER_N_EXPERTS = PEER_N_KEYS * PEER_N_KEYS
PEER_QUERY_DIM = 256
PEER_TOPK = 16
RMS_EPS = 1e-6

GLA_QK_WIDTH = GLA_HEADS * GLA_DK
GLA_WIDTH = GLA_HEADS * GLA_DV
SWA_WIDTH = SWA_HEADS * SWA_HEAD_DIM
SWA_KV_WIDTH = SWA_KV_HEADS * SWA_HEAD_DIM
MIX_WIDTH = GLA_WIDTH + SWA_WIDTH
IN_WIDTH = 2 * GLA_QK_WIDTH + GLA_WIDTH + GLA_GATE_RANK + GLA_WIDTH + SWA_WIDTH + 2 * SWA_KV_WIDTH


def rms_norm(t, g):
    tf = t.astype(jnp.float32)
    tf = tf * lax.rsqrt(jnp.mean(tf * tf, axis=-1, keepdims=True) + RMS_EPS)
    return (tf * g.astype(jnp.float32)).astype(t.dtype)


def modulate(t, g, shift, scale):
    return rms_norm(t, g) * (1 + scale[:, None, :]) + shift[:, None, :]


def gla_mixer(q, k, v, log_a):
    bsz, s = q.shape[0], q.shape[1]
    nc = s // GLA_CHUNK

    def to_chunks(t):
        return t.astype(jnp.float32).reshape(bsz, nc, GLA_CHUNK, GLA_HEADS, -1).transpose(1, 0, 3, 2, 4)

    causal = jnp.tril(jnp.ones((GLA_CHUNK, GLA_CHUNK), dtype=bool))

    def step(state, inp):
        qc, kc, vc, ac = inp
        b = jnp.cumsum(ac, axis=2)
        o_inter = jnp.einsum('bhtk,bhkv->bhtv', qc * jnp.exp(b), state)
        diff = b[:, :, :, None, :] - b[:, :, None, :, :]
        decay = jnp.exp(jnp.where(causal[None, None, :, :, None], diff, -jnp.inf))
        scores = jnp.einsum('bhtk,bhsk,bhtsk->bhts', qc, kc, decay)
        o_intra = jnp.einsum('bhts,bhsv->bhtv', scores, vc)
        b_last = b[:, :, -1:, :]
        state = jnp.exp(b_last[:, :, 0, :, None]) * state + jnp.einsum(
            'bhsk,bhsv->bhkv', kc * jnp.exp(b_last - b), vc)
        return state, o_inter + o_intra

    init = jnp.zeros((bsz, GLA_HEADS, GLA_DK, GLA_DV), jnp.float32)
    xs = (to_chunks(q * GLA_DK ** -0.5), to_chunks(k), to_chunks(v), to_chunks(log_a))
    _, o = lax.scan(step, init, xs)
    return o.transpose(1, 0, 3, 2, 4).reshape(bsz, s, GLA_HEADS, GLA_DV).astype(v.dtype)


def swa_mixer(q, k, v, sinks):
    bsz, s = q.shape[0], q.shape[1]
    nb = s // SWA_BLOCK
    grp = SWA_HEADS // SWA_KV_HEADS
    qb = q.reshape(bsz, nb, SWA_BLOCK, SWA_KV_HEADS, grp, SWA_HEAD_DIM) * SWA_HEAD_DIM ** -0.5

    def with_prev(t):
        tb = t.reshape(bsz, nb, SWA_BLOCK, SWA_KV_HEADS, SWA_HEAD_DIM)
        prev = jnp.pad(tb, ((0, 0), (1, 0), (0, 0), (0, 0), (0, 0)))[:, :-1]
        return jnp.concatenate([prev, tb], axis=2)

    kb, vb = with_prev(k), with_prev(v)
    sc = jnp.einsum('bnqhgd,bnkhd->bnhgqk', qb, kb).astype(jnp.float32)
    qi = jnp.arange(SWA_BLOCK)[:, None]
    kj = jnp.arange(2 * SWA_BLOCK)[None, :]
    dist = qi + SWA_BLOCK - kj
    key_pos = jnp.arange(nb)[:, None] * SWA_BLOCK - SWA_BLOCK + kj[0][None, :]
    valid = ((dist >= 0) & (dist < SWA_WINDOW))[None] & (key_pos >= 0)[:, None, :]
    slopes = (2.0 ** (-8.0 * (jnp.arange(SWA_HEADS) + 1) / SWA_HEADS)).astype(jnp.float32)
    slopes = slopes.reshape(SWA_KV_HEADS, grp)[None, None, :, :, None, None]
    sc = sc - slopes * dist.astype(jnp.float32)
    sc = jnp.where(valid[None, :, None, None], sc, -jnp.inf)
    sink = sinks.astype(jnp.float32).reshape(SWA_KV_HEADS, grp)[None, None, :, :, None, None]
    m = jnp.maximum(jnp.max(sc, axis=-1, keepdims=True), sink)
    p = jnp.exp(sc - m)
    p = p / (jnp.sum(p, axis=-1, keepdims=True) + jnp.exp(sink - m))
    o = jnp.einsum('bnhgqk,bnkhd->bnqhgd', p.astype(vb.dtype), vb)
    return o.reshape(bsz, s, SWA_WIDTH)


def peer_mixer(h, w_q, keys_1, keys_2, u_tab, v_tab):
    bsz, s, d = h.shape
    t = bsz * s
    hf = h.reshape(t, d)
    experts, gates = _peer_route(hf, w_q, keys_1, keys_2)
    return _peer_experts(hf, experts, gates, u_tab, v_tab).reshape(bsz, s, d)


ROUTE_TT = 256
NEG_INF = float("-inf")


def _top16_rows(s, code, big):
    vals, codes = [], []
    for _ in range(PEER_TOPK):
        m = jnp.max(s, axis=0, keepdims=True)
        c = jnp.min(jnp.where(s == m, code, big), axis=0, keepdims=True)
        s = jnp.where(code == c, NEG_INF, s)
        vals.append(m)
        codes.append(c)
    return vals, codes


def _peer_route_kernel(h_ref, wq_ref, k1_ref, k2_ref, e_ref, g_ref, q_ref):
    half = PEER_QUERY_DIM // 2
    tt = ROUTE_TT
    q_ref[...] = jnp.dot(h_ref[...].astype(jnp.bfloat16), wq_ref[...], preferred_element_type=jnp.float32)
    key_id = lax.broadcasted_iota(jnp.int32, (PEER_N_KEYS, 2 * tt), 0)
    sub8 = lax.broadcasted_iota(jnp.int32, (8, tt), 0)
    nt = (((1,), (1,)), ((), ()))

    def head_body(hd, carry):
        off = pl.multiple_of(hd * PEER_QUERY_DIM, PEER_QUERY_DIM)
        q1 = q_ref[:, pl.ds(off, half)].astype(jnp.bfloat16)
        q2 = q_ref[:, pl.ds(off + half, half)].astype(jnp.bfloat16)
        s1 = lax.dot_general(k1_ref[...], q1, nt, preferred_element_type=jnp.float32)
        s2 = lax.dot_general(k2_ref[...], q2, nt, preferred_element_type=jnp.float32)
        vals, ids = _top16_rows(jnp.concatenate([s1, s2], axis=1), key_id, PEER_N_KEYS)
        v = jnp.concatenate(vals, axis=0)
        ix = jnp.concatenate(ids, axis=0)
        v1, v2, i1, i2 = v[:, :tt], v[:, tt:], ix[:, :tt], ix[:, tt:]
        cand, code = [], []
        for a, b0 in [(0, 0), (0, 8), (1, 0), (2, 0), (3, 0), (4, 0), (5, 0), (6, 0), (7, 0)]:
            cand.append(v1[a:a + 1] + v2[b0:b0 + 8])
            flat = a * PEER_TOPK + b0 + sub8
            code.append((flat << 14) | (i1[a:a + 1] * PEER_N_KEYS + i2[b0:b0 + 8]))
        cand.append(v1[8:16] + v2[0:1])
        code.append((((8 + sub8) * PEER_TOPK) << 14) | (i1[8:16] * PEER_N_KEYS + i2[0:1]))
        sc, cc = _top16_rows(jnp.concatenate(cand, axis=0), jnp.concatenate(code, axis=0), 1 << 30)
        sc = jnp.concatenate(sc, axis=0)
        ex = jnp.exp(sc - sc[0:1])
        g_ref[hd] = ex / jnp.sum(ex, axis=0, keepdims=True)
        e_ref[hd] = jnp.concatenate(cc, axis=0) & (PEER_N_EXPERTS - 1)
        return carry

    lax.fori_loop(0, PEER_HEADS, head_body, 0)


def _peer_route(hf, w_q, keys_1, keys_2):
    t, d = hf.shape
    qw = PEER_HEADS * PEER_QUERY_DIM
    const = lambda shape: pl.BlockSpec(shape, lambda i: tuple(0 for _ in shape))
    out_spec = pl.BlockSpec((PEER_HEADS, PEER_TOPK, ROUTE_TT), lambda i: (0, 0, i))
    e_t, g_t = pl.pallas_call(
        _peer_route_kernel,
        grid=(t // ROUTE_TT,),
        in_specs=[pl.BlockSpec((ROUTE_TT, d), lambda i: (i, 0)), const((d, qw)),
                  const(keys_1.shape), const(keys_2.shape)],
        out_specs=[out_spec, out_spec],
        out_shape=[jax.ShapeDtypeStruct((PEER_HEADS, PEER_TOPK, t), jnp.int32),
                   jax.ShapeDtypeStruct((PEER_HEADS, PEER_TOPK, t), jnp.float32)],
        scratch_shapes=[pltpu.VMEM((ROUTE_TT, qw), jnp.float32)],
        compiler_params=pltpu.CompilerParams(dimension_semantics=("arbitrary",), vmem_limit_bytes=48 * 1024 * 1024),
        name="peer_route",
    )(hf, w_q.astype(jnp.bfloat16), keys_1.astype(jnp.bfloat16), keys_2.astype(jnp.bfloat16))
    to_rows = lambda a: a.reshape(PEER_HEADS * PEER_TOPK, t).T
    return to_rows(e_t), to_rows(g_t)


PEER_PAIRS = PEER_HEADS * PEER_TOPK
PEER_TB = 64
PEER_UNROLL = 32
HALF_D = D_MODEL // 2
WORD_SUB, WORD_LANE = HALF_D // 128, 128
ROW_SUB = 2 * WORD_SUB
HI_MASK = -65536


def _pack_table(tab):
    b = lax.bitcast_convert_type(tab.astype(jnp.bfloat16), jnp.uint16).astype(jnp.uint32)
    word = b[:, :HALF_D] | (b[:, HALF_D:] << 16)
    return lax.bitcast_convert_type(word, jnp.int32).reshape(tab.shape[0] * WORD_SUB, WORD_LANE)


def _unpack_row(tab_ref, e):
    w = tab_ref[pl.ds(pl.multiple_of(e, WORD_SUB), WORD_SUB), :]
    return (lax.bitcast_convert_type(w << 16, jnp.float32), lax.bitcast_convert_type(w & HI_MASK, jnp.float32))


_ORDER = (6, 2, 4, 0, 7, 3, 5, 1)


def _fold_pairs(ps, sub):
    b1 = (sub & 2) != 0
    w = []
    for i in range(2):
        x, y = ps[2 * i], ps[2 * i + 1]
        w.append(jnp.where(b1, x + pltpu.roll(x, 2, 0), y + pltpu.roll(y, 6, 0)))
    b0 = (sub & 1) != 0
    x, y = w
    return jnp.where(b0, x + pltpu.roll(x, 1, 0), y + pltpu.roll(y, 7, 0))


def _peer_u_kernel(idx_ref, h_ref, tab_ref, a_ref):
    sub = lax.broadcasted_iota(jnp.int32, (ROW_SUB, WORD_LANE), 0)
    lane = lax.broadcasted_iota(jnp.int32, (ROW_SUB, WORD_LANE), 1)

    def group_body(tg, carry):
        def tok_body(tl, acc):
            t = tg * 8 + tl
            h = h_ref[t]
            h_lo = jnp.concatenate([h[:WORD_SUB], h[:WORD_SUB]], axis=0)
            h_hi = jnp.concatenate([h[WORD_SUB:], h[WORD_SUB:]], axis=0)
            row_idx = idx_ref.at[pl.ds(t * PEER_PAIRS, PEER_PAIRS)]
            for g in range(PEER_PAIRS // 8):
                ps = []
                for i in range(4):
                    w = jnp.concatenate([tab_ref[pl.ds(pl.multiple_of(row_idx[g * 8 + 2 * i + j], WORD_SUB), WORD_SUB), :]
                                         for j in range(2)], axis=0)
                    lo = lax.bitcast_convert_type(w << 16, jnp.float32)
                    hi = lax.bitcast_convert_type(w & HI_MASK, jnp.float32)
                    ps.append(lo * h_lo + hi * h_hi)
                col = jnp.sum(_fold_pairs(ps, sub), axis=1, keepdims=True)
                acc = jnp.where(lane == tl * 16 + g, col, acc)
            return acc

        a_ref[tg] = lax.fori_loop(0, 8, tok_body, jnp.zeros((ROW_SUB, WORD_LANE), jnp.float32))
        return carry

    lax.fori_loop(0, PEER_TB // 8, group_body, 0)


def _peer_v_kernel(idx_ref, w_ref, tab_ref, o_ref):
    def tok_body(t, carry):
        def chunk_body(ci, accs):
            accs = list(accs)
            start = t * PEER_PAIRS + ci * PEER_UNROLL
            idx_c = idx_ref.at[pl.ds(start, PEER_UNROLL)]
            w_c = w_ref.at[pl.ds(start, PEER_UNROLL)]
            for k in range(PEER_UNROLL):
                lo, hi = _unpack_row(tab_ref, idx_c[k])
                w = w_c[k]
                accs[2 * (k % 2)] = accs[2 * (k % 2)] + lo * w
                accs[2 * (k % 2) + 1] = accs[2 * (k % 2) + 1] + hi * w
            return tuple(accs)

        z = jnp.zeros((WORD_SUB, WORD_LANE), jnp.float32)
        accs = lax.fori_loop(0, PEER_PAIRS // PEER_UNROLL, chunk_body, (z, z, z, z))
        o_ref[t] = jnp.concatenate([accs[0] + accs[2], accs[1] + accs[3]], axis=0)
        return carry

    lax.fori_loop(0, PEER_TB, tok_body, 0)


def _gate_kernel(g_ref, a_ref, w_ref):
    a = a_ref[...]
    w_ref[...] = g_ref[...] * (0.5 * a * (1.0 + lax.erf(a * (2.0 ** -0.5))))


_PEER_VMEM_LIMIT = 56 * 1024 * 1024


def _peer_experts(hf, experts, gates, u_tab, v_tab):
    t, d = hf.shape
    nblk = t // PEER_TB
    smem_spec = pl.BlockSpec((PEER_TB * PEER_PAIRS,), lambda i: (i,), memory_space=pltpu.SMEM)
    tab_spec = pl.BlockSpec(memory_space=pltpu.VMEM)
    tok_spec = pl.BlockSpec((PEER_TB, PEER_PAIRS), lambda i: (i, 0))
    row_spec = pl.BlockSpec((PEER_TB, ROW_SUB, WORD_LANE), lambda i: (i, 0, 0))
    params = pltpu.CompilerParams(dimension_semantics=("arbitrary",), vmem_limit_bytes=_PEER_VMEM_LIMIT)
    idx = (experts.astype(jnp.int32) * WORD_SUB).reshape(-1)
    a_tiles = pl.pallas_call(
        _peer_u_kernel,
        grid=(nblk,),
        in_specs=[smem_spec, row_spec, tab_spec],
        out_specs=pl.BlockSpec((PEER_TB // 8, ROW_SUB, WORD_LANE), lambda i: (i, 0, 0)),
        out_shape=jax.ShapeDtypeStruct((t // 8, ROW_SUB, WORD_LANE), jnp.float32),
        compiler_params=params, name="peer_u",
    )(idx, hf.reshape(t, ROW_SUB, WORD_LANE), _pack_table(u_tab))
    inv = np.argsort(np.array(_ORDER))
    a = a_tiles.reshape(t // 8, 8, 8, PEER_PAIRS // 8)[:, inv].transpose(0, 2, 3, 1).reshape(t, PEER_PAIRS)
    rb = 2048
    blk = pl.BlockSpec((rb, PEER_PAIRS), lambda i: (i, 0))
    w = pl.pallas_call(
        _gate_kernel, grid=(t // rb,), in_specs=[blk, blk], out_specs=blk,
        out_shape=jax.ShapeDtypeStruct((t, PEER_PAIRS), jnp.float32), name="peer_gate",
    )(gates, a)
    out = pl.pallas_call(
        _peer_v_kernel,
        grid=(nblk,),
        in_specs=[smem_spec, smem_spec, tab_spec],
        out_specs=row_spec,
        out_shape=jax.ShapeDtypeStruct((t, ROW_SUB, WORD_LANE), jnp.float32),
        compiler_params=params, name="peer_v",
    )(idx, w.reshape(-1), _pack_table(v_tab))
    return out.reshape(t, d)


def _final_norm_kernel(x_ref, g_ref, o_ref):
    xf = x_ref[...]
    ms = jnp.mean(xf * xf, axis=-1, keepdims=True)
    o_ref[...] = xf * lax.rsqrt(ms + RMS_EPS) * g_ref[...]


def _final_norm(x, g):
    b, s, d = x.shape
    xf = x.reshape(b * s, d)
    tm = 1024
    out = pl.pallas_call(
        _final_norm_kernel,
        grid=(b * s // tm,),
        in_specs=[pl.BlockSpec((tm, d), lambda i: (i, 0)), pl.BlockSpec((1, d), lambda i: (0, 0))],
        out_specs=pl.BlockSpec((tm, d), lambda i: (i, 0)),
        out_shape=jax.ShapeDtypeStruct((b * s, d), x.dtype),
    )(xf, g.reshape(1, d))
    return out.reshape(b, s, d)


def kernel(x, c, w_ada, b_ada, norm1_g, norm2_g, w_in, w_gla_alpha, b_gla_alpha, gla_norm_g,
           swa_sinks, swa_norm_g, w_out, w_peer_q, peer_keys_1, peer_keys_2, peer_u, peer_v, final_g):
    bsz, s, _ = x.shape
    widths = (GLA_QK_WIDTH, GLA_QK_WIDTH, GLA_WIDTH, GLA_GATE_RANK, GLA_WIDTH, SWA_WIDTH, SWA_KV_WIDTH, SWA_KV_WIDTH)
    split_at = [int(v) for v in np.cumsum(widths)[:-1]]
    c_act = jax.nn.silu(c)
    for l in range(DEPTH):
        mod = c_act @ w_ada[l] + b_ada[l]
        shift1, scale1, gate1, shift2, scale2, gate2 = jnp.split(mod, 6, axis=-1)
        h = modulate(x, norm1_g[l], shift1, scale1)
        proj = h @ w_in[l]
        q_g, k_g, v_g, a_lr, r_g, q_s, k_s, v_s = jnp.split(proj, split_at, axis=-1)
        log_a = jax.nn.log_sigmoid((a_lr @ w_gla_alpha[l] + b_gla_alpha[l]).astype(jnp.float32)) / GLA_GATE_TEMP
        o_gla = gla_mixer(q_g.reshape(bsz, s, GLA_HEADS, GLA_DK),
                          k_g.reshape(bsz, s, GLA_HEADS, GLA_DK),
                          v_g.reshape(bsz, s, GLA_HEADS, GLA_DV),
                          log_a.reshape(bsz, s, GLA_HEADS, GLA_DK))
        o_gla = rms_norm(o_gla, gla_norm_g[l].reshape(GLA_HEADS, GLA_DV)).reshape(bsz, s, GLA_WIDTH)
        o_gla = o_gla * jax.nn.silu(r_g)
        o_swa = swa_mixer(q_s.reshape(bsz, s, SWA_HEADS, SWA_HEAD_DIM),
                          k_s.reshape(bsz, s, SWA_KV_HEADS, SWA_HEAD_DIM),
                          v_s.reshape(bsz, s, SWA_KV_HEADS, SWA_HEAD_DIM),
                          swa_sinks[l])
        o_swa = rms_norm(o_swa, swa_norm_g[l])
        y = jnp.concatenate([o_gla, o_swa], axis=-1) @ w_out[l]
        x = x + gate1[:, None, :] * y
        h2 = modulate(x, norm2_g[l], shift2, scale2)
        x = x + gate2[:, None, :] * peer_mixer(h2, w_peer_q[l], peer_keys_1[l], peer_keys_2[l], peer_u[l], peer_v[l])
    return _final_norm(x, final_g)
```

```python
import jax, jax.numpy as jnp
from jax import lax
import numpy as np
from jax.experimental import pallas as pl
from jax.experimental.pallas import tpu as pltpu

D_MODEL = 1024
BATCH = 4
SEQ = 8192
DEPTH = 1

GLA_HEADS = 4
GLA_DK = 64
GLA_DV = 128
GLA_GATE_RANK = 16
GLA_GATE_TEMP = 16.0
GLA_CHUNK = 64
SWA_HEADS = 8
SWA_KV_HEADS = 2
SWA_HEAD_DIM = 64
SWA_WINDOW = 128
SWA_BLOCK = 128
PEER_HEADS = 8
PEER_N_KEYS = 128
PEER_N_EXPERTS = PEER_N_KEYS * PEER_N_KEYS
PEER_QUERY_DIM = 256
PEER_TOPK = 16
RMS_EPS = 1e-6

GLA_QK_WIDTH = GLA_HEADS * GLA_DK
GLA_WIDTH = GLA_HEADS * GLA_DV
SWA_WIDTH = SWA_HEADS * SWA_HEAD_DIM
SWA_KV_WIDTH = SWA_KV_HEADS * SWA_HEAD_DIM
MIX_WIDTH = GLA_WIDTH + SWA_WIDTH
IN_WIDTH = 2 * GLA_QK_WIDTH + GLA_WIDTH + GLA_GATE_RANK + GLA_WIDTH + SWA_WIDTH + 2 * SWA_KV_WIDTH


def rms_norm(t, g):
    tf = t.astype(jnp.float32)
    tf = tf * lax.rsqrt(jnp.mean(tf * tf, axis=-1, keepdims=True) + RMS_EPS)
    return (tf * g.astype(jnp.float32)).astype(t.dtype)


def modulate(t, g, shift, scale):
    return rms_norm(t, g) * (1 + scale[:, None, :]) + shift[:, None, :]


def gla_mixer(q, k, v, log_a):
    bsz, s = q.shape[0], q.shape[1]
    nc = s // GLA_CHUNK

    def to_chunks(t):
        return t.astype(jnp.float32).reshape(bsz, nc, GLA_CHUNK, GLA_HEADS, -1).transpose(1, 0, 3, 2, 4)

    causal = jnp.tril(jnp.ones((GLA_CHUNK, GLA_CHUNK), dtype=bool))

    def step(state, inp):
        qc, kc, vc, ac = inp
        b = jnp.cumsum(ac, axis=2)
        o_inter = jnp.einsum('bhtk,bhkv->bhtv', qc * jnp.exp(b), state)
        diff = b[:, :, :, None, :] - b[:, :, None, :, :]
        decay = jnp.exp(jnp.where(causal[None, None, :, :, None], diff, -jnp.inf))
        scores = jnp.einsum('bhtk,bhsk,bhtsk->bhts', qc, kc, decay)
        o_intra = jnp.einsum('bhts,bhsv->bhtv', scores, vc)
        b_last = b[:, :, -1:, :]
        state = jnp.exp(b_last[:, :, 0, :, None]) * state + jnp.einsum(
            'bhsk,bhsv->bhkv', kc * jnp.exp(b_last - b), vc)
        return state, o_inter + o_intra

    init = jnp.zeros((bsz, GLA_HEADS, GLA_DK, GLA_DV), jnp.float32)
    xs = (to_chunks(q * GLA_DK ** -0.5), to_chunks(k), to_chunks(v), to_chunks(log_a))
    _, o = lax.scan(step, init, xs)
    return o.transpose(1, 0, 3, 2, 4).reshape(bsz, s, GLA_HEADS, GLA_DV).astype(v.dtype)


def swa_mixer(q, k, v, sinks):
    bsz, s = q.shape[0], q.shape[1]
    nb = s // SWA_BLOCK
    grp = SWA_HEADS // SWA_KV_HEADS
    qb = q.reshape(bsz, nb, SWA_BLOCK, SWA_KV_HEADS, grp, SWA_HEAD_DIM) * SWA_HEAD_DIM ** -0.5

    def with_prev(t):
        tb = t.reshape(bsz, nb, SWA_BLOCK, SWA_KV_HEADS, SWA_HEAD_DIM)
        prev = jnp.pad(tb, ((0, 0), (1, 0), (0, 0), (0, 0), (0, 0)))[:, :-1]
        return jnp.concatenate([prev, tb], axis=2)

    kb, vb = with_prev(k), with_prev(v)
    sc = jnp.einsum('bnqhgd,bnkhd->bnhgqk', qb, kb).astype(jnp.float32)
    qi = jnp.arange(SWA_BLOCK)[:, None]
    kj = jnp.arange(2 * SWA_BLOCK)[None, :]
    dist = qi + SWA_BLOCK - kj
    key_pos = jnp.arange(nb)[:, None] * SWA_BLOCK - SWA_BLOCK + kj[0][None, :]
    valid = ((dist >= 0) & (dist < SWA_WINDOW))[None] & (key_pos >= 0)[:, None, :]
    slopes = (2.0 ** (-8.0 * (jnp.arange(SWA_HEADS) + 1) / SWA_HEADS)).astype(jnp.float32)
    slopes = slopes.reshape(SWA_KV_HEADS, grp)[None, None, :, :, None, None]
    sc = sc - slopes * dist.astype(jnp.float32)
    sc = jnp.where(valid[None, :, None, None], sc, -jnp.inf)
    sink = sinks.astype(jnp.float32).reshape(SWA_KV_HEADS, grp)[None, None, :, :, None, None]
    m = jnp.maximum(jnp.max(sc, axis=-1, keepdims=True), sink)
    p = jnp.exp(sc - m)
    p = p / (jnp.sum(p, axis=-1, keepdims=True) + jnp.exp(sink - m))
    o = jnp.einsum('bnhgqk,bnkhd->bnqhgd', p.astype(vb.dtype), vb)
    return o.reshape(bsz, s, SWA_WIDTH)


NT_DIMS = (((1,), (1,)), ((), ()))
GLA_SUB = 16
GLA_BLOCK = 128
GLA_PAIR = 2 * GLA_DK


def _gla_kernel(q_ref, k_ref, v_ref, alr_ref, r_ref, wa_ref, ba_ref, g_ref, tri_ref, o_ref, st_ref, b_ref, q2_ref):
    f32, bf16 = jnp.float32, jnp.bfloat16

    @pl.when(pl.program_id(1) == 0)
    def _():
        st_ref[...] = jnp.zeros(st_ref.shape, f32)

    z = jnp.dot(alr_ref[...].astype(bf16), wa_ref[...], preferred_element_type=f32) + ba_ref[...]
    log_a = -(jnp.maximum(-z, 0.0) + jnp.log1p(jnp.exp(-jnp.abs(z)))) * (1.0 / GLA_GATE_TEMP)
    b_ref[...] = jnp.dot(tri_ref[...], log_a, preferred_element_type=f32, precision=lax.Precision.HIGHEST)
    q2_ref[...] = q_ref[...] * (GLA_DK ** -0.5)
    lane = lax.broadcasted_iota(jnp.int32, (GLA_SUB, GLA_PAIR), 1)
    sub = lax.broadcasted_iota(jnp.int32, (GLA_SUB, GLA_PAIR), 0)
    first = lane < GLA_DK
    lane_s = lax.broadcasted_iota(jnp.int32, (GLA_DV, GLA_PAIR), 1) < GLA_DK

    for hp in range(GLA_HEADS // 2):
        cols = pl.ds(hp * GLA_PAIR, GLA_PAIR)
        state = st_ref[hp]
        for c in range(GLA_BLOCK // GLA_SUB):
            rows = pl.ds(c * GLA_SUB, GLA_SUB)
            b = b_ref[rows, cols]
            b_last = b_ref[pl.ds(c * GLA_SUB + GLA_SUB - 1, 1), cols]
            q2, k2 = q2_ref[rows, cols], k_ref[rows, cols]
            qe = q2 * jnp.exp(b)
            kd = (k2 * jnp.exp(b_last - b)).astype(bf16)
            sb = state.astype(bf16)
            cols_a, cols_b = [], []
            for t in range(GLA_SUB):
                bt = b_ref[pl.ds(c * GLA_SUB + t, 1), cols]
                qt = q2_ref[pl.ds(c * GLA_SUB + t, 1), cols]
                p = (qt * k2) * jnp.exp(bt - b)
                keep = sub[:, :1] <= t
                cols_a.append(jnp.where(keep, jnp.sum(jnp.where(first, p, 0.0), axis=1, keepdims=True), 0.0))
                cols_b.append(jnp.where(keep, jnp.sum(jnp.where(first, 0.0, p), axis=1, keepdims=True), 0.0))
            ut = []
            for j, sc_cols in enumerate((cols_a, cols_b)):
                head = 2 * hp + j
                vcols = pl.ds(head * GLA_DV, GLA_DV)
                vh = v_ref[rows, vcols]
                qm = jnp.where(first if j == 0 else ~first, qe, 0.0).astype(bf16)
                o = lax.dot_general(qm, sb, NT_DIMS, preferred_element_type=f32)
                o = o + jnp.concatenate([jnp.sum(cj * vh, axis=0, keepdims=True) for cj in sc_cols], axis=0)
                ms = jnp.mean(o * o, axis=1, keepdims=True)
                r = r_ref[rows, vcols]
                o_ref[rows, vcols] = (o * lax.rsqrt(ms + RMS_EPS) * g_ref[:, vcols]) * (r * jax.nn.sigmoid(r))
                ut.append(jnp.dot(vh.T.astype(bf16), kd, preferred_element_type=f32))
            state = jnp.exp(b_last) * state + jnp.where(lane_s, ut[0], ut[1])
        st_ref[hp] = state


def _gla_tri():
    r = np.arange(GLA_BLOCK)
    return jnp.asarray(((r[:, None] // GLA_SUB == r[None, :] // GLA_SUB) & (r[None, :] <= r[:, None])).astype(np.float32))


def _gla(q_g, k_g, v_g, a_lr, r_g, w_alpha, b_alpha, norm_g):
    bsz, s, _ = q_g.shape
    rank = a_lr.shape[-1]
    alr = jnp.pad(a_lr, ((0, 0), (0, 0), (0, 128 - rank)))
    wa = jnp.pad(w_alpha, ((0, 128 - rank), (0, 0))).astype(jnp.bfloat16)
    blk = lambda w: pl.BlockSpec((None, GLA_BLOCK, w), lambda b, i: (b, i, 0))
    const = lambda shape: pl.BlockSpec(shape, lambda b, i: (0, 0))
    return pl.pallas_call(
        _gla_kernel,
        grid=(bsz, s // GLA_BLOCK),
        in_specs=[blk(GLA_QK_WIDTH), blk(GLA_QK_WIDTH), blk(GLA_WIDTH), blk(128), blk(GLA_WIDTH),
                  const((128, GLA_QK_WIDTH)), const((1, GLA_QK_WIDTH)), const((1, GLA_WIDTH)),
                  const((GLA_BLOCK, GLA_BLOCK))],
        out_specs=blk(GLA_WIDTH),
        out_shape=jax.ShapeDtypeStruct((bsz, s, GLA_WIDTH), jnp.float32),
        scratch_shapes=[pltpu.VMEM((GLA_HEADS // 2, GLA_DV, GLA_PAIR), jnp.float32),
                        pltpu.VMEM((GLA_BLOCK, GLA_QK_WIDTH), jnp.float32),
                        pltpu.VMEM((GLA_BLOCK, GLA_QK_WIDTH), jnp.float32)],
        compiler_params=pltpu.CompilerParams(dimension_semantics=("arbitrary", "arbitrary")),
        name="gla",
    )(q_g, k_g, v_g, alr, r_g, wa, b_alpha.reshape(1, -1), norm_g.reshape(1, -1), _gla_tri())


def peer_mixer(h, w_q, keys_1, keys_2, u_tab, v_tab):
    bsz, s, d = h.shape
    t = bsz * s
    hf = h.reshape(t, d)
    experts, gates = _peer_route(hf, w_q, keys_1, keys_2)
    return _peer_experts(hf, experts, gates, u_tab, v_tab).reshape(bsz, s, d)


ROUTE_TT = 256
NEG_INF = float("-inf")


def _top16_rows(s, code, big):
    vals, codes = [], []
    for _ in range(PEER_TOPK):
        m = jnp.max(s, axis=0, keepdims=True)
        c = jnp.min(jnp.where(s == m, code, big), axis=0, keepdims=True)
        s = jnp.where(code == c, NEG_INF, s)
        vals.append(m)
        codes.append(c)
    return vals, codes


def _peer_route_kernel(h_ref, wq_ref, k1_ref, k2_ref, e_ref, g_ref, q_ref):
    half = PEER_QUERY_DIM // 2
    tt = ROUTE_TT
    q_ref[...] = jnp.dot(h_ref[...].astype(jnp.bfloat16), wq_ref[...], preferred_element_type=jnp.float32)
    key_id = lax.broadcasted_iota(jnp.int32, (PEER_N_KEYS, 2 * tt), 0)
    sub8 = lax.broadcasted_iota(jnp.int32, (8, tt), 0)
    nt = (((1,), (1,)), ((), ()))

    def head_body(hd, carry):
        off = pl.multiple_of(hd * PEER_QUERY_DIM, PEER_QUERY_DIM)
        q1 = q_ref[:, pl.ds(off, half)].astype(jnp.bfloat16)
        q2 = q_ref[:, pl.ds(off + half, half)].astype(jnp.bfloat16)
        s1 = lax.dot_general(k1_ref[...], q1, nt, preferred_element_type=jnp.float32)
        s2 = lax.dot_general(k2_ref[...], q2, nt, preferred_element_type=jnp.float32)
        vals, ids = _top16_rows(jnp.concatenate([s1, s2], axis=1), key_id, PEER_N_KEYS)
        v = jnp.concatenate(vals, axis=0)
        ix = jnp.concatenate(ids, axis=0)
        v1, v2, i1, i2 = v[:, :tt], v[:, tt:], ix[:, :tt], ix[:, tt:]
        cand, code = [], []
        for a, b0 in [(0, 0), (0, 8), (1, 0), (2, 0), (3, 0), (4, 0), (5, 0), (6, 0), (7, 0)]:
            cand.append(v1[a:a + 1] + v2[b0:b0 + 8])
            flat = a * PEER_TOPK + b0 + sub8
            code.append((flat << 14) | (i1[a:a + 1] * PEER_N_KEYS + i2[b0:b0 + 8]))
        cand.append(v1[8:16] + v2[0:1])
        code.append((((8 + sub8) * PEER_TOPK) << 14) | (i1[8:16] * PEER_N_KEYS + i2[0:1]))
        sc, cc = _top16_rows(jnp.concatenate(cand, axis=0), jnp.concatenate(code, axis=0), 1 << 30)
        sc = jnp.concatenate(sc, axis=0)
        ex = jnp.exp(sc - sc[0:1])
        g_ref[hd] = ex / jnp.sum(ex, axis=0, keepdims=True)
        e_ref[hd] = jnp.concatenate(cc, axis=0) & (PEER_N_EXPERTS - 1)
        return carry

    lax.fori_loop(0, PEER_HEADS, head_body, 0)


def _peer_route(hf, w_q, keys_1, keys_2):
    t, d = hf.shape
    qw = PEER_HEADS * PEER_QUERY_DIM
    const = lambda shape: pl.BlockSpec(shape, lambda i: tuple(0 for _ in shape))
    out_spec = pl.BlockSpec((PEER_HEADS, PEER_TOPK, ROUTE_TT), lambda i: (0, 0, i))
    e_t, g_t = pl.pallas_call(
        _peer_route_kernel,
        grid=(t // ROUTE_TT,),
        in_specs=[pl.BlockSpec((ROUTE_TT, d), lambda i: (i, 0)), const((d, qw)),
                  const(keys_1.shape), const(keys_2.shape)],
        out_specs=[out_spec, out_spec],
        out_shape=[jax.ShapeDtypeStruct((PEER_HEADS, PEER_TOPK, t), jnp.int32),
                   jax.ShapeDtypeStruct((PEER_HEADS, PEER_TOPK, t), jnp.float32)],
        scratch_shapes=[pltpu.VMEM((ROUTE_TT, qw), jnp.float32)],
        compiler_params=pltpu.CompilerParams(dimension_semantics=("arbitrary",), vmem_limit_bytes=48 * 1024 * 1024),
        name="peer_route",
    )(hf, w_q.astype(jnp.bfloat16), keys_1.astype(jnp.bfloat16), keys_2.astype(jnp.bfloat16))
    to_rows = lambda a: a.reshape(PEER_HEADS * PEER_TOPK, t).T
    return to_rows(e_t), to_rows(g_t)


PEER_PAIRS = PEER_HEADS * PEER_TOPK
PEER_TB = 64
PEER_UNROLL = 32
HALF_D = D_MODEL // 2
WORD_SUB, WORD_LANE = HALF_D // 128, 128
ROW_SUB = 2 * WORD_SUB
HI_MASK = -65536


def _pack_table(tab):
    b = lax.bitcast_convert_type(tab.astype(jnp.bfloat16), jnp.uint16).astype(jnp.uint32)
    word = b[:, :HALF_D] | (b[:, HALF_D:] << 16)
    return lax.bitcast_convert_type(word, jnp.int32).reshape(tab.shape[0] * WORD_SUB, WORD_LANE)


def _unpack_row(tab_ref, e):
    w = tab_ref[pl.ds(pl.multiple_of(e, WORD_SUB), WORD_SUB), :]
    return (lax.bitcast_convert_type(w << 16, jnp.float32), lax.bitcast_convert_type(w & HI_MASK, jnp.float32))


_ORDER = (6, 2, 4, 0, 7, 3, 5, 1)


def _fold_pairs(ps, sub):
    b1 = (sub & 2) != 0
    w = []
    for i in range(2):
        x, y = ps[2 * i], ps[2 * i + 1]
        w.append(jnp.where(b1, x + pltpu.roll(x, 2, 0), y + pltpu.roll(y, 6, 0)))
    b0 = (sub & 1) != 0
    x, y = w
    return jnp.where(b0, x + pltpu.roll(x, 1, 0), y + pltpu.roll(y, 7, 0))


def _peer_u_kernel(idx_ref, h_ref, tab_ref, a_ref):
    sub = lax.broadcasted_iota(jnp.int32, (ROW_SUB, WORD_LANE), 0)
    lane = lax.broadcasted_iota(jnp.int32, (ROW_SUB, WORD_LANE), 1)

    def group_body(tg, carry):
        def tok_body(tl, acc):
            t = tg * 8 + tl
            h = h_ref[t]
            h_lo = jnp.concatenate([h[:WORD_SUB], h[:WORD_SUB]], axis=0)
            h_hi = jnp.concatenate([h[WORD_SUB:], h[WORD_SUB:]], axis=0)
            row_idx = idx_ref.at[pl.ds(t * PEER_PAIRS, PEER_PAIRS)]
            for g in range(PEER_PAIRS // 8):
                ps = []
                for i in range(4):
                    w = jnp.concatenate([tab_ref[pl.ds(pl.multiple_of(row_idx[g * 8 + 2 * i + j], WORD_SUB), WORD_SUB), :]
                                         for j in range(2)], axis=0)
                    lo = lax.bitcast_convert_type(w << 16, jnp.float32)
                    hi = lax.bitcast_convert_type(w & HI_MASK, jnp.float32)
                    ps.append(lo * h_lo + hi * h_hi)
                col = jnp.sum(_fold_pairs(ps, sub), axis=1, keepdims=True)
                acc = jnp.where(lane == tl * 16 + g, col, acc)
            return acc

        a_ref[tg] = lax.fori_loop(0, 8, tok_body, jnp.zeros((ROW_SUB, WORD_LANE), jnp.float32))
        return carry

    lax.fori_loop(0, PEER_TB // 8, group_body, 0)


def _peer_v_kernel(idx_ref, w_ref, tab_ref, o_ref):
    def tok_body(t, carry):
        def chunk_body(ci, accs):
            accs = list(accs)
            start = t * PEER_PAIRS + ci * PEER_UNROLL
            idx_c = idx_ref.at[pl.ds(start, PEER_UNROLL)]
            w_c = w_ref.at[pl.ds(start, PEER_UNROLL)]
            for k in range(PEER_UNROLL):
                lo, hi = _unpack_row(tab_ref, idx_c[k])
                w = w_c[k]
                accs[2 * (k % 2)] = accs[2 * (k % 2)] + lo * w
                accs[2 * (k % 2) + 1] = accs[2 * (k % 2) + 1] + hi * w
            return tuple(accs)

        z = jnp.zeros((WORD_SUB, WORD_LANE), jnp.float32)
        accs = lax.fori_loop(0, PEER_PAIRS // PEER_UNROLL, chunk_body, (z, z, z, z))
        o_ref[t] = jnp.concatenate([accs[0] + accs[2], accs[1] + accs[3]], axis=0)
        return carry

    lax.fori_loop(0, PEER_TB, tok_body, 0)


def _gate_kernel(g_ref, a_ref, w_ref):
    a = a_ref[...]
    w_ref[...] = g_ref[...] * (0.5 * a * (1.0 + lax.erf(a * (2.0 ** -0.5))))


_PEER_VMEM_LIMIT = 56 * 1024 * 1024


def _peer_experts(hf, experts, gates, u_tab, v_tab):
    t, d = hf.shape
    nblk = t // PEER_TB
    smem_spec = pl.BlockSpec((PEER_TB * PEER_PAIRS,), lambda i: (i,), memory_space=pltpu.SMEM)
    tab_spec = pl.BlockSpec(memory_space=pltpu.VMEM)
    tok_spec = pl.BlockSpec((PEER_TB, PEER_PAIRS), lambda i: (i, 0))
    row_spec = pl.BlockSpec((PEER_TB, ROW_SUB, WORD_LANE), lambda i: (i, 0, 0))
    params = pltpu.CompilerParams(dimension_semantics=("arbitrary",), vmem_limit_bytes=_PEER_VMEM_LIMIT)
    idx = (experts.astype(jnp.int32) * WORD_SUB).reshape(-1)
    a_tiles = pl.pallas_call(
        _peer_u_kernel,
        grid=(nblk,),
        in_specs=[smem_spec, row_spec, tab_spec],
        out_specs=pl.BlockSpec((PEER_TB // 8, ROW_SUB, WORD_LANE), lambda i: (i, 0, 0)),
        out_shape=jax.ShapeDtypeStruct((t // 8, ROW_SUB, WORD_LANE), jnp.float32),
        compiler_params=params, name="peer_u",
    )(idx, hf.reshape(t, ROW_SUB, WORD_LANE), _pack_table(u_tab))
    inv = np.argsort(np.array(_ORDER))
    a = a_tiles.reshape(t // 8, 8, 8, PEER_PAIRS // 8)[:, inv].transpose(0, 2, 3, 1).reshape(t, PEER_PAIRS)
    rb = 2048
    blk = pl.BlockSpec((rb, PEER_PAIRS), lambda i: (i, 0))
    w = pl.pallas_call(
        _gate_kernel, grid=(t // rb,), in_specs=[blk, blk], out_specs=blk,
        out_shape=jax.ShapeDtypeStruct((t, PEER_PAIRS), jnp.float32), name="peer_gate",
    )(gates, a)
    out = pl.pallas_call(
        _peer_v_kernel,
        grid=(nblk,),
        in_specs=[smem_spec, smem_spec, tab_spec],
        out_specs=row_spec,
        out_shape=jax.ShapeDtypeStruct((t, ROW_SUB, WORD_LANE), jnp.float32),
        compiler_params=params, name="peer_v",
    )(idx, w.reshape(-1), _pack_table(v_tab))
    return out.reshape(t, d)


def _final_norm_kernel(x_ref, g_ref, o_ref):
    xf = x_ref[...]
    ms = jnp.mean(xf * xf, axis=-1, keepdims=True)
    o_ref[...] = xf * lax.rsqrt(ms + RMS_EPS) * g_ref[...]


def _final_norm(x, g):
    b, s, d = x.shape
    xf = x.reshape(b * s, d)
    tm = 1024
    out = pl.pallas_call(
        _final_norm_kernel,
        grid=(b * s // tm,),
        in_specs=[pl.BlockSpec((tm, d), lambda i: (i, 0)), pl.BlockSpec((1, d), lambda i: (0, 0))],
        out_specs=pl.BlockSpec((tm, d), lambda i: (i, 0)),
        out_shape=jax.ShapeDtypeStruct((b * s, d), x.dtype),
    )(xf, g.reshape(1, d))
    return out.reshape(b, s, d)


def kernel(x, c, w_ada, b_ada, norm1_g, norm2_g, w_in, w_gla_alpha, b_gla_alpha, gla_norm_g,
           swa_sinks, swa_norm_g, w_out, w_peer_q, peer_keys_1, peer_keys_2, peer_u, peer_v, final_g):
    bsz, s, _ = x.shape
    widths = (GLA_QK_WIDTH, GLA_QK_WIDTH, GLA_WIDTH, GLA_GATE_RANK, GLA_WIDTH, SWA_WIDTH, SWA_KV_WIDTH, SWA_KV_WIDTH)
    split_at = [int(v) for v in np.cumsum(widths)[:-1]]
    c_act = jax.nn.silu(c)
    for l in range(DEPTH):
        mod = c_act @ w_ada[l] + b_ada[l]
        shift1, scale1, gate1, shift2, scale2, gate2 = jnp.split(mod, 6, axis=-1)
        h = modulate(x, norm1_g[l], shift1, scale1)
        proj = h @ w_in[l]
        q_g, k_g, v_g, a_lr, r_g, q_s, k_s, v_s = jnp.split(proj, split_at, axis=-1)
        o_gla = _gla(q_g, k_g, v_g, a_lr, r_g, w_gla_alpha[l], b_gla_alpha[l], gla_norm_g[l])
        o_swa = swa_mixer(q_s.reshape(bsz, s, SWA_HEADS, SWA_HEAD_DIM),
                          k_s.reshape(bsz, s, SWA_KV_HEADS, SWA_HEAD_DIM),
                          v_s.reshape(bsz, s, SWA_KV_HEADS, SWA_HEAD_DIM),
                          swa_sinks[l])
        o_swa = rms_norm(o_swa, swa_norm_g[l])
        y = jnp.concatenate([o_gla, o_swa], axis=-1) @ w_out[l]
        x = x + gate1[:, None, :] * y
        h2 = modulate(x, norm2_g[l], shift2, scale2)
        x = x + gate2[:, None, :] * peer_mixer(h2, w_peer_q[l], peer_keys_1[l], peer_keys_2[l], peer_u[l], peer_v[l])
    return _final_norm(x, final_g)
```

```python
import jax, jax.numpy as jnp
from jax import lax
import numpy as np
from jax.experimental import pallas as pl
from jax.experimental.pallas import tpu as pltpu

D_MODEL = 1024
BATCH = 4
SEQ = 8192
DEPTH = 1

GLA_HEADS = 4
GLA_DK = 64
GLA_DV = 128
GLA_GATE_RANK = 16
GLA_GATE_TEMP = 16.0
GLA_CHUNK = 64
SWA_HEADS = 8
SWA_KV_HEADS = 2
SWA_HEAD_DIM = 64
SWA_WINDOW = 128
SWA_BLOCK = 128
PEER_HEADS = 8
PEER_N_KEYS = 128
PEER_N_EXPERTS = PEER_N_KEYS * PEER_N_KEYS
PEER_QUERY_DIM = 256
PEER_TOPK = 16
RMS_EPS = 1e-6

GLA_QK_WIDTH = GLA_HEADS * GLA_DK
GLA_WIDTH = GLA_HEADS * GLA_DV
SWA_WIDTH = SWA_HEADS * SWA_HEAD_DIM
SWA_KV_WIDTH = SWA_KV_HEADS * SWA_HEAD_DIM
MIX_WIDTH = GLA_WIDTH + SWA_WIDTH
IN_WIDTH = 2 * GLA_QK_WIDTH + GLA_WIDTH + GLA_GATE_RANK + GLA_WIDTH + SWA_WIDTH + 2 * SWA_KV_WIDTH


LANES = 128
NEG_INF = float("-inf")
NT_DIMS = (((1,), (1,)), ((), ()))
ROW_TILE = 512
WIDE = 2 * SWA_HEAD_DIM
SWA_QW = SWA_HEADS * WIDE
ALR_W = LANES
COL_QS, COL_QG, COL_KG, COL_VG, COL_RG = 0, SWA_QW, SWA_QW + GLA_QK_WIDTH, SWA_QW + 2 * GLA_QK_WIDTH, SWA_QW + 2 * GLA_QK_WIDTH + GLA_WIDTH
COL_KS = COL_RG + GLA_WIDTH
COL_VS = COL_KS + SWA_KV_WIDTH
COL_ALR = COL_VS + SWA_KV_WIDTH
PROJ_W = COL_ALR + ALR_W
MIX_W = GLA_WIDTH + SWA_QW


def _wide_cols():
    grp = SWA_HEADS // SWA_KV_HEADS
    h = np.arange(SWA_WIDTH) // SWA_HEAD_DIM
    return h * WIDE + (h // grp) * SWA_HEAD_DIM + np.arange(SWA_WIDTH) % SWA_HEAD_DIM


def _rms(x):
    return x * lax.rsqrt(jnp.mean(x * x, axis=-1, keepdims=True) + RMS_EPS)


def _ada_kernel(c_ref, w_ref, b_ref, o_ref):
    c = c_ref[...]
    ca = (c * jax.nn.sigmoid(c)).astype(jnp.bfloat16)
    o_ref[...] = jnp.dot(ca, w_ref[...].astype(jnp.bfloat16), preferred_element_type=jnp.float32) + b_ref[...]


def _ada(c, w, b):
    bsz, d = c.shape
    n = w.shape[1]
    tn = 1024
    return pl.pallas_call(
        _ada_kernel,
        grid=(n // tn,),
        in_specs=[pl.BlockSpec((bsz, d), lambda j: (0, 0)), pl.BlockSpec((d, tn), lambda j: (0, j)),
                  pl.BlockSpec((1, tn), lambda j: (0, j))],
        out_specs=pl.BlockSpec((bsz, tn), lambda j: (0, j)),
        out_shape=jax.ShapeDtypeStruct((bsz, n), jnp.float32), name="ada",
    )(c, w, b.reshape(1, n))


def _in_proj_kernel(x_ref, g_ref, sh_ref, sc_ref, w_ref, o_ref):
    h = _rms(x_ref[...]) * g_ref[...] * (1.0 + sc_ref[...]) + sh_ref[...]
    o_ref[...] = jnp.dot(h.astype(jnp.bfloat16), w_ref[...], preferred_element_type=jnp.float32)


def _in_proj(x, g, shift, scale, w):
    bsz, s, d = x.shape
    n = w.shape[1]
    row = lambda width: pl.BlockSpec((None, ROW_TILE, width), lambda b, i: (b, i, 0))
    per_b = pl.BlockSpec((None, 1, d), lambda b, i: (b, 0, 0))
    return pl.pallas_call(
        _in_proj_kernel,
        grid=(bsz, s // ROW_TILE),
        in_specs=[row(d), pl.BlockSpec((1, d), lambda b, i: (0, 0)), per_b, per_b,
                  pl.BlockSpec((d, n), lambda b, i: (0, 0))],
        out_specs=row(n),
        out_shape=jax.ShapeDtypeStruct((bsz, s, n), jnp.float32),
        compiler_params=pltpu.CompilerParams(dimension_semantics=("arbitrary", "arbitrary"),
                                             vmem_limit_bytes=48 * 1024 * 1024),
        name="in_proj",
    )(x, g.reshape(1, d), shift, scale, w)


def _swa_kernel(sink_ref, q_ref, kp_ref, kc_ref, vp_ref, vc_ref, g_ref, o_ref):
    f32, bf16 = jnp.float32, jnp.bfloat16
    blk = SWA_BLOCK
    k2 = jnp.concatenate([kp_ref[...], kc_ref[...]], axis=0).astype(bf16)
    v2 = jnp.concatenate([vp_ref[...], vc_ref[...]], axis=0).astype(bf16)
    qi = lax.broadcasted_iota(jnp.int32, (blk, 2 * blk), 0)
    kj = lax.broadcasted_iota(jnp.int32, (blk, 2 * blk), 1)
    dist = qi + blk - kj
    valid = (dist >= 0) & (dist < SWA_WINDOW) & ((kj >= blk) | (pl.program_id(1) > 0))
    distf = dist.astype(f32)
    lane_kv = lax.broadcasted_iota(jnp.int32, (blk, WIDE), 1) // SWA_HEAD_DIM
    grp = SWA_HEADS // SWA_KV_HEADS
    outs, ssq = [], jnp.zeros((blk, 1), f32)
    for h in range(SWA_HEADS):
        q = (q_ref[:, h * WIDE:(h + 1) * WIDE] * (SWA_HEAD_DIM ** -0.5)).astype(bf16)
        sc = lax.dot_general(q, k2, NT_DIMS, preferred_element_type=f32)
        sc = sc - (2.0 ** (-8.0 * (h + 1) / SWA_HEADS)) * distf
        sc = jnp.where(valid, sc, NEG_INF)
        sink = sink_ref[h]
        m = jnp.maximum(jnp.max(sc, axis=-1, keepdims=True), sink)
        p = jnp.exp(sc - m)
        p = p / (jnp.sum(p, axis=-1, keepdims=True) + jnp.exp(sink - m))
        o = jnp.dot(p.astype(bf16), v2, preferred_element_type=f32)
        o = jnp.where(lane_kv == h // grp, o, 0.0)
        ssq = ssq + jnp.sum(o * o, axis=-1, keepdims=True)
        outs.append(o)
    inv = lax.rsqrt(ssq * (1.0 / SWA_WIDTH) + RMS_EPS)
    for h in range(SWA_HEADS):
        o_ref[:, h * WIDE:(h + 1) * WIDE] = outs[h] * inv * g_ref[:, h * WIDE:(h + 1) * WIDE]


def _swa(proj, sinks, norm_g_wide):
    bsz, s, _ = proj.shape
    col = lambda width, off, prev: pl.BlockSpec(
        (None, SWA_BLOCK, width), lambda b, n: (b, jnp.maximum(n - 1, 0) if prev else n, off // width))
    return pl.pallas_call(
        _swa_kernel,
        grid=(bsz, s // SWA_BLOCK),
        in_specs=[pl.BlockSpec(memory_space=pltpu.SMEM), col(SWA_QW, COL_QS, False),
                  col(SWA_KV_WIDTH, COL_KS, True), col(SWA_KV_WIDTH, COL_KS, False),
                  col(SWA_KV_WIDTH, COL_VS, True), col(SWA_KV_WIDTH, COL_VS, False),
                  pl.BlockSpec((1, SWA_QW), lambda b, n: (0, 0))],
        out_specs=pl.BlockSpec((None, SWA_BLOCK, SWA_QW), lambda b, n: (b, n, 0)),
        out_shape=jax.ShapeDtypeStruct((bsz, s, SWA_QW), jnp.float32),
        compiler_params=pltpu.CompilerParams(dimension_semantics=("arbitrary", "arbitrary")),
        name="swa",
    )(sinks, proj, proj, proj, proj, proj, norm_g_wide)


def _out_proj_kernel(x_ref, og_ref, os_ref, w_ref, gate_ref, g2_ref, sh_ref, sc_ref, x1_ref, h2_ref):
    mix = jnp.concatenate([og_ref[...], os_ref[...]], axis=-1).astype(jnp.bfloat16)
    x1 = x_ref[...] + gate_ref[...] * jnp.dot(mix, w_ref[...], preferred_element_type=jnp.float32)
    x1_ref[...] = x1
    h2_ref[...] = _rms(x1) * g2_ref[...] * (1.0 + sc_ref[...]) + sh_ref[...]


def _out_proj(x, o_gla, o_swa, w, gate1, g2, shift2, scale2):
    bsz, s, d = x.shape
    row = lambda width: pl.BlockSpec((None, ROW_TILE, width), lambda b, i: (b, i, 0))
    per_b = pl.BlockSpec((None, 1, d), lambda b, i: (b, 0, 0))
    shp = jax.ShapeDtypeStruct((bsz, s, d), jnp.float32)
    return pl.pallas_call(
        _out_proj_kernel,
        grid=(bsz, s // ROW_TILE),
        in_specs=[row(d), row(GLA_WIDTH), row(SWA_QW), pl.BlockSpec((MIX_W, d), lambda b, i: (0, 0)),
                  per_b, pl.BlockSpec((1, d), lambda b, i: (0, 0)), per_b, per_b],
        out_specs=[row(d), row(d)],
        out_shape=[shp, shp],
        compiler_params=pltpu.CompilerParams(dimension_semantics=("arbitrary", "arbitrary"),
                                             vmem_limit_bytes=48 * 1024 * 1024),
        name="out_proj",
    )(x, o_gla, o_swa, w, gate1, g2.reshape(1, d), shift2, scale2)


GLA_SUB = 16
GLA_BLOCK = 128
GLA_PAIR = 2 * GLA_DK


def _gla_kernel(q_ref, k_ref, v_ref, alr_ref, r_ref, wa_ref, ba_ref, g_ref, tri_ref, o_ref, st_ref, b_ref, q2_ref):
    f32, bf16 = jnp.float32, jnp.bfloat16

    @pl.when(pl.program_id(1) == 0)
    def _():
        st_ref[...] = jnp.zeros(st_ref.shape, f32)

    z = jnp.dot(alr_ref[...].astype(bf16), wa_ref[...], preferred_element_type=f32) + ba_ref[...]
    log_a = -(jnp.maximum(-z, 0.0) + jnp.log1p(jnp.exp(-jnp.abs(z)))) * (1.0 / GLA_GATE_TEMP)
    b_ref[...] = jnp.dot(tri_ref[...], log_a, preferred_element_type=f32, precision=lax.Precision.HIGHEST)
    q2_ref[...] = q_ref[...] * (GLA_DK ** -0.5)
    lane = lax.broadcasted_iota(jnp.int32, (GLA_SUB, GLA_PAIR), 1)
    sub = lax.broadcasted_iota(jnp.int32, (GLA_SUB, GLA_PAIR), 0)
    first = lane < GLA_DK
    lane_s = lax.broadcasted_iota(jnp.int32, (GLA_DV, GLA_PAIR), 1) < GLA_DK

    for hp in range(GLA_HEADS // 2):
        cols = pl.ds(hp * GLA_PAIR, GLA_PAIR)
        state = st_ref[hp]
        for c in range(GLA_BLOCK // GLA_SUB):
            rows = pl.ds(c * GLA_SUB, GLA_SUB)
            b = b_ref[rows, cols]
            b_last = b_ref[pl.ds(c * GLA_SUB + GLA_SUB - 1, 1), cols]
            q2, k2 = q2_ref[rows, cols], k_ref[rows, cols]
            qe = q2 * jnp.exp(b)
            kd = (k2 * jnp.exp(b_last - b)).astype(bf16)
            sb = state.astype(bf16)
            cols_a, cols_b = [], []
            for t in range(GLA_SUB):
                bt = b_ref[pl.ds(c * GLA_SUB + t, 1), cols]
                qt = q2_ref[pl.ds(c * GLA_SUB + t, 1), cols]
                p = (qt * k2) * jnp.exp(bt - b)
                keep = sub[:, :1] <= t
                cols_a.append(jnp.where(keep, jnp.sum(jnp.where(first, p, 0.0), axis=1, keepdims=True), 0.0))
                cols_b.append(jnp.where(keep, jnp.sum(jnp.where(first, 0.0, p), axis=1, keepdims=True), 0.0))
            ut = []
            for j, sc_cols in enumerate((cols_a, cols_b)):
                head = 2 * hp + j
                vcols = pl.ds(head * GLA_DV, GLA_DV)
                vh = v_ref[rows, vcols]
                qm = jnp.where(first if j == 0 else ~first, qe, 0.0).astype(bf16)
                o = lax.dot_general(qm, sb, NT_DIMS, preferred_element_type=f32)
                o = o + jnp.concatenate([jnp.sum(cj * vh, axis=0, keepdims=True) for cj in sc_cols], axis=0)
                ms = jnp.mean(o * o, axis=1, keepdims=True)
                r = r_ref[rows, vcols]
                o_ref[rows, vcols] = (o * lax.rsqrt(ms + RMS_EPS) * g_ref[:, vcols]) * (r * jax.nn.sigmoid(r))
                ut.append(jnp.dot(vh.T.astype(bf16), kd, preferred_element_type=f32))
            state = jnp.exp(b_last) * state + jnp.where(lane_s, ut[0], ut[1])
        st_ref[hp] = state


def _gla_tri():
    r = np.arange(GLA_BLOCK)
    return jnp.asarray(((r[:, None] // GLA_SUB == r[None, :] // GLA_SUB) & (r[None, :] <= r[:, None])).astype(np.float32))


def _gla(proj, w_alpha, b_alpha, norm_g):
    bsz, s, _ = proj.shape
    rank = w_alpha.shape[0]
    wa = jnp.pad(w_alpha, ((0, ALR_W - rank), (0, 0))).astype(jnp.bfloat16)
    col = lambda width, off: pl.BlockSpec((None, GLA_BLOCK, width), lambda b, i: (b, i, off // width))
    const = lambda shape: pl.BlockSpec(shape, lambda b, i: (0, 0))
    return pl.pallas_call(
        _gla_kernel,
        grid=(bsz, s // GLA_BLOCK),
        in_specs=[col(GLA_QK_WIDTH, COL_QG), col(GLA_QK_WIDTH, COL_KG), col(GLA_WIDTH, COL_VG), col(ALR_W, COL_ALR),
                  col(GLA_WIDTH, COL_RG), const((ALR_W, GLA_QK_WIDTH)), const((1, GLA_QK_WIDTH)),
                  const((1, GLA_WIDTH)), const((GLA_BLOCK, GLA_BLOCK))],
        out_specs=pl.BlockSpec((None, GLA_BLOCK, GLA_WIDTH), lambda b, i: (b, i, 0)),
        out_shape=jax.ShapeDtypeStruct((bsz, s, GLA_WIDTH), jnp.float32),
        scratch_shapes=[pltpu.VMEM((GLA_HEADS // 2, GLA_DV, GLA_PAIR), jnp.float32),
                        pltpu.VMEM((GLA_BLOCK, GLA_QK_WIDTH), jnp.float32),
                        pltpu.VMEM((GLA_BLOCK, GLA_QK_WIDTH), jnp.float32)],
        compiler_params=pltpu.CompilerParams(dimension_semantics=("arbitrary", "arbitrary")),
        name="gla",
    )(proj, proj, proj, proj, proj, wa, b_alpha.reshape(1, -1), norm_g.reshape(1, -1), _gla_tri())


def peer_mixer(h, w_q, keys_1, keys_2, u_tab, v_tab):
    bsz, s, d = h.shape
    t = bsz * s
    hf = h.reshape(t, d)
    experts, gates = _peer_route(hf, w_q, keys_1, keys_2)
    return _peer_experts(hf, experts, gates, u_tab, v_tab).reshape(bsz, s, d)


ROUTE_TT = 256


def _top16_rows(s, code, big):
    vals, codes = [], []
    for _ in range(PEER_TOPK):
        m = jnp.max(s, axis=0, keepdims=True)
        c = jnp.min(jnp.where(s == m, code, big), axis=0, keepdims=True)
        s = jnp.where(code == c, NEG_INF, s)
        vals.append(m)
        codes.append(c)
    return vals, codes


def _peer_route_kernel(h_ref, wq_ref, k1_ref, k2_ref, e_ref, g_ref, q_ref):
    half = PEER_QUERY_DIM // 2
    tt = ROUTE_TT
    q_ref[...] = jnp.dot(h_ref[...].astype(jnp.bfloat16), wq_ref[...], preferred_element_type=jnp.float32)
    key_id = lax.broadcasted_iota(jnp.int32, (PEER_N_KEYS, 2 * tt), 0)
    sub8 = lax.broadcasted_iota(jnp.int32, (8, tt), 0)
    nt = (((1,), (1,)), ((), ()))

    def head_body(hd, carry):
        off = pl.multiple_of(hd * PEER_QUERY_DIM, PEER_QUERY_DIM)
        q1 = q_ref[:, pl.ds(off, half)].astype(jnp.bfloat16)
        q2 = q_ref[:, pl.ds(off + half, half)].astype(jnp.bfloat16)
        s1 = lax.dot_general(k1_ref[...], q1, nt, preferred_element_type=jnp.float32)
        s2 = lax.dot_general(k2_ref[...], q2, nt, preferred_element_type=jnp.float32)
        vals, ids = _top16_rows(jnp.concatenate([s1, s2], axis=1), key_id, PEER_N_KEYS)
        v = jnp.concatenate(vals, axis=0)
        ix = jnp.concatenate(ids, axis=0)
        v1, v2, i1, i2 = v[:, :tt], v[:, tt:], ix[:, :tt], ix[:, tt:]
        cand, code = [], []
        for a, b0 in [(0, 0), (0, 8), (1, 0), (2, 0), (3, 0), (4, 0), (5, 0), (6, 0), (7, 0)]:
            cand.append(v1[a:a + 1] + v2[b0:b0 + 8])
            flat = a * PEER_TOPK + b0 + sub8
            code.append((flat << 14) | (i1[a:a + 1] * PEER_N_KEYS + i2[b0:b0 + 8]))
        cand.append(v1[8:16] + v2[0:1])
        code.append((((8 + sub8) * PEER_TOPK) << 14) | (i1[8:16] * PEER_N_KEYS + i2[0:1]))
        sc, cc = _top16_rows(jnp.concatenate(cand, axis=0), jnp.concatenate(code, axis=0), 1 << 30)
        sc = jnp.concatenate(sc, axis=0)
        ex = jnp.exp(sc - sc[0:1])
        g_ref[hd] = ex / jnp.sum(ex, axis=0, keepdims=True)
        e_ref[hd] = jnp.concatenate(cc, axis=0) & (PEER_N_EXPERTS - 1)
        return carry

    lax.fori_loop(0, PEER_HEADS, head_body, 0)


def _peer_route(hf, w_q, keys_1, keys_2):
    t, d = hf.shape
    qw = PEER_HEADS * PEER_QUERY_DIM
    const = lambda shape: pl.BlockSpec(shape, lambda i: tuple(0 for _ in shape))
    out_spec = pl.BlockSpec((PEER_HEADS, PEER_TOPK, ROUTE_TT), lambda i: (0, 0, i))
    e_t, g_t = pl.pallas_call(
        _peer_route_kernel,
        grid=(t // ROUTE_TT,),
        in_specs=[pl.BlockSpec((ROUTE_TT, d), lambda i: (i, 0)), const((d, qw)),
                  const(keys_1.shape), const(keys_2.shape)],
        out_specs=[out_spec, out_spec],
        out_shape=[jax.ShapeDtypeStruct((PEER_HEADS, PEER_TOPK, t), jnp.int32),
                   jax.ShapeDtypeStruct((PEER_HEADS, PEER_TOPK, t), jnp.float32)],
        scratch_shapes=[pltpu.VMEM((ROUTE_TT, qw), jnp.float32)],
        compiler_params=pltpu.CompilerParams(dimension_semantics=("arbitrary",), vmem_limit_bytes=48 * 1024 * 1024),
        name="peer_route",
    )(hf, w_q.astype(jnp.bfloat16), keys_1.astype(jnp.bfloat16), keys_2.astype(jnp.bfloat16))
    to_rows = lambda a: a.reshape(PEER_HEADS * PEER_TOPK, t).T
    return to_rows(e_t), to_rows(g_t)


PEER_PAIRS = PEER_HEADS * PEER_TOPK
PEER_TB = 64
PEER_UNROLL = 32
HALF_D = D_MODEL // 2
WORD_SUB, WORD_LANE = HALF_D // 128, 128
ROW_SUB = 2 * WORD_SUB
HI_MASK = -65536


def _pack_table(tab):
    b = lax.bitcast_convert_type(tab.astype(jnp.bfloat16), jnp.uint16).astype(jnp.uint32)
    word = b[:, :HALF_D] | (b[:, HALF_D:] << 16)
    return lax.bitcast_convert_type(word, jnp.int32).reshape(tab.shape[0] * WORD_SUB, WORD_LANE)


def _unpack_row(tab_ref, e):
    w = tab_ref[pl.ds(pl.multiple_of(e, WORD_SUB), WORD_SUB), :]
    return (lax.bitcast_convert_type(w << 16, jnp.float32), lax.bitcast_convert_type(w & HI_MASK, jnp.float32))


_ORDER = (6, 2, 4, 0, 7, 3, 5, 1)


def _fold_pairs(ps, sub):
    b1 = (sub & 2) != 0
    w = []
    for i in range(2):
        x, y = ps[2 * i], ps[2 * i + 1]
        w.append(jnp.where(b1, x + pltpu.roll(x, 2, 0), y + pltpu.roll(y, 6, 0)))
    b0 = (sub & 1) != 0
    x, y = w
    return jnp.where(b0, x + pltpu.roll(x, 1, 0), y + pltpu.roll(y, 7, 0))


def _peer_u_kernel(idx_ref, h_ref, tab_ref, a_ref):
    sub = lax.broadcasted_iota(jnp.int32, (ROW_SUB, WORD_LANE), 0)
    lane = lax.broadcasted_iota(jnp.int32, (ROW_SUB, WORD_LANE), 1)

    def group_body(tg, carry):
        def tok_body(tl, acc):
            t = tg * 8 + tl
            h = h_ref[t]
            h_lo = jnp.concatenate([h[:WORD_SUB], h[:WORD_SUB]], axis=0)
            h_hi = jnp.concatenate([h[WORD_SUB:], h[WORD_SUB:]], axis=0)
            row_idx = idx_ref.at[pl.ds(t * PEER_PAIRS, PEER_PAIRS)]
            for g in range(PEER_PAIRS // 8):
                ps = []
                for i in range(4):
                    w = jnp.concatenate([tab_ref[pl.ds(pl.multiple_of(row_idx[g * 8 + 2 * i + j], WORD_SUB), WORD_SUB), :]
                                         for j in range(2)], axis=0)
                    lo = lax.bitcast_convert_type(w << 16, jnp.float32)
                    hi = lax.bitcast_convert_type(w & HI_MASK, jnp.float32)
                    ps.append(lo * h_lo + hi * h_hi)
                col = jnp.sum(_fold_pairs(ps, sub), axis=1, keepdims=True)
                acc = jnp.where(lane == tl * 16 + g, col, acc)
            return acc

        a_ref[tg] = lax.fori_loop(0, 8, tok_body, jnp.zeros((ROW_SUB, WORD_LANE), jnp.float32))
        return carry

    lax.fori_loop(0, PEER_TB // 8, group_body, 0)


def _peer_v_kernel(idx_ref, w_ref, tab_ref, o_ref):
    def tok_body(t, carry):
        def chunk_body(ci, accs):
            accs = list(accs)
            start = t * PEER_PAIRS + ci * PEER_UNROLL
            idx_c = idx_ref.at[pl.ds(start, PEER_UNROLL)]
            w_c = w_ref.at[pl.ds(start, PEER_UNROLL)]
            for k in range(PEER_UNROLL):
                lo, hi = _unpack_row(tab_ref, idx_c[k])
                w = w_c[k]
                accs[2 * (k % 2)] = accs[2 * (k % 2)] + lo * w
                accs[2 * (k % 2) + 1] = accs[2 * (k % 2) + 1] + hi * w
            return tuple(accs)

        z = jnp.zeros((WORD_SUB, WORD_LANE), jnp.float32)
        accs = lax.fori_loop(0, PEER_PAIRS // PEER_UNROLL, chunk_body, (z, z, z, z))
        o_ref[t] = jnp.concatenate([accs[0] + accs[2], accs[1] + accs[3]], axis=0)
        return carry

    lax.fori_loop(0, PEER_TB, tok_body, 0)


def _gate_kernel(g_ref, a_ref, w_ref):
    a = a_ref[...]
    w_ref[...] = g_ref[...] * (0.5 * a * (1.0 + lax.erf(a * (2.0 ** -0.5))))


_PEER_VMEM_LIMIT = 56 * 1024 * 1024


def _peer_experts(hf, experts, gates, u_tab, v_tab):
    t, d = hf.shape
    nblk = t // PEER_TB
    smem_spec = pl.BlockSpec((PEER_TB * PEER_PAIRS,), lambda i: (i,), memory_space=pltpu.SMEM)
    tab_spec = pl.BlockSpec(memory_space=pltpu.VMEM)
    tok_spec = pl.BlockSpec((PEER_TB, PEER_PAIRS), lambda i: (i, 0))
    row_spec = pl.BlockSpec((PEER_TB, ROW_SUB, WORD_LANE), lambda i: (i, 0, 0))
    params = pltpu.CompilerParams(dimension_semantics=("arbitrary",), vmem_limit_bytes=_PEER_VMEM_LIMIT)
    idx = (experts.astype(jnp.int32) * WORD_SUB).reshape(-1)
    a_tiles = pl.pallas_call(
        _peer_u_kernel,
        grid=(nblk,),
        in_specs=[smem_spec, row_spec, tab_spec],
        out_specs=pl.BlockSpec((PEER_TB // 8, ROW_SUB, WORD_LANE), lambda i: (i, 0, 0)),
        out_shape=jax.ShapeDtypeStruct((t // 8, ROW_SUB, WORD_LANE), jnp.float32),
        compiler_params=params, name="peer_u",
    )(idx, hf.reshape(t, ROW_SUB, WORD_LANE), _pack_table(u_tab))
    inv = np.argsort(np.array(_ORDER))
    a = a_tiles.reshape(t // 8, 8, 8, PEER_PAIRS // 8)[:, inv].transpose(0, 2, 3, 1).reshape(t, PEER_PAIRS)
    rb = 2048
    blk = pl.BlockSpec((rb, PEER_PAIRS), lambda i: (i, 0))
    w = pl.pallas_call(
        _gate_kernel, grid=(t // rb,), in_specs=[blk, blk], out_specs=blk,
        out_shape=jax.ShapeDtypeStruct((t, PEER_PAIRS), jnp.float32), name="peer_gate",
    )(gates, a)
    out = pl.pallas_call(
        _peer_v_kernel,
        grid=(nblk,),
        in_specs=[smem_spec, smem_spec, tab_spec],
        out_specs=row_spec,
        out_shape=jax.ShapeDtypeStruct((t, ROW_SUB, WORD_LANE), jnp.float32),
        compiler_params=params, name="peer_v",
    )(idx, w.reshape(-1), _pack_table(v_tab))
    return out.reshape(t, d)


def _final_kernel(x_ref, p_ref, gate_ref, g_ref, o_ref):
    o_ref[...] = _rms(x_ref[...] + gate_ref[...] * p_ref[...]) * g_ref[...]


def _final(x1, peer, gate2, g):
    bsz, s, d = x1.shape
    row = pl.BlockSpec((None, ROW_TILE, d), lambda b, i: (b, i, 0))
    return pl.pallas_call(
        _final_kernel,
        grid=(bsz, s // ROW_TILE),
        in_specs=[row, row, pl.BlockSpec((None, 1, d), lambda b, i: (b, 0, 0)), pl.BlockSpec((1, d), lambda b, i: (0, 0))],
        out_specs=row,
        out_shape=jax.ShapeDtypeStruct((bsz, s, d), jnp.float32), name="final",
    )(x1, peer, gate2, g.reshape(1, d))


def _in_proj_weights(w_in):
    o = np.cumsum([0, GLA_QK_WIDTH, GLA_QK_WIDTH, GLA_WIDTH, GLA_GATE_RANK, GLA_WIDTH, SWA_WIDTH, SWA_KV_WIDTH])
    src = np.full((PROJ_W,), IN_WIDTH, np.int32)
    src[COL_QS + _wide_cols()] = o[5] + np.arange(SWA_WIDTH)
    for dst, start, width in ((COL_QG, o[0], GLA_QK_WIDTH), (COL_KG, o[1], GLA_QK_WIDTH), (COL_VG, o[2], GLA_WIDTH),
                              (COL_ALR, o[3], GLA_GATE_RANK), (COL_RG, o[4], GLA_WIDTH),
                              (COL_KS, o[6], SWA_KV_WIDTH), (COL_VS, o[7], SWA_KV_WIDTH)):
        src[dst:dst + width] = start + np.arange(width)
    w_ext = jnp.concatenate([w_in, jnp.zeros((w_in.shape[0], 1), w_in.dtype)], axis=1)
    return w_ext[:, src].astype(jnp.bfloat16)


def _out_proj_weights(w_out):
    src = np.full((MIX_W,), MIX_WIDTH, np.int32)
    src[:GLA_WIDTH] = np.arange(GLA_WIDTH)
    src[GLA_WIDTH + _wide_cols()] = GLA_WIDTH + np.arange(SWA_WIDTH)
    w_ext = jnp.concatenate([w_out, jnp.zeros((1, w_out.shape[1]), w_out.dtype)], axis=0)
    return w_ext[src].astype(jnp.bfloat16)


def kernel(x, c, w_ada, b_ada, norm1_g, norm2_g, w_in, w_gla_alpha, b_gla_alpha, gla_norm_g,
           swa_sinks, swa_norm_g, w_out, w_peer_q, peer_keys_1, peer_keys_2, peer_u, peer_v, final_g):
    assert w_ada.shape[0] == 1, "single-layer block"
    l = 0
    mod = _ada(c, w_ada[l], b_ada[l])
    shift1, scale1, gate1, shift2, scale2, gate2 = [m[:, None, :] for m in jnp.split(mod, 6, axis=-1)]
    proj = _in_proj(x, norm1_g[l], shift1, scale1, _in_proj_weights(w_in[l]))
    o_gla = _gla(proj, w_gla_alpha[l], b_gla_alpha[l], gla_norm_g[l])
    g_wide = jnp.zeros((1, SWA_QW), jnp.float32).at[0, _wide_cols()].set(swa_norm_g[l])
    o_swa = _swa(proj, swa_sinks[l], g_wide)
    x1, h2 = _out_proj(x, o_gla, o_swa, _out_proj_weights(w_out[l]), gate1, norm2_g[l], shift2, scale2)
    peer = peer_mixer(h2, w_peer_q[l], peer_keys_1[l], peer_keys_2[l], peer_u[l], peer_v[l])
    return _final(x1, peer, gate2, final_g)
```

```python
import jax, jax.numpy as jnp
from jax import lax
import numpy as np
from jax.experimental import pallas as pl
from jax.experimental.pallas import tpu as pltpu

D_MODEL = 1024
BATCH = 4
SEQ = 8192
DEPTH = 1

GLA_HEADS = 4
GLA_DK = 64
GLA_DV = 128
GLA_GATE_RANK = 16
GLA_GATE_TEMP = 16.0
GLA_CHUNK = 64
SWA_HEADS = 8
SWA_KV_HEADS = 2
SWA_HEAD_DIM = 64
SWA_WINDOW = 128
SWA_BLOCK = 128
PEER_HEADS = 8
PEER_N_KEYS = 128
PEER_N_EXPERTS = PEER_N_KEYS * PEER_N_KEYS
PEER_QUERY_DIM = 256
PEER_TOPK = 16
RMS_EPS = 1e-6

GLA_QK_WIDTH = GLA_HEADS * GLA_DK
GLA_WIDTH = GLA_HEADS * GLA_DV
SWA_WIDTH = SWA_HEADS * SWA_HEAD_DIM
SWA_KV_WIDTH = SWA_KV_HEADS * SWA_HEAD_DIM
MIX_WIDTH = GLA_WIDTH + SWA_WIDTH
IN_WIDTH = 2 * GLA_QK_WIDTH + GLA_WIDTH + GLA_GATE_RANK + GLA_WIDTH + SWA_WIDTH + 2 * SWA_KV_WIDTH


LANES = 128
NEG_INF = float("-inf")
NT_DIMS = (((1,), (1,)), ((), ()))
ROW_TILE = 512
WIDE = 2 * SWA_HEAD_DIM
SWA_QW = SWA_HEADS * WIDE
ALR_W = LANES
COL_QS, COL_QG, COL_KG, COL_VG, COL_RG = 0, SWA_QW, SWA_QW + GLA_QK_WIDTH, SWA_QW + 2 * GLA_QK_WIDTH, SWA_QW + 2 * GLA_QK_WIDTH + GLA_WIDTH
COL_KS = COL_RG + GLA_WIDTH
COL_VS = COL_KS + SWA_KV_WIDTH
COL_ALR = COL_VS + SWA_KV_WIDTH
PROJ_W = COL_ALR + ALR_W
MIX_W = GLA_WIDTH + SWA_QW


def _wide_cols():
    grp = SWA_HEADS // SWA_KV_HEADS
    h = np.arange(SWA_WIDTH) // SWA_HEAD_DIM
    return h * WIDE + (h // grp) * SWA_HEAD_DIM + np.arange(SWA_WIDTH) % SWA_HEAD_DIM


def _rms(x):
    return x * lax.rsqrt(jnp.mean(x * x, axis=-1, keepdims=True) + RMS_EPS)


def _ada_kernel(c_ref, w_ref, b_ref, o_ref):
    c = c_ref[...]
    ca = (c * jax.nn.sigmoid(c)).astype(jnp.bfloat16)
    o_ref[...] = jnp.dot(ca, w_ref[...].astype(jnp.bfloat16), preferred_element_type=jnp.float32) + b_ref[...]


def _ada(c, w, b):
    bsz, d = c.shape
    n = w.shape[1]
    tn = 1024
    return pl.pallas_call(
        _ada_kernel,
        grid=(n // tn,),
        in_specs=[pl.BlockSpec((bsz, d), lambda j: (0, 0)), pl.BlockSpec((d, tn), lambda j: (0, j)),
                  pl.BlockSpec((1, tn), lambda j: (0, j))],
        out_specs=pl.BlockSpec((bsz, tn), lambda j: (0, j)),
        out_shape=jax.ShapeDtypeStruct((bsz, n), jnp.float32), name="ada",
    )(c, w, b.reshape(1, n))


def _in_proj_kernel(x_ref, g_ref, sh_ref, sc_ref, w_ref, o_ref):
    h = _rms(x_ref[...]) * g_ref[...] * (1.0 + sc_ref[...]) + sh_ref[...]
    o_ref[...] = jnp.dot(h.astype(jnp.bfloat16), w_ref[...], preferred_element_type=jnp.float32)


def _in_proj(x, g, shift, scale, w):
    bsz, s, d = x.shape
    n = w.shape[1]
    row = lambda width: pl.BlockSpec((None, ROW_TILE, width), lambda b, i: (b, i, 0))
    per_b = pl.BlockSpec((None, 1, d), lambda b, i: (b, 0, 0))
    return pl.pallas_call(
        _in_proj_kernel,
        grid=(bsz, s // ROW_TILE),
        in_specs=[row(d), pl.BlockSpec((1, d), lambda b, i: (0, 0)), per_b, per_b,
                  pl.BlockSpec((d, n), lambda b, i: (0, 0))],
        out_specs=row(n),
        out_shape=jax.ShapeDtypeStruct((bsz, s, n), jnp.float32),
        compiler_params=pltpu.CompilerParams(dimension_semantics=("arbitrary", "arbitrary"),
                                             vmem_limit_bytes=48 * 1024 * 1024),
        name="in_proj",
    )(x, g.reshape(1, d), shift, scale, w)


def _swa_kernel(sink_ref, q_ref, kp_ref, kc_ref, vp_ref, vc_ref, g_ref, o_ref):
    f32, bf16 = jnp.float32, jnp.bfloat16
    blk = SWA_BLOCK
    k2 = jnp.concatenate([kp_ref[...], kc_ref[...]], axis=0).astype(bf16)
    v2 = jnp.concatenate([vp_ref[...], vc_ref[...]], axis=0).astype(bf16)
    qi = lax.broadcasted_iota(jnp.int32, (blk, 2 * blk), 0)
    kj = lax.broadcasted_iota(jnp.int32, (blk, 2 * blk), 1)
    dist = qi + blk - kj
    valid = (dist >= 0) & (dist < SWA_WINDOW) & ((kj >= blk) | (pl.program_id(1) > 0))
    distf = dist.astype(f32)
    lane_kv = lax.broadcasted_iota(jnp.int32, (blk, WIDE), 1) // SWA_HEAD_DIM
    grp = SWA_HEADS // SWA_KV_HEADS
    outs, ssq = [], jnp.zeros((blk, 1), f32)
    for h in range(SWA_HEADS):
        q = (q_ref[:, h * WIDE:(h + 1) * WIDE] * (SWA_HEAD_DIM ** -0.5)).astype(bf16)
        sc = lax.dot_general(q, k2, NT_DIMS, preferred_element_type=f32)
        sc = sc - (2.0 ** (-8.0 * (h + 1) / SWA_HEADS)) * distf
        sc = jnp.where(valid, sc, NEG_INF)
        sink = sink_ref[h]
        m = jnp.maximum(jnp.max(sc, axis=-1, keepdims=True), sink)
        p = jnp.exp(sc - m)
        p = p / (jnp.sum(p, axis=-1, keepdims=True) + jnp.exp(sink - m))
        o = jnp.dot(p.astype(bf16), v2, preferred_element_type=f32)
        o = jnp.where(lane_kv == h // grp, o, 0.0)
        ssq = ssq + jnp.sum(o * o, axis=-1, keepdims=True)
        outs.append(o)
    inv = lax.rsqrt(ssq * (1.0 / SWA_WIDTH) + RMS_EPS)
    for h in range(SWA_HEADS):
        o_ref[:, h * WIDE:(h + 1) * WIDE] = outs[h] * inv * g_ref[:, h * WIDE:(h + 1) * WIDE]


def _swa(proj, sinks, norm_g_wide):
    bsz, s, _ = proj.shape
    col = lambda width, off, prev: pl.BlockSpec(
        (None, SWA_BLOCK, width), lambda b, n: (b, jnp.maximum(n - 1, 0) if prev else n, off // width))
    return pl.pallas_call(
        _swa_kernel,
        grid=(bsz, s // SWA_BLOCK),
        in_specs=[pl.BlockSpec(memory_space=pltpu.SMEM), col(SWA_QW, COL_QS, False),
                  col(SWA_KV_WIDTH, COL_KS, True), col(SWA_KV_WIDTH, COL_KS, False),
                  col(SWA_KV_WIDTH, COL_VS, True), col(SWA_KV_WIDTH, COL_VS, False),
                  pl.BlockSpec((1, SWA_QW), lambda b, n: (0, 0))],
        out_specs=pl.BlockSpec((None, SWA_BLOCK, SWA_QW), lambda b, n: (b, n, 0)),
        out_shape=jax.ShapeDtypeStruct((bsz, s, SWA_QW), jnp.float32),
        compiler_params=pltpu.CompilerParams(dimension_semantics=("arbitrary", "arbitrary")),
        name="swa",
    )(sinks, proj, proj, proj, proj, proj, norm_g_wide)


def _out_proj_kernel(x_ref, og_ref, os_ref, w_ref, gate_ref, g2_ref, sh_ref, sc_ref, x1_ref, h2_ref):
    mix = jnp.concatenate([og_ref[...], os_ref[...]], axis=-1).astype(jnp.bfloat16)
    x1 = x_ref[...] + gate_ref[...] * jnp.dot(mix, w_ref[...], preferred_element_type=jnp.float32)
    x1_ref[...] = x1
    h2_ref[...] = _rms(x1) * g2_ref[...] * (1.0 + sc_ref[...]) + sh_ref[...]


def _out_proj(x, o_gla, o_swa, w, gate1, g2, shift2, scale2):
    bsz, s, d = x.shape
    row = lambda width: pl.BlockSpec((None, ROW_TILE, width), lambda b, i: (b, i, 0))
    per_b = pl.BlockSpec((None, 1, d), lambda b, i: (b, 0, 0))
    shp = jax.ShapeDtypeStruct((bsz, s, d), jnp.float32)
    return pl.pallas_call(
        _out_proj_kernel,
        grid=(bsz, s // ROW_TILE),
        in_specs=[row(d), row(GLA_WIDTH), row(SWA_QW), pl.BlockSpec((MIX_W, d), lambda b, i: (0, 0)),
                  per_b, pl.BlockSpec((1, d), lambda b, i: (0, 0)), per_b, per_b],
        out_specs=[row(d), row(d)],
        out_shape=[shp, shp],
        compiler_params=pltpu.CompilerParams(dimension_semantics=("arbitrary", "arbitrary"),
                                             vmem_limit_bytes=48 * 1024 * 1024),
        name="out_proj",
    )(x, o_gla, o_swa, w, gate1, g2.reshape(1, d), shift2, scale2)


GLA_SUB = 16
GLA_BLOCK = 128
GLA_PAIR = 2 * GLA_DK


def _gla_kernel(q_ref, k_ref, v_ref, alr_ref, r_ref, wa_ref, ba_ref, g_ref, tri_ref, o_ref, st_ref, b_ref, q2_ref):
    f32, bf16 = jnp.float32, jnp.bfloat16

    @pl.when(pl.program_id(1) == 0)
    def _():
        st_ref[...] = jnp.zeros(st_ref.shape, f32)

    z = jnp.dot(alr_ref[...].astype(bf16), wa_ref[...], preferred_element_type=f32) + ba_ref[...]
    log_a = -(jnp.maximum(-z, 0.0) + jnp.log1p(jnp.exp(-jnp.abs(z)))) * (1.0 / GLA_GATE_TEMP)
    b_ref[...] = jnp.dot(tri_ref[...], log_a, preferred_element_type=f32, precision=lax.Precision.HIGHEST)
    q2_ref[...] = q_ref[...] * (GLA_DK ** -0.5)
    lane = lax.broadcasted_iota(jnp.int32, (GLA_SUB, GLA_PAIR), 1)
    sub = lax.broadcasted_iota(jnp.int32, (GLA_SUB, GLA_PAIR), 0)
    first = lane < GLA_DK
    lane_s = lax.broadcasted_iota(jnp.int32, (GLA_DV, GLA_PAIR), 1) < GLA_DK

    for hp in range(GLA_HEADS // 2):
        cols = pl.ds(hp * GLA_PAIR, GLA_PAIR)
        state = st_ref[hp]
        for c in range(GLA_BLOCK // GLA_SUB):
            rows = pl.ds(c * GLA_SUB, GLA_SUB)
            b = b_ref[rows, cols]
            b_last = b_ref[pl.ds(c * GLA_SUB + GLA_SUB - 1, 1), cols]
            q2, k2 = q2_ref[rows, cols], k_ref[rows, cols]
            qe = q2 * jnp.exp(b)
            kd = (k2 * jnp.exp(b_last - b)).astype(bf16)
            sb = state.astype(bf16)
            cols_a, cols_b = [], []
            for t in range(GLA_SUB):
                bt = b_ref[pl.ds(c * GLA_SUB + t, 1), cols]
                qt = q2_ref[pl.ds(c * GLA_SUB + t, 1), cols]
                p = (qt * k2) * jnp.exp(bt - b)
                keep = sub[:, :1] <= t
                cols_a.append(jnp.where(keep, jnp.sum(jnp.where(first, p, 0.0), axis=1, keepdims=True), 0.0))
                cols_b.append(jnp.where(keep, jnp.sum(jnp.where(first, 0.0, p), axis=1, keepdims=True), 0.0))
            ut = []
            for j, sc_cols in enumerate((cols_a, cols_b)):
                head = 2 * hp + j
                vcols = pl.ds(head * GLA_DV, GLA_DV)
                vh = v_ref[rows, vcols]
                qm = jnp.where(first if j == 0 else ~first, qe, 0.0).astype(bf16)
                o = lax.dot_general(qm, sb, NT_DIMS, preferred_element_type=f32)
                o = o + jnp.concatenate([jnp.sum(cj * vh, axis=0, keepdims=True) for cj in sc_cols], axis=0)
                ms = jnp.mean(o * o, axis=1, keepdims=True)
                r = r_ref[rows, vcols]
                o_ref[rows, vcols] = (o * lax.rsqrt(ms + RMS_EPS) * g_ref[:, vcols]) * (r * jax.nn.sigmoid(r))
                ut.append(jnp.dot(vh.T.astype(bf16), kd, preferred_element_type=f32))
            state = jnp.exp(b_last) * state + jnp.where(lane_s, ut[0], ut[1])
        st_ref[hp] = state


def _gla_tri():
    r = np.arange(GLA_BLOCK)
    return jnp.asarray(((r[:, None] // GLA_SUB == r[None, :] // GLA_SUB) & (r[None, :] <= r[:, None])).astype(np.float32))


def _gla(proj, w_alpha, b_alpha, norm_g):
    bsz, s, _ = proj.shape
    rank = w_alpha.shape[0]
    wa = jnp.pad(w_alpha, ((0, ALR_W - rank), (0, 0))).astype(jnp.bfloat16)
    col = lambda width, off: pl.BlockSpec((None, GLA_BLOCK, width), lambda b, i: (b, i, off // width))
    const = lambda shape: pl.BlockSpec(shape, lambda b, i: (0, 0))
    return pl.pallas_call(
        _gla_kernel,
        grid=(bsz, s // GLA_BLOCK),
        in_specs=[col(GLA_QK_WIDTH, COL_QG), col(GLA_QK_WIDTH, COL_KG), col(GLA_WIDTH, COL_VG), col(ALR_W, COL_ALR),
                  col(GLA_WIDTH, COL_RG), const((ALR_W, GLA_QK_WIDTH)), const((1, GLA_QK_WIDTH)),
                  const((1, GLA_WIDTH)), const((GLA_BLOCK, GLA_BLOCK))],
        out_specs=pl.BlockSpec((None, GLA_BLOCK, GLA_WIDTH), lambda b, i: (b, i, 0)),
        out_shape=jax.ShapeDtypeStruct((bsz, s, GLA_WIDTH), jnp.float32),
        scratch_shapes=[pltpu.VMEM((GLA_HEADS // 2, GLA_DV, GLA_PAIR), jnp.float32),
                        pltpu.VMEM((GLA_BLOCK, GLA_QK_WIDTH), jnp.float32),
                        pltpu.VMEM((GLA_BLOCK, GLA_QK_WIDTH), jnp.float32)],
        compiler_params=pltpu.CompilerParams(dimension_semantics=("arbitrary", "arbitrary")),
        name="gla",
    )(proj, proj, proj, proj, proj, wa, b_alpha.reshape(1, -1), norm_g.reshape(1, -1), _gla_tri())


def peer_mixer(h, w_q, keys_1, keys_2, u_tab, v_tab):
    bsz, s, d = h.shape
    t = bsz * s
    hf = h.reshape(t, d)
    experts, gates = _peer_route(hf, w_q, keys_1, keys_2)
    return _peer_experts(hf, experts, gates, u_tab, v_tab).reshape(bsz, s, d)


ROUTE_TT = 256


def _top16_rows(s, code, big):
    vals, codes = [], []
    for _ in range(PEER_TOPK):
        m = jnp.max(s, axis=0, keepdims=True)
        c = jnp.min(jnp.where(s == m, code, big), axis=0, keepdims=True)
        s = jnp.where(code == c, NEG_INF, s)
        vals.append(m)
        codes.append(c)
    return vals, codes


def _peer_route_kernel(h_ref, wq_ref, k1_ref, k2_ref, e_ref, g_ref, q_ref):
    half = PEER_QUERY_DIM // 2
    tt = ROUTE_TT
    q_ref[...] = jnp.dot(h_ref[...].astype(jnp.bfloat16), wq_ref[...], preferred_element_type=jnp.float32)
    key_id = lax.broadcasted_iota(jnp.int32, (PEER_N_KEYS, 2 * tt), 0)
    sub8 = lax.broadcasted_iota(jnp.int32, (8, tt), 0)
    nt = (((1,), (1,)), ((), ()))

    def head_body(hd, carry):
        off = pl.multiple_of(hd * PEER_QUERY_DIM, PEER_QUERY_DIM)
        q1 = q_ref[:, pl.ds(off, half)].astype(jnp.bfloat16)
        q2 = q_ref[:, pl.ds(off + half, half)].astype(jnp.bfloat16)
        s1 = lax.dot_general(k1_ref[...], q1, nt, preferred_element_type=jnp.float32)
        s2 = lax.dot_general(k2_ref[...], q2, nt, preferred_element_type=jnp.float32)
        vals, ids = _top16_rows(jnp.concatenate([s1, s2], axis=1), key_id, PEER_N_KEYS)
        v = jnp.concatenate(vals, axis=0)
        ix = jnp.concatenate(ids, axis=0)
        v1, v2, i1, i2 = v[:, :tt], v[:, tt:], ix[:, :tt], ix[:, tt:]
        cand, code = [], []
        for a, b0 in [(0, 0), (0, 8), (1, 0), (2, 0), (3, 0), (4, 0), (5, 0), (6, 0), (7, 0)]:
            cand.append(v1[a:a + 1] + v2[b0:b0 + 8])
            flat = a * PEER_TOPK + b0 + sub8
            code.append((flat << 14) | (i1[a:a + 1] * PEER_N_KEYS + i2[b0:b0 + 8]))
        cand.append(v1[8:16] + v2[0:1])
        code.append((((8 + sub8) * PEER_TOPK) << 14) | (i1[8:16] * PEER_N_KEYS + i2[0:1]))
        sc, cc = _top16_rows(jnp.concatenate(cand, axis=0), jnp.concatenate(code, axis=0), 1 << 30)
        sc = jnp.concatenate(sc, axis=0)
        ex = jnp.exp(sc - sc[0:1])
        g_ref[hd] = ex / jnp.sum(ex, axis=0, keepdims=True)
        e_ref[hd] = jnp.concatenate(cc, axis=0) & (PEER_N_EXPERTS - 1)
        return carry

    lax.fori_loop(0, PEER_HEADS, head_body, 0)


def _peer_route(hf, w_q, keys_1, keys_2):
    t, d = hf.shape
    qw = PEER_HEADS * PEER_QUERY_DIM
    const = lambda shape: pl.BlockSpec(shape, lambda i: tuple(0 for _ in shape))
    out_spec = pl.BlockSpec((PEER_HEADS, PEER_TOPK, ROUTE_TT), lambda i: (0, 0, i))
    e_t, g_t = pl.pallas_call(
        _peer_route_kernel,
        grid=(t // ROUTE_TT,),
        in_specs=[pl.BlockSpec((ROUTE_TT, d), lambda i: (i, 0)), const((d, qw)),
                  const(keys_1.shape), const(keys_2.shape)],
        out_specs=[out_spec, out_spec],
        out_shape=[jax.ShapeDtypeStruct((PEER_HEADS, PEER_TOPK, t), jnp.int32),
                   jax.ShapeDtypeStruct((PEER_HEADS, PEER_TOPK, t), jnp.float32)],
        scratch_shapes=[pltpu.VMEM((ROUTE_TT, qw), jnp.float32)],
        compiler_params=pltpu.CompilerParams(dimension_semantics=("arbitrary",), vmem_limit_bytes=48 * 1024 * 1024),
        name="peer_route",
    )(hf, w_q.astype(jnp.bfloat16), keys_1.astype(jnp.bfloat16), keys_2.astype(jnp.bfloat16))
    to_rows = lambda a: a.reshape(PEER_HEADS * PEER_TOPK, t).T
    return to_rows(e_t), to_rows(g_t)


PEER_PAIRS = PEER_HEADS * PEER_TOPK
PEER_TB = 64
PEER_UNROLL = 32
HALF_D = D_MODEL // 2
WORD_SUB, WORD_LANE = HALF_D // 128, 128
ROW_SUB = 2 * WORD_SUB
HI_MASK = -65536


def _pack_table(tab):
    b = lax.bitcast_convert_type(tab.astype(jnp.bfloat16), jnp.uint16).astype(jnp.uint32)
    word = b[:, :HALF_D] | (b[:, HALF_D:] << 16)
    return lax.bitcast_convert_type(word, jnp.int32).reshape(tab.shape[0] * WORD_SUB, WORD_LANE)


def _unpack_row(tab_ref, e):
    w = tab_ref[pl.ds(pl.multiple_of(e, WORD_SUB), WORD_SUB), :]
    return (lax.bitcast_convert_type(w << 16, jnp.float32), lax.bitcast_convert_type(w & HI_MASK, jnp.float32))


_ORDER = (6, 2, 4, 0, 7, 3, 5, 1)


def _fold_pairs(ps, sub):
    b1 = (sub & 2) != 0
    w = []
    for i in range(2):
        x, y = ps[2 * i], ps[2 * i + 1]
        w.append(jnp.where(b1, x + pltpu.roll(x, 2, 0), y + pltpu.roll(y, 6, 0)))
    b0 = (sub & 1) != 0
    x, y = w
    return jnp.where(b0, x + pltpu.roll(x, 1, 0), y + pltpu.roll(y, 7, 0))


def _peer_u_kernel(idx_ref, h_ref, tab_ref, a_ref):
    sub = lax.broadcasted_iota(jnp.int32, (ROW_SUB, WORD_LANE), 0)
    lane = lax.broadcasted_iota(jnp.int32, (ROW_SUB, WORD_LANE), 1)

    n_groups = PEER_PAIRS // 8

    def folded_tiles(t):
        h = h_ref[t]
        h_lo = jnp.concatenate([h[:WORD_SUB], h[:WORD_SUB]], axis=0)
        h_hi = jnp.concatenate([h[WORD_SUB:], h[WORD_SUB:]], axis=0)
        row_idx = idx_ref.at[pl.ds(t * PEER_PAIRS, PEER_PAIRS)]
        tiles = []
        for g in range(n_groups):
            ps = []
            for i in range(4):
                w = jnp.concatenate([tab_ref[pl.ds(pl.multiple_of(row_idx[g * 8 + 2 * i + j], WORD_SUB), WORD_SUB), :]
                                     for j in range(2)], axis=0)
                lo = lax.bitcast_convert_type(w << 16, jnp.float32)
                hi = lax.bitcast_convert_type(w & HI_MASK, jnp.float32)
                ps.append(lo * h_lo + hi * h_hi)
            tiles.append(_fold_pairs(ps, sub))
        return tuple(tiles)

    def lane_sums(t, tiles, acc):
        for g in range(n_groups):
            acc = jnp.where(lane == (t % 8) * n_groups + g, jnp.sum(tiles[g], axis=1, keepdims=True), acc)
        a_ref[t // 8] = acc
        return acc

    def tok_body(t, carry):
        acc, prev = carry
        return lane_sums(t - 1, prev, acc), folded_tiles(t)

    acc, last = lax.fori_loop(1, PEER_TB, tok_body,
                              (jnp.zeros((ROW_SUB, WORD_LANE), jnp.float32), folded_tiles(0)))
    lane_sums(PEER_TB - 1, last, acc)


def _split3_bf16(x):
    mask = lambda v: lax.bitcast_convert_type(lax.bitcast_convert_type(v, jnp.int32) & HI_MASK, jnp.float32)
    hi = mask(x)
    mid = mask(x - hi)
    lo = x - hi - mid
    return [p.astype(jnp.bfloat16) for p in (hi, mid, lo)]


def _peer_v_kernel(idx_ref, a_ref, g_ref, tab_ref, sel_ref, o_ref, wrep_ref):
    a = a_ref[...]
    w = g_ref[...] * (0.5 * a * (1.0 + lax.erf(a * (2.0 ** -0.5))))
    parts = _split3_bf16(w.T)
    wrep_ref[...] = sum(jnp.dot(p, sel_ref[...], preferred_element_type=jnp.float32) for p in parts)

    def tok_body(t, carry):
        col = pl.multiple_of(t * WORD_LANE, WORD_LANE)

        idx_t = idx_ref.at[pl.ds(t * PEER_PAIRS, PEER_PAIRS)]
        accs = [jnp.zeros((WORD_SUB, WORD_LANE), jnp.float32) for _ in range(4)]
        for p in range(PEER_PAIRS):
            lo, hi = _unpack_row(tab_ref, idx_t[p])
            w_p = wrep_ref[pl.ds(p, 1), pl.ds(col, WORD_LANE)]
            accs[2 * (p % 2)] = accs[2 * (p % 2)] + lo * w_p
            accs[2 * (p % 2) + 1] = accs[2 * (p % 2) + 1] + hi * w_p
        o_ref[t] = jnp.concatenate([accs[0] + accs[2], accs[1] + accs[3]], axis=0)
        return carry

    lax.fori_loop(0, PEER_TB, tok_body, 0)


_PEER_VMEM_LIMIT = 56 * 1024 * 1024


def _peer_experts(hf, experts, gates, u_tab, v_tab):
    t, d = hf.shape
    nblk = t // PEER_TB
    smem_spec = pl.BlockSpec((PEER_TB * PEER_PAIRS,), lambda i: (i,), memory_space=pltpu.SMEM)
    tab_spec = pl.BlockSpec(memory_space=pltpu.VMEM)
    tok_spec = pl.BlockSpec((PEER_TB, PEER_PAIRS), lambda i: (i, 0))
    row_spec = pl.BlockSpec((PEER_TB, ROW_SUB, WORD_LANE), lambda i: (i, 0, 0))
    params = pltpu.CompilerParams(dimension_semantics=("arbitrary",), vmem_limit_bytes=_PEER_VMEM_LIMIT)
    idx = (experts.astype(jnp.int32) * WORD_SUB).reshape(-1)
    a_tiles = pl.pallas_call(
        _peer_u_kernel,
        grid=(nblk,),
        in_specs=[smem_spec, row_spec, tab_spec],
        out_specs=pl.BlockSpec((PEER_TB // 8, ROW_SUB, WORD_LANE), lambda i: (i, 0, 0)),
        out_shape=jax.ShapeDtypeStruct((t // 8, ROW_SUB, WORD_LANE), jnp.float32),
        compiler_params=params, name="peer_u",
    )(idx, hf.reshape(t, ROW_SUB, WORD_LANE), _pack_table(u_tab))
    inv = np.argsort(np.array(_ORDER))
    a = a_tiles.reshape(t // 8, 8, 8, PEER_PAIRS // 8)[:, inv].transpose(0, 2, 3, 1).reshape(t, PEER_PAIRS)
    sel = jnp.asarray(np.repeat(np.eye(PEER_TB, dtype=np.float32), WORD_LANE, axis=1), jnp.bfloat16)
    out = pl.pallas_call(
        _peer_v_kernel,
        grid=(nblk,),
        in_specs=[smem_spec, tok_spec, tok_spec, tab_spec, pl.BlockSpec(sel.shape, lambda i: (0, 0))],
        out_specs=row_spec,
        out_shape=jax.ShapeDtypeStruct((t, ROW_SUB, WORD_LANE), jnp.float32),
        scratch_shapes=[pltpu.VMEM((PEER_PAIRS, PEER_TB * WORD_LANE), jnp.float32)],
        compiler_params=params, name="peer_v",
    )(idx, a, gates, _pack_table(v_tab), sel)
    return out.reshape(t, d)


def _final_kernel(x_ref, p_ref, gate_ref, g_ref, o_ref):
    o_ref[...] = _rms(x_ref[...] + gate_ref[...] * p_ref[...]) * g_ref[...]


def _final(x1, peer, gate2, g):
    bsz, s, d = x1.shape
    row = pl.BlockSpec((None, ROW_TILE, d), lambda b, i: (b, i, 0))
    return pl.pallas_call(
        _final_kernel,
        grid=(bsz, s // ROW_TILE),
        in_specs=[row, row, pl.BlockSpec((None, 1, d), lambda b, i: (b, 0, 0)), pl.BlockSpec((1, d), lambda b, i: (0, 0))],
        out_specs=row,
        out_shape=jax.ShapeDtypeStruct((bsz, s, d), jnp.float32), name="final",
    )(x1, peer, gate2, g.reshape(1, d))


def _in_proj_weights(w_in):
    o = np.cumsum([0, GLA_QK_WIDTH, GLA_QK_WIDTH, GLA_WIDTH, GLA_GATE_RANK, GLA_WIDTH, SWA_WIDTH, SWA_KV_WIDTH])
    src = np.full((PROJ_W,), IN_WIDTH, np.int32)
    src[COL_QS + _wide_cols()] = o[5] + np.arange(SWA_WIDTH)
    for dst, start, width in ((COL_QG, o[0], GLA_QK_WIDTH), (COL_KG, o[1], GLA_QK_WIDTH), (COL_VG, o[2], GLA_WIDTH),
                              (COL_ALR, o[3], GLA_GATE_RANK), (COL_RG, o[4], GLA_WIDTH),
                              (COL_KS, o[6], SWA_KV_WIDTH), (COL_VS, o[7], SWA_KV_WIDTH)):
        src[dst:dst + width] = start + np.arange(width)
    w_ext = jnp.concatenate([w_in, jnp.zeros((w_in.shape[0], 1), w_in.dtype)], axis=1)
    return w_ext[:, src].astype(jnp.bfloat16)


def _out_proj_weights(w_out):
    src = np.full((MIX_W,), MIX_WIDTH, np.int32)
    src[:GLA_WIDTH] = np.arange(GLA_WIDTH)
    src[GLA_WIDTH + _wide_cols()] = GLA_WIDTH + np.arange(SWA_WIDTH)
    w_ext = jnp.concatenate([w_out, jnp.zeros((1, w_out.shape[1]), w_out.dtype)], axis=0)
    return w_ext[src].astype(jnp.bfloat16)


def kernel(x, c, w_ada, b_ada, norm1_g, norm2_g, w_in, w_gla_alpha, b_gla_alpha, gla_norm_g,
           swa_sinks, swa_norm_g, w_out, w_peer_q, peer_keys_1, peer_keys_2, peer_u, peer_v, final_g):
    assert w_ada.shape[0] == 1, "single-layer block"
    l = 0
    mod = _ada(c, w_ada[l], b_ada[l])
    shift1, scale1, gate1, shift2, scale2, gate2 = [m[:, None, :] for m in jnp.split(mod, 6, axis=-1)]
    proj = _in_proj(x, norm1_g[l], shift1, scale1, _in_proj_weights(w_in[l]))
    o_gla = _gla(proj, w_gla_alpha[l], b_gla_alpha[l], gla_norm_g[l])
    g_wide = jnp.zeros((1, SWA_QW), jnp.float32).at[0, _wide_cols()].set(swa_norm_g[l])
    o_swa = _swa(proj, swa_sinks[l], g_wide)
    x1, h2 = _out_proj(x, o_gla, o_swa, _out_proj_weights(w_out[l]), gate1, norm2_g[l], shift2, scale2)
    peer = peer_mixer(h2, w_peer_q[l], peer_keys_1[l], peer_keys_2[l], peer_u[l], peer_v[l])
    return _final(x1, peer, gate2, final_g)
```

```python
import jax, jax.numpy as jnp
from jax import lax
import numpy as np
from jax.experimental import pallas as pl
from jax.experimental.pallas import tpu as pltpu

D_MODEL = 1024
BATCH = 4
SEQ = 8192
DEPTH = 1

GLA_HEADS = 4
GLA_DK = 64
GLA_DV = 128
GLA_GATE_RANK = 16
GLA_GATE_TEMP = 16.0
GLA_CHUNK = 64
SWA_HEADS = 8
SWA_KV_HEADS = 2
SWA_HEAD_DIM = 64
SWA_WINDOW = 128
SWA_BLOCK = 128
PEER_HEADS = 8
PEER_N_KEYS = 128
PEER_N_EXPERTS = PEER_N_KEYS * PEER_N_KEYS
PEER_QUERY_DIM = 256
PEER_TOPK = 16
RMS_EPS = 1e-6

GLA_QK_WIDTH = GLA_HEADS * GLA_DK
GLA_WIDTH = GLA_HEADS * GLA_DV
SWA_WIDTH = SWA_HEADS * SWA_HEAD_DIM
SWA_KV_WIDTH = SWA_KV_HEADS * SWA_HEAD_DIM
MIX_WIDTH = GLA_WIDTH + SWA_WIDTH
IN_WIDTH = 2 * GLA_QK_WIDTH + GLA_WIDTH + GLA_GATE_RANK + GLA_WIDTH + SWA_WIDTH + 2 * SWA_KV_WIDTH


LANES = 128
NEG_INF = float("-inf")
NT_DIMS = (((1,), (1,)), ((), ()))
ROW_TILE = 512
WIDE = 2 * SWA_HEAD_DIM
SWA_QW = SWA_HEADS * WIDE
ALR_W = LANES
COL_QS, COL_QG, COL_KG, COL_VG, COL_RG = 0, SWA_QW, SWA_QW + GLA_QK_WIDTH, SWA_QW + 2 * GLA_QK_WIDTH, SWA_QW + 2 * GLA_QK_WIDTH + GLA_WIDTH
COL_KS = COL_RG + GLA_WIDTH
COL_VS = COL_KS + SWA_KV_WIDTH
COL_ALR = COL_VS + SWA_KV_WIDTH
PROJ_W = COL_ALR + ALR_W
MIX_W = GLA_WIDTH + SWA_QW


def _wide_cols():
    grp = SWA_HEADS // SWA_KV_HEADS
    h = np.arange(SWA_WIDTH) // SWA_HEAD_DIM
    return h * WIDE + (h // grp) * SWA_HEAD_DIM + np.arange(SWA_WIDTH) % SWA_HEAD_DIM


def _rms(x):
    return x * lax.rsqrt(jnp.mean(x * x, axis=-1, keepdims=True) + RMS_EPS)


def _ada_kernel(c_ref, w_ref, b_ref, o_ref):
    c = c_ref[...]
    ca = (c * jax.nn.sigmoid(c)).astype(jnp.bfloat16)
    o_ref[...] = jnp.dot(ca, w_ref[...].astype(jnp.bfloat16), preferred_element_type=jnp.float32) + b_ref[...]


def _ada(c, w, b):
    bsz, d = c.shape
    n = w.shape[1]
    tn = 1024
    return pl.pallas_call(
        _ada_kernel,
        grid=(n // tn,),
        in_specs=[pl.BlockSpec((bsz, d), lambda j: (0, 0)), pl.BlockSpec((d, tn), lambda j: (0, j)),
                  pl.BlockSpec((1, tn), lambda j: (0, j))],
        out_specs=pl.BlockSpec((bsz, tn), lambda j: (0, j)),
        out_shape=jax.ShapeDtypeStruct((bsz, n), jnp.float32), name="ada",
    )(c, w, b.reshape(1, n))


def _in_proj_kernel(x_ref, g_ref, sh_ref, sc_ref, w_ref, o_ref):
    h = _rms(x_ref[...]) * g_ref[...] * (1.0 + sc_ref[...]) + sh_ref[...]
    o_ref[...] = jnp.dot(h.astype(jnp.bfloat16), w_ref[...], preferred_element_type=jnp.float32)


def _in_proj(x, g, shift, scale, w):
    bsz, s, d = x.shape
    n = w.shape[1]
    row = lambda width: pl.BlockSpec((None, ROW_TILE, width), lambda b, i: (b, i, 0))
    per_b = pl.BlockSpec((None, 1, d), lambda b, i: (b, 0, 0))
    return pl.pallas_call(
        _in_proj_kernel,
        grid=(bsz, s // ROW_TILE),
        in_specs=[row(d), pl.BlockSpec((1, d), lambda b, i: (0, 0)), per_b, per_b,
                  pl.BlockSpec((d, n), lambda b, i: (0, 0))],
        out_specs=row(n),
        out_shape=jax.ShapeDtypeStruct((bsz, s, n), jnp.float32),
        compiler_params=pltpu.CompilerParams(dimension_semantics=("arbitrary", "arbitrary"),
                                             vmem_limit_bytes=48 * 1024 * 1024),
        name="in_proj",
    )(x, g.reshape(1, d), shift, scale, w)


def _swa_kernel(sink_ref, q_ref, kp_ref, kc_ref, vp_ref, vc_ref, g_ref, o_ref):
    f32, bf16 = jnp.float32, jnp.bfloat16
    blk = SWA_BLOCK
    k2 = jnp.concatenate([kp_ref[...], kc_ref[...]], axis=0).astype(bf16)
    v2 = jnp.concatenate([vp_ref[...], vc_ref[...]], axis=0).astype(bf16)
    qi = lax.broadcasted_iota(jnp.int32, (blk, 2 * blk), 0)
    kj = lax.broadcasted_iota(jnp.int32, (blk, 2 * blk), 1)
    dist = qi + blk - kj
    valid = (dist >= 0) & (dist < SWA_WINDOW) & ((kj >= blk) | (pl.program_id(1) > 0))
    distf = dist.astype(f32)
    lane_kv = lax.broadcasted_iota(jnp.int32, (blk, WIDE), 1) // SWA_HEAD_DIM
    grp = SWA_HEADS // SWA_KV_HEADS
    outs, ssq = [], jnp.zeros((blk, 1), f32)
    for h in range(SWA_HEADS):
        q = (q_ref[:, h * WIDE:(h + 1) * WIDE] * (SWA_HEAD_DIM ** -0.5)).astype(bf16)
        sc = lax.dot_general(q, k2, NT_DIMS, preferred_element_type=f32)
        sc = sc - (2.0 ** (-8.0 * (h + 1) / SWA_HEADS)) * distf
        sc = jnp.where(valid, sc, NEG_INF)
        sink = sink_ref[h]
        m = jnp.maximum(jnp.max(sc, axis=-1, keepdims=True), sink)
        p = jnp.exp(sc - m)
        p = p / (jnp.sum(p, axis=-1, keepdims=True) + jnp.exp(sink - m))
        o = jnp.dot(p.astype(bf16), v2, preferred_element_type=f32)
        o = jnp.where(lane_kv == h // grp, o, 0.0)
        ssq = ssq + jnp.sum(o * o, axis=-1, keepdims=True)
        outs.append(o)
    inv = lax.rsqrt(ssq * (1.0 / SWA_WIDTH) + RMS_EPS)
    for h in range(SWA_HEADS):
        o_ref[:, h * WIDE:(h + 1) * WIDE] = outs[h] * inv * g_ref[:, h * WIDE:(h + 1) * WIDE]


def _swa(proj, sinks, norm_g_wide):
    bsz, s, _ = proj.shape
    col = lambda width, off, prev: pl.BlockSpec(
        (None, SWA_BLOCK, width), lambda b, n: (b, jnp.maximum(n - 1, 0) if prev else n, off // width))
    return pl.pallas_call(
        _swa_kernel,
        grid=(bsz, s // SWA_BLOCK),
        in_specs=[pl.BlockSpec(memory_space=pltpu.SMEM), col(SWA_QW, COL_QS, False),
                  col(SWA_KV_WIDTH, COL_KS, True), col(SWA_KV_WIDTH, COL_KS, False),
                  col(SWA_KV_WIDTH, COL_VS, True), col(SWA_KV_WIDTH, COL_VS, False),
                  pl.BlockSpec((1, SWA_QW), lambda b, n: (0, 0))],
        out_specs=pl.BlockSpec((None, SWA_BLOCK, SWA_QW), lambda b, n: (b, n, 0)),
        out_shape=jax.ShapeDtypeStruct((bsz, s, SWA_QW), jnp.float32),
        compiler_params=pltpu.CompilerParams(dimension_semantics=("arbitrary", "arbitrary")),
        name="swa",
    )(sinks, proj, proj, proj, proj, proj, norm_g_wide)


def _out_proj_kernel(x_ref, og_ref, os_ref, w_ref, gate_ref, g2_ref, sh_ref, sc_ref, x1_ref, h2_ref):
    mix = jnp.concatenate([og_ref[...], os_ref[...]], axis=-1).astype(jnp.bfloat16)
    x1 = x_ref[...] + gate_ref[...] * jnp.dot(mix, w_ref[...], preferred_element_type=jnp.float32)
    x1_ref[...] = x1
    h2_ref[...] = _rms(x1) * g2_ref[...] * (1.0 + sc_ref[...]) + sh_ref[...]


def _out_proj(x, o_gla, o_swa, w, gate1, g2, shift2, scale2):
    bsz, s, d = x.shape
    row = lambda width: pl.BlockSpec((None, ROW_TILE, width), lambda b, i: (b, i, 0))
    per_b = pl.BlockSpec((None, 1, d), lambda b, i: (b, 0, 0))
    shp = jax.ShapeDtypeStruct((bsz, s, d), jnp.float32)
    return pl.pallas_call(
        _out_proj_kernel,
        grid=(bsz, s // ROW_TILE),
        in_specs=[row(d), row(GLA_WIDTH), row(SWA_QW), pl.BlockSpec((MIX_W, d), lambda b, i: (0, 0)),
                  per_b, pl.BlockSpec((1, d), lambda b, i: (0, 0)), per_b, per_b],
        out_specs=[row(d), row(d)],
        out_shape=[shp, shp],
        compiler_params=pltpu.CompilerParams(dimension_semantics=("arbitrary", "arbitrary"),
                                             vmem_limit_bytes=48 * 1024 * 1024),
        name="out_proj",
    )(x, o_gla, o_swa, w, gate1, g2.reshape(1, d), shift2, scale2)


GLA_SUB = 16
GLA_BLOCK = 128
GLA_PAIR = 2 * GLA_DK


def _gla_kernel(q_ref, k_ref, v_ref, alr_ref, r_ref, wa_ref, ba_ref, g_ref, tri_ref, o_ref, st_ref, b_ref, q2_ref):
    f32, bf16 = jnp.float32, jnp.bfloat16

    @pl.when(pl.program_id(1) == 0)
    def _():
        st_ref[...] = jnp.zeros(st_ref.shape, f32)

    z = jnp.dot(alr_ref[...].astype(bf16), wa_ref[...], preferred_element_type=f32) + ba_ref[...]
    log_a = -(jnp.maximum(-z, 0.0) + jnp.log1p(jnp.exp(-jnp.abs(z)))) * (1.0 / GLA_GATE_TEMP)
    b_ref[...] = jnp.dot(tri_ref[...], log_a, preferred_element_type=f32, precision=lax.Precision.HIGHEST)
    q2_ref[...] = q_ref[...] * (GLA_DK ** -0.5)
    lane = lax.broadcasted_iota(jnp.int32, (GLA_SUB, GLA_PAIR), 1)
    sub = lax.broadcasted_iota(jnp.int32, (GLA_SUB, GLA_PAIR), 0)
    first = lane < GLA_DK
    lane_s = lax.broadcasted_iota(jnp.int32, (GLA_DV, GLA_PAIR), 1) < GLA_DK

    for hp in range(GLA_HEADS // 2):
        cols = pl.ds(hp * GLA_PAIR, GLA_PAIR)
        state = st_ref[hp]
        for c in range(GLA_BLOCK // GLA_SUB):
            rows = pl.ds(c * GLA_SUB, GLA_SUB)
            b = b_ref[rows, cols]
            b_last = b_ref[pl.ds(c * GLA_SUB + GLA_SUB - 1, 1), cols]
            q2, k2 = q2_ref[rows, cols], k_ref[rows, cols]
            qe = q2 * jnp.exp(b)
            kd = (k2 * jnp.exp(b_last - b)).astype(bf16)
            sb = state.astype(bf16)
            cols_a, cols_b = [], []
            for t in range(GLA_SUB):
                bt = b_ref[pl.ds(c * GLA_SUB + t, 1), cols]
                qt = q2_ref[pl.ds(c * GLA_SUB + t, 1), cols]
                p = (qt * k2) * jnp.exp(bt - b)
                keep = sub[:, :1] <= t
                cols_a.append(jnp.where(keep, jnp.sum(jnp.where(first, p, 0.0), axis=1, keepdims=True), 0.0))
                cols_b.append(jnp.where(keep, jnp.sum(jnp.where(first, 0.0, p), axis=1, keepdims=True), 0.0))
            ut = []
            for j, sc_cols in enumerate((cols_a, cols_b)):
                head = 2 * hp + j
                vcols = pl.ds(head * GLA_DV, GLA_DV)
                vh = v_ref[rows, vcols]
                qm = jnp.where(first if j == 0 else ~first, qe, 0.0).astype(bf16)
                o = lax.dot_general(qm, sb, NT_DIMS, preferred_element_type=f32)
                o = o + jnp.concatenate([jnp.sum(cj * vh, axis=0, keepdims=True) for cj in sc_cols], axis=0)
                ms = jnp.mean(o * o, axis=1, keepdims=True)
                r = r_ref[rows, vcols]
                o_ref[rows, vcols] = (o * lax.rsqrt(ms + RMS_EPS) * g_ref[:, vcols]) * (r * jax.nn.sigmoid(r))
                ut.append(jnp.dot(vh.T.astype(bf16), kd, preferred_element_type=f32))
            state = jnp.exp(b_last) * state + jnp.where(lane_s, ut[0], ut[1])
        st_ref[hp] = state


def _gla_tri():
    r = np.arange(GLA_BLOCK)
    return jnp.asarray(((r[:, None] // GLA_SUB == r[None, :] // GLA_SUB) & (r[None, :] <= r[:, None])).astype(np.float32))


def _gla(proj, w_alpha, b_alpha, norm_g):
    bsz, s, _ = proj.shape
    rank = w_alpha.shape[0]
    wa = jnp.pad(w_alpha, ((0, ALR_W - rank), (0, 0))).astype(jnp.bfloat16)
    col = lambda width, off: pl.BlockSpec((None, GLA_BLOCK, width), lambda b, i: (b, i, off // width))
    const = lambda shape: pl.BlockSpec(shape, lambda b, i: (0, 0))
    return pl.pallas_call(
        _gla_kernel,
        grid=(bsz, s // GLA_BLOCK),
        in_specs=[col(GLA_QK_WIDTH, COL_QG), col(GLA_QK_WIDTH, COL_KG), col(GLA_WIDTH, COL_VG), col(ALR_W, COL_ALR),
                  col(GLA_WIDTH, COL_RG), const((ALR_W, GLA_QK_WIDTH)), const((1, GLA_QK_WIDTH)),
                  const((1, GLA_WIDTH)), const((GLA_BLOCK, GLA_BLOCK))],
        out_specs=pl.BlockSpec((None, GLA_BLOCK, GLA_WIDTH), lambda b, i: (b, i, 0)),
        out_shape=jax.ShapeDtypeStruct((bsz, s, GLA_WIDTH), jnp.float32),
        scratch_shapes=[pltpu.VMEM((GLA_HEADS // 2, GLA_DV, GLA_PAIR), jnp.float32),
                        pltpu.VMEM((GLA_BLOCK, GLA_QK_WIDTH), jnp.float32),
                        pltpu.VMEM((GLA_BLOCK, GLA_QK_WIDTH), jnp.float32)],
        compiler_params=pltpu.CompilerParams(dimension_semantics=("arbitrary", "arbitrary")),
        name="gla",
    )(proj, proj, proj, proj, proj, wa, b_alpha.reshape(1, -1), norm_g.reshape(1, -1), _gla_tri())


def peer_mixer(h, w_q, keys_1, keys_2, u_tab, v_tab):
    bsz, s, d = h.shape
    t = bsz * s
    hf = h.reshape(t, d)
    experts, gates = _peer_route(hf, w_q, keys_1, keys_2)
    return _peer_experts(hf, experts, gates, u_tab, v_tab).reshape(bsz, s, d)


ROUTE_TT = 256


def _top16_rows(s, code, big):
    vals, codes = [], []
    for _ in range(PEER_TOPK):
        m = jnp.max(s, axis=0, keepdims=True)
        c = jnp.min(jnp.where(s == m, code, big), axis=0, keepdims=True)
        s = jnp.where(code == c, NEG_INF, s)
        vals.append(m)
        codes.append(c)
    return vals, codes


def _peer_route_kernel(h_ref, wq_ref, k1_ref, k2_ref, e_ref, g_ref, q_ref):
    half = PEER_QUERY_DIM // 2
    tt = ROUTE_TT
    q_ref[...] = jnp.dot(h_ref[...].astype(jnp.bfloat16), wq_ref[...], preferred_element_type=jnp.float32)
    key_id = lax.broadcasted_iota(jnp.int32, (PEER_N_KEYS, 2 * tt), 0)
    sub8 = lax.broadcasted_iota(jnp.int32, (8, tt), 0)
    nt = (((1,), (1,)), ((), ()))

    def head_body(hd, carry):
        off = pl.multiple_of(hd * PEER_QUERY_DIM, PEER_QUERY_DIM)
        q1 = q_ref[:, pl.ds(off, half)].astype(jnp.bfloat16)
        q2 = q_ref[:, pl.ds(off + half, half)].astype(jnp.bfloat16)
        s1 = lax.dot_general(k1_ref[...], q1, nt, preferred_element_type=jnp.float32)
        s2 = lax.dot_general(k2_ref[...], q2, nt, preferred_element_type=jnp.float32)
        vals, ids = _top16_rows(jnp.concatenate([s1, s2], axis=1), key_id, PEER_N_KEYS)
        v = jnp.concatenate(vals, axis=0)
        ix = jnp.concatenate(ids, axis=0)
        v1, v2, i1, i2 = v[:, :tt], v[:, tt:], ix[:, :tt], ix[:, tt:]
        cand, code = [], []
        for a, b0 in [(0, 0), (0, 8), (1, 0), (2, 0), (3, 0), (4, 0), (5, 0), (6, 0), (7, 0)]:
            cand.append(v1[a:a + 1] + v2[b0:b0 + 8])
            flat = a * PEER_TOPK + b0 + sub8
            code.append((flat << 14) | (i1[a:a + 1] * PEER_N_KEYS + i2[b0:b0 + 8]))
        cand.append(v1[8:16] + v2[0:1])
        code.append((((8 + sub8) * PEER_TOPK) << 14) | (i1[8:16] * PEER_N_KEYS + i2[0:1]))
        sc, cc = _top16_rows(jnp.concatenate(cand, axis=0), jnp.concatenate(code, axis=0), 1 << 30)
        sc = jnp.concatenate(sc, axis=0)
        ex = jnp.exp(sc - sc[0:1])
        g_ref[hd] = ex / jnp.sum(ex, axis=0, keepdims=True)
        e_ref[hd] = jnp.concatenate(cc, axis=0) & (PEER_N_EXPERTS - 1)
        return carry

    lax.fori_loop(0, PEER_HEADS, head_body, 0)


def _peer_route(hf, w_q, keys_1, keys_2):
    t, d = hf.shape
    qw = PEER_HEADS * PEER_QUERY_DIM
    const = lambda shape: pl.BlockSpec(shape, lambda i: tuple(0 for _ in shape))
    out_spec = pl.BlockSpec((PEER_HEADS, PEER_TOPK, ROUTE_TT), lambda i: (0, 0, i))
    e_t, g_t = pl.pallas_call(
        _peer_route_kernel,
        grid=(t // ROUTE_TT,),
        in_specs=[pl.BlockSpec((ROUTE_TT, d), lambda i: (i, 0)), const((d, qw)),
                  const(keys_1.shape), const(keys_2.shape)],
        out_specs=[out_spec, out_spec],
        out_shape=[jax.ShapeDtypeStruct((PEER_HEADS, PEER_TOPK, t), jnp.int32),
                   jax.ShapeDtypeStruct((PEER_HEADS, PEER_TOPK, t), jnp.float32)],
        scratch_shapes=[pltpu.VMEM((ROUTE_TT, qw), jnp.float32)],
        compiler_params=pltpu.CompilerParams(dimension_semantics=("arbitrary",), vmem_limit_bytes=48 * 1024 * 1024),
        name="peer_route",
    )(hf, w_q.astype(jnp.bfloat16), keys_1.astype(jnp.bfloat16), keys_2.astype(jnp.bfloat16))
    to_rows = lambda a: a.reshape(PEER_HEADS * PEER_TOPK, t).T
    return to_rows(e_t), to_rows(g_t)


PEER_PAIRS = PEER_HEADS * PEER_TOPK
PEER_TB = 64
HALF_D = D_MODEL // 2
WORD_SUB, WORD_LANE = HALF_D // 128, 128
ROW_SUB = 2 * WORD_SUB
HI_MASK = -65536
IDX_WAYS = 8


def _pack_table(tab):
    b = lax.bitcast_convert_type(tab.astype(jnp.bfloat16), jnp.uint16).astype(jnp.uint32)
    word = b[:, :HALF_D] | (b[:, HALF_D:] << 16)
    return lax.bitcast_convert_type(word, jnp.int32).reshape(tab.shape[0] * WORD_SUB, WORD_LANE)


def _table_row(tab_ref, e4):
    return tab_ref[pl.ds(pl.multiple_of(e4, WORD_SUB), WORD_SUB), :]


_ORDER = (6, 2, 4, 0, 7, 3, 5, 1)


def _fold_pairs(ps, sub):
    b1 = (sub & 2) != 0
    w = []
    for i in range(2):
        x, y = ps[2 * i], ps[2 * i + 1]
        w.append(jnp.where(b1, x + pltpu.roll(x, 2, 0), y + pltpu.roll(y, 6, 0)))
    b0 = (sub & 1) != 0
    x, y = w
    return jnp.where(b0, x + pltpu.roll(x, 1, 0), y + pltpu.roll(y, 7, 0))


def _peer_u_kernel(*refs):
    idx_refs = refs[:IDX_WAYS]
    h_ref, tab_ref, a_ref = refs[IDX_WAYS:]
    sub = lax.broadcasted_iota(jnp.int32, (ROW_SUB, WORD_LANE), 0)
    lane = lax.broadcasted_iota(jnp.int32, (ROW_SUB, WORD_LANE), 1)
    n_groups = PEER_PAIRS // IDX_WAYS

    def folded_tiles(t):
        h = h_ref[t]
        h_lo = jnp.concatenate([h[:WORD_SUB], h[:WORD_SUB]], axis=0)
        h_hi = jnp.concatenate([h[WORD_SUB:], h[WORD_SUB:]], axis=0)
        ways = [r.at[pl.ds(t * n_groups, n_groups)] for r in idx_refs]
        tiles = []
        for g in range(n_groups):
            ps = []
            for i in range(4):
                w = jnp.concatenate([_table_row(tab_ref, ways[2 * i + j][g]) for j in range(2)], axis=0)
                lo = lax.bitcast_convert_type(w << 16, jnp.float32)
                hi = lax.bitcast_convert_type(w & HI_MASK, jnp.float32)
                ps.append(lo * h_lo + hi * h_hi)
            tiles.append(_fold_pairs(ps, sub))
        return tuple(tiles)

    def lane_sums(t, tiles, acc):
        for g in range(n_groups):
            acc = jnp.where(lane == (t % 8) * n_groups + g, jnp.sum(tiles[g], axis=1, keepdims=True), acc)
        a_ref[t // 8] = acc
        return acc

    def tok_body(t, carry):
        acc, prev = carry
        return lane_sums(t - 1, prev, acc), folded_tiles(t)

    acc, last = lax.fori_loop(1, PEER_TB, tok_body,
                              (jnp.zeros((ROW_SUB, WORD_LANE), jnp.float32), folded_tiles(0)))
    lane_sums(PEER_TB - 1, last, acc)


def _split3_lanes(x, tb):
    mask = lambda v: lax.bitcast_convert_type(lax.bitcast_convert_type(v, jnp.int32) & HI_MASK, jnp.float32)
    hi = mask(x)
    mid = mask(x - hi)
    lo = x - hi - mid
    part = lax.broadcasted_iota(jnp.int32, x.shape, 1) // tb
    return jnp.where(part == 0, hi, jnp.where(part == 1, mid, jnp.where(part == 2, lo, 0.0))).astype(jnp.bfloat16)


def _peer_v_kernel(*refs):
    idx_refs = refs[:IDX_WAYS]
    a_ref, g_ref, tab_ref, sel_ref, o_ref, wrep_ref = refs[IDX_WAYS:]
    a = a_ref[...]
    w = g_ref[...] * (0.5 * a * (1.0 + lax.erf(a * (2.0 ** -0.5))))
    wcat = _split3_lanes(jnp.concatenate([w, w, w, w], axis=0).T, PEER_TB)
    wrep_ref[...] = jnp.dot(wcat, sel_ref[...], preferred_element_type=jnp.float32)
    upper = lax.broadcasted_iota(jnp.int32, (ROW_SUB, WORD_LANE), 0) < WORD_SUB
    per_way = PEER_PAIRS // IDX_WAYS

    def tok_body(t, carry):
        col = pl.multiple_of(t * WORD_LANE, WORD_LANE)
        ways = [r.at[pl.ds(t * per_way, per_way)] for r in idx_refs]
        z = jnp.zeros((ROW_SUB, WORD_LANE), jnp.float32)
        acc_lo, acc_hi = [z, z], [z, z]
        for m in range(per_way):
            for jj in range(IDX_WAYS // 2):
                p = m * IDX_WAYS + 2 * jj
                words = jnp.concatenate([_table_row(tab_ref, ways[2 * jj + j][m]) for j in range(2)], axis=0)
                w2 = jnp.where(upper, wrep_ref[pl.ds(p, 1), pl.ds(col, WORD_LANE)],
                               wrep_ref[pl.ds(p + 1, 1), pl.ds(col, WORD_LANE)])
                k = jj % 2
                acc_lo[k] = acc_lo[k] + lax.bitcast_convert_type(words << 16, jnp.float32) * w2
                acc_hi[k] = acc_hi[k] + lax.bitcast_convert_type(words & HI_MASK, jnp.float32) * w2
        lo, hi = acc_lo[0] + acc_lo[1], acc_hi[0] + acc_hi[1]
        o_ref[t] = jnp.concatenate([lo[:WORD_SUB] + lo[WORD_SUB:], hi[:WORD_SUB] + hi[WORD_SUB:]], axis=0)
        return carry

    lax.fori_loop(0, PEER_TB, tok_body, 0)


_PEER_VMEM_LIMIT = 56 * 1024 * 1024


def _peer_experts(hf, experts, gates, u_tab, v_tab):
    t, d = hf.shape
    nblk = t // PEER_TB
    smem_spec = pl.BlockSpec((PEER_TB * PEER_PAIRS // IDX_WAYS,), lambda i: (i,), memory_space=pltpu.SMEM)
    tab_spec = pl.BlockSpec(memory_space=pltpu.VMEM)
    tok_spec = pl.BlockSpec((PEER_TB, PEER_PAIRS), lambda i: (i, 0))
    row_spec = pl.BlockSpec((PEER_TB, ROW_SUB, WORD_LANE), lambda i: (i, 0, 0))
    params = pltpu.CompilerParams(dimension_semantics=("arbitrary",), vmem_limit_bytes=_PEER_VMEM_LIMIT)
    idx = experts.astype(jnp.int32) * WORD_SUB
    idx_ways = [idx[:, j::IDX_WAYS].reshape(-1) for j in range(IDX_WAYS)]
    a_tiles = pl.pallas_call(
        _peer_u_kernel,
        grid=(nblk,),
        in_specs=[smem_spec] * IDX_WAYS + [row_spec, tab_spec],
        out_specs=pl.BlockSpec((PEER_TB // 8, ROW_SUB, WORD_LANE), lambda i: (i, 0, 0)),
        out_shape=jax.ShapeDtypeStruct((t // 8, ROW_SUB, WORD_LANE), jnp.float32),
        compiler_params=params, name="peer_u",
    )(*idx_ways, hf.reshape(t, ROW_SUB, WORD_LANE), _pack_table(u_tab))
    inv = np.argsort(np.array(_ORDER))
    a = a_tiles.reshape(t // 8, 8, 8, PEER_PAIRS // 8)[:, inv].transpose(0, 2, 3, 1).reshape(t, PEER_PAIRS)
    eye = np.repeat(np.eye(PEER_TB, dtype=np.float32), WORD_LANE, axis=1)
    sel = jnp.asarray(np.concatenate([eye, eye, eye, np.zeros_like(eye)], axis=0), jnp.bfloat16)
    out = pl.pallas_call(
        _peer_v_kernel,
        grid=(nblk,),
        in_specs=[smem_spec] * IDX_WAYS + [tok_spec, tok_spec, tab_spec, pl.BlockSpec(sel.shape, lambda i: (0, 0))],
        out_specs=row_spec,
        out_shape=jax.ShapeDtypeStruct((t, ROW_SUB, WORD_LANE), jnp.float32),
        scratch_shapes=[pltpu.VMEM((PEER_PAIRS, PEER_TB * WORD_LANE), jnp.float32)],
        compiler_params=params, name="peer_v",
    )(*idx_ways, a, gates, _pack_table(v_tab), sel)
    return out.reshape(t, d)


def _final_kernel(x_ref, p_ref, gate_ref, g_ref, o_ref):
    o_ref[...] = _rms(x_ref[...] + gate_ref[...] * p_ref[...]) * g_ref[...]


def _final(x1, peer, gate2, g):
    bsz, s, d = x1.shape
    row = pl.BlockSpec((None, ROW_TILE, d), lambda b, i: (b, i, 0))
    return pl.pallas_call(
        _final_kernel,
        grid=(bsz, s // ROW_TILE),
        in_specs=[row, row, pl.BlockSpec((None, 1, d), lambda b, i: (b, 0, 0)), pl.BlockSpec((1, d), lambda b, i: (0, 0))],
        out_specs=row,
        out_shape=jax.ShapeDtypeStruct((bsz, s, d), jnp.float32), name="final",
    )(x1, peer, gate2, g.reshape(1, d))


def _in_proj_weights(w_in):
    o = np.cumsum([0, GLA_QK_WIDTH, GLA_QK_WIDTH, GLA_WIDTH, GLA_GATE_RANK, GLA_WIDTH, SWA_WIDTH, SWA_KV_WIDTH])
    src = np.full((PROJ_W,), IN_WIDTH, np.int32)
    src[COL_QS + _wide_cols()] = o[5] + np.arange(SWA_WIDTH)
    for dst, start, width in ((COL_QG, o[0], GLA_QK_WIDTH), (COL_KG, o[1], GLA_QK_WIDTH), (COL_VG, o[2], GLA_WIDTH),
                              (COL_ALR, o[3], GLA_GATE_RANK), (COL_RG, o[4], GLA_WIDTH),
                              (COL_KS, o[6], SWA_KV_WIDTH), (COL_VS, o[7], SWA_KV_WIDTH)):
        src[dst:dst + width] = start + np.arange(width)
    w_ext = jnp.concatenate([w_in, jnp.zeros((w_in.shape[0], 1), w_in.dtype)], axis=1)
    return w_ext[:, src].astype(jnp.bfloat16)


def _out_proj_weights(w_out):
    src = np.full((MIX_W,), MIX_WIDTH, np.int32)
    src[:GLA_WIDTH] = np.arange(GLA_WIDTH)
    src[GLA_WIDTH + _wide_cols()] = GLA_WIDTH + np.arange(SWA_WIDTH)
    w_ext = jnp.concatenate([w_out, jnp.zeros((1, w_out.shape[1]), w_out.dtype)], axis=0)
    return w_ext[src].astype(jnp.bfloat16)


def kernel(x, c, w_ada, b_ada, norm1_g, norm2_g, w_in, w_gla_alpha, b_gla_alpha, gla_norm_g,
           swa_sinks, swa_norm_g, w_out, w_peer_q, peer_keys_1, peer_keys_2, peer_u, peer_v, final_g):
    assert w_ada.shape[0] == 1, "single-layer block"
    l = 0
    mod = _ada(c, w_ada[l], b_ada[l])
    shift1, scale1, gate1, shift2, scale2, gate2 = [m[:, None, :] for m in jnp.split(mod, 6, axis=-1)]
    proj = _in_proj(x, norm1_g[l], shift1, scale1, _in_proj_weights(w_in[l]))
    o_gla = _gla(proj, w_gla_alpha[l], b_gla_alpha[l], gla_norm_g[l])
    g_wide = jnp.zeros((1, SWA_QW), jnp.float32).at[0, _wide_cols()].set(swa_norm_g[l])
    o_swa = _swa(proj, swa_sinks[l], g_wide)
    x1, h2 = _out_proj(x, o_gla, o_swa, _out_proj_weights(w_out[l]), gate1, norm2_g[l], shift2, scale2)
    peer = peer_mixer(h2, w_peer_q[l], peer_keys_1[l], peer_keys_2[l], peer_u[l], peer_v[l])
    return _final(x1, peer, gate2, final_g)
```

```python
import jax, jax.numpy as jnp
from jax import lax
import numpy as np
from jax.experimental import pallas as pl
from jax.experimental.pallas import tpu as pltpu

D_MODEL = 1024
BATCH = 4
SEQ = 8192
DEPTH = 1

GLA_HEADS = 4
GLA_DK = 64
GLA_DV = 128
GLA_GATE_RANK = 16
GLA_GATE_TEMP = 16.0
GLA_CHUNK = 64
SWA_HEADS = 8
SWA_KV_HEADS = 2
SWA_HEAD_DIM = 64
SWA_WINDOW = 128
SWA_BLOCK = 128
PEER_HEADS = 8
PEER_N_KEYS = 128
PEER_N_EXPERTS = PEER_N_KEYS * PEER_N_KEYS
PEER_QUERY_DIM = 256
PEER_TOPK = 16
RMS_EPS = 1e-6

GLA_QK_WIDTH = GLA_HEADS * GLA_DK
GLA_WIDTH = GLA_HEADS * GLA_DV
SWA_WIDTH = SWA_HEADS * SWA_HEAD_DIM
SWA_KV_WIDTH = SWA_KV_HEADS * SWA_HEAD_DIM
MIX_WIDTH = GLA_WIDTH + SWA_WIDTH
IN_WIDTH = 2 * GLA_QK_WIDTH + GLA_WIDTH + GLA_GATE_RANK + GLA_WIDTH + SWA_WIDTH + 2 * SWA_KV_WIDTH


LANES = 128
NEG_INF = float("-inf")
NT_DIMS = (((1,), (1,)), ((), ()))
ROW_TILE = 512
WIDE = 2 * SWA_HEAD_DIM
SWA_QW = SWA_HEADS * WIDE
ALR_W = LANES
COL_QS, COL_QG, COL_KG, COL_VG, COL_RG = 0, SWA_QW, SWA_QW + GLA_QK_WIDTH, SWA_QW + 2 * GLA_QK_WIDTH, SWA_QW + 2 * GLA_QK_WIDTH + GLA_WIDTH
COL_KS = COL_RG + GLA_WIDTH
COL_VS = COL_KS + SWA_KV_WIDTH
COL_ALR = COL_VS + SWA_KV_WIDTH
PROJ_W = COL_ALR + ALR_W
MIX_W = GLA_WIDTH + SWA_QW


def _wide_cols():
    grp = SWA_HEADS // SWA_KV_HEADS
    h = np.arange(SWA_WIDTH) // SWA_HEAD_DIM
    return h * WIDE + (h // grp) * SWA_HEAD_DIM + np.arange(SWA_WIDTH) % SWA_HEAD_DIM


def _rms(x):
    return x * lax.rsqrt(jnp.mean(x * x, axis=-1, keepdims=True) + RMS_EPS)


def _ada_kernel(c_ref, w_ref, b_ref, o_ref):
    c = c_ref[...]
    ca = (c * jax.nn.sigmoid(c)).astype(jnp.bfloat16)
    o_ref[...] = jnp.dot(ca, w_ref[...].astype(jnp.bfloat16), preferred_element_type=jnp.float32) + b_ref[...]


def _ada(c, w, b):
    bsz, d = c.shape
    n = w.shape[1]
    tn = 1024
    return pl.pallas_call(
        _ada_kernel,
        grid=(n // tn,),
        in_specs=[pl.BlockSpec((bsz, d), lambda j: (0, 0)), pl.BlockSpec((d, tn), lambda j: (0, j)),
                  pl.BlockSpec((1, tn), lambda j: (0, j))],
        out_specs=pl.BlockSpec((bsz, tn), lambda j: (0, j)),
        out_shape=jax.ShapeDtypeStruct((bsz, n), jnp.float32), name="ada",
    )(c, w, b.reshape(1, n))


def _in_proj_kernel(x_ref, g_ref, sh_ref, sc_ref, w_ref, o_ref):
    h = _rms(x_ref[...]) * g_ref[...] * (1.0 + sc_ref[...]) + sh_ref[...]
    o_ref[...] = jnp.dot(h.astype(jnp.bfloat16), w_ref[...], preferred_element_type=jnp.float32)


def _in_proj(x, g, shift, scale, w):
    bsz, s, d = x.shape
    n = w.shape[1]
    row = lambda width: pl.BlockSpec((None, ROW_TILE, width), lambda b, i: (b, i, 0))
    per_b = pl.BlockSpec((None, 1, d), lambda b, i: (b, 0, 0))
    return pl.pallas_call(
        _in_proj_kernel,
        grid=(bsz, s // ROW_TILE),
        in_specs=[row(d), pl.BlockSpec((1, d), lambda b, i: (0, 0)), per_b, per_b,
                  pl.BlockSpec((d, n), lambda b, i: (0, 0))],
        out_specs=row(n),
        out_shape=jax.ShapeDtypeStruct((bsz, s, n), jnp.float32),
        compiler_params=pltpu.CompilerParams(dimension_semantics=("arbitrary", "arbitrary"),
                                             vmem_limit_bytes=48 * 1024 * 1024),
        name="in_proj",
    )(x, g.reshape(1, d), shift, scale, w)


def _swa_kernel(sink_ref, q_ref, kp_ref, kc_ref, vp_ref, vc_ref, g_ref, o_ref):
    f32, bf16 = jnp.float32, jnp.bfloat16
    blk = SWA_BLOCK
    k2 = jnp.concatenate([kp_ref[...], kc_ref[...]], axis=0).astype(bf16)
    v2 = jnp.concatenate([vp_ref[...], vc_ref[...]], axis=0).astype(bf16)
    qi = lax.broadcasted_iota(jnp.int32, (blk, 2 * blk), 0)
    kj = lax.broadcasted_iota(jnp.int32, (blk, 2 * blk), 1)
    dist = qi + blk - kj
    valid = (dist >= 0) & (dist < SWA_WINDOW) & ((kj >= blk) | (pl.program_id(1) > 0))
    distf = dist.astype(f32)
    lane_kv = lax.broadcasted_iota(jnp.int32, (blk, WIDE), 1) // SWA_HEAD_DIM
    grp = SWA_HEADS // SWA_KV_HEADS
    outs, ssq = [], jnp.zeros((blk, 1), f32)
    for h in range(SWA_HEADS):
        q = (q_ref[:, h * WIDE:(h + 1) * WIDE] * (SWA_HEAD_DIM ** -0.5)).astype(bf16)
        sc = lax.dot_general(q, k2, NT_DIMS, preferred_element_type=f32)
        sc = sc - (2.0 ** (-8.0 * (h + 1) / SWA_HEADS)) * distf
        sc = jnp.where(valid, sc, NEG_INF)
        sink = sink_ref[h]
        m = jnp.maximum(jnp.max(sc, axis=-1, keepdims=True), sink)
        p = jnp.exp(sc - m)
        p = p / (jnp.sum(p, axis=-1, keepdims=True) + jnp.exp(sink - m))
        o = jnp.dot(p.astype(bf16), v2, preferred_element_type=f32)
        o = jnp.where(lane_kv == h // grp, o, 0.0)
        ssq = ssq + jnp.sum(o * o, axis=-1, keepdims=True)
        outs.append(o)
    inv = lax.rsqrt(ssq * (1.0 / SWA_WIDTH) + RMS_EPS)
    for h in range(SWA_HEADS):
        o_ref[:, h * WIDE:(h + 1) * WIDE] = outs[h] * inv * g_ref[:, h * WIDE:(h + 1) * WIDE]


def _swa(proj, sinks, norm_g_wide):
    bsz, s, _ = proj.shape
    col = lambda width, off, prev: pl.BlockSpec(
        (None, SWA_BLOCK, width), lambda b, n: (b, jnp.maximum(n - 1, 0) if prev else n, off // width))
    return pl.pallas_call(
        _swa_kernel,
        grid=(bsz, s // SWA_BLOCK),
        in_specs=[pl.BlockSpec(memory_space=pltpu.SMEM), col(SWA_QW, COL_QS, False),
                  col(SWA_KV_WIDTH, COL_KS, True), col(SWA_KV_WIDTH, COL_KS, False),
                  col(SWA_KV_WIDTH, COL_VS, True), col(SWA_KV_WIDTH, COL_VS, False),
                  pl.BlockSpec((1, SWA_QW), lambda b, n: (0, 0))],
        out_specs=pl.BlockSpec((None, SWA_BLOCK, SWA_QW), lambda b, n: (b, n, 0)),
        out_shape=jax.ShapeDtypeStruct((bsz, s, SWA_QW), jnp.float32),
        compiler_params=pltpu.CompilerParams(dimension_semantics=("arbitrary", "arbitrary")),
        name="swa",
    )(sinks, proj, proj, proj, proj, proj, norm_g_wide)


def _out_proj_kernel(x_ref, og_ref, os_ref, w_ref, gate_ref, g2_ref, sh_ref, sc_ref, x1_ref, h2_ref):
    mix = jnp.concatenate([og_ref[...], os_ref[...]], axis=-1).astype(jnp.bfloat16)
    x1 = x_ref[...] + gate_ref[...] * jnp.dot(mix, w_ref[...], preferred_element_type=jnp.float32)
    x1_ref[...] = x1
    h2_ref[...] = _rms(x1) * g2_ref[...] * (1.0 + sc_ref[...]) + sh_ref[...]


def _out_proj(x, o_gla, o_swa, w, gate1, g2, shift2, scale2):
    bsz, s, d = x.shape
    row = lambda width: pl.BlockSpec((None, ROW_TILE, width), lambda b, i: (b, i, 0))
    per_b = pl.BlockSpec((None, 1, d), lambda b, i: (b, 0, 0))
    shp = jax.ShapeDtypeStruct((bsz, s, d), jnp.float32)
    return pl.pallas_call(
        _out_proj_kernel,
        grid=(bsz, s // ROW_TILE),
        in_specs=[row(d), row(GLA_WIDTH), row(SWA_QW), pl.BlockSpec((MIX_W, d), lambda b, i: (0, 0)),
                  per_b, pl.BlockSpec((1, d), lambda b, i: (0, 0)), per_b, per_b],
        out_specs=[row(d), row(d)],
        out_shape=[shp, shp],
        compiler_params=pltpu.CompilerParams(dimension_semantics=("arbitrary", "arbitrary"),
                                             vmem_limit_bytes=48 * 1024 * 1024),
        name="out_proj",
    )(x, o_gla, o_swa, w, gate1, g2.reshape(1, d), shift2, scale2)


GLA_SUB = 16
GLA_BLOCK = 128
GLA_PAIR = 2 * GLA_DK


def _gla_kernel(q_ref, k_ref, v_ref, alr_ref, r_ref, wa_ref, ba_ref, g_ref, tri_ref, o_ref, st_ref, b_ref, q2_ref):
    f32, bf16 = jnp.float32, jnp.bfloat16

    @pl.when(pl.program_id(1) == 0)
    def _():
        st_ref[...] = jnp.zeros(st_ref.shape, f32)

    z = jnp.dot(alr_ref[...].astype(bf16), wa_ref[...], preferred_element_type=f32) + ba_ref[...]
    log_a = -(jnp.maximum(-z, 0.0) + jnp.log1p(jnp.exp(-jnp.abs(z)))) * (1.0 / GLA_GATE_TEMP)
    b_ref[...] = jnp.dot(tri_ref[...], log_a, preferred_element_type=f32, precision=lax.Precision.HIGHEST)
    q2_ref[...] = q_ref[...] * (GLA_DK ** -0.5)
    lane = lax.broadcasted_iota(jnp.int32, (GLA_SUB, GLA_PAIR), 1)
    sub = lax.broadcasted_iota(jnp.int32, (GLA_SUB, GLA_PAIR), 0)
    first = lane < GLA_DK
    lane_s = lax.broadcasted_iota(jnp.int32, (GLA_DV, GLA_PAIR), 1) < GLA_DK

    for hp in range(GLA_HEADS // 2):
        cols = pl.ds(hp * GLA_PAIR, GLA_PAIR)
        state = st_ref[hp]
        for c in range(GLA_BLOCK // GLA_SUB):
            rows = pl.ds(c * GLA_SUB, GLA_SUB)
            b = b_ref[rows, cols]
            b_last = b_ref[pl.ds(c * GLA_SUB + GLA_SUB - 1, 1), cols]
            q2, k2 = q2_ref[rows, cols], k_ref[rows, cols]
            qe = q2 * jnp.exp(b)
            kd = (k2 * jnp.exp(b_last - b)).astype(bf16)
            sb = state.astype(bf16)
            cols_a, cols_b = [], []
            for t in range(GLA_SUB):
                bt = b_ref[pl.ds(c * GLA_SUB + t, 1), cols]
                qt = q2_ref[pl.ds(c * GLA_SUB + t, 1), cols]
                p = (qt * k2) * jnp.exp(bt - b)
                keep = sub[:, :1] <= t
                cols_a.append(jnp.where(keep, jnp.sum(jnp.where(first, p, 0.0), axis=1, keepdims=True), 0.0))
                cols_b.append(jnp.where(keep, jnp.sum(jnp.where(first, 0.0, p), axis=1, keepdims=True), 0.0))
            ut = []
            for j, sc_cols in enumerate((cols_a, cols_b)):
                head = 2 * hp + j
                vcols = pl.ds(head * GLA_DV, GLA_DV)
                vh = v_ref[rows, vcols]
                qm = jnp.where(first if j == 0 else ~first, qe, 0.0).astype(bf16)
                o = lax.dot_general(qm, sb, NT_DIMS, preferred_element_type=f32)
                o = o + jnp.concatenate([jnp.sum(cj * vh, axis=0, keepdims=True) for cj in sc_cols], axis=0)
                ms = jnp.mean(o * o, axis=1, keepdims=True)
                r = r_ref[rows, vcols]
                o_ref[rows, vcols] = (o * lax.rsqrt(ms + RMS_EPS) * g_ref[:, vcols]) * (r * jax.nn.sigmoid(r))
                ut.append(jnp.dot(vh.T.astype(bf16), kd, preferred_element_type=f32))
            state = jnp.exp(b_last) * state + jnp.where(lane_s, ut[0], ut[1])
        st_ref[hp] = state


def _gla_tri():
    r = np.arange(GLA_BLOCK)
    return jnp.asarray(((r[:, None] // GLA_SUB == r[None, :] // GLA_SUB) & (r[None, :] <= r[:, None])).astype(np.float32))


def _gla(proj, w_alpha, b_alpha, norm_g):
    bsz, s, _ = proj.shape
    rank = w_alpha.shape[0]
    wa = jnp.pad(w_alpha, ((0, ALR_W - rank), (0, 0))).astype(jnp.bfloat16)
    col = lambda width, off: pl.BlockSpec((None, GLA_BLOCK, width), lambda b, i: (b, i, off // width))
    const = lambda shape: pl.BlockSpec(shape, lambda b, i: (0, 0))
    return pl.pallas_call(
        _gla_kernel,
        grid=(bsz, s // GLA_BLOCK),
        in_specs=[col(GLA_QK_WIDTH, COL_QG), col(GLA_QK_WIDTH, COL_KG), col(GLA_WIDTH, COL_VG), col(ALR_W, COL_ALR),
                  col(GLA_WIDTH, COL_RG), const((ALR_W, GLA_QK_WIDTH)), const((1, GLA_QK_WIDTH)),
                  const((1, GLA_WIDTH)), const((GLA_BLOCK, GLA_BLOCK))],
        out_specs=pl.BlockSpec((None, GLA_BLOCK, GLA_WIDTH), lambda b, i: (b, i, 0)),
        out_shape=jax.ShapeDtypeStruct((bsz, s, GLA_WIDTH), jnp.float32),
        scratch_shapes=[pltpu.VMEM((GLA_HEADS // 2, GLA_DV, GLA_PAIR), jnp.float32),
                        pltpu.VMEM((GLA_BLOCK, GLA_QK_WIDTH), jnp.float32),
                        pltpu.VMEM((GLA_BLOCK, GLA_QK_WIDTH), jnp.float32)],
        compiler_params=pltpu.CompilerParams(dimension_semantics=("arbitrary", "arbitrary")),
        name="gla",
    )(proj, proj, proj, proj, proj, wa, b_alpha.reshape(1, -1), norm_g.reshape(1, -1), _gla_tri())


def peer_mixer(h, w_q, keys_1, keys_2, u_tab, v_tab):
    bsz, s, d = h.shape
    t = bsz * s
    hf = h.reshape(t, d)
    experts, gates = _peer_route(hf, w_q, keys_1, keys_2)
    return _peer_experts(hf, experts, gates, u_tab, v_tab).reshape(bsz, s, d)


ROUTE_TT = 256


def _top16_rows(s, code, big):
    vals, codes = [], []
    for _ in range(PEER_TOPK):
        m = jnp.max(s, axis=0, keepdims=True)
        c = jnp.min(jnp.where(s == m, code, big), axis=0, keepdims=True)
        s = jnp.where(code == c, NEG_INF, s)
        vals.append(m)
        codes.append(c)
    return vals, codes


def _peer_route_kernel(h_ref, wq_ref, k1_ref, k2_ref, e_ref, g_ref, q_ref):
    half = PEER_QUERY_DIM // 2
    tt = ROUTE_TT
    q_ref[...] = jnp.dot(h_ref[...].astype(jnp.bfloat16), wq_ref[...], preferred_element_type=jnp.float32)
    key_id = lax.broadcasted_iota(jnp.int32, (PEER_N_KEYS, 2 * tt), 0).astype(jnp.float32)
    sub8 = lax.broadcasted_iota(jnp.int32, (8, tt), 0)
    nt = (((1,), (1,)), ((), ()))

    def head_body(hd, carry):
        off = pl.multiple_of(hd * PEER_QUERY_DIM, PEER_QUERY_DIM)
        q1 = q_ref[:, pl.ds(off, half)].astype(jnp.bfloat16)
        q2 = q_ref[:, pl.ds(off + half, half)].astype(jnp.bfloat16)
        s1 = lax.dot_general(k1_ref[...], q1, nt, preferred_element_type=jnp.float32)
        s2 = lax.dot_general(k2_ref[...], q2, nt, preferred_element_type=jnp.float32)
        vals, ids = _top16_rows(jnp.concatenate([s1, s2], axis=1), key_id, float(PEER_N_KEYS))
        v = jnp.concatenate(vals, axis=0)
        ix = jnp.concatenate(ids, axis=0).astype(jnp.int32)
        v1, v2, i1, i2 = v[:, :tt], v[:, tt:], ix[:, :tt], ix[:, tt:]
        cand, code = [], []
        for a, b0 in [(0, 0), (0, 8), (1, 0), (2, 0), (3, 0), (4, 0), (5, 0), (6, 0), (7, 0)]:
            cand.append(v1[a:a + 1] + v2[b0:b0 + 8])
            flat = a * PEER_TOPK + b0 + sub8
            code.append((flat << 14) | (i1[a:a + 1] * PEER_N_KEYS + i2[b0:b0 + 8]))
        cand.append(v1[8:16] + v2[0:1])
        code.append((((8 + sub8) * PEER_TOPK) << 14) | (i1[8:16] * PEER_N_KEYS + i2[0:1]))
        sc, cc = _top16_rows(jnp.concatenate(cand, axis=0), jnp.concatenate(code, axis=0).astype(jnp.float32),
                             float(1 << 30))
        sc = jnp.concatenate(sc, axis=0)
        ex = jnp.exp(sc - sc[0:1])
        g_ref[hd] = ex / jnp.sum(ex, axis=0, keepdims=True)
        e_ref[hd] = jnp.concatenate(cc, axis=0).astype(jnp.int32) & (PEER_N_EXPERTS - 1)
        return carry

    lax.fori_loop(0, PEER_HEADS, head_body, 0)


def _peer_route(hf, w_q, keys_1, keys_2):
    t, d = hf.shape
    qw = PEER_HEADS * PEER_QUERY_DIM
    const = lambda shape: pl.BlockSpec(shape, lambda i: tuple(0 for _ in shape))
    out_spec = pl.BlockSpec((PEER_HEADS, PEER_TOPK, ROUTE_TT), lambda i: (0, 0, i))
    e_t, g_t = pl.pallas_call(
        _peer_route_kernel,
        grid=(t // ROUTE_TT,),
        in_specs=[pl.BlockSpec((ROUTE_TT, d), lambda i: (i, 0)), const((d, qw)),
                  const(keys_1.shape), const(keys_2.shape)],
        out_specs=[out_spec, out_spec],
        out_shape=[jax.ShapeDtypeStruct((PEER_HEADS, PEER_TOPK, t), jnp.int32),
                   jax.ShapeDtypeStruct((PEER_HEADS, PEER_TOPK, t), jnp.float32)],
        scratch_shapes=[pltpu.VMEM((ROUTE_TT, qw), jnp.float32)],
        compiler_params=pltpu.CompilerParams(dimension_semantics=("arbitrary",), vmem_limit_bytes=48 * 1024 * 1024),
        name="peer_route",
    )(hf, w_q.astype(jnp.bfloat16), keys_1.astype(jnp.bfloat16), keys_2.astype(jnp.bfloat16))
    return e_t, g_t


PEER_PAIRS = PEER_HEADS * PEER_TOPK
PEER_TB = 64
HALF_D = D_MODEL // 2
WORD_SUB, WORD_LANE = HALF_D // 128, 128
ROW_SUB = 2 * WORD_SUB
HI_MASK = -65536
IDX_WAYS = 8


def _pack_table(tab):
    b = lax.bitcast_convert_type(tab.astype(jnp.bfloat16), jnp.uint16).astype(jnp.uint32)
    word = b[:, :HALF_D] | (b[:, HALF_D:] << 16)
    return lax.bitcast_convert_type(word, jnp.int32).reshape(tab.shape[0] * WORD_SUB, WORD_LANE)


def _table_row(tab_ref, e4):
    return tab_ref[pl.ds(pl.multiple_of(e4, WORD_SUB), WORD_SUB), :]


_ORDER = (6, 2, 4, 0, 7, 3, 5, 1)


def _fold_pairs(ps, sub):
    b1 = (sub & 2) != 0
    w = []
    for i in range(2):
        x, y = ps[2 * i], ps[2 * i + 1]
        w.append(jnp.where(b1, x + pltpu.roll(x, 2, 0), y + pltpu.roll(y, 6, 0)))
    b0 = (sub & 1) != 0
    x, y = w
    return jnp.where(b0, x + pltpu.roll(x, 1, 0), y + pltpu.roll(y, 7, 0))


def _peer_u_kernel(*refs):
    idx_refs = refs[:IDX_WAYS]
    h_ref, tab_ref, a_ref = refs[IDX_WAYS:]
    sub = lax.broadcasted_iota(jnp.int32, (ROW_SUB, WORD_LANE), 0)
    lane = lax.broadcasted_iota(jnp.int32, (ROW_SUB, WORD_LANE), 1)
    n_groups = PEER_PAIRS // IDX_WAYS

    def folded_tiles(t):
        h = h_ref[t]
        h_lo = jnp.concatenate([h[:WORD_SUB], h[:WORD_SUB]], axis=0)
        h_hi = jnp.concatenate([h[WORD_SUB:], h[WORD_SUB:]], axis=0)
        ways = [r.at[pl.ds(t * n_groups, n_groups)] for r in idx_refs]
        tiles = []
        for g in range(n_groups):
            ps = []
            for i in range(4):
                w = jnp.concatenate([_table_row(tab_ref, ways[2 * i + j][g]) for j in range(2)], axis=0)
                lo = lax.bitcast_convert_type(w << 16, jnp.float32)
                hi = lax.bitcast_convert_type(w & HI_MASK, jnp.float32)
                ps.append(lo * h_lo + hi * h_hi)
            tiles.append(_fold_pairs(ps, sub))
        return tuple(tiles)

    def lane_sums(t, tiles, acc):
        for g in range(n_groups):
            acc = jnp.where(lane == (t % 8) * n_groups + g, jnp.sum(tiles[g], axis=1, keepdims=True), acc)
        a_ref[t // 8] = acc
        return acc

    def tok_body(t, carry):
        acc, prev = carry
        return lane_sums(t - 1, prev, acc), folded_tiles(t)

    acc, last = lax.fori_loop(1, PEER_TB, tok_body,
                              (jnp.zeros((ROW_SUB, WORD_LANE), jnp.float32), folded_tiles(0)))
    lane_sums(PEER_TB - 1, last, acc)


def _split3_lanes(x, tb):
    mask = lambda v: lax.bitcast_convert_type(lax.bitcast_convert_type(v, jnp.int32) & HI_MASK, jnp.float32)
    hi = mask(x)
    mid = mask(x - hi)
    lo = x - hi - mid
    part = lax.broadcasted_iota(jnp.int32, x.shape, 1) // tb
    return jnp.where(part == 0, hi, jnp.where(part == 1, mid, jnp.where(part == 2, lo, 0.0))).astype(jnp.bfloat16)


def _peer_v_kernel(*refs):
    idx_refs = refs[:IDX_WAYS]
    a_ref, g_ref, tab_ref, sel_ref, o_ref, wrep_ref = refs[IDX_WAYS:]
    a = a_ref[...]
    w = g_ref[...] * (0.5 * a * (1.0 + lax.erf(a * (2.0 ** -0.5))))
    wcat = _split3_lanes(jnp.concatenate([w, w, w, w], axis=0).T, PEER_TB)
    wrep_ref[...] = jnp.dot(wcat, sel_ref[...], preferred_element_type=jnp.float32)
    upper = lax.broadcasted_iota(jnp.int32, (ROW_SUB, WORD_LANE), 0) < WORD_SUB
    per_way = PEER_PAIRS // IDX_WAYS

    def tok_body(t, carry):
        col = pl.multiple_of(t * WORD_LANE, WORD_LANE)
        ways = [r.at[pl.ds(t * per_way, per_way)] for r in idx_refs]
        z = jnp.zeros((ROW_SUB, WORD_LANE), jnp.float32)
        acc_lo, acc_hi = [z, z], [z, z]
        for m in range(per_way):
            for jj in range(IDX_WAYS // 2):
                p = m * IDX_WAYS + 2 * jj
                words = jnp.concatenate([_table_row(tab_ref, ways[2 * jj + j][m]) for j in range(2)], axis=0)
                w2 = jnp.where(upper, wrep_ref[pl.ds(p, 1), pl.ds(col, WORD_LANE)],
                               wrep_ref[pl.ds(p + 1, 1), pl.ds(col, WORD_LANE)])
                k = jj % 2
                acc_lo[k] = acc_lo[k] + lax.bitcast_convert_type(words << 16, jnp.float32) * w2
                acc_hi[k] = acc_hi[k] + lax.bitcast_convert_type(words & HI_MASK, jnp.float32) * w2
        lo, hi = acc_lo[0] + acc_lo[1], acc_hi[0] + acc_hi[1]
        o_ref[t] = jnp.concatenate([lo[:WORD_SUB] + lo[WORD_SUB:], hi[:WORD_SUB] + hi[WORD_SUB:]], axis=0)
        return carry

    lax.fori_loop(0, PEER_TB, tok_body, 0)


_PEER_VMEM_LIMIT = 56 * 1024 * 1024


def _peer_experts(hf, experts, gates, u_tab, v_tab):
    t, d = hf.shape
    nblk = t // PEER_TB
    smem_spec = pl.BlockSpec((PEER_TB * PEER_PAIRS // IDX_WAYS,), lambda i: (i,), memory_space=pltpu.SMEM)
    tab_spec = pl.BlockSpec(memory_space=pltpu.VMEM)
    tok_spec = pl.BlockSpec((PEER_TB, PEER_PAIRS), lambda i: (i, 0))
    row_spec = pl.BlockSpec((PEER_TB, ROW_SUB, WORD_LANE), lambda i: (i, 0, 0))
    params = pltpu.CompilerParams(dimension_semantics=("arbitrary",), vmem_limit_bytes=_PEER_VMEM_LIMIT)
    assert IDX_WAYS == PEER_HEADS
    idx_ways = [(experts[j] * WORD_SUB).T.reshape(-1) for j in range(IDX_WAYS)]
    gates = gates.transpose(2, 1, 0).reshape(t, PEER_PAIRS)
    a_tiles = pl.pallas_call(
        _peer_u_kernel,
        grid=(nblk,),
        in_specs=[smem_spec] * IDX_WAYS + [row_spec, tab_spec],
        out_specs=pl.BlockSpec((PEER_TB // 8, ROW_SUB, WORD_LANE), lambda i: (i, 0, 0)),
        out_shape=jax.ShapeDtypeStruct((t // 8, ROW_SUB, WORD_LANE), jnp.float32),
        compiler_params=params, name="peer_u",
    )(*idx_ways, hf.reshape(t, ROW_SUB, WORD_LANE), _pack_table(u_tab))
    inv = np.argsort(np.array(_ORDER))
    a = a_tiles.reshape(t // 8, 8, 8, PEER_PAIRS // 8)[:, inv].transpose(0, 2, 3, 1).reshape(t, PEER_PAIRS)
    eye = np.repeat(np.eye(PEER_TB, dtype=np.float32), WORD_LANE, axis=1)
    sel = jnp.asarray(np.concatenate([eye, eye, eye, np.zeros_like(eye)], axis=0), jnp.bfloat16)
    out = pl.pallas_call(
        _peer_v_kernel,
        grid=(nblk,),
        in_specs=[smem_spec] * IDX_WAYS + [tok_spec, tok_spec, tab_spec, pl.BlockSpec(sel.shape, lambda i: (0, 0))],
        out_specs=row_spec,
        out_shape=jax.ShapeDtypeStruct((t, ROW_SUB, WORD_LANE), jnp.float32),
        scratch_shapes=[pltpu.VMEM((PEER_PAIRS, PEER_TB * WORD_LANE), jnp.float32)],
        compiler_params=params, name="peer_v",
    )(*idx_ways, a, gates, _pack_table(v_tab), sel)
    return out.reshape(t, d)


def _final_kernel(x_ref, p_ref, gate_ref, g_ref, o_ref):
    o_ref[...] = _rms(x_ref[...] + gate_ref[...] * p_ref[...]) * g_ref[...]


def _final(x1, peer, gate2, g):
    bsz, s, d = x1.shape
    row = pl.BlockSpec((None, ROW_TILE, d), lambda b, i: (b, i, 0))
    return pl.pallas_call(
        _final_kernel,
        grid=(bsz, s // ROW_TILE),
        in_specs=[row, row, pl.BlockSpec((None, 1, d), lambda b, i: (b, 0, 0)), pl.BlockSpec((1, d), lambda b, i: (0, 0))],
        out_specs=row,
        out_shape=jax.ShapeDtypeStruct((bsz, s, d), jnp.float32), name="final",
    )(x1, peer, gate2, g.reshape(1, d))


def _in_proj_weights(w_in):
    o = np.cumsum([0, GLA_QK_WIDTH, GLA_QK_WIDTH, GLA_WIDTH, GLA_GATE_RANK, GLA_WIDTH, SWA_WIDTH, SWA_KV_WIDTH])
    src = np.full((PROJ_W,), IN_WIDTH, np.int32)
    src[COL_QS + _wide_cols()] = o[5] + np.arange(SWA_WIDTH)
    for dst, start, width in ((COL_QG, o[0], GLA_QK_WIDTH), (COL_KG, o[1], GLA_QK_WIDTH), (COL_VG, o[2], GLA_WIDTH),
                              (COL_ALR, o[3], GLA_GATE_RANK), (COL_RG, o[4], GLA_WIDTH),
                              (COL_KS, o[6], SWA_KV_WIDTH), (COL_VS, o[7], SWA_KV_WIDTH)):
        src[dst:dst + width] = start + np.arange(width)
    w_ext = jnp.concatenate([w_in, jnp.zeros((w_in.shape[0], 1), w_in.dtype)], axis=1)
    return w_ext[:, src].astype(jnp.bfloat16)


def _out_proj_weights(w_out):
    src = np.full((MIX_W,), MIX_WIDTH, np.int32)
    src[:GLA_WIDTH] = np.arange(GLA_WIDTH)
    src[GLA_WIDTH + _wide_cols()] = GLA_WIDTH + np.arange(SWA_WIDTH)
    w_ext = jnp.concatenate([w_out, jnp.zeros((1, w_out.shape[1]), w_out.dtype)], axis=0)
    return w_ext[src].astype(jnp.bfloat16)


def kernel(x, c, w_ada, b_ada, norm1_g, norm2_g, w_in, w_gla_alpha, b_gla_alpha, gla_norm_g,
           swa_sinks, swa_norm_g, w_out, w_peer_q, peer_keys_1, peer_keys_2, peer_u, peer_v, final_g):
    assert w_ada.shape[0] == 1, "single-layer block"
    l = 0
    mod = _ada(c, w_ada[l], b_ada[l])
    shift1, scale1, gate1, shift2, scale2, gate2 = [m[:, None, :] for m in jnp.split(mod, 6, axis=-1)]
    proj = _in_proj(x, norm1_g[l], shift1, scale1, _in_proj_weights(w_in[l]))
    o_gla = _gla(proj, w_gla_alpha[l], b_gla_alpha[l], gla_norm_g[l])
    g_wide = jnp.zeros((1, SWA_QW), jnp.float32).at[0, _wide_cols()].set(swa_norm_g[l])
    o_swa = _swa(proj, swa_sinks[l], g_wide)
    x1, h2 = _out_proj(x, o_gla, o_swa, _out_proj_weights(w_out[l]), gate1, norm2_g[l], shift2, scale2)
    peer = peer_mixer(h2, w_peer_q[l], peer_keys_1[l], peer_keys_2[l], peer_u[l], peer_v[l])
    return _final(x1, peer, gate2, final_g)
```

```python
import jax, jax.numpy as jnp
from jax import lax
import numpy as np
from jax.experimental import pallas as pl
from jax.experimental.pallas import tpu as pltpu

D_MODEL = 1024
BATCH = 4
SEQ = 8192
DEPTH = 1

GLA_HEADS = 4
GLA_DK = 64
GLA_DV = 128
GLA_GATE_RANK = 16
GLA_GATE_TEMP = 16.0
GLA_CHUNK = 64
SWA_HEADS = 8
SWA_KV_HEADS = 2
SWA_HEAD_DIM = 64
SWA_WINDOW = 128
SWA_BLOCK = 128
PEER_HEADS = 8
PEER_N_KEYS = 128
PEER_N_EXPERTS = PEER_N_KEYS * PEER_N_KEYS
PEER_QUERY_DIM = 256
PEER_TOPK = 16
RMS_EPS = 1e-6

GLA_QK_WIDTH = GLA_HEADS * GLA_DK
GLA_WIDTH = GLA_HEADS * GLA_DV
SWA_WIDTH = SWA_HEADS * SWA_HEAD_DIM
SWA_KV_WIDTH = SWA_KV_HEADS * SWA_HEAD_DIM
MIX_WIDTH = GLA_WIDTH + SWA_WIDTH
IN_WIDTH = 2 * GLA_QK_WIDTH + GLA_WIDTH + GLA_GATE_RANK + GLA_WIDTH + SWA_WIDTH + 2 * SWA_KV_WIDTH


LANES = 128
NEG_INF = float("-inf")
NT_DIMS = (((1,), (1,)), ((), ()))
ROW_TILE = 512
WIDE = 2 * SWA_HEAD_DIM
SWA_QW = SWA_HEADS * WIDE
ALR_W = LANES
COL_QS, COL_QG, COL_KG, COL_VG, COL_RG = 0, SWA_QW, SWA_QW + GLA_QK_WIDTH, SWA_QW + 2 * GLA_QK_WIDTH, SWA_QW + 2 * GLA_QK_WIDTH + GLA_WIDTH
COL_KS = COL_RG + GLA_WIDTH
COL_VS = COL_KS + SWA_KV_WIDTH
COL_ALR = COL_VS + SWA_KV_WIDTH
PROJ_W = COL_ALR + ALR_W
MIX_W = GLA_WIDTH + SWA_QW


def _wide_cols():
    grp = SWA_HEADS // SWA_KV_HEADS
    h = np.arange(SWA_WIDTH) // SWA_HEAD_DIM
    return h * WIDE + (h // grp) * SWA_HEAD_DIM + np.arange(SWA_WIDTH) % SWA_HEAD_DIM


def _rms(x):
    return x * lax.rsqrt(jnp.mean(x * x, axis=-1, keepdims=True) + RMS_EPS)


def _ada_kernel(c_ref, w_ref, b_ref, o_ref):
    c = c_ref[...]
    ca = (c * jax.nn.sigmoid(c)).astype(jnp.bfloat16)
    o_ref[...] = jnp.dot(ca, w_ref[...].astype(jnp.bfloat16), preferred_element_type=jnp.float32) + b_ref[...]


def _ada(c, w, b):
    bsz, d = c.shape
    n = w.shape[1]
    tn = 1024
    return pl.pallas_call(
        _ada_kernel,
        grid=(n // tn,),
        in_specs=[pl.BlockSpec((bsz, d), lambda j: (0, 0)), pl.BlockSpec((d, tn), lambda j: (0, j)),
                  pl.BlockSpec((1, tn), lambda j: (0, j))],
        out_specs=pl.BlockSpec((bsz, tn), lambda j: (0, j)),
        out_shape=jax.ShapeDtypeStruct((bsz, n), jnp.float32), name="ada",
    )(c, w, b.reshape(1, n))


def _in_proj_kernel(x_ref, g_ref, sh_ref, sc_ref, w_ref, o_ref):
    h = _rms(x_ref[...]) * g_ref[...] * (1.0 + sc_ref[...]) + sh_ref[...]
    o_ref[...] = jnp.dot(h.astype(jnp.bfloat16), w_ref[...], preferred_element_type=jnp.float32)


def _in_proj(x, g, shift, scale, w):
    bsz, s, d = x.shape
    n = w.shape[1]
    row = lambda width: pl.BlockSpec((None, ROW_TILE, width), lambda b, i: (b, i, 0))
    per_b = pl.BlockSpec((None, 1, d), lambda b, i: (b, 0, 0))
    return pl.pallas_call(
        _in_proj_kernel,
        grid=(bsz, s // ROW_TILE),
        in_specs=[row(d), pl.BlockSpec((1, d), lambda b, i: (0, 0)), per_b, per_b,
                  pl.BlockSpec((d, n), lambda b, i: (0, 0))],
        out_specs=row(n),
        out_shape=jax.ShapeDtypeStruct((bsz, s, n), jnp.float32),
        compiler_params=pltpu.CompilerParams(dimension_semantics=("arbitrary", "arbitrary"),
                                             vmem_limit_bytes=48 * 1024 * 1024),
        name="in_proj",
    )(x, g.reshape(1, d), shift, scale, w)


def _swa_kernel(sink_ref, q_ref, kp_ref, kc_ref, vp_ref, vc_ref, g_ref, o_ref):
    f32, bf16 = jnp.float32, jnp.bfloat16
    blk = SWA_BLOCK
    k2 = jnp.concatenate([kp_ref[...], kc_ref[...]], axis=0).astype(bf16)
    v2 = jnp.concatenate([vp_ref[...], vc_ref[...]], axis=0).astype(bf16)
    qi = lax.broadcasted_iota(jnp.int32, (blk, 2 * blk), 0)
    kj = lax.broadcasted_iota(jnp.int32, (blk, 2 * blk), 1)
    dist = qi + blk - kj
    valid = (dist >= 0) & (dist < SWA_WINDOW) & ((kj >= blk) | (pl.program_id(1) > 0))
    distf = dist.astype(f32)
    lane_kv = lax.broadcasted_iota(jnp.int32, (blk, WIDE), 1) // SWA_HEAD_DIM
    grp = SWA_HEADS // SWA_KV_HEADS
    outs, ssq = [], jnp.zeros((blk, 1), f32)
    for h in range(SWA_HEADS):
        q = (q_ref[:, h * WIDE:(h + 1) * WIDE] * (SWA_HEAD_DIM ** -0.5)).astype(bf16)
        sc = lax.dot_general(q, k2, NT_DIMS, preferred_element_type=f32)
        sc = sc - (2.0 ** (-8.0 * (h + 1) / SWA_HEADS)) * distf
        sc = jnp.where(valid, sc, NEG_INF)
        sink = sink_ref[h]
        m = jnp.maximum(jnp.max(sc, axis=-1, keepdims=True), sink)
        p = jnp.exp(sc - m)
        p = p / (jnp.sum(p, axis=-1, keepdims=True) + jnp.exp(sink - m))
        o = jnp.dot(p.astype(bf16), v2, preferred_element_type=f32)
        o = jnp.where(lane_kv == h // grp, o, 0.0)
        ssq = ssq + jnp.sum(o * o, axis=-1, keepdims=True)
        outs.append(o)
    inv = lax.rsqrt(ssq * (1.0 / SWA_WIDTH) + RMS_EPS)
    for h in range(SWA_HEADS):
        o_ref[:, h * WIDE:(h + 1) * WIDE] = outs[h] * inv * g_ref[:, h * WIDE:(h + 1) * WIDE]


def _swa(proj, sinks, norm_g_wide):
    bsz, s, _ = proj.shape
    col = lambda width, off, prev: pl.BlockSpec(
        (None, SWA_BLOCK, width), lambda b, n: (b, jnp.maximum(n - 1, 0) if prev else n, off // width))
    return pl.pallas_call(
        _swa_kernel,
        grid=(bsz, s // SWA_BLOCK),
        in_specs=[pl.BlockSpec(memory_space=pltpu.SMEM), col(SWA_QW, COL_QS, False),
                  col(SWA_KV_WIDTH, COL_KS, True), col(SWA_KV_WIDTH, COL_KS, False),
                  col(SWA_KV_WIDTH, COL_VS, True), col(SWA_KV_WIDTH, COL_VS, False),
                  pl.BlockSpec((1, SWA_QW), lambda b, n: (0, 0))],
        out_specs=pl.BlockSpec((None, SWA_BLOCK, SWA_QW), lambda b, n: (b, n, 0)),
        out_shape=jax.ShapeDtypeStruct((bsz, s, SWA_QW), jnp.float32),
        compiler_params=pltpu.CompilerParams(dimension_semantics=("arbitrary", "arbitrary")),
        name="swa",
    )(sinks, proj, proj, proj, proj, proj, norm_g_wide)


def _out_proj_kernel(x_ref, og_ref, os_ref, w_ref, gate_ref, g2_ref, sh_ref, sc_ref, x1_ref, h2_ref):
    mix = jnp.concatenate([og_ref[...], os_ref[...]], axis=-1).astype(jnp.bfloat16)
    x1 = x_ref[...] + gate_ref[...] * jnp.dot(mix, w_ref[...], preferred_element_type=jnp.float32)
    x1_ref[...] = x1
    h2 = _rms(x1) * g2_ref[...] * (1.0 + sc_ref[...]) + sh_ref[...]
    for c in range(ROW_SUB):
        h2_ref[:, c, :] = h2[:, c * LANES:(c + 1) * LANES]


def _out_proj(x, o_gla, o_swa, w, gate1, g2, shift2, scale2):
    bsz, s, d = x.shape
    row = lambda width: pl.BlockSpec((None, ROW_TILE, width), lambda b, i: (b, i, 0))
    per_b = pl.BlockSpec((None, 1, d), lambda b, i: (b, 0, 0))
    shp = jax.ShapeDtypeStruct((bsz, s, d), jnp.float32)
    return pl.pallas_call(
        _out_proj_kernel,
        grid=(bsz, s // ROW_TILE),
        in_specs=[row(d), row(GLA_WIDTH), row(SWA_QW), pl.BlockSpec((MIX_W, d), lambda b, i: (0, 0)),
                  per_b, pl.BlockSpec((1, d), lambda b, i: (0, 0)), per_b, per_b],
        out_specs=[row(d), pl.BlockSpec((None, ROW_TILE, ROW_SUB, d // ROW_SUB), lambda b, i: (b, i, 0, 0))],
        out_shape=[shp, jax.ShapeDtypeStruct((bsz, s, ROW_SUB, d // ROW_SUB), jnp.float32)],
        compiler_params=pltpu.CompilerParams(dimension_semantics=("arbitrary", "arbitrary"),
                                             vmem_limit_bytes=48 * 1024 * 1024),
        name="out_proj",
    )(x, o_gla, o_swa, w, gate1, g2.reshape(1, d), shift2, scale2)


GLA_SUB = 16
GLA_BLOCK = 128
GLA_PAIR = 2 * GLA_DK


def _gla_kernel(q_ref, k_ref, v_ref, alr_ref, r_ref, wa_ref, ba_ref, g_ref, tri_ref, o_ref, st_ref, b_ref, q2_ref):
    f32, bf16 = jnp.float32, jnp.bfloat16

    @pl.when(pl.program_id(1) == 0)
    def _():
        st_ref[...] = jnp.zeros(st_ref.shape, f32)

    z = jnp.dot(alr_ref[...].astype(bf16), wa_ref[...], preferred_element_type=f32) + ba_ref[...]
    log_a = -(jnp.maximum(-z, 0.0) + jnp.log1p(jnp.exp(-jnp.abs(z)))) * (1.0 / GLA_GATE_TEMP)
    b_ref[...] = jnp.dot(tri_ref[...], log_a, preferred_element_type=f32, precision=lax.Precision.HIGHEST)
    q2_ref[...] = q_ref[...] * (GLA_DK ** -0.5)
    lane = lax.broadcasted_iota(jnp.int32, (GLA_SUB, GLA_PAIR), 1)
    sub = lax.broadcasted_iota(jnp.int32, (GLA_SUB, GLA_PAIR), 0)
    first = lane < GLA_DK
    lane_s = lax.broadcasted_iota(jnp.int32, (GLA_DV, GLA_PAIR), 1) < GLA_DK

    for hp in range(GLA_HEADS // 2):
        cols = pl.ds(hp * GLA_PAIR, GLA_PAIR)
        state = st_ref[hp]
        for c in range(GLA_BLOCK // GLA_SUB):
            rows = pl.ds(c * GLA_SUB, GLA_SUB)
            b = b_ref[rows, cols]
            b_last = b_ref[pl.ds(c * GLA_SUB + GLA_SUB - 1, 1), cols]
            q2, k2 = q2_ref[rows, cols], k_ref[rows, cols]
            qe = q2 * jnp.exp(b)
            kd = (k2 * jnp.exp(b_last - b)).astype(bf16)
            sb = state.astype(bf16)
            cols_a, cols_b = [], []
            for t in range(GLA_SUB):
                bt = b_ref[pl.ds(c * GLA_SUB + t, 1), cols]
                qt = q2_ref[pl.ds(c * GLA_SUB + t, 1), cols]
                p = (qt * k2) * jnp.exp(bt - b)
                keep = sub[:, :1] <= t
                cols_a.append(jnp.where(keep, jnp.sum(jnp.where(first, p, 0.0), axis=1, keepdims=True), 0.0))
                cols_b.append(jnp.where(keep, jnp.sum(jnp.where(first, 0.0, p), axis=1, keepdims=True), 0.0))
            ut = []
            for j, sc_cols in enumerate((cols_a, cols_b)):
                head = 2 * hp + j
                vcols = pl.ds(head * GLA_DV, GLA_DV)
                vh = v_ref[rows, vcols]
                qm = jnp.where(first if j == 0 else ~first, qe, 0.0).astype(bf16)
                o = lax.dot_general(qm, sb, NT_DIMS, preferred_element_type=f32)
                o = o + jnp.concatenate([jnp.sum(cj * vh, axis=0, keepdims=True) for cj in sc_cols], axis=0)
                ms = jnp.mean(o * o, axis=1, keepdims=True)
                r = r_ref[rows, vcols]
                o_ref[rows, vcols] = (o * lax.rsqrt(ms + RMS_EPS) * g_ref[:, vcols]) * (r * jax.nn.sigmoid(r))
                ut.append(jnp.dot(vh.T.astype(bf16), kd, preferred_element_type=f32))
            state = jnp.exp(b_last) * state + jnp.where(lane_s, ut[0], ut[1])
        st_ref[hp] = state


def _gla_tri():
    r = np.arange(GLA_BLOCK)
    return jnp.asarray(((r[:, None] // GLA_SUB == r[None, :] // GLA_SUB) & (r[None, :] <= r[:, None])).astype(np.float32))


def _gla(proj, w_alpha, b_alpha, norm_g):
    bsz, s, _ = proj.shape
    rank = w_alpha.shape[0]
    wa = jnp.pad(w_alpha, ((0, ALR_W - rank), (0, 0))).astype(jnp.bfloat16)
    col = lambda width, off: pl.BlockSpec((None, GLA_BLOCK, width), lambda b, i: (b, i, off // width))
    const = lambda shape: pl.BlockSpec(shape, lambda b, i: (0, 0))
    return pl.pallas_call(
        _gla_kernel,
        grid=(bsz, s // GLA_BLOCK),
        in_specs=[col(GLA_QK_WIDTH, COL_QG), col(GLA_QK_WIDTH, COL_KG), col(GLA_WIDTH, COL_VG), col(ALR_W, COL_ALR),
                  col(GLA_WIDTH, COL_RG), const((ALR_W, GLA_QK_WIDTH)), const((1, GLA_QK_WIDTH)),
                  const((1, GLA_WIDTH)), const((GLA_BLOCK, GLA_BLOCK))],
        out_specs=pl.BlockSpec((None, GLA_BLOCK, GLA_WIDTH), lambda b, i: (b, i, 0)),
        out_shape=jax.ShapeDtypeStruct((bsz, s, GLA_WIDTH), jnp.float32),
        scratch_shapes=[pltpu.VMEM((GLA_HEADS // 2, GLA_DV, GLA_PAIR), jnp.float32),
                        pltpu.VMEM((GLA_BLOCK, GLA_QK_WIDTH), jnp.float32),
                        pltpu.VMEM((GLA_BLOCK, GLA_QK_WIDTH), jnp.float32)],
        compiler_params=pltpu.CompilerParams(dimension_semantics=("arbitrary", "arbitrary")),
        name="gla",
    )(proj, proj, proj, proj, proj, wa, b_alpha.reshape(1, -1), norm_g.reshape(1, -1), _gla_tri())


def peer_mixer(h_tiles, w_q, keys_1, keys_2, u_tab, v_tab):
    bsz, s, rs, rl = h_tiles.shape
    ht = h_tiles.reshape(bsz * s, rs, rl)
    experts, gates = _peer_route(ht, w_q, keys_1, keys_2)
    return _peer_experts(ht, experts, gates, u_tab, v_tab).reshape(bsz, s, rs, rl)


ROUTE_TT = 256


def _top16_rows(s, code, big):
    vals, codes = [], []
    for _ in range(PEER_TOPK):
        m = jnp.max(s, axis=0, keepdims=True)
        c = jnp.min(jnp.where(s == m, code, big), axis=0, keepdims=True)
        s = jnp.where(code == c, NEG_INF, s)
        vals.append(m)
        codes.append(c)
    return vals, codes


def _peer_route_kernel(h_ref, wq_ref, k1_ref, k2_ref, e_ref, g_ref, q_ref):
    half = PEER_QUERY_DIM // 2
    tt = ROUTE_TT
    h = jnp.concatenate([h_ref[:, c, :] for c in range(h_ref.shape[1])], axis=-1)
    q_ref[...] = jnp.dot(h.astype(jnp.bfloat16), wq_ref[...], preferred_element_type=jnp.float32)
    key_id = lax.broadcasted_iota(jnp.int32, (PEER_N_KEYS, 2 * tt), 0).astype(jnp.float32)
    sub8 = lax.broadcasted_iota(jnp.int32, (8, tt), 0)
    nt = (((1,), (1,)), ((), ()))

    def head_body(hd, carry):
        off = pl.multiple_of(hd * PEER_QUERY_DIM, PEER_QUERY_DIM)
        q1 = q_ref[:, pl.ds(off, half)].astype(jnp.bfloat16)
        q2 = q_ref[:, pl.ds(off + half, half)].astype(jnp.bfloat16)
        s1 = lax.dot_general(k1_ref[...], q1, nt, preferred_element_type=jnp.float32)
        s2 = lax.dot_general(k2_ref[...], q2, nt, preferred_element_type=jnp.float32)
        vals, ids = _top16_rows(jnp.concatenate([s1, s2], axis=1), key_id, float(PEER_N_KEYS))
        v = jnp.concatenate(vals, axis=0)
        ix = jnp.concatenate(ids, axis=0).astype(jnp.int32)
        v1, v2, i1, i2 = v[:, :tt], v[:, tt:], ix[:, :tt], ix[:, tt:]
        cand, code = [], []
        for a, b0 in [(0, 0), (0, 8), (1, 0), (2, 0), (3, 0), (4, 0), (5, 0), (6, 0), (7, 0)]:
            cand.append(v1[a:a + 1] + v2[b0:b0 + 8])
            flat = a * PEER_TOPK + b0 + sub8
            code.append((flat << 14) | (i1[a:a + 1] * PEER_N_KEYS + i2[b0:b0 + 8]))
        cand.append(v1[8:16] + v2[0:1])
        code.append((((8 + sub8) * PEER_TOPK) << 14) | (i1[8:16] * PEER_N_KEYS + i2[0:1]))
        sc, cc = _top16_rows(jnp.concatenate(cand, axis=0), jnp.concatenate(code, axis=0).astype(jnp.float32),
                             float(1 << 30))
        sc = jnp.concatenate(sc, axis=0)
        ex = jnp.exp(sc - sc[0:1])
        g_ref[hd] = ex / jnp.sum(ex, axis=0, keepdims=True)
        e_ref[hd] = jnp.concatenate(cc, axis=0).astype(jnp.int32) & (PEER_N_EXPERTS - 1)
        return carry

    lax.fori_loop(0, PEER_HEADS, head_body, 0)


def _peer_route(hf, w_q, keys_1, keys_2):
    t, rs, rl = hf.shape
    d = rs * rl
    qw = PEER_HEADS * PEER_QUERY_DIM
    const = lambda shape: pl.BlockSpec(shape, lambda i: tuple(0 for _ in shape))
    out_spec = pl.BlockSpec((PEER_HEADS, PEER_TOPK, ROUTE_TT), lambda i: (0, 0, i))
    e_t, g_t = pl.pallas_call(
        _peer_route_kernel,
        grid=(t // ROUTE_TT,),
        in_specs=[pl.BlockSpec((ROUTE_TT, rs, rl), lambda i: (i, 0, 0)), const((d, qw)),
                  const(keys_1.shape), const(keys_2.shape)],
        out_specs=[out_spec, out_spec],
        out_shape=[jax.ShapeDtypeStruct((PEER_HEADS, PEER_TOPK, t), jnp.int32),
                   jax.ShapeDtypeStruct((PEER_HEADS, PEER_TOPK, t), jnp.float32)],
        scratch_shapes=[pltpu.VMEM((ROUTE_TT, qw), jnp.float32)],
        compiler_params=pltpu.CompilerParams(dimension_semantics=("arbitrary",), vmem_limit_bytes=48 * 1024 * 1024),
        name="peer_route",
    )(hf, w_q.astype(jnp.bfloat16), keys_1.astype(jnp.bfloat16), keys_2.astype(jnp.bfloat16))
    return e_t, g_t


PEER_PAIRS = PEER_HEADS * PEER_TOPK
PEER_TB = 64
HALF_D = D_MODEL // 2
WORD_SUB, WORD_LANE = HALF_D // 128, 128
ROW_SUB = 2 * WORD_SUB
HI_MASK = -65536
IDX_WAYS = 8


def _pack_table(tab):
    b = lax.bitcast_convert_type(tab.astype(jnp.bfloat16), jnp.uint16).astype(jnp.uint32)
    word = b[:, :HALF_D] | (b[:, HALF_D:] << 16)
    rows = lax.bitcast_convert_type(word, jnp.int32).reshape(tab.shape[0] * WORD_SUB, WORD_LANE)
    return jnp.pad(rows, ((WORD_SUB, WORD_SUB), (0, 0)))


def _pair_tile(tab_ref, ia, ib, upper):
    wa = tab_ref[pl.ds(pl.multiple_of(ia, WORD_SUB), ROW_SUB), :]
    wb = tab_ref[pl.ds(pl.multiple_of(ib - WORD_SUB, WORD_SUB), ROW_SUB), :]
    return jnp.where(upper, wa, wb)


_ORDER = (6, 2, 4, 0, 7, 3, 5, 1)


def _fold_pairs(ps, sub):
    b1 = (sub & 2) != 0
    w = []
    for i in range(2):
        x, y = ps[2 * i], ps[2 * i + 1]
        w.append(jnp.where(b1, x + pltpu.roll(x, 2, 0), y + pltpu.roll(y, 6, 0)))
    b0 = (sub & 1) != 0
    x, y = w
    return jnp.where(b0, x + pltpu.roll(x, 1, 0), y + pltpu.roll(y, 7, 0))


def _peer_u_kernel(*refs):
    idx_refs = refs[:IDX_WAYS]
    h_ref, tab_ref, a_ref = refs[IDX_WAYS:]
    sub = lax.broadcasted_iota(jnp.int32, (ROW_SUB, WORD_LANE), 0)
    lane = lax.broadcasted_iota(jnp.int32, (ROW_SUB, WORD_LANE), 1)
    n_groups = PEER_PAIRS // IDX_WAYS

    def folded_tiles(t):
        h = h_ref[t]
        h_lo = jnp.concatenate([h[:WORD_SUB], h[:WORD_SUB]], axis=0)
        h_hi = jnp.concatenate([h[WORD_SUB:], h[WORD_SUB:]], axis=0)
        ways = [r.at[pl.ds(t * n_groups, n_groups)] for r in idx_refs]
        tiles = []
        for g in range(n_groups):
            ps = []
            for i in range(4):
                w = _pair_tile(tab_ref, ways[2 * i][g], ways[2 * i + 1][g], sub < WORD_SUB)
                lo = lax.bitcast_convert_type(w << 16, jnp.float32)
                hi = lax.bitcast_convert_type(w & HI_MASK, jnp.float32)
                ps.append(lo * h_lo + hi * h_hi)
            tiles.append(_fold_pairs(ps, sub))
        return tuple(tiles)

    def lane_sums(t, tiles, acc):
        for g in range(n_groups):
            acc = jnp.where(lane == (t % 8) * n_groups + g, jnp.sum(tiles[g], axis=1, keepdims=True), acc)
        a_ref[t // 8] = acc
        return acc

    def tok_body(t, carry):
        acc, prev = carry
        return lane_sums(t - 1, prev, acc), folded_tiles(t)

    acc, last = lax.fori_loop(1, PEER_TB, tok_body,
                              (jnp.zeros((ROW_SUB, WORD_LANE), jnp.float32), folded_tiles(0)))
    lane_sums(PEER_TB - 1, last, acc)


def _split3_lanes(x, tb):
    mask = lambda v: lax.bitcast_convert_type(lax.bitcast_convert_type(v, jnp.int32) & HI_MASK, jnp.float32)
    hi = mask(x)
    mid = mask(x - hi)
    lo = x - hi - mid
    part = lax.broadcasted_iota(jnp.int32, x.shape, 1) // tb
    return jnp.where(part == 0, hi, jnp.where(part == 1, mid, jnp.where(part == 2, lo, 0.0))).astype(jnp.bfloat16)


def _peer_v_kernel(*refs):
    idx_refs = refs[:IDX_WAYS]
    a_ref, g_ref, tab_ref, sel_ref, o_ref, wrep_ref = refs[IDX_WAYS:]
    a = a_ref[...]
    w = g_ref[...] * (0.5 * a * (1.0 + lax.erf(a * (2.0 ** -0.5))))
    wcat = _split3_lanes(jnp.concatenate([w, w, w, w], axis=0).T, PEER_TB)
    wrep_ref[...] = jnp.dot(wcat, sel_ref[...], preferred_element_type=jnp.float32)
    upper = lax.broadcasted_iota(jnp.int32, (ROW_SUB, WORD_LANE), 0) < WORD_SUB
    per_way = PEER_PAIRS // IDX_WAYS

    def tok_body(t, carry):
        col = pl.multiple_of(t * WORD_LANE, WORD_LANE)
        ways = [r.at[pl.ds(t * per_way, per_way)] for r in idx_refs]
        z = jnp.zeros((ROW_SUB, WORD_LANE), jnp.float32)
        acc_lo, acc_hi = [z, z], [z, z]
        for m in range(per_way):
            for jj in range(IDX_WAYS // 2):
                p = m * IDX_WAYS + 2 * jj
                words = _pair_tile(tab_ref, ways[2 * jj][m], ways[2 * jj + 1][m], upper)
                w2 = jnp.where(upper, wrep_ref[pl.ds(p, 1), pl.ds(col, WORD_LANE)],
                               wrep_ref[pl.ds(p + 1, 1), pl.ds(col, WORD_LANE)])
                k = jj % 2
                acc_lo[k] = acc_lo[k] + lax.bitcast_convert_type(words << 16, jnp.float32) * w2
                acc_hi[k] = acc_hi[k] + lax.bitcast_convert_type(words & HI_MASK, jnp.float32) * w2
        lo, hi = acc_lo[0] + acc_lo[1], acc_hi[0] + acc_hi[1]
        o_ref[t] = jnp.concatenate([lo[:WORD_SUB] + lo[WORD_SUB:], hi[:WORD_SUB] + hi[WORD_SUB:]], axis=0)
        return carry

    lax.fori_loop(0, PEER_TB, tok_body, 0)


_PEER_VMEM_LIMIT = 56 * 1024 * 1024


def _peer_experts(hf, experts, gates, u_tab, v_tab):
    t = hf.shape[0]
    nblk = t // PEER_TB
    smem_spec = pl.BlockSpec((PEER_TB * PEER_PAIRS // IDX_WAYS,), lambda i: (i,), memory_space=pltpu.SMEM)
    tab_spec = pl.BlockSpec(memory_space=pltpu.VMEM)
    tok_spec = pl.BlockSpec((PEER_TB, PEER_PAIRS), lambda i: (i, 0))
    row_spec = pl.BlockSpec((PEER_TB, ROW_SUB, WORD_LANE), lambda i: (i, 0, 0))
    params = pltpu.CompilerParams(dimension_semantics=("arbitrary",), vmem_limit_bytes=_PEER_VMEM_LIMIT)
    assert IDX_WAYS == PEER_HEADS
    idx_ways = [(experts[j] * WORD_SUB + WORD_SUB).T.reshape(-1) for j in range(IDX_WAYS)]
    gates = gates.transpose(2, 1, 0).reshape(t, PEER_PAIRS)
    a_tiles = pl.pallas_call(
        _peer_u_kernel,
        grid=(nblk,),
        in_specs=[smem_spec] * IDX_WAYS + [row_spec, tab_spec],
        out_specs=pl.BlockSpec((PEER_TB // 8, ROW_SUB, WORD_LANE), lambda i: (i, 0, 0)),
        out_shape=jax.ShapeDtypeStruct((t // 8, ROW_SUB, WORD_LANE), jnp.float32),
        compiler_params=params, name="peer_u",
    )(*idx_ways, hf, _pack_table(u_tab))
    inv = np.argsort(np.array(_ORDER))
    a = a_tiles.reshape(t // 8, 8, 8, PEER_PAIRS // 8)[:, inv].transpose(0, 2, 3, 1).reshape(t, PEER_PAIRS)
    eye = np.repeat(np.eye(PEER_TB, dtype=np.float32), WORD_LANE, axis=1)
    sel = jnp.asarray(np.concatenate([eye, eye, eye, np.zeros_like(eye)], axis=0), jnp.bfloat16)
    out = pl.pallas_call(
        _peer_v_kernel,
        grid=(nblk,),
        in_specs=[smem_spec] * IDX_WAYS + [tok_spec, tok_spec, tab_spec, pl.BlockSpec(sel.shape, lambda i: (0, 0))],
        out_specs=row_spec,
        out_shape=jax.ShapeDtypeStruct((t, ROW_SUB, WORD_LANE), jnp.float32),
        scratch_shapes=[pltpu.VMEM((PEER_PAIRS, PEER_TB * WORD_LANE), jnp.float32)],
        compiler_params=params, name="peer_v",
    )(*idx_ways, a, gates, _pack_table(v_tab), sel)
    return out


def _final_kernel(x_ref, p_ref, gate_ref, g_ref, o_ref):
    peer = jnp.concatenate([p_ref[:, c, :] for c in range(ROW_SUB)], axis=-1)
    o_ref[...] = _rms(x_ref[...] + gate_ref[...] * peer) * g_ref[...]


def _final(x1, peer, gate2, g):
    bsz, s, d = x1.shape
    row = pl.BlockSpec((None, ROW_TILE, d), lambda b, i: (b, i, 0))
    return pl.pallas_call(
        _final_kernel,
        grid=(bsz, s // ROW_TILE),
        in_specs=[row, pl.BlockSpec((None, ROW_TILE, ROW_SUB, d // ROW_SUB), lambda b, i: (b, i, 0, 0)),
                  pl.BlockSpec((None, 1, d), lambda b, i: (b, 0, 0)), pl.BlockSpec((1, d), lambda b, i: (0, 0))],
        out_specs=row,
        out_shape=jax.ShapeDtypeStruct((bsz, s, d), jnp.float32), name="final",
    )(x1, peer, gate2, g.reshape(1, d))


def _in_proj_weights(w_in):
    o = np.cumsum([0, GLA_QK_WIDTH, GLA_QK_WIDTH, GLA_WIDTH, GLA_GATE_RANK, GLA_WIDTH, SWA_WIDTH, SWA_KV_WIDTH])
    src = np.full((PROJ_W,), IN_WIDTH, np.int32)
    src[COL_QS + _wide_cols()] = o[5] + np.arange(SWA_WIDTH)
    for dst, start, width in ((COL_QG, o[0], GLA_QK_WIDTH), (COL_KG, o[1], GLA_QK_WIDTH), (COL_VG, o[2], GLA_WIDTH),
                              (COL_ALR, o[3], GLA_GATE_RANK), (COL_RG, o[4], GLA_WIDTH),
                              (COL_KS, o[6], SWA_KV_WIDTH), (COL_VS, o[7], SWA_KV_WIDTH)):
        src[dst:dst + width] = start + np.arange(width)
    w_ext = jnp.concatenate([w_in, jnp.zeros((w_in.shape[0], 1), w_in.dtype)], axis=1)
    return w_ext[:, src].astype(jnp.bfloat16)


def _out_proj_weights(w_out):
    src = np.full((MIX_W,), MIX_WIDTH, np.int32)
    src[:GLA_WIDTH] = np.arange(GLA_WIDTH)
    src[GLA_WIDTH + _wide_cols()] = GLA_WIDTH + np.arange(SWA_WIDTH)
    w_ext = jnp.concatenate([w_out, jnp.zeros((1, w_out.shape[1]), w_out.dtype)], axis=0)
    return w_ext[src].astype(jnp.bfloat16)


def kernel(x, c, w_ada, b_ada, norm1_g, norm2_g, w_in, w_gla_alpha, b_gla_alpha, gla_norm_g,
           swa_sinks, swa_norm_g, w_out, w_peer_q, peer_keys_1, peer_keys_2, peer_u, peer_v, final_g):
    assert w_ada.shape[0] == 1, "single-layer block"
    l = 0
    mod = _ada(c, w_ada[l], b_ada[l])
    shift1, scale1, gate1, shift2, scale2, gate2 = [m[:, None, :] for m in jnp.split(mod, 6, axis=-1)]
    proj = _in_proj(x, norm1_g[l], shift1, scale1, _in_proj_weights(w_in[l]))
    o_gla = _gla(proj, w_gla_alpha[l], b_gla_alpha[l], gla_norm_g[l])
    g_wide = jnp.zeros((1, SWA_QW), jnp.float32).at[0, _wide_cols()].set(swa_norm_g[l])
    o_swa = _swa(proj, swa_sinks[l], g_wide)
    x1, h2 = _out_proj(x, o_gla, o_swa, _out_proj_weights(w_out[l]), gate1, norm2_g[l], shift2, scale2)
    peer = peer_mixer(h2, w_peer_q[l], peer_keys_1[l], peer_keys_2[l], peer_u[l], peer_v[l])
    return _final(x1, peer, gate2, final_g)
```

```python
import jax, jax.numpy as jnp
from jax import lax
import numpy as np
from jax.experimental import pallas as pl
from jax.experimental.pallas import tpu as pltpu

D_MODEL = 1024
BATCH = 4
SEQ = 8192
DEPTH = 1

GLA_HEADS = 4
GLA_DK = 64
GLA_DV = 128
GLA_GATE_RANK = 16
GLA_GATE_TEMP = 16.0
GLA_CHUNK = 64
SWA_HEADS = 8
SWA_KV_HEADS = 2
SWA_HEAD_DIM = 64
SWA_WINDOW = 128
SWA_BLOCK = 128
PEER_HEADS = 8
PEER_N_KEYS = 128
PEER_N_EXPERTS = PEER_N_KEYS * PEER_N_KEYS
PEER_QUERY_DIM = 256
PEER_TOPK = 16
RMS_EPS = 1e-6

GLA_QK_WIDTH = GLA_HEADS * GLA_DK
GLA_WIDTH = GLA_HEADS * GLA_DV
SWA_WIDTH = SWA_HEADS * SWA_HEAD_DIM
SWA_KV_WIDTH = SWA_KV_HEADS * SWA_HEAD_DIM
MIX_WIDTH = GLA_WIDTH + SWA_WIDTH
IN_WIDTH = 2 * GLA_QK_WIDTH + GLA_WIDTH + GLA_GATE_RANK + GLA_WIDTH + SWA_WIDTH + 2 * SWA_KV_WIDTH


LANES = 128
NEG_INF = float("-inf")
NT_DIMS = (((1,), (1,)), ((), ()))
ROW_TILE = 512
WIDE = 2 * SWA_HEAD_DIM
SWA_QW = SWA_HEADS * WIDE
ALR_W = LANES
COL_QS, COL_QG, COL_KG, COL_VG, COL_RG = 0, SWA_QW, SWA_QW + GLA_QK_WIDTH, SWA_QW + 2 * GLA_QK_WIDTH, SWA_QW + 2 * GLA_QK_WIDTH + GLA_WIDTH
COL_KS = COL_RG + GLA_WIDTH
COL_VS = COL_KS + SWA_KV_WIDTH
COL_ALR = COL_VS + SWA_KV_WIDTH
PROJ_W = COL_ALR + ALR_W
MIX_W = GLA_WIDTH + SWA_QW


def _wide_cols():
    grp = SWA_HEADS // SWA_KV_HEADS
    h = np.arange(SWA_WIDTH) // SWA_HEAD_DIM
    return h * WIDE + (h // grp) * SWA_HEAD_DIM + np.arange(SWA_WIDTH) % SWA_HEAD_DIM


def _rms(x):
    return x * lax.rsqrt(jnp.mean(x * x, axis=-1, keepdims=True) + RMS_EPS)


def _ada_kernel(c_ref, w_ref, b_ref, o_ref):
    c = c_ref[...]
    ca = (c * jax.nn.sigmoid(c)).astype(jnp.bfloat16)
    o_ref[...] = jnp.dot(ca, w_ref[...].astype(jnp.bfloat16), preferred_element_type=jnp.float32) + b_ref[...]


def _ada(c, w, b):
    bsz, d = c.shape
    n = w.shape[1]
    tn = 1024
    return pl.pallas_call(
        _ada_kernel,
        grid=(n // tn,),
        in_specs=[pl.BlockSpec((bsz, d), lambda j: (0, 0)), pl.BlockSpec((d, tn), lambda j: (0, j)),
                  pl.BlockSpec((1, tn), lambda j: (0, j))],
        out_specs=pl.BlockSpec((bsz, tn), lambda j: (0, j)),
        out_shape=jax.ShapeDtypeStruct((bsz, n), jnp.float32), name="ada",
    )(c, w, b.reshape(1, n))


def _in_proj_kernel(x_ref, g_ref, sh_ref, sc_ref, w_ref, o_ref):
    h = _rms(x_ref[...]) * g_ref[...] * (1.0 + sc_ref[...]) + sh_ref[...]
    o_ref[...] = jnp.dot(h.astype(jnp.bfloat16), w_ref[...], preferred_element_type=jnp.float32)


def _in_proj(x, g, shift, scale, w):
    bsz, s, d = x.shape
    n = w.shape[1]
    row = lambda width: pl.BlockSpec((None, ROW_TILE, width), lambda b, i: (b, i, 0))
    per_b = pl.BlockSpec((None, 1, d), lambda b, i: (b, 0, 0))
    return pl.pallas_call(
        _in_proj_kernel,
        grid=(bsz, s // ROW_TILE),
        in_specs=[row(d), pl.BlockSpec((1, d), lambda b, i: (0, 0)), per_b, per_b,
                  pl.BlockSpec((d, n), lambda b, i: (0, 0))],
        out_specs=row(n),
        out_shape=jax.ShapeDtypeStruct((bsz, s, n), jnp.float32),
        compiler_params=pltpu.CompilerParams(dimension_semantics=("arbitrary", "arbitrary"),
                                             vmem_limit_bytes=48 * 1024 * 1024),
        name="in_proj",
    )(x, g.reshape(1, d), shift, scale, w)


def _swa_kernel(sink_ref, q_ref, kp_ref, kc_ref, vp_ref, vc_ref, g_ref, o_ref):
    f32, bf16 = jnp.float32, jnp.bfloat16
    blk = SWA_BLOCK
    k2 = jnp.concatenate([kp_ref[...], kc_ref[...]], axis=0).astype(bf16)
    v2 = jnp.concatenate([vp_ref[...], vc_ref[...]], axis=0).astype(bf16)
    qi = lax.broadcasted_iota(jnp.int32, (blk, 2 * blk), 0)
    kj = lax.broadcasted_iota(jnp.int32, (blk, 2 * blk), 1)
    dist = qi + blk - kj
    valid = (dist >= 0) & (dist < SWA_WINDOW) & ((kj >= blk) | (pl.program_id(1) > 0))
    distf = dist.astype(f32)
    lane_kv = lax.broadcasted_iota(jnp.int32, (blk, WIDE), 1) // SWA_HEAD_DIM
    grp = SWA_HEADS // SWA_KV_HEADS
    outs, ssq = [], jnp.zeros((blk, 1), f32)
    for h in range(SWA_HEADS):
        q = (q_ref[:, h * WIDE:(h + 1) * WIDE] * (SWA_HEAD_DIM ** -0.5)).astype(bf16)
        sc = lax.dot_general(q, k2, NT_DIMS, preferred_element_type=f32)
        sc = sc - (2.0 ** (-8.0 * (h + 1) / SWA_HEADS)) * distf
        sc = jnp.where(valid, sc, NEG_INF)
        sink = sink_ref[h]
        m = jnp.maximum(jnp.max(sc, axis=-1, keepdims=True), sink)
        p = jnp.exp(sc - m)
        p = p / (jnp.sum(p, axis=-1, keepdims=True) + jnp.exp(sink - m))
        o = jnp.dot(p.astype(bf16), v2, preferred_element_type=f32)
        o = jnp.where(lane_kv == h // grp, o, 0.0)
        ssq = ssq + jnp.sum(o * o, axis=-1, keepdims=True)
        outs.append(o)
    inv = lax.rsqrt(ssq * (1.0 / SWA_WIDTH) + RMS_EPS)
    for h in range(SWA_HEADS):
        o_ref[:, h * WIDE:(h + 1) * WIDE] = outs[h] * inv * g_ref[:, h * WIDE:(h + 1) * WIDE]


def _swa(proj, sinks, norm_g_wide):
    bsz, s, _ = proj.shape
    col = lambda width, off, prev: pl.BlockSpec(
        (None, SWA_BLOCK, width), lambda b, n: (b, jnp.maximum(n - 1, 0) if prev else n, off // width))
    return pl.pallas_call(
        _swa_kernel,
        grid=(bsz, s // SWA_BLOCK),
        in_specs=[pl.BlockSpec(memory_space=pltpu.SMEM), col(SWA_QW, COL_QS, False),
                  col(SWA_KV_WIDTH, COL_KS, True), col(SWA_KV_WIDTH, COL_KS, False),
                  col(SWA_KV_WIDTH, COL_VS, True), col(SWA_KV_WIDTH, COL_VS, False),
                  pl.BlockSpec((1, SWA_QW), lambda b, n: (0, 0))],
        out_specs=pl.BlockSpec((None, SWA_BLOCK, SWA_QW), lambda b, n: (b, n, 0)),
        out_shape=jax.ShapeDtypeStruct((bsz, s, SWA_QW), jnp.float32),
        compiler_params=pltpu.CompilerParams(dimension_semantics=("arbitrary", "arbitrary")),
        name="swa",
    )(sinks, proj, proj, proj, proj, proj, norm_g_wide)


def _out_proj_kernel(x_ref, og_ref, os_ref, w_ref, gate_ref, g2_ref, sh_ref, sc_ref, x1_ref, h2_ref):
    mix = jnp.concatenate([og_ref[...], os_ref[...]], axis=-1).astype(jnp.bfloat16)
    x1 = x_ref[...] + gate_ref[...] * jnp.dot(mix, w_ref[...], preferred_element_type=jnp.float32)
    x1_ref[...] = x1
    h2 = _rms(x1) * g2_ref[...] * (1.0 + sc_ref[...]) + sh_ref[...]
    for c in range(ROW_SUB):
        h2_ref[:, c, :] = h2[:, c * LANES:(c + 1) * LANES]


def _out_proj(x, o_gla, o_swa, w, gate1, g2, shift2, scale2):
    bsz, s, d = x.shape
    row = lambda width: pl.BlockSpec((None, ROW_TILE, width), lambda b, i: (b, i, 0))
    per_b = pl.BlockSpec((None, 1, d), lambda b, i: (b, 0, 0))
    shp = jax.ShapeDtypeStruct((bsz, s, d), jnp.float32)
    return pl.pallas_call(
        _out_proj_kernel,
        grid=(bsz, s // ROW_TILE),
        in_specs=[row(d), row(GLA_WIDTH), row(SWA_QW), pl.BlockSpec((MIX_W, d), lambda b, i: (0, 0)),
                  per_b, pl.BlockSpec((1, d), lambda b, i: (0, 0)), per_b, per_b],
        out_specs=[row(d), pl.BlockSpec((None, ROW_TILE, ROW_SUB, d // ROW_SUB), lambda b, i: (b, i, 0, 0))],
        out_shape=[shp, jax.ShapeDtypeStruct((bsz, s, ROW_SUB, d // ROW_SUB), jnp.float32)],
        compiler_params=pltpu.CompilerParams(dimension_semantics=("arbitrary", "arbitrary"),
                                             vmem_limit_bytes=48 * 1024 * 1024),
        name="out_proj",
    )(x, o_gla, o_swa, w, gate1, g2.reshape(1, d), shift2, scale2)


GLA_SUB = 16
GLA_BLOCK = 128
GLA_PAIR = 2 * GLA_DK


def _gla_kernel(q_ref, k_ref, v_ref, alr_ref, r_ref, wa_ref, ba_ref, g_ref, tri_ref, o_ref, st_ref, b_ref, q2_ref):
    f32, bf16 = jnp.float32, jnp.bfloat16

    @pl.when(pl.program_id(1) == 0)
    def _():
        st_ref[...] = jnp.zeros(st_ref.shape, f32)

    z = jnp.dot(alr_ref[...].astype(bf16), wa_ref[...], preferred_element_type=f32) + ba_ref[...]
    log_a = -(jnp.maximum(-z, 0.0) + jnp.log1p(jnp.exp(-jnp.abs(z)))) * (1.0 / GLA_GATE_TEMP)
    b_ref[...] = jnp.dot(tri_ref[...], log_a, preferred_element_type=f32, precision=lax.Precision.HIGHEST)
    q2_ref[...] = q_ref[...] * (GLA_DK ** -0.5)
    lane = lax.broadcasted_iota(jnp.int32, (GLA_SUB, GLA_PAIR), 1)
    sub = lax.broadcasted_iota(jnp.int32, (GLA_SUB, GLA_PAIR), 0)
    first = lane < GLA_DK
    lane_s = lax.broadcasted_iota(jnp.int32, (GLA_DV, GLA_PAIR), 1) < GLA_DK

    for hp in range(GLA_HEADS // 2):
        cols = pl.ds(hp * GLA_PAIR, GLA_PAIR)
        state = st_ref[hp]
        for c in range(GLA_BLOCK // GLA_SUB):
            rows = pl.ds(c * GLA_SUB, GLA_SUB)
            b = b_ref[rows, cols]
            b_last = b_ref[pl.ds(c * GLA_SUB + GLA_SUB - 1, 1), cols]
            q2, k2 = q2_ref[rows, cols], k_ref[rows, cols]
            qe = q2 * jnp.exp(b)
            kd = (k2 * jnp.exp(b_last - b)).astype(bf16)
            sb = state.astype(bf16)
            cols_a, cols_b = [], []
            for t in range(GLA_SUB):
                bt = b_ref[pl.ds(c * GLA_SUB + t, 1), cols]
                qt = q2_ref[pl.ds(c * GLA_SUB + t, 1), cols]
                p = (qt * k2) * jnp.exp(bt - b)
                keep = sub[:, :1] <= t
                cols_a.append(jnp.where(keep, jnp.sum(jnp.where(first, p, 0.0), axis=1, keepdims=True), 0.0))
                cols_b.append(jnp.where(keep, jnp.sum(jnp.where(first, 0.0, p), axis=1, keepdims=True), 0.0))
            ut = []
            for j, sc_cols in enumerate((cols_a, cols_b)):
                head = 2 * hp + j
                vcols = pl.ds(head * GLA_DV, GLA_DV)
                vh = v_ref[rows, vcols]
                qm = jnp.where(first if j == 0 else ~first, qe, 0.0).astype(bf16)
                o = lax.dot_general(qm, sb, NT_DIMS, preferred_element_type=f32)
                o = o + jnp.concatenate([jnp.sum(cj * vh, axis=0, keepdims=True) for cj in sc_cols], axis=0)
                ms = jnp.mean(o * o, axis=1, keepdims=True)
                r = r_ref[rows, vcols]
                o_ref[rows, vcols] = (o * lax.rsqrt(ms + RMS_EPS) * g_ref[:, vcols]) * (r * jax.nn.sigmoid(r))
                ut.append(jnp.dot(vh.T.astype(bf16), kd, preferred_element_type=f32))
            state = jnp.exp(b_last) * state + jnp.where(lane_s, ut[0], ut[1])
        st_ref[hp] = state


def _gla_tri():
    r = np.arange(GLA_BLOCK)
    return jnp.asarray(((r[:, None] // GLA_SUB == r[None, :] // GLA_SUB) & (r[None, :] <= r[:, None])).astype(np.float32))


def _gla(proj, w_alpha, b_alpha, norm_g):
    bsz, s, _ = proj.shape
    rank = w_alpha.shape[0]
    wa = jnp.pad(w_alpha, ((0, ALR_W - rank), (0, 0))).astype(jnp.bfloat16)
    col = lambda width, off: pl.BlockSpec((None, GLA_BLOCK, width), lambda b, i: (b, i, off // width))
    const = lambda shape: pl.BlockSpec(shape, lambda b, i: (0, 0))
    return pl.pallas_call(
        _gla_kernel,
        grid=(bsz, s // GLA_BLOCK),
        in_specs=[col(GLA_QK_WIDTH, COL_QG), col(GLA_QK_WIDTH, COL_KG), col(GLA_WIDTH, COL_VG), col(ALR_W, COL_ALR),
                  col(GLA_WIDTH, COL_RG), const((ALR_W, GLA_QK_WIDTH)), const((1, GLA_QK_WIDTH)),
                  const((1, GLA_WIDTH)), const((GLA_BLOCK, GLA_BLOCK))],
        out_specs=pl.BlockSpec((None, GLA_BLOCK, GLA_WIDTH), lambda b, i: (b, i, 0)),
        out_shape=jax.ShapeDtypeStruct((bsz, s, GLA_WIDTH), jnp.float32),
        scratch_shapes=[pltpu.VMEM((GLA_HEADS // 2, GLA_DV, GLA_PAIR), jnp.float32),
                        pltpu.VMEM((GLA_BLOCK, GLA_QK_WIDTH), jnp.float32),
                        pltpu.VMEM((GLA_BLOCK, GLA_QK_WIDTH), jnp.float32)],
        compiler_params=pltpu.CompilerParams(dimension_semantics=("arbitrary", "arbitrary")),
        name="gla",
    )(proj, proj, proj, proj, proj, wa, b_alpha.reshape(1, -1), norm_g.reshape(1, -1), _gla_tri())


def peer_mixer(h_tiles, w_q, keys_1, keys_2, u_tab, v_tab):
    bsz, s, rs, rl = h_tiles.shape
    ht = h_tiles.reshape(bsz * s, rs, rl)
    experts, gates = _peer_route(ht, w_q, keys_1, keys_2)
    return _peer_experts(ht, experts, gates, u_tab, v_tab).reshape(bsz, s, rs, rl)


ROUTE_TT = 256


def _top16_rows(s, code, big):
    vals, codes = [], []
    for _ in range(PEER_TOPK):
        m = jnp.max(s, axis=0, keepdims=True)
        c = jnp.min(jnp.where(s == m, code, big), axis=0, keepdims=True)
        s = jnp.where(code == c, NEG_INF, s)
        vals.append(m)
        codes.append(c)
    return vals, codes


def _peer_route_kernel(h_ref, wq_ref, k1_ref, k2_ref, e_ref, g_ref, q_ref):
    half = PEER_QUERY_DIM // 2
    tt = ROUTE_TT
    h = jnp.concatenate([h_ref[:, c, :] for c in range(h_ref.shape[1])], axis=-1)
    q_ref[...] = jnp.dot(h.astype(jnp.bfloat16), wq_ref[...], preferred_element_type=jnp.float32)
    key_id = lax.broadcasted_iota(jnp.int32, (PEER_N_KEYS, 2 * tt), 0).astype(jnp.float32)
    sub8 = lax.broadcasted_iota(jnp.int32, (8, tt), 0)
    nt = (((1,), (1,)), ((), ()))

    def head_body(hd, carry):
        off = pl.multiple_of(hd * PEER_QUERY_DIM, PEER_QUERY_DIM)
        q1 = q_ref[:, pl.ds(off, half)].astype(jnp.bfloat16)
        q2 = q_ref[:, pl.ds(off + half, half)].astype(jnp.bfloat16)
        s1 = lax.dot_general(k1_ref[...], q1, nt, preferred_element_type=jnp.float32)
        s2 = lax.dot_general(k2_ref[...], q2, nt, preferred_element_type=jnp.float32)
        vals, ids = _top16_rows(jnp.concatenate([s1, s2], axis=1), key_id, float(PEER_N_KEYS))
        v = jnp.concatenate(vals, axis=0)
        ix = jnp.concatenate(ids, axis=0).astype(jnp.int32)
        v1, v2, i1, i2 = v[:, :tt], v[:, tt:], ix[:, :tt], ix[:, tt:]
        cand, code = [], []
        for a, b0 in [(0, 0), (0, 8), (1, 0), (2, 0), (3, 0), (4, 0), (5, 0), (6, 0), (7, 0)]:
            cand.append(v1[a:a + 1] + v2[b0:b0 + 8])
            flat = a * PEER_TOPK + b0 + sub8
            code.append((flat << 14) | (i1[a:a + 1] * PEER_N_KEYS + i2[b0:b0 + 8]))
        cand.append(v1[8:16] + v2[0:1])
        code.append((((8 + sub8) * PEER_TOPK) << 14) | (i1[8:16] * PEER_N_KEYS + i2[0:1]))
        sc, cc = _top16_rows(jnp.concatenate(cand, axis=0), jnp.concatenate(code, axis=0).astype(jnp.float32),
                             float(1 << 30))
        sc = jnp.concatenate(sc, axis=0)
        ex = jnp.exp(sc - sc[0:1])
        g_ref[hd] = ex / jnp.sum(ex, axis=0, keepdims=True)
        e_ref[hd] = jnp.concatenate(cc, axis=0).astype(jnp.int32) & (PEER_N_EXPERTS - 1)
        return carry

    lax.fori_loop(0, PEER_HEADS, head_body, 0)


def _peer_route(hf, w_q, keys_1, keys_2):
    t, rs, rl = hf.shape
    d = rs * rl
    qw = PEER_HEADS * PEER_QUERY_DIM
    const = lambda shape: pl.BlockSpec(shape, lambda i: tuple(0 for _ in shape))
    out_spec = pl.BlockSpec((PEER_HEADS, PEER_TOPK, ROUTE_TT), lambda i: (0, 0, i))
    e_t, g_t = pl.pallas_call(
        _peer_route_kernel,
        grid=(t // ROUTE_TT,),
        in_specs=[pl.BlockSpec((ROUTE_TT, rs, rl), lambda i: (i, 0, 0)), const((d, qw)),
                  const(keys_1.shape), const(keys_2.shape)],
        out_specs=[out_spec, out_spec],
        out_shape=[jax.ShapeDtypeStruct((PEER_HEADS, PEER_TOPK, t), jnp.int32),
                   jax.ShapeDtypeStruct((PEER_HEADS, PEER_TOPK, t), jnp.float32)],
        scratch_shapes=[pltpu.VMEM((ROUTE_TT, qw), jnp.float32)],
        compiler_params=pltpu.CompilerParams(dimension_semantics=("arbitrary",), vmem_limit_bytes=48 * 1024 * 1024),
        name="peer_route",
    )(hf, w_q.astype(jnp.bfloat16), keys_1.astype(jnp.bfloat16), keys_2.astype(jnp.bfloat16))
    return e_t, g_t


PEER_PAIRS = PEER_HEADS * PEER_TOPK
PEER_TB = 64
PEER_TOK_UNROLL = 4
HALF_D = D_MODEL // 2
WORD_SUB, WORD_LANE = HALF_D // 128, 128
ROW_SUB = 2 * WORD_SUB
HI_MASK = -65536
IDX_WAYS = 8


PACK_EB = 512
TAB_PAD = PACK_EB * WORD_SUB


def _pack_kernel(x_ref, o_ref):
    i = pl.program_id(0)

    @pl.when((i == 0) | (i == pl.num_programs(0) - 1))
    def _():
        o_ref[...] = jnp.zeros(o_ref.shape, o_ref.dtype)

    @pl.when((i > 0) & (i < pl.num_programs(0) - 1))
    def _():
        x = x_ref[...].astype(jnp.bfloat16).astype(jnp.float32)
        bits = lax.bitcast_convert_type(x, jnp.int32)
        word = lax.shift_right_logical(bits[:, :HALF_D], 16) | (bits[:, HALF_D:] & HI_MASK)
        for c in range(WORD_SUB):
            o_ref[:, c, :] = word[:, c * WORD_LANE:(c + 1) * WORD_LANE]


def _pack_table(tab):
    e, d = tab.shape
    nblk = e // PACK_EB
    out = pl.pallas_call(
        _pack_kernel,
        grid=(nblk + 2,),
        in_specs=[pl.BlockSpec((PACK_EB, d), lambda i: (jnp.clip(i - 1, 0, nblk - 1), 0))],
        out_specs=pl.BlockSpec((PACK_EB, WORD_SUB, WORD_LANE), lambda i: (i, 0, 0)),
        out_shape=jax.ShapeDtypeStruct(((nblk + 2) * PACK_EB, WORD_SUB, WORD_LANE), jnp.int32),
        name="pack_table",
    )(tab)
    return out.reshape((nblk + 2) * TAB_PAD, WORD_LANE)


def _pair_tile(tab_ref, ia, ib, upper):
    wa = tab_ref[pl.ds(pl.multiple_of(ia, WORD_SUB), ROW_SUB), :]
    wb = tab_ref[pl.ds(pl.multiple_of(ib - WORD_SUB, WORD_SUB), ROW_SUB), :]
    return jnp.where(upper, wa, wb)


_ORDER = (6, 2, 4, 0, 7, 3, 5, 1)


def _fold_pairs(ps, sub):
    b1 = (sub & 2) != 0
    w = []
    for i in range(2):
        x, y = ps[2 * i], ps[2 * i + 1]
        w.append(jnp.where(b1, x + pltpu.roll(x, 2, 0), y + pltpu.roll(y, 6, 0)))
    b0 = (sub & 1) != 0
    x, y = w
    return jnp.where(b0, x + pltpu.roll(x, 1, 0), y + pltpu.roll(y, 7, 0))


def _peer_u_kernel(*refs):
    idx_refs = refs[:IDX_WAYS]
    h_ref, tab_ref, a_ref = refs[IDX_WAYS:]
    sub = lax.broadcasted_iota(jnp.int32, (ROW_SUB, WORD_LANE), 0)
    lane = lax.broadcasted_iota(jnp.int32, (ROW_SUB, WORD_LANE), 1)
    n_groups = PEER_PAIRS // IDX_WAYS

    def folded_tiles(t):
        h = h_ref[t]
        h_lo = jnp.concatenate([h[:WORD_SUB], h[:WORD_SUB]], axis=0)
        h_hi = jnp.concatenate([h[WORD_SUB:], h[WORD_SUB:]], axis=0)
        ways = [r.at[pl.ds(t * n_groups, n_groups)] for r in idx_refs]
        tiles = []
        for g in range(n_groups):
            ps = []
            for i in range(4):
                w = _pair_tile(tab_ref, ways[2 * i][g], ways[2 * i + 1][g], sub < WORD_SUB)
                lo = lax.bitcast_convert_type(w << 16, jnp.float32)
                hi = lax.bitcast_convert_type(w & HI_MASK, jnp.float32)
                ps.append(lo * h_lo + hi * h_hi)
            tiles.append(_fold_pairs(ps, sub))
        return tuple(tiles)

    def lane_sums(t, tiles, acc):
        for g in range(n_groups):
            acc = jnp.where(lane == (t % 8) * n_groups + g, jnp.sum(tiles[g], axis=1, keepdims=True), acc)
        a_ref[t // 8] = acc
        return acc

    def tok_body(t, carry):
        acc, prev = carry
        return lane_sums(t - 1, prev, acc), folded_tiles(t)

    acc, last = lax.fori_loop(1, PEER_TB, tok_body,
                              (jnp.zeros((ROW_SUB, WORD_LANE), jnp.float32), folded_tiles(0)))
    lane_sums(PEER_TB - 1, last, acc)


def _split3_lanes(x, tb):
    mask = lambda v: lax.bitcast_convert_type(lax.bitcast_convert_type(v, jnp.int32) & HI_MASK, jnp.float32)
    hi = mask(x)
    mid = mask(x - hi)
    lo = x - hi - mid
    part = lax.broadcasted_iota(jnp.int32, x.shape, 1) // tb
    return jnp.where(part == 0, hi, jnp.where(part == 1, mid, jnp.where(part == 2, lo, 0.0))).astype(jnp.bfloat16)


def _peer_v_kernel(*refs):
    idx_refs = refs[:IDX_WAYS]
    a_ref, g_ref, tab_ref, sel_ref, o_ref, wrep_ref = refs[IDX_WAYS:]
    a = a_ref[...]
    w = g_ref[...] * (0.5 * a * (1.0 + lax.erf(a * (2.0 ** -0.5))))
    wcat = _split3_lanes(jnp.concatenate([w, w, w, w], axis=0).T, PEER_TB)
    wrep_ref[...] = jnp.dot(wcat, sel_ref[...], preferred_element_type=jnp.float32)
    upper = lax.broadcasted_iota(jnp.int32, (ROW_SUB, WORD_LANE), 0) < WORD_SUB
    per_way = PEER_PAIRS // IDX_WAYS

    def one_token(t):
        col = pl.multiple_of(t * WORD_LANE, WORD_LANE)
        ways = [r.at[pl.ds(t * per_way, per_way)] for r in idx_refs]
        z = jnp.zeros((ROW_SUB, WORD_LANE), jnp.float32)
        acc_lo, acc_hi = [z, z], [z, z]
        for m in range(per_way):
            for jj in range(IDX_WAYS // 2):
                p = m * IDX_WAYS + 2 * jj
                words = _pair_tile(tab_ref, ways[2 * jj][m], ways[2 * jj + 1][m], upper)
                w2 = jnp.where(upper, wrep_ref[pl.ds(p, 1), pl.ds(col, WORD_LANE)],
                               wrep_ref[pl.ds(p + 1, 1), pl.ds(col, WORD_LANE)])
                k = jj % 2
                acc_lo[k] = acc_lo[k] + lax.bitcast_convert_type(words << 16, jnp.float32) * w2
                acc_hi[k] = acc_hi[k] + lax.bitcast_convert_type(words & HI_MASK, jnp.float32) * w2
        lo, hi = acc_lo[0] + acc_lo[1], acc_hi[0] + acc_hi[1]
        o_ref[t] = jnp.concatenate([lo[:WORD_SUB] + lo[WORD_SUB:], hi[:WORD_SUB] + hi[WORD_SUB:]], axis=0)

    def tok_body(i, carry):
        for j in range(PEER_TOK_UNROLL):
            one_token(PEER_TOK_UNROLL * i + j)
        return carry

    lax.fori_loop(0, PEER_TB // PEER_TOK_UNROLL, tok_body, 0)


_PEER_VMEM_LIMIT = 56 * 1024 * 1024


def _peer_experts(hf, experts, gates, u_tab, v_tab):
    t = hf.shape[0]
    nblk = t // PEER_TB
    smem_spec = pl.BlockSpec((PEER_TB * PEER_PAIRS // IDX_WAYS,), lambda i: (i,), memory_space=pltpu.SMEM)
    tab_spec = pl.BlockSpec(memory_space=pltpu.VMEM)
    tok_spec = pl.BlockSpec((PEER_TB, PEER_PAIRS), lambda i: (i, 0))
    row_spec = pl.BlockSpec((PEER_TB, ROW_SUB, WORD_LANE), lambda i: (i, 0, 0))
    params = pltpu.CompilerParams(dimension_semantics=("arbitrary",), vmem_limit_bytes=_PEER_VMEM_LIMIT)
    assert IDX_WAYS == PEER_HEADS
    idx_ways = [(experts[j] * WORD_SUB + TAB_PAD).T.reshape(-1) for j in range(IDX_WAYS)]
    gates = gates.transpose(2, 1, 0).reshape(t, PEER_PAIRS)
    a_tiles = pl.pallas_call(
        _peer_u_kernel,
        grid=(nblk,),
        in_specs=[smem_spec] * IDX_WAYS + [row_spec, tab_spec],
        out_specs=pl.BlockSpec((PEER_TB // 8, ROW_SUB, WORD_LANE), lambda i: (i, 0, 0)),
        out_shape=jax.ShapeDtypeStruct((t // 8, ROW_SUB, WORD_LANE), jnp.float32),
        compiler_params=params, name="peer_u",
    )(*idx_ways, hf, _pack_table(u_tab))
    inv = np.argsort(np.array(_ORDER))
    a = a_tiles.reshape(t // 8, 8, 8, PEER_PAIRS // 8)[:, inv].transpose(0, 2, 3, 1).reshape(t, PEER_PAIRS)
    eye = np.repeat(np.eye(PEER_TB, dtype=np.float32), WORD_LANE, axis=1)
    sel = jnp.asarray(np.concatenate([eye, eye, eye, np.zeros_like(eye)], axis=0), jnp.bfloat16)
    out = pl.pallas_call(
        _peer_v_kernel,
        grid=(nblk,),
        in_specs=[smem_spec] * IDX_WAYS + [tok_spec, tok_spec, tab_spec, pl.BlockSpec(sel.shape, lambda i: (0, 0))],
        out_specs=row_spec,
        out_shape=jax.ShapeDtypeStruct((t, ROW_SUB, WORD_LANE), jnp.float32),
        scratch_shapes=[pltpu.VMEM((PEER_PAIRS, PEER_TB * WORD_LANE), jnp.float32)],
        compiler_params=params, name="peer_v",
    )(*idx_ways, a, gates, _pack_table(v_tab), sel)
    return out


def _final_kernel(x_ref, p_ref, gate_ref, g_ref, o_ref):
    peer = jnp.concatenate([p_ref[:, c, :] for c in range(ROW_SUB)], axis=-1)
    o_ref[...] = _rms(x_ref[...] + gate_ref[...] * peer) * g_ref[...]


def _final(x1, peer, gate2, g):
    bsz, s, d = x1.shape
    row = pl.BlockSpec((None, ROW_TILE, d), lambda b, i: (b, i, 0))
    return pl.pallas_call(
        _final_kernel,
        grid=(bsz, s // ROW_TILE),
        in_specs=[row, pl.BlockSpec((None, ROW_TILE, ROW_SUB, d // ROW_SUB), lambda b, i: (b, i, 0, 0)),
                  pl.BlockSpec((None, 1, d), lambda b, i: (b, 0, 0)), pl.BlockSpec((1, d), lambda b, i: (0, 0))],
        out_specs=row,
        out_shape=jax.ShapeDtypeStruct((bsz, s, d), jnp.float32), name="final",
    )(x1, peer, gate2, g.reshape(1, d))


def _in_proj_weights(w_in):
    o = np.cumsum([0, GLA_QK_WIDTH, GLA_QK_WIDTH, GLA_WIDTH, GLA_GATE_RANK, GLA_WIDTH, SWA_WIDTH, SWA_KV_WIDTH])
    src = np.full((PROJ_W,), IN_WIDTH, np.int32)
    src[COL_QS + _wide_cols()] = o[5] + np.arange(SWA_WIDTH)
    for dst, start, width in ((COL_QG, o[0], GLA_QK_WIDTH), (COL_KG, o[1], GLA_QK_WIDTH), (COL_VG, o[2], GLA_WIDTH),
                              (COL_ALR, o[3], GLA_GATE_RANK), (COL_RG, o[4], GLA_WIDTH),
                              (COL_KS, o[6], SWA_KV_WIDTH), (COL_VS, o[7], SWA_KV_WIDTH)):
        src[dst:dst + width] = start + np.arange(width)
    w_ext = jnp.concatenate([w_in, jnp.zeros((w_in.shape[0], 1), w_in.dtype)], axis=1)
    return w_ext[:, src].astype(jnp.bfloat16)


def _out_proj_weights(w_out):
    src = np.full((MIX_W,), MIX_WIDTH, np.int32)
    src[:GLA_WIDTH] = np.arange(GLA_WIDTH)
    src[GLA_WIDTH + _wide_cols()] = GLA_WIDTH + np.arange(SWA_WIDTH)
    w_ext = jnp.concatenate([w_out, jnp.zeros((1, w_out.shape[1]), w_out.dtype)], axis=0)
    return w_ext[src].astype(jnp.bfloat16)


def kernel(x, c, w_ada, b_ada, norm1_g, norm2_g, w_in, w_gla_alpha, b_gla_alpha, gla_norm_g,
           swa_sinks, swa_norm_g, w_out, w_peer_q, peer_keys_1, peer_keys_2, peer_u, peer_v, final_g):
    assert w_ada.shape[0] == 1, "single-layer block"
    l = 0
    mod = _ada(c, w_ada[l], b_ada[l])
    shift1, scale1, gate1, shift2, scale2, gate2 = [m[:, None, :] for m in jnp.split(mod, 6, axis=-1)]
    proj = _in_proj(x, norm1_g[l], shift1, scale1, _in_proj_weights(w_in[l]))
    o_gla = _gla(proj, w_gla_alpha[l], b_gla_alpha[l], gla_norm_g[l])
    g_wide = jnp.zeros((1, SWA_QW), jnp.float32).at[0, _wide_cols()].set(swa_norm_g[l])
    o_swa = _swa(proj, swa_sinks[l], g_wide)
    x1, h2 = _out_proj(x, o_gla, o_swa, _out_proj_weights(w_out[l]), gate1, norm2_g[l], shift2, scale2)
    peer = peer_mixer(h2, w_peer_q[l], peer_keys_1[l], peer_keys_2[l], peer_u[l], peer_v[l])
    return _final(x1, peer, gate2, final_g)
```

```python
import jax, jax.numpy as jnp
from jax import lax
import numpy as np
from jax.experimental import pallas as pl
from jax.experimental.pallas import tpu as pltpu

D_MODEL = 1024
BATCH = 4
SEQ = 8192
DEPTH = 1

GLA_HEADS = 4
GLA_DK = 64
GLA_DV = 128
GLA_GATE_RANK = 16
GLA_GATE_TEMP = 16.0
GLA_CHUNK = 64
SWA_HEADS = 8
SWA_KV_HEADS = 2
SWA_HEAD_DIM = 64
SWA_WINDOW = 128
SWA_BLOCK = 128
PEER_HEADS = 8
PEER_N_KEYS = 128
PEER_N_EXPERTS = PEER_N_KEYS * PEER_N_KEYS
PEER_QUERY_DIM = 256
PEER_TOPK = 16
RMS_EPS = 1e-6

GLA_QK_WIDTH = GLA_HEADS * GLA_DK
GLA_WIDTH = GLA_HEADS * GLA_DV
SWA_WIDTH = SWA_HEADS * SWA_HEAD_DIM
SWA_KV_WIDTH = SWA_KV_HEADS * SWA_HEAD_DIM
MIX_WIDTH = GLA_WIDTH + SWA_WIDTH
IN_WIDTH = 2 * GLA_QK_WIDTH + GLA_WIDTH + GLA_GATE_RANK + GLA_WIDTH + SWA_WIDTH + 2 * SWA_KV_WIDTH


LANES = 128
NEG_INF = float("-inf")
NT_DIMS = (((1,), (1,)), ((), ()))
ROW_TILE = 512
WIDE = 2 * SWA_HEAD_DIM
SWA_QW = SWA_HEADS * WIDE
ALR_W = LANES
COL_QS, COL_QG, COL_KG, COL_VG, COL_RG = 0, SWA_QW, SWA_QW + GLA_QK_WIDTH, SWA_QW + 2 * GLA_QK_WIDTH, SWA_QW + 2 * GLA_QK_WIDTH + GLA_WIDTH
COL_KS = COL_RG + GLA_WIDTH
COL_VS = COL_KS + SWA_KV_WIDTH
COL_ALR = COL_VS + SWA_KV_WIDTH
PROJ_W = COL_ALR + ALR_W
MIX_W = GLA_WIDTH + SWA_QW


def _wide_cols():
    grp = SWA_HEADS // SWA_KV_HEADS
    h = np.arange(SWA_WIDTH) // SWA_HEAD_DIM
    return h * WIDE + (h // grp) * SWA_HEAD_DIM + np.arange(SWA_WIDTH) % SWA_HEAD_DIM


def _rms(x):
    return x * lax.rsqrt(jnp.mean(x * x, axis=-1, keepdims=True) + RMS_EPS)


def _ada_kernel(c_ref, w_ref, b_ref, o_ref):
    c = c_ref[...]
    ca = (c * jax.nn.sigmoid(c)).astype(jnp.bfloat16)
    o_ref[...] = jnp.dot(ca, w_ref[...].astype(jnp.bfloat16), preferred_element_type=jnp.float32) + b_ref[...]


def _ada(c, w, b):
    bsz, d = c.shape
    n = w.shape[1]
    tn = 1024
    return pl.pallas_call(
        _ada_kernel,
        grid=(n // tn,),
        in_specs=[pl.BlockSpec((bsz, d), lambda j: (0, 0)), pl.BlockSpec((d, tn), lambda j: (0, j)),
                  pl.BlockSpec((1, tn), lambda j: (0, j))],
        out_specs=pl.BlockSpec((bsz, tn), lambda j: (0, j)),
        out_shape=jax.ShapeDtypeStruct((bsz, n), jnp.float32), name="ada",
    )(c, w, b.reshape(1, n))


def _in_proj_kernel(x_ref, g_ref, sh_ref, sc_ref, w_ref, o_ref):
    h = _rms(x_ref[...]) * g_ref[...] * (1.0 + sc_ref[...]) + sh_ref[...]
    o_ref[...] = jnp.dot(h.astype(jnp.bfloat16), w_ref[...], preferred_element_type=jnp.float32)


def _in_proj(x, g, shift, scale, w):
    bsz, s, d = x.shape
    n = w.shape[1]
    row = lambda width: pl.BlockSpec((None, ROW_TILE, width), lambda b, i: (b, i, 0))
    per_b = pl.BlockSpec((None, 1, d), lambda b, i: (b, 0, 0))
    return pl.pallas_call(
        _in_proj_kernel,
        grid=(bsz, s // ROW_TILE),
        in_specs=[row(d), pl.BlockSpec((1, d), lambda b, i: (0, 0)), per_b, per_b,
                  pl.BlockSpec((d, n), lambda b, i: (0, 0))],
        out_specs=row(n),
        out_shape=jax.ShapeDtypeStruct((bsz, s, n), jnp.float32),
        compiler_params=pltpu.CompilerParams(dimension_semantics=("arbitrary", "arbitrary"),
                                             vmem_limit_bytes=48 * 1024 * 1024),
        name="in_proj",
    )(x, g.reshape(1, d), shift, scale, w)


def _swa_kernel(sink_ref, q_ref, kp_ref, kc_ref, vp_ref, vc_ref, g_ref, o_ref):
    f32, bf16 = jnp.float32, jnp.bfloat16
    blk = SWA_BLOCK
    k2 = jnp.concatenate([kp_ref[...], kc_ref[...]], axis=0).astype(bf16)
    v2 = jnp.concatenate([vp_ref[...], vc_ref[...]], axis=0).astype(bf16)
    qi = lax.broadcasted_iota(jnp.int32, (blk, 2 * blk), 0)
    kj = lax.broadcasted_iota(jnp.int32, (blk, 2 * blk), 1)
    dist = qi + blk - kj
    valid = (dist >= 0) & (dist < SWA_WINDOW) & ((kj >= blk) | (pl.program_id(1) > 0))
    distf = dist.astype(f32)
    lane_kv = lax.broadcasted_iota(jnp.int32, (blk, WIDE), 1) // SWA_HEAD_DIM
    grp = SWA_HEADS // SWA_KV_HEADS
    outs, ssq = [], jnp.zeros((blk, 1), f32)
    for h in range(SWA_HEADS):
        q = (q_ref[:, h * WIDE:(h + 1) * WIDE] * (SWA_HEAD_DIM ** -0.5)).astype(bf16)
        sc = lax.dot_general(q, k2, NT_DIMS, preferred_element_type=f32)
        sc = sc - (2.0 ** (-8.0 * (h + 1) / SWA_HEADS)) * distf
        sc = jnp.where(valid, sc, NEG_INF)
        sink = sink_ref[h]
        m = jnp.maximum(jnp.max(sc, axis=-1, keepdims=True), sink)
        p = jnp.exp(sc - m)
        p = p / (jnp.sum(p, axis=-1, keepdims=True) + jnp.exp(sink - m))
        o = jnp.dot(p.astype(bf16), v2, preferred_element_type=f32)
        o = jnp.where(lane_kv == h // grp, o, 0.0)
        ssq = ssq + jnp.sum(o * o, axis=-1, keepdims=True)
        outs.append(o)
    inv = lax.rsqrt(ssq * (1.0 / SWA_WIDTH) + RMS_EPS)
    for h in range(SWA_HEADS):
        o_ref[:, h * WIDE:(h + 1) * WIDE] = outs[h] * inv * g_ref[:, h * WIDE:(h + 1) * WIDE]


def _swa(proj, sinks, norm_g_wide):
    bsz, s, _ = proj.shape
    col = lambda width, off, prev: pl.BlockSpec(
        (None, SWA_BLOCK, width), lambda b, n: (b, jnp.maximum(n - 1, 0) if prev else n, off // width))
    return pl.pallas_call(
        _swa_kernel,
        grid=(bsz, s // SWA_BLOCK),
        in_specs=[pl.BlockSpec(memory_space=pltpu.SMEM), col(SWA_QW, COL_QS, False),
                  col(SWA_KV_WIDTH, COL_KS, True), col(SWA_KV_WIDTH, COL_KS, False),
                  col(SWA_KV_WIDTH, COL_VS, True), col(SWA_KV_WIDTH, COL_VS, False),
                  pl.BlockSpec((1, SWA_QW), lambda b, n: (0, 0))],
        out_specs=pl.BlockSpec((None, SWA_BLOCK, SWA_QW), lambda b, n: (b, n, 0)),
        out_shape=jax.ShapeDtypeStruct((bsz, s, SWA_QW), jnp.float32),
        compiler_params=pltpu.CompilerParams(dimension_semantics=("arbitrary", "arbitrary")),
        name="swa",
    )(sinks, proj, proj, proj, proj, proj, norm_g_wide)


def _out_proj_kernel(x_ref, og_ref, os_ref, w_ref, gate_ref, g2_ref, sh_ref, sc_ref, x1_ref, h2_ref):
    mix = jnp.concatenate([og_ref[...], os_ref[...]], axis=-1).astype(jnp.bfloat16)
    x1 = x_ref[...] + gate_ref[...] * jnp.dot(mix, w_ref[...], preferred_element_type=jnp.float32)
    x1_ref[...] = x1
    h2 = _rms(x1) * g2_ref[...] * (1.0 + sc_ref[...]) + sh_ref[...]
    for c in range(ROW_SUB):
        h2_ref[:, c, :] = h2[:, c * LANES:(c + 1) * LANES]


def _out_proj(x, o_gla, o_swa, w, gate1, g2, shift2, scale2):
    bsz, s, d = x.shape
    row = lambda width: pl.BlockSpec((None, ROW_TILE, width), lambda b, i: (b, i, 0))
    per_b = pl.BlockSpec((None, 1, d), lambda b, i: (b, 0, 0))
    shp = jax.ShapeDtypeStruct((bsz, s, d), jnp.float32)
    return pl.pallas_call(
        _out_proj_kernel,
        grid=(bsz, s // ROW_TILE),
        in_specs=[row(d), row(GLA_WIDTH), row(SWA_QW), pl.BlockSpec((MIX_W, d), lambda b, i: (0, 0)),
                  per_b, pl.BlockSpec((1, d), lambda b, i: (0, 0)), per_b, per_b],
        out_specs=[row(d), pl.BlockSpec((None, ROW_TILE, ROW_SUB, d // ROW_SUB), lambda b, i: (b, i, 0, 0))],
        out_shape=[shp, jax.ShapeDtypeStruct((bsz, s, ROW_SUB, d // ROW_SUB), jnp.float32)],
        compiler_params=pltpu.CompilerParams(dimension_semantics=("arbitrary", "arbitrary"),
                                             vmem_limit_bytes=48 * 1024 * 1024),
        name="out_proj",
    )(x, o_gla, o_swa, w, gate1, g2.reshape(1, d), shift2, scale2)


GLA_SUB = 16
GLA_BLOCK = 128
GLA_PAIR = 2 * GLA_DK
GLA_NB = 1


def _gla_kernel(q_ref, k_ref, v_ref, alr_ref, r_ref, wa_ref, ba_ref, g_ref, tri_ref, o_ref, st_ref, b_ref, q2_ref):
    f32, bf16 = jnp.float32, jnp.bfloat16

    @pl.when(pl.program_id(1) == 0)
    def _():
        st_ref[...] = jnp.zeros(st_ref.shape, f32)

    for bb in range(GLA_NB):
        z = jnp.dot(alr_ref[bb].astype(bf16), wa_ref[...], preferred_element_type=f32) + ba_ref[...]
        log_a = -(jnp.maximum(-z, 0.0) + jnp.log1p(jnp.exp(-jnp.abs(z)))) * (1.0 / GLA_GATE_TEMP)
        b_ref[bb] = jnp.dot(tri_ref[...], log_a, preferred_element_type=f32, precision=lax.Precision.HIGHEST)
        q2_ref[bb] = q_ref[bb] * (GLA_DK ** -0.5)
    lane = lax.broadcasted_iota(jnp.int32, (GLA_SUB, GLA_PAIR), 1)
    sub = lax.broadcasted_iota(jnp.int32, (GLA_SUB, GLA_PAIR), 0)
    first = lane < GLA_DK
    lane_s = lax.broadcasted_iota(jnp.int32, (GLA_DV, GLA_PAIR), 1) < GLA_DK

    chains = [(bb, hp) for bb in range(GLA_NB) for hp in range(GLA_HEADS // 2)]
    states = [st_ref[i] for i in range(len(chains))]
    for c in range(GLA_BLOCK // GLA_SUB):
        rows = pl.ds(c * GLA_SUB, GLA_SUB)
        for ci, (bb, hp) in enumerate(chains):
            cols = pl.ds(hp * GLA_PAIR, GLA_PAIR)
            state = states[ci]
            b = b_ref[bb, rows, cols]
            b_last = b_ref[bb, pl.ds(c * GLA_SUB + GLA_SUB - 1, 1), cols]
            q2, k2 = q2_ref[bb, rows, cols], k_ref[bb, rows, cols]
            qe = q2 * jnp.exp(b)
            kd = (k2 * jnp.exp(b_last - b)).astype(bf16)
            sb = state.astype(bf16)
            cols_a, cols_b = [], []
            for t in range(GLA_SUB):
                bt = b_ref[bb, pl.ds(c * GLA_SUB + t, 1), cols]
                qt = q2_ref[bb, pl.ds(c * GLA_SUB + t, 1), cols]
                p = (qt * k2) * jnp.exp(bt - b)
                keep = sub[:, :1] <= t
                cols_a.append(jnp.where(keep, jnp.sum(jnp.where(first, p, 0.0), axis=1, keepdims=True), 0.0))
                cols_b.append(jnp.where(keep, jnp.sum(jnp.where(first, 0.0, p), axis=1, keepdims=True), 0.0))
            ut = []
            for j, sc_cols in enumerate((cols_a, cols_b)):
                head = 2 * hp + j
                vcols = pl.ds(head * GLA_DV, GLA_DV)
                vh = v_ref[bb, rows, vcols]
                qm = jnp.where(first if j == 0 else ~first, qe, 0.0).astype(bf16)
                o = lax.dot_general(qm, sb, NT_DIMS, preferred_element_type=f32)
                o = o + jnp.concatenate([jnp.sum(cj * vh, axis=0, keepdims=True) for cj in sc_cols], axis=0)
                ms = jnp.mean(o * o, axis=1, keepdims=True)
                r = r_ref[bb, rows, vcols]
                o_ref[bb, rows, vcols] = (o * lax.rsqrt(ms + RMS_EPS) * g_ref[:, vcols]) * (r * jax.nn.sigmoid(r))
                ut.append(jnp.dot(vh.T.astype(bf16), kd, preferred_element_type=f32))
            states[ci] = jnp.exp(b_last) * state + jnp.where(lane_s, ut[0], ut[1])
    for ci in range(len(chains)):
        st_ref[ci] = states[ci]


def _gla_tri():
    r = np.arange(GLA_BLOCK)
    return jnp.asarray(((r[:, None] // GLA_SUB == r[None, :] // GLA_SUB) & (r[None, :] <= r[:, None])).astype(np.float32))


def _gla(proj, w_alpha, b_alpha, norm_g):
    bsz, s, _ = proj.shape
    rank = w_alpha.shape[0]
    wa = jnp.pad(w_alpha, ((0, ALR_W - rank), (0, 0))).astype(jnp.bfloat16)
    col = lambda width, off: pl.BlockSpec((GLA_NB, GLA_BLOCK, width), lambda b, i: (b, i, off // width))
    const = lambda shape: pl.BlockSpec(shape, lambda b, i: (0, 0))
    return pl.pallas_call(
        _gla_kernel,
        grid=(bsz // GLA_NB, s // GLA_BLOCK),
        in_specs=[col(GLA_QK_WIDTH, COL_QG), col(GLA_QK_WIDTH, COL_KG), col(GLA_WIDTH, COL_VG), col(ALR_W, COL_ALR),
                  col(GLA_WIDTH, COL_RG), const((ALR_W, GLA_QK_WIDTH)), const((1, GLA_QK_WIDTH)),
                  const((1, GLA_WIDTH)), const((GLA_BLOCK, GLA_BLOCK))],
        out_specs=pl.BlockSpec((GLA_NB, GLA_BLOCK, GLA_WIDTH), lambda b, i: (b, i, 0)),
        out_shape=jax.ShapeDtypeStruct((bsz, s, GLA_WIDTH), jnp.float32),
        scratch_shapes=[pltpu.VMEM((GLA_NB * GLA_HEADS // 2, GLA_DV, GLA_PAIR), jnp.float32),
                        pltpu.VMEM((GLA_NB, GLA_BLOCK, GLA_QK_WIDTH), jnp.float32),
                        pltpu.VMEM((GLA_NB, GLA_BLOCK, GLA_QK_WIDTH), jnp.float32)],
        compiler_params=pltpu.CompilerParams(dimension_semantics=("arbitrary", "arbitrary")),
        name="gla",
    )(proj, proj, proj, proj, proj, wa, b_alpha.reshape(1, -1), norm_g.reshape(1, -1), _gla_tri())


def peer_mixer(h_tiles, w_q, keys_1, keys_2, u_tab, v_tab):
    bsz, s, rs, rl = h_tiles.shape
    ht = h_tiles.reshape(bsz * s, rs, rl)
    experts, gates = _peer_route(ht, w_q, keys_1, keys_2)
    return _peer_experts(ht, experts, gates, u_tab, v_tab).reshape(bsz, s, rs, rl)


ROUTE_TT = 256


def _top16_rows(s, code, big):
    vals, codes = [], []
    for _ in range(PEER_TOPK):
        m = jnp.max(s, axis=0, keepdims=True)
        c = jnp.min(jnp.where(s == m, code, big), axis=0, keepdims=True)
        s = jnp.where(code == c, NEG_INF, s)
        vals.append(m)
        codes.append(c)
    return vals, codes


def _peer_route_kernel(h_ref, wq_ref, k1_ref, k2_ref, e_ref, g_ref, q_ref):
    half = PEER_QUERY_DIM // 2
    tt = ROUTE_TT
    h = jnp.concatenate([h_ref[:, c, :] for c in range(h_ref.shape[1])], axis=-1)
    q_ref[...] = jnp.dot(h.astype(jnp.bfloat16), wq_ref[...], preferred_element_type=jnp.float32)
    key_id = lax.broadcasted_iota(jnp.int32, (PEER_N_KEYS, 2 * tt), 0).astype(jnp.float32)
    sub8 = lax.broadcasted_iota(jnp.int32, (8, tt), 0)
    nt = (((1,), (1,)), ((), ()))

    def head_body(hd, carry):
        off = pl.multiple_of(hd * PEER_QUERY_DIM, PEER_QUERY_DIM)
        q1 = q_ref[:, pl.ds(off, half)].astype(jnp.bfloat16)
        q2 = q_ref[:, pl.ds(off + half, half)].astype(jnp.bfloat16)
        s1 = lax.dot_general(k1_ref[...], q1, nt, preferred_element_type=jnp.float32)
        s2 = lax.dot_general(k2_ref[...], q2, nt, preferred_element_type=jnp.float32)
        vals, ids = _top16_rows(jnp.concatenate([s1, s2], axis=1), key_id, float(PEER_N_KEYS))
        v = jnp.concatenate(vals, axis=0)
        ix = jnp.concatenate(ids, axis=0).astype(jnp.int32)
        v1, v2, i1, i2 = v[:, :tt], v[:, tt:], ix[:, :tt], ix[:, tt:]
        cand, code = [], []
        for a, b0 in [(0, 0), (0, 8), (1, 0), (2, 0), (3, 0), (4, 0), (5, 0), (6, 0), (7, 0)]:
            cand.append(v1[a:a + 1] + v2[b0:b0 + 8])
            flat = a * PEER_TOPK + b0 + sub8
            code.append((flat << 14) | (i1[a:a + 1] * PEER_N_KEYS + i2[b0:b0 + 8]))
        cand.append(v1[8:16] + v2[0:1])
        code.append((((8 + sub8) * PEER_TOPK) << 14) | (i1[8:16] * PEER_N_KEYS + i2[0:1]))
        sc, cc = _top16_rows(jnp.concatenate(cand, axis=0), jnp.concatenate(code, axis=0).astype(jnp.float32),
                             float(1 << 30))
        sc = jnp.concatenate(sc, axis=0)
        ex = jnp.exp(sc - sc[0:1])
        g_ref[hd] = ex / jnp.sum(ex, axis=0, keepdims=True)
        e_ref[hd] = jnp.concatenate(cc, axis=0).astype(jnp.int32) & (PEER_N_EXPERTS - 1)
        return carry

    lax.fori_loop(0, PEER_HEADS, head_body, 0)


def _peer_route(hf, w_q, keys_1, keys_2):
    t, rs, rl = hf.shape
    d = rs * rl
    qw = PEER_HEADS * PEER_QUERY_DIM
    const = lambda shape: pl.BlockSpec(shape, lambda i: tuple(0 for _ in shape))
    out_spec = pl.BlockSpec((PEER_HEADS, PEER_TOPK, ROUTE_TT), lambda i: (0, 0, i))
    e_t, g_t = pl.pallas_call(
        _peer_route_kernel,
        grid=(t // ROUTE_TT,),
        in_specs=[pl.BlockSpec((ROUTE_TT, rs, rl), lambda i: (i, 0, 0)), const((d, qw)),
                  const(keys_1.shape), const(keys_2.shape)],
        out_specs=[out_spec, out_spec],
        out_shape=[jax.ShapeDtypeStruct((PEER_HEADS, PEER_TOPK, t), jnp.int32),
                   jax.ShapeDtypeStruct((PEER_HEADS, PEER_TOPK, t), jnp.float32)],
        scratch_shapes=[pltpu.VMEM((ROUTE_TT, qw), jnp.float32)],
        compiler_params=pltpu.CompilerParams(dimension_semantics=("arbitrary",), vmem_limit_bytes=48 * 1024 * 1024),
        name="peer_route",
    )(hf, w_q.astype(jnp.bfloat16), keys_1.astype(jnp.bfloat16), keys_2.astype(jnp.bfloat16))
    return e_t, g_t


PEER_PAIRS = PEER_HEADS * PEER_TOPK
PEER_TB = 64
PEER_TOK_UNROLL = 4
HALF_D = D_MODEL // 2
WORD_SUB, WORD_LANE = HALF_D // 128, 128
ROW_SUB = 2 * WORD_SUB
HI_MASK = -65536
IDX_WAYS = 8


PACK_EB = 512
TAB_PAD = PACK_EB * WORD_SUB


def _pack_kernel(x_ref, o_ref):
    i = pl.program_id(0)

    @pl.when((i == 0) | (i == pl.num_programs(0) - 1))
    def _():
        o_ref[...] = jnp.zeros(o_ref.shape, o_ref.dtype)

    @pl.when((i > 0) & (i < pl.num_programs(0) - 1))
    def _():
        x = x_ref[...].astype(jnp.bfloat16).astype(jnp.float32)
        bits = lax.bitcast_convert_type(x, jnp.int32)
        word = lax.shift_right_logical(bits[:, :HALF_D], 16) | (bits[:, HALF_D:] & HI_MASK)
        for c in range(WORD_SUB):
            o_ref[:, c, :] = word[:, c * WORD_LANE:(c + 1) * WORD_LANE]


def _pack_table(tab):
    e, d = tab.shape
    nblk = e // PACK_EB
    out = pl.pallas_call(
        _pack_kernel,
        grid=(nblk + 2,),
        in_specs=[pl.BlockSpec((PACK_EB, d), lambda i: (jnp.clip(i - 1, 0, nblk - 1), 0))],
        out_specs=pl.BlockSpec((PACK_EB, WORD_SUB, WORD_LANE), lambda i: (i, 0, 0)),
        out_shape=jax.ShapeDtypeStruct(((nblk + 2) * PACK_EB, WORD_SUB, WORD_LANE), jnp.int32),
        name="pack_table",
    )(tab)
    return out.reshape((nblk + 2) * TAB_PAD, WORD_LANE)


def _pair_tile(tab_ref, ia, ib, upper):
    wa = tab_ref[pl.ds(pl.multiple_of(ia, WORD_SUB), ROW_SUB), :]
    wb = tab_ref[pl.ds(pl.multiple_of(ib - WORD_SUB, WORD_SUB), ROW_SUB), :]
    return jnp.where(upper, wa, wb)


_ORDER = (6, 2, 4, 0, 7, 3, 5, 1)


def _fold_pairs(ps, sub):
    b1 = (sub & 2) != 0
    w = []
    for i in range(2):
        x, y = ps[2 * i], ps[2 * i + 1]
        w.append(jnp.where(b1, x + pltpu.roll(x, 2, 0), y + pltpu.roll(y, 6, 0)))
    b0 = (sub & 1) != 0
    x, y = w
    return jnp.where(b0, x + pltpu.roll(x, 1, 0), y + pltpu.roll(y, 7, 0))


def _peer_u_kernel(*refs):
    idx_refs = refs[:IDX_WAYS]
    h_ref, tab_ref, a_ref = refs[IDX_WAYS:]
    sub = lax.broadcasted_iota(jnp.int32, (ROW_SUB, WORD_LANE), 0)
    lane = lax.broadcasted_iota(jnp.int32, (ROW_SUB, WORD_LANE), 1)
    n_groups = PEER_PAIRS // IDX_WAYS

    def folded_tiles(t):
        h = h_ref[t]
        h_lo = jnp.concatenate([h[:WORD_SUB], h[:WORD_SUB]], axis=0)
        h_hi = jnp.concatenate([h[WORD_SUB:], h[WORD_SUB:]], axis=0)
        ways = [r.at[pl.ds(t * n_groups, n_groups)] for r in idx_refs]
        tiles = []
        for g in range(n_groups):
            ps = []
            for i in range(4):
                w = _pair_tile(tab_ref, ways[2 * i][g], ways[2 * i + 1][g], sub < WORD_SUB)
                lo = lax.bitcast_convert_type(w << 16, jnp.float32)
                hi = lax.bitcast_convert_type(w & HI_MASK, jnp.float32)
                ps.append(lo * h_lo + hi * h_hi)
            tiles.append(_fold_pairs(ps, sub))
        return tuple(tiles)

    def lane_sums(t, tiles, acc):
        for g in range(n_groups):
            acc = jnp.where(lane == (t % 8) * n_groups + g, jnp.sum(tiles[g], axis=1, keepdims=True), acc)
        a_ref[t // 8] = acc
        return acc

    def tok_body(t, carry):
        acc, prev = carry
        return lane_sums(t - 1, prev, acc), folded_tiles(t)

    acc, last = lax.fori_loop(1, PEER_TB, tok_body,
                              (jnp.zeros((ROW_SUB, WORD_LANE), jnp.float32), folded_tiles(0)))
    lane_sums(PEER_TB - 1, last, acc)


def _split3_lanes(x, tb):
    mask = lambda v: lax.bitcast_convert_type(lax.bitcast_convert_type(v, jnp.int32) & HI_MASK, jnp.float32)
    hi = mask(x)
    mid = mask(x - hi)
    lo = x - hi - mid
    part = lax.broadcasted_iota(jnp.int32, x.shape, 1) // tb
    return jnp.where(part == 0, hi, jnp.where(part == 1, mid, jnp.where(part == 2, lo, 0.0))).astype(jnp.bfloat16)


def _peer_v_kernel(*refs):
    idx_refs = refs[:IDX_WAYS]
    a_ref, g_ref, tab_ref, sel_ref, o_ref, wrep_ref = refs[IDX_WAYS:]
    a = a_ref[...]
    w = g_ref[...] * (0.5 * a * (1.0 + lax.erf(a * (2.0 ** -0.5))))
    wcat = _split3_lanes(jnp.concatenate([w, w, w, w], axis=0).T, PEER_TB)
    wrep_ref[...] = jnp.dot(wcat, sel_ref[...], preferred_element_type=jnp.float32)
    upper = lax.broadcasted_iota(jnp.int32, (ROW_SUB, WORD_LANE), 0) < WORD_SUB
    per_way = PEER_PAIRS // IDX_WAYS

    def one_token(t):
        col = pl.multiple_of(t * WORD_LANE, WORD_LANE)
        ways = [r.at[pl.ds(t * per_way, per_way)] for r in idx_refs]
        z = jnp.zeros((ROW_SUB, WORD_LANE), jnp.float32)
        acc_lo, acc_hi = [z, z], [z, z]
        for m in range(per_way):
            for jj in range(IDX_WAYS // 2):
                p = m * IDX_WAYS + 2 * jj
                words = _pair_tile(tab_ref, ways[2 * jj][m], ways[2 * jj + 1][m], upper)
                w2 = jnp.where(upper, wrep_ref[pl.ds(p, 1), pl.ds(col, WORD_LANE)],
                               wrep_ref[pl.ds(p + 1, 1), pl.ds(col, WORD_LANE)])
                k = jj % 2
                acc_lo[k] = acc_lo[k] + lax.bitcast_convert_type(words << 16, jnp.float32) * w2
                acc_hi[k] = acc_hi[k] + lax.bitcast_convert_type(words & HI_MASK, jnp.float32) * w2
        lo, hi = acc_lo[0] + acc_lo[1], acc_hi[0] + acc_hi[1]
        o_ref[t] = jnp.concatenate([lo[:WORD_SUB] + lo[WORD_SUB:], hi[:WORD_SUB] + hi[WORD_SUB:]], axis=0)

    def tok_body(i, carry):
        for j in range(PEER_TOK_UNROLL):
            one_token(PEER_TOK_UNROLL * i + j)
        return carry

    lax.fori_loop(0, PEER_TB // PEER_TOK_UNROLL, tok_body, 0)


_PEER_VMEM_LIMIT = 56 * 1024 * 1024


def _peer_experts(hf, experts, gates, u_tab, v_tab):
    t = hf.shape[0]
    nblk = t // PEER_TB
    smem_spec = pl.BlockSpec((PEER_TB * PEER_PAIRS // IDX_WAYS,), lambda i: (i,), memory_space=pltpu.SMEM)
    tab_spec = pl.BlockSpec(memory_space=pltpu.VMEM)
    tok_spec = pl.BlockSpec((PEER_TB, PEER_PAIRS), lambda i: (i, 0))
    row_spec = pl.BlockSpec((PEER_TB, ROW_SUB, WORD_LANE), lambda i: (i, 0, 0))
    params = pltpu.CompilerParams(dimension_semantics=("arbitrary",), vmem_limit_bytes=_PEER_VMEM_LIMIT)
    assert IDX_WAYS == PEER_HEADS
    idx_ways = [(experts[j] * WORD_SUB + TAB_PAD).T.reshape(-1) for j in range(IDX_WAYS)]
    gates = gates.transpose(2, 1, 0).reshape(t, PEER_PAIRS)
    a_tiles = pl.pallas_call(
        _peer_u_kernel,
        grid=(nblk,),
        in_specs=[smem_spec] * IDX_WAYS + [row_spec, tab_spec],
        out_specs=pl.BlockSpec((PEER_TB // 8, ROW_SUB, WORD_LANE), lambda i: (i, 0, 0)),
        out_shape=jax.ShapeDtypeStruct((t // 8, ROW_SUB, WORD_LANE), jnp.float32),
        compiler_params=params, name="peer_u",
    )(*idx_ways, hf, _pack_table(u_tab))
    inv = np.argsort(np.array(_ORDER))
    a = a_tiles.reshape(t // 8, 8, 8, PEER_PAIRS // 8)[:, inv].transpose(0, 2, 3, 1).reshape(t, PEER_PAIRS)
    eye = np.repeat(np.eye(PEER_TB, dtype=np.float32), WORD_LANE, axis=1)
    sel = jnp.asarray(np.concatenate([eye, eye, eye, np.zeros_like(eye)], axis=0), jnp.bfloat16)
    out = pl.pallas_call(
        _peer_v_kernel,
        grid=(nblk,),
        in_specs=[smem_spec] * IDX_WAYS + [tok_spec, tok_spec, tab_spec, pl.BlockSpec(sel.shape, lambda i: (0, 0))],
        out_specs=row_spec,
        out_shape=jax.ShapeDtypeStruct((t, ROW_SUB, WORD_LANE), jnp.float32),
        scratch_shapes=[pltpu.VMEM((PEER_PAIRS, PEER_TB * WORD_LANE), jnp.float32)],
        compiler_params=params, name="peer_v",
    )(*idx_ways, a, gates, _pack_table(v_tab), sel)
    return out


def _final_kernel(x_ref, p_ref, gate_ref, g_ref, o_ref):
    peer = jnp.concatenate([p_ref[:, c, :] for c in range(ROW_SUB)], axis=-1)
    o_ref[...] = _rms(x_ref[...] + gate_ref[...] * peer) * g_ref[...]


def _final(x1, peer, gate2, g):
    bsz, s, d = x1.shape
    row = pl.BlockSpec((None, ROW_TILE, d), lambda b, i: (b, i, 0))
    return pl.pallas_call(
        _final_kernel,
        grid=(bsz, s // ROW_TILE),
        in_specs=[row, pl.BlockSpec((None, ROW_TILE, ROW_SUB, d // ROW_SUB), lambda b, i: (b, i, 0, 0)),
                  pl.BlockSpec((None, 1, d), lambda b, i: (b, 0, 0)), pl.BlockSpec((1, d), lambda b, i: (0, 0))],
        out_specs=row,
        out_shape=jax.ShapeDtypeStruct((bsz, s, d), jnp.float32), name="final",
    )(x1, peer, gate2, g.reshape(1, d))


def _in_proj_weights(w_in):
    o = np.cumsum([0, GLA_QK_WIDTH, GLA_QK_WIDTH, GLA_WIDTH, GLA_GATE_RANK, GLA_WIDTH, SWA_WIDTH, SWA_KV_WIDTH])
    src = np.full((PROJ_W,), IN_WIDTH, np.int32)
    src[COL_QS + _wide_cols()] = o[5] + np.arange(SWA_WIDTH)
    for dst, start, width in ((COL_QG, o[0], GLA_QK_WIDTH), (COL_KG, o[1], GLA_QK_WIDTH), (COL_VG, o[2], GLA_WIDTH),
                              (COL_ALR, o[3], GLA_GATE_RANK), (COL_RG, o[4], GLA_WIDTH),
                              (COL_KS, o[6], SWA_KV_WIDTH), (COL_VS, o[7], SWA_KV_WIDTH)):
        src[dst:dst + width] = start + np.arange(width)
    w_ext = jnp.concatenate([w_in, jnp.zeros((w_in.shape[0], 1), w_in.dtype)], axis=1)
    return w_ext[:, src].astype(jnp.bfloat16)


def _out_proj_weights(w_out):
    src = np.full((MIX_W,), MIX_WIDTH, np.int32)
    src[:GLA_WIDTH] = np.arange(GLA_WIDTH)
    src[GLA_WIDTH + _wide_cols()] = GLA_WIDTH + np.arange(SWA_WIDTH)
    w_ext = jnp.concatenate([w_out, jnp.zeros((1, w_out.shape[1]), w_out.dtype)], axis=0)
    return w_ext[src].astype(jnp.bfloat16)


def kernel(x, c, w_ada, b_ada, norm1_g, norm2_g, w_in, w_gla_alpha, b_gla_alpha, gla_norm_g,
           swa_sinks, swa_norm_g, w_out, w_peer_q, peer_keys_1, peer_keys_2, peer_u, peer_v, final_g):
    assert w_ada.shape[0] == 1, "single-layer block"
    l = 0
    mod = _ada(c, w_ada[l], b_ada[l])
    shift1, scale1, gate1, shift2, scale2, gate2 = [m[:, None, :] for m in jnp.split(mod, 6, axis=-1)]
    proj = _in_proj(x, norm1_g[l], shift1, scale1, _in_proj_weights(w_in[l]))
    o_gla = _gla(proj, w_gla_alpha[l], b_gla_alpha[l], gla_norm_g[l])
    g_wide = jnp.zeros((1, SWA_QW), jnp.float32).at[0, _wide_cols()].set(swa_norm_g[l])
    o_swa = _swa(proj, swa_sinks[l], g_wide)
    x1, h2 = _out_proj(x, o_gla, o_swa, _out_proj_weights(w_out[l]), gate1, norm2_g[l], shift2, scale2)
    peer = peer_mixer(h2, w_peer_q[l], peer_keys_1[l], peer_keys_2[l], peer_u[l], peer_v[l])
    return _final(x1, peer, gate2, final_g)
```

```python
import jax, jax.numpy as jnp
from jax import lax
import numpy as np
from jax.experimental import pallas as pl
from jax.experimental.pallas import tpu as pltpu

D_MODEL = 1024
BATCH = 4
SEQ = 8192
DEPTH = 1

GLA_HEADS = 4
GLA_DK = 64
GLA_DV = 128
GLA_GATE_RANK = 16
GLA_GATE_TEMP = 16.0
GLA_CHUNK = 64
SWA_HEADS = 8
SWA_KV_HEADS = 2
SWA_HEAD_DIM = 64
SWA_WINDOW = 128
SWA_BLOCK = 128
PEER_HEADS = 8
PEER_N_KEYS = 128
PEER_N_EXPERTS = PEER_N_KEYS * PEER_N_KEYS
PEER_QUERY_DIM = 256
PEER_TOPK = 16
RMS_EPS = 1e-6

GLA_QK_WIDTH = GLA_HEADS * GLA_DK
GLA_WIDTH = GLA_HEADS * GLA_DV
SWA_WIDTH = SWA_HEADS * SWA_HEAD_DIM
SWA_KV_WIDTH = SWA_KV_HEADS * SWA_HEAD_DIM
MIX_WIDTH = GLA_WIDTH + SWA_WIDTH
IN_WIDTH = 2 * GLA_QK_WIDTH + GLA_WIDTH + GLA_GATE_RANK + GLA_WIDTH + SWA_WIDTH + 2 * SWA_KV_WIDTH


LANES = 128
NEG_INF = float("-inf")
NT_DIMS = (((1,), (1,)), ((), ()))
ROW_TILE = 512
WIDE = 2 * SWA_HEAD_DIM
SWA_QW = SWA_HEADS * WIDE
ALR_W = LANES
COL_QS, COL_QG, COL_KG, COL_VG, COL_RG = 0, SWA_QW, SWA_QW + GLA_QK_WIDTH, SWA_QW + 2 * GLA_QK_WIDTH, SWA_QW + 2 * GLA_QK_WIDTH + GLA_WIDTH
COL_KS = COL_RG + GLA_WIDTH
COL_VS = COL_KS + SWA_KV_WIDTH
COL_ALR = COL_VS + SWA_KV_WIDTH
PROJ_W = COL_ALR + ALR_W
MIX_W = GLA_WIDTH + SWA_QW


def _wide_cols():
    grp = SWA_HEADS // SWA_KV_HEADS
    h = np.arange(SWA_WIDTH) // SWA_HEAD_DIM
    return h * WIDE + (h // grp) * SWA_HEAD_DIM + np.arange(SWA_WIDTH) % SWA_HEAD_DIM


def _rms(x):
    return x * lax.rsqrt(jnp.mean(x * x, axis=-1, keepdims=True) + RMS_EPS)


def _ada_kernel(c_ref, w_ref, b_ref, o_ref):
    c = c_ref[...]
    ca = (c * jax.nn.sigmoid(c)).astype(jnp.bfloat16)
    o_ref[...] = jnp.dot(ca, w_ref[...].astype(jnp.bfloat16), preferred_element_type=jnp.float32) + b_ref[...]


def _ada(c, w, b):
    bsz, d = c.shape
    n = w.shape[1]
    tn = 1024
    return pl.pallas_call(
        _ada_kernel,
        grid=(n // tn,),
        in_specs=[pl.BlockSpec((bsz, d), lambda j: (0, 0)), pl.BlockSpec((d, tn), lambda j: (0, j)),
                  pl.BlockSpec((1, tn), lambda j: (0, j))],
        out_specs=pl.BlockSpec((bsz, tn), lambda j: (0, j)),
        out_shape=jax.ShapeDtypeStruct((bsz, n), jnp.float32), name="ada",
    )(c, w, b.reshape(1, n))


def _in_proj_kernel(x_ref, g_ref, sh_ref, sc_ref, w_ref, o_ref):
    h = _rms(x_ref[...]) * g_ref[...] * (1.0 + sc_ref[...]) + sh_ref[...]
    o_ref[...] = jnp.dot(h.astype(jnp.bfloat16), w_ref[...], preferred_element_type=jnp.float32)


def _in_proj(x, g, shift, scale, w):
    bsz, s, d = x.shape
    n = w.shape[1]
    row = lambda width: pl.BlockSpec((None, ROW_TILE, width), lambda b, i: (b, i, 0))
    per_b = pl.BlockSpec((None, 1, d), lambda b, i: (b, 0, 0))
    return pl.pallas_call(
        _in_proj_kernel,
        grid=(bsz, s // ROW_TILE),
        in_specs=[row(d), pl.BlockSpec((1, d), lambda b, i: (0, 0)), per_b, per_b,
                  pl.BlockSpec((d, n), lambda b, i: (0, 0))],
        out_specs=row(n),
        out_shape=jax.ShapeDtypeStruct((bsz, s, n), jnp.float32),
        compiler_params=pltpu.CompilerParams(dimension_semantics=("arbitrary", "arbitrary"),
                                             vmem_limit_bytes=48 * 1024 * 1024),
        name="in_proj",
    )(x, g.reshape(1, d), shift, scale, w)


def _swa_kernel(sink_ref, q_ref, kp_ref, kc_ref, vp_ref, vc_ref, g_ref, o_ref):
    f32, bf16 = jnp.float32, jnp.bfloat16
    blk = SWA_BLOCK
    k2 = jnp.concatenate([kp_ref[...], kc_ref[...]], axis=0).astype(bf16)
    v2 = jnp.concatenate([vp_ref[...], vc_ref[...]], axis=0).astype(bf16)
    qi = lax.broadcasted_iota(jnp.int32, (blk, 2 * blk), 0)
    kj = lax.broadcasted_iota(jnp.int32, (blk, 2 * blk), 1)
    dist = qi + blk - kj
    valid = (dist >= 0) & (dist < SWA_WINDOW) & ((kj >= blk) | (pl.program_id(1) > 0))
    distf = dist.astype(f32)
    lane_kv = lax.broadcasted_iota(jnp.int32, (blk, WIDE), 1) // SWA_HEAD_DIM
    grp = SWA_HEADS // SWA_KV_HEADS
    heads = range(SWA_HEADS)
    scores = []
    for h in heads:
        q = (q_ref[:, h * WIDE:(h + 1) * WIDE] * (SWA_HEAD_DIM ** -0.5)).astype(bf16)
        scores.append(lax.dot_general(q, k2, NT_DIMS, preferred_element_type=f32))
    probs = []
    for h in heads:
        sc = scores[h] - (2.0 ** (-8.0 * (h + 1) / SWA_HEADS)) * distf
        sc = jnp.where(valid, sc, NEG_INF)
        sink = sink_ref[h]
        m = jnp.maximum(jnp.max(sc, axis=-1, keepdims=True), sink)
        p = jnp.exp(sc - m)
        probs.append((p / (jnp.sum(p, axis=-1, keepdims=True) + jnp.exp(sink - m))).astype(bf16))
    outs = [jnp.dot(probs[h], v2, preferred_element_type=f32) for h in heads]
    outs = [jnp.where(lane_kv == h // grp, outs[h], 0.0) for h in heads]
    ssq = sum(jnp.sum(o * o, axis=-1, keepdims=True) for o in outs)
    inv = lax.rsqrt(ssq * (1.0 / SWA_WIDTH) + RMS_EPS)
    for h in range(SWA_HEADS):
        o_ref[:, h * WIDE:(h + 1) * WIDE] = outs[h] * inv * g_ref[:, h * WIDE:(h + 1) * WIDE]


def _swa(proj, sinks, norm_g_wide):
    bsz, s, _ = proj.shape
    col = lambda width, off, prev: pl.BlockSpec(
        (None, SWA_BLOCK, width), lambda b, n: (b, jnp.maximum(n - 1, 0) if prev else n, off // width))
    return pl.pallas_call(
        _swa_kernel,
        grid=(bsz, s // SWA_BLOCK),
        in_specs=[pl.BlockSpec(memory_space=pltpu.SMEM), col(SWA_QW, COL_QS, False),
                  col(SWA_KV_WIDTH, COL_KS, True), col(SWA_KV_WIDTH, COL_KS, False),
                  col(SWA_KV_WIDTH, COL_VS, True), col(SWA_KV_WIDTH, COL_VS, False),
                  pl.BlockSpec((1, SWA_QW), lambda b, n: (0, 0))],
        out_specs=pl.BlockSpec((None, SWA_BLOCK, SWA_QW), lambda b, n: (b, n, 0)),
        out_shape=jax.ShapeDtypeStruct((bsz, s, SWA_QW), jnp.float32),
        compiler_params=pltpu.CompilerParams(dimension_semantics=("arbitrary", "arbitrary")),
        name="swa",
    )(sinks, proj, proj, proj, proj, proj, norm_g_wide)


def _out_proj_kernel(x_ref, og_ref, os_ref, w_ref, gate_ref, g2_ref, sh_ref, sc_ref, x1_ref, h2_ref):
    mix = jnp.concatenate([og_ref[...], os_ref[...]], axis=-1).astype(jnp.bfloat16)
    x1 = x_ref[...] + gate_ref[...] * jnp.dot(mix, w_ref[...], preferred_element_type=jnp.float32)
    x1_ref[...] = x1
    h2 = _rms(x1) * g2_ref[...] * (1.0 + sc_ref[...]) + sh_ref[...]
    for c in range(ROW_SUB):
        h2_ref[:, c, :] = h2[:, c * LANES:(c + 1) * LANES]


def _out_proj(x, o_gla, o_swa, w, gate1, g2, shift2, scale2):
    bsz, s, d = x.shape
    row = lambda width: pl.BlockSpec((None, ROW_TILE, width), lambda b, i: (b, i, 0))
    per_b = pl.BlockSpec((None, 1, d), lambda b, i: (b, 0, 0))
    shp = jax.ShapeDtypeStruct((bsz, s, d), jnp.float32)
    return pl.pallas_call(
        _out_proj_kernel,
        grid=(bsz, s // ROW_TILE),
        in_specs=[row(d), row(GLA_WIDTH), row(SWA_QW), pl.BlockSpec((MIX_W, d), lambda b, i: (0, 0)),
                  per_b, pl.BlockSpec((1, d), lambda b, i: (0, 0)), per_b, per_b],
        out_specs=[row(d), pl.BlockSpec((None, ROW_TILE, ROW_SUB, d // ROW_SUB), lambda b, i: (b, i, 0, 0))],
        out_shape=[shp, jax.ShapeDtypeStruct((bsz, s, ROW_SUB, d // ROW_SUB), jnp.float32)],
        compiler_params=pltpu.CompilerParams(dimension_semantics=("arbitrary", "arbitrary"),
                                             vmem_limit_bytes=48 * 1024 * 1024),
        name="out_proj",
    )(x, o_gla, o_swa, w, gate1, g2.reshape(1, d), shift2, scale2)


GLA_SUB = 16
GLA_BLOCK = 128
GLA_PAIR = 2 * GLA_DK
GLA_NB = 1


def _gla_kernel(q_ref, k_ref, v_ref, alr_ref, r_ref, wa_ref, ba_ref, g_ref, tri_ref, o_ref, st_ref, b_ref, q2_ref):
    f32, bf16 = jnp.float32, jnp.bfloat16

    @pl.when(pl.program_id(1) == 0)
    def _():
        st_ref[...] = jnp.zeros(st_ref.shape, f32)

    for bb in range(GLA_NB):
        z = jnp.dot(alr_ref[bb].astype(bf16), wa_ref[...], preferred_element_type=f32) + ba_ref[...]
        log_a = -(jnp.maximum(-z, 0.0) + jnp.log1p(jnp.exp(-jnp.abs(z)))) * (1.0 / GLA_GATE_TEMP)
        b_ref[bb] = jnp.dot(tri_ref[...], log_a, preferred_element_type=f32, precision=lax.Precision.HIGHEST)
        q2_ref[bb] = q_ref[bb] * (GLA_DK ** -0.5)
    lane = lax.broadcasted_iota(jnp.int32, (GLA_SUB, GLA_PAIR), 1)
    sub = lax.broadcasted_iota(jnp.int32, (GLA_SUB, GLA_PAIR), 0)
    first = lane < GLA_DK
    lane_s = lax.broadcasted_iota(jnp.int32, (GLA_DV, GLA_PAIR), 1) < GLA_DK

    chains = [(bb, hp) for bb in range(GLA_NB) for hp in range(GLA_HEADS // 2)]
    states = [st_ref[i] for i in range(len(chains))]
    for c in range(GLA_BLOCK // GLA_SUB):
        rows = pl.ds(c * GLA_SUB, GLA_SUB)
        for ci, (bb, hp) in enumerate(chains):
            cols = pl.ds(hp * GLA_PAIR, GLA_PAIR)
            state = states[ci]
            b = b_ref[bb, rows, cols]
            b_last = b_ref[bb, pl.ds(c * GLA_SUB + GLA_SUB - 1, 1), cols]
            q2, k2 = q2_ref[bb, rows, cols], k_ref[bb, rows, cols]
            qe = q2 * jnp.exp(b)
            kd = (k2 * jnp.exp(b_last - b)).astype(bf16)
            sb = state.astype(bf16)
            cols_a, cols_b = [], []
            for t in range(GLA_SUB):
                bt = b_ref[bb, pl.ds(c * GLA_SUB + t, 1), cols]
                qt = q2_ref[bb, pl.ds(c * GLA_SUB + t, 1), cols]
                p = (qt * k2) * jnp.exp(bt - b)
                keep = sub[:, :1] <= t
                cols_a.append(jnp.where(keep, jnp.sum(jnp.where(first, p, 0.0), axis=1, keepdims=True), 0.0))
                cols_b.append(jnp.where(keep, jnp.sum(jnp.where(first, 0.0, p), axis=1, keepdims=True), 0.0))
            ut = []
            for j, sc_cols in enumerate((cols_a, cols_b)):
                head = 2 * hp + j
                vcols = pl.ds(head * GLA_DV, GLA_DV)
                vh = v_ref[bb, rows, vcols]
                qm = jnp.where(first if j == 0 else ~first, qe, 0.0).astype(bf16)
                o = lax.dot_general(qm, sb, NT_DIMS, preferred_element_type=f32)
                o = o + jnp.concatenate([jnp.sum(cj * vh, axis=0, keepdims=True) for cj in sc_cols], axis=0)
                ms = jnp.mean(o * o, axis=1, keepdims=True)
                r = r_ref[bb, rows, vcols]
                o_ref[bb, rows, vcols] = (o * lax.rsqrt(ms + RMS_EPS) * g_ref[:, vcols]) * (r * jax.nn.sigmoid(r))
                ut.append(jnp.dot(vh.T.astype(bf16), kd, preferred_element_type=f32))
            states[ci] = jnp.exp(b_last) * state + jnp.where(lane_s, ut[0], ut[1])
    for ci in range(len(chains)):
        st_ref[ci] = states[ci]


def _gla_tri():
    r = np.arange(GLA_BLOCK)
    return jnp.asarray(((r[:, None] // GLA_SUB == r[None, :] // GLA_SUB) & (r[None, :] <= r[:, None])).astype(np.float32))


def _gla(proj, w_alpha, b_alpha, norm_g):
    bsz, s, _ = proj.shape
    rank = w_alpha.shape[0]
    wa = jnp.pad(w_alpha, ((0, ALR_W - rank), (0, 0))).astype(jnp.bfloat16)
    col = lambda width, off: pl.BlockSpec((GLA_NB, GLA_BLOCK, width), lambda b, i: (b, i, off // width))
    const = lambda shape: pl.BlockSpec(shape, lambda b, i: (0, 0))
    return pl.pallas_call(
        _gla_kernel,
        grid=(bsz // GLA_NB, s // GLA_BLOCK),
        in_specs=[col(GLA_QK_WIDTH, COL_QG), col(GLA_QK_WIDTH, COL_KG), col(GLA_WIDTH, COL_VG), col(ALR_W, COL_ALR),
                  col(GLA_WIDTH, COL_RG), const((ALR_W, GLA_QK_WIDTH)), const((1, GLA_QK_WIDTH)),
                  const((1, GLA_WIDTH)), const((GLA_BLOCK, GLA_BLOCK))],
        out_specs=pl.BlockSpec((GLA_NB, GLA_BLOCK, GLA_WIDTH), lambda b, i: (b, i, 0)),
        out_shape=jax.ShapeDtypeStruct((bsz, s, GLA_WIDTH), jnp.float32),
        scratch_shapes=[pltpu.VMEM((GLA_NB * GLA_HEADS // 2, GLA_DV, GLA_PAIR), jnp.float32),
                        pltpu.VMEM((GLA_NB, GLA_BLOCK, GLA_QK_WIDTH), jnp.float32),
                        pltpu.VMEM((GLA_NB, GLA_BLOCK, GLA_QK_WIDTH), jnp.float32)],
        compiler_params=pltpu.CompilerParams(dimension_semantics=("arbitrary", "arbitrary")),
        name="gla",
    )(proj, proj, proj, proj, proj, wa, b_alpha.reshape(1, -1), norm_g.reshape(1, -1), _gla_tri())


def peer_mixer(h_tiles, w_q, keys_1, keys_2, u_tab, v_tab):
    bsz, s, rs, rl = h_tiles.shape
    ht = h_tiles.reshape(bsz * s, rs, rl)
    experts, gates = _peer_route(ht, w_q, keys_1, keys_2)
    return _peer_experts(ht, experts, gates, u_tab, v_tab).reshape(bsz, s, rs, rl)


ROUTE_TT = 512


def _top16_rows(s, code, big):
    vals, codes = [], []
    for _ in range(PEER_TOPK):
        m = jnp.max(s, axis=0, keepdims=True)
        c = jnp.min(jnp.where(s == m, code, big), axis=0, keepdims=True)
        s = jnp.where(code == c, NEG_INF, s)
        vals.append(m)
        codes.append(c)
    return vals, codes


def _peer_route_kernel(h_ref, wq_ref, k1_ref, k2_ref, e_ref, g_ref, q_ref):
    half = PEER_QUERY_DIM // 2
    tt = ROUTE_TT
    h = jnp.concatenate([h_ref[:, c, :] for c in range(h_ref.shape[1])], axis=-1)
    q_ref[...] = jnp.dot(h.astype(jnp.bfloat16), wq_ref[...], preferred_element_type=jnp.float32)
    key_id = lax.broadcasted_iota(jnp.int32, (PEER_N_KEYS, 2 * tt), 0).astype(jnp.float32)
    sub8 = lax.broadcasted_iota(jnp.int32, (8, tt), 0)
    nt = (((1,), (1,)), ((), ()))

    def head_body(hd, carry):
        off = pl.multiple_of(hd * PEER_QUERY_DIM, PEER_QUERY_DIM)
        q1 = q_ref[:, pl.ds(off, half)].astype(jnp.bfloat16)
        q2 = q_ref[:, pl.ds(off + half, half)].astype(jnp.bfloat16)
        s1 = lax.dot_general(k1_ref[...], q1, nt, preferred_element_type=jnp.float32)
        s2 = lax.dot_general(k2_ref[...], q2, nt, preferred_element_type=jnp.float32)
        vals, ids = _top16_rows(jnp.concatenate([s1, s2], axis=1), key_id, float(PEER_N_KEYS))
        v = jnp.concatenate(vals, axis=0)
        ix = jnp.concatenate(ids, axis=0).astype(jnp.int32)
        v1, v2, i1, i2 = v[:, :tt], v[:, tt:], ix[:, :tt], ix[:, tt:]
        cand, code = [], []
        for a, b0 in [(0, 0), (0, 8), (1, 0), (2, 0), (3, 0), (4, 0), (5, 0), (6, 0), (7, 0)]:
            cand.append(v1[a:a + 1] + v2[b0:b0 + 8])
            flat = a * PEER_TOPK + b0 + sub8
            code.append((flat << 14) | (i1[a:a + 1] * PEER_N_KEYS + i2[b0:b0 + 8]))
        cand.append(v1[8:16] + v2[0:1])
        code.append((((8 + sub8) * PEER_TOPK) << 14) | (i1[8:16] * PEER_N_KEYS + i2[0:1]))
        sc, cc = _top16_rows(jnp.concatenate(cand, axis=0), jnp.concatenate(code, axis=0).astype(jnp.float32),
                             float(1 << 30))
        sc = jnp.concatenate(sc, axis=0)
        ex = jnp.exp(sc - sc[0:1])
        g_ref[hd] = ex / jnp.sum(ex, axis=0, keepdims=True)
        e_ref[hd] = jnp.concatenate(cc, axis=0).astype(jnp.int32) & (PEER_N_EXPERTS - 1)
        return carry

    lax.fori_loop(0, PEER_HEADS, head_body, 0)


def _peer_route(hf, w_q, keys_1, keys_2):
    t, rs, rl = hf.shape
    d = rs * rl
    qw = PEER_HEADS * PEER_QUERY_DIM
    const = lambda shape: pl.BlockSpec(shape, lambda i: tuple(0 for _ in shape))
    out_spec = pl.BlockSpec((PEER_HEADS, PEER_TOPK, ROUTE_TT), lambda i: (0, 0, i))
    e_t, g_t = pl.pallas_call(
        _peer_route_kernel,
        grid=(t // ROUTE_TT,),
        in_specs=[pl.BlockSpec((ROUTE_TT, rs, rl), lambda i: (i, 0, 0)), const((d, qw)),
                  const(keys_1.shape), const(keys_2.shape)],
        out_specs=[out_spec, out_spec],
        out_shape=[jax.ShapeDtypeStruct((PEER_HEADS, PEER_TOPK, t), jnp.int32),
                   jax.ShapeDtypeStruct((PEER_HEADS, PEER_TOPK, t), jnp.float32)],
        scratch_shapes=[pltpu.VMEM((ROUTE_TT, qw), jnp.float32)],
        compiler_params=pltpu.CompilerParams(dimension_semantics=("arbitrary",), vmem_limit_bytes=48 * 1024 * 1024),
        name="peer_route",
    )(hf, w_q.astype(jnp.bfloat16), keys_1.astype(jnp.bfloat16), keys_2.astype(jnp.bfloat16))
    return e_t, g_t


PEER_PAIRS = PEER_HEADS * PEER_TOPK
PEER_TB = 64
PEER_TOK_UNROLL = 4
HALF_D = D_MODEL // 2
WORD_SUB, WORD_LANE = HALF_D // 128, 128
ROW_SUB = 2 * WORD_SUB
HI_MASK = -65536
IDX_WAYS = 8


PACK_EB = 512
TAB_PAD = PACK_EB * WORD_SUB


def _pack_kernel(x_ref, o_ref):
    i = pl.program_id(0)

    @pl.when((i == 0) | (i == pl.num_programs(0) - 1))
    def _():
        o_ref[...] = jnp.zeros(o_ref.shape, o_ref.dtype)

    @pl.when((i > 0) & (i < pl.num_programs(0) - 1))
    def _():
        x = x_ref[...].astype(jnp.bfloat16).astype(jnp.float32)
        bits = lax.bitcast_convert_type(x, jnp.int32)
        word = lax.shift_right_logical(bits[:, :HALF_D], 16) | (bits[:, HALF_D:] & HI_MASK)
        for c in range(WORD_SUB):
            o_ref[:, c, :] = word[:, c * WORD_LANE:(c + 1) * WORD_LANE]


def _pack_table(tab):
    e, d = tab.shape
    nblk = e // PACK_EB
    out = pl.pallas_call(
        _pack_kernel,
        grid=(nblk + 2,),
        in_specs=[pl.BlockSpec((PACK_EB, d), lambda i: (jnp.clip(i - 1, 0, nblk - 1), 0))],
        out_specs=pl.BlockSpec((PACK_EB, WORD_SUB, WORD_LANE), lambda i: (i, 0, 0)),
        out_shape=jax.ShapeDtypeStruct(((nblk + 2) * PACK_EB, WORD_SUB, WORD_LANE), jnp.int32),
        name="pack_table",
    )(tab)
    return out.reshape((nblk + 2) * TAB_PAD, WORD_LANE)


def _pair_tile(tab_ref, ia, ib, upper):
    wa = tab_ref[pl.ds(pl.multiple_of(ia, WORD_SUB), ROW_SUB), :]
    wb = tab_ref[pl.ds(pl.multiple_of(ib - WORD_SUB, WORD_SUB), ROW_SUB), :]
    return jnp.where(upper, wa, wb)


_ORDER = (6, 2, 4, 0, 7, 3, 5, 1)


def _fold_pairs(ps, sub):
    b1 = (sub & 2) != 0
    w = []
    for i in range(2):
        x, y = ps[2 * i], ps[2 * i + 1]
        w.append(jnp.where(b1, x + pltpu.roll(x, 2, 0), y + pltpu.roll(y, 6, 0)))
    b0 = (sub & 1) != 0
    x, y = w
    return jnp.where(b0, x + pltpu.roll(x, 1, 0), y + pltpu.roll(y, 7, 0))


def _peer_u_kernel(*refs):
    idx_refs = refs[:IDX_WAYS]
    h_ref, tab_ref, a_ref = refs[IDX_WAYS:]
    sub = lax.broadcasted_iota(jnp.int32, (ROW_SUB, WORD_LANE), 0)
    lane = lax.broadcasted_iota(jnp.int32, (ROW_SUB, WORD_LANE), 1)
    n_groups = PEER_PAIRS // IDX_WAYS

    def folded_tiles(t):
        h = h_ref[t]
        h_lo = jnp.concatenate([h[:WORD_SUB], h[:WORD_SUB]], axis=0)
        h_hi = jnp.concatenate([h[WORD_SUB:], h[WORD_SUB:]], axis=0)
        ways = [r.at[pl.ds(t * n_groups, n_groups)] for r in idx_refs]
        tiles = []
        for g in range(n_groups):
            ps = []
            for i in range(4):
                w = _pair_tile(tab_ref, ways[2 * i][g], ways[2 * i + 1][g], sub < WORD_SUB)
                lo = lax.bitcast_convert_type(w << 16, jnp.float32)
                hi = lax.bitcast_convert_type(w & HI_MASK, jnp.float32)
                ps.append(lo * h_lo + hi * h_hi)
            tiles.append(_fold_pairs(ps, sub))
        return tuple(tiles)

    def lane_sums(t, tiles, acc):
        for g in range(n_groups):
            acc = jnp.where(lane == (t % 8) * n_groups + g, jnp.sum(tiles[g], axis=1, keepdims=True), acc)
        a_ref[t // 8] = acc
        return acc

    def tok_body(t, carry):
        acc, prev = carry
        return lane_sums(t - 1, prev, acc), folded_tiles(t)

    acc, last = lax.fori_loop(1, PEER_TB, tok_body,
                              (jnp.zeros((ROW_SUB, WORD_LANE), jnp.float32), folded_tiles(0)))
    lane_sums(PEER_TB - 1, last, acc)


def _split3_lanes(x, tb):
    mask = lambda v: lax.bitcast_convert_type(lax.bitcast_convert_type(v, jnp.int32) & HI_MASK, jnp.float32)
    hi = mask(x)
    mid = mask(x - hi)
    lo = x - hi - mid
    part = lax.broadcasted_iota(jnp.int32, x.shape, 1) // tb
    return jnp.where(part == 0, hi, jnp.where(part == 1, mid, jnp.where(part == 2, lo, 0.0))).astype(jnp.bfloat16)


def _peer_v_kernel(*refs):
    idx_refs = refs[:IDX_WAYS]
    a_ref, g_ref, tab_ref, sel_ref, o_ref, wrep_ref = refs[IDX_WAYS:]
    a = a_ref[...]
    w = g_ref[...] * (0.5 * a * (1.0 + lax.erf(a * (2.0 ** -0.5))))
    wcat = _split3_lanes(jnp.concatenate([w, w, w, w], axis=0).T, PEER_TB)
    wrep_ref[...] = jnp.dot(wcat, sel_ref[...], preferred_element_type=jnp.float32)
    upper = lax.broadcasted_iota(jnp.int32, (ROW_SUB, WORD_LANE), 0) < WORD_SUB
    per_way = PEER_PAIRS // IDX_WAYS

    def one_token(t):
        col = pl.multiple_of(t * WORD_LANE, WORD_LANE)
        ways = [r.at[pl.ds(t * per_way, per_way)] for r in idx_refs]
        z = jnp.zeros((ROW_SUB, WORD_LANE), jnp.float32)
        acc_lo, acc_hi = [z, z], [z, z]
        for m in range(per_way):
            for jj in range(IDX_WAYS // 2):
                p = m * IDX_WAYS + 2 * jj
                words = _pair_tile(tab_ref, ways[2 * jj][m], ways[2 * jj + 1][m], upper)
                w2 = jnp.where(upper, wrep_ref[pl.ds(p, 1), pl.ds(col, WORD_LANE)],
                               wrep_ref[pl.ds(p + 1, 1), pl.ds(col, WORD_LANE)])
                k = jj % 2
                acc_lo[k] = acc_lo[k] + lax.bitcast_convert_type(words << 16, jnp.float32) * w2
                acc_hi[k] = acc_hi[k] + lax.bitcast_convert_type(words & HI_MASK, jnp.float32) * w2
        lo, hi = acc_lo[0] + acc_lo[1], acc_hi[0] + acc_hi[1]
        o_ref[t] = jnp.concatenate([lo[:WORD_SUB] + lo[WORD_SUB:], hi[:WORD_SUB] + hi[WORD_SUB:]], axis=0)

    def tok_body(i, carry):
        for j in range(PEER_TOK_UNROLL):
            one_token(PEER_TOK_UNROLL * i + j)
        return carry

    lax.fori_loop(0, PEER_TB // PEER_TOK_UNROLL, tok_body, 0)


_PEER_VMEM_LIMIT = 56 * 1024 * 1024


def _peer_experts(hf, experts, gates, u_tab, v_tab):
    t = hf.shape[0]
    nblk = t // PEER_TB
    smem_spec = pl.BlockSpec((PEER_TB * PEER_PAIRS // IDX_WAYS,), lambda i: (i,), memory_space=pltpu.SMEM)
    tab_spec = pl.BlockSpec(memory_space=pltpu.VMEM)
    tok_spec = pl.BlockSpec((PEER_TB, PEER_PAIRS), lambda i: (i, 0))
    row_spec = pl.BlockSpec((PEER_TB, ROW_SUB, WORD_LANE), lambda i: (i, 0, 0))
    params = pltpu.CompilerParams(dimension_semantics=("arbitrary",), vmem_limit_bytes=_PEER_VMEM_LIMIT)
    assert IDX_WAYS == PEER_HEADS
    idx_ways = [(experts[j] * WORD_SUB + TAB_PAD).T.reshape(-1) for j in range(IDX_WAYS)]
    gates = gates.transpose(2, 1, 0).reshape(t, PEER_PAIRS)
    a_tiles = pl.pallas_call(
        _peer_u_kernel,
        grid=(nblk,),
        in_specs=[smem_spec] * IDX_WAYS + [row_spec, tab_spec],
        out_specs=pl.BlockSpec((PEER_TB // 8, ROW_SUB, WORD_LANE), lambda i: (i, 0, 0)),
        out_shape=jax.ShapeDtypeStruct((t // 8, ROW_SUB, WORD_LANE), jnp.float32),
        compiler_params=params, name="peer_u",
    )(*idx_ways, hf, _pack_table(u_tab))
    inv = np.argsort(np.array(_ORDER))
    a = a_tiles.reshape(t // 8, 8, 8, PEER_PAIRS // 8)[:, inv].transpose(0, 2, 3, 1).reshape(t, PEER_PAIRS)
    eye = np.repeat(np.eye(PEER_TB, dtype=np.float32), WORD_LANE, axis=1)
    sel = jnp.asarray(np.concatenate([eye, eye, eye, np.zeros_like(eye)], axis=0), jnp.bfloat16)
    out = pl.pallas_call(
        _peer_v_kernel,
        grid=(nblk,),
        in_specs=[smem_spec] * IDX_WAYS + [tok_spec, tok_spec, tab_spec, pl.BlockSpec(sel.shape, lambda i: (0, 0))],
        out_specs=row_spec,
        out_shape=jax.ShapeDtypeStruct((t, ROW_SUB, WORD_LANE), jnp.float32),
        scratch_shapes=[pltpu.VMEM((PEER_PAIRS, PEER_TB * WORD_LANE), jnp.float32)],
        compiler_params=params, name="peer_v",
    )(*idx_ways, a, gates, _pack_table(v_tab), sel)
    return out


def _final_kernel(x_ref, p_ref, gate_ref, g_ref, o_ref):
    peer = jnp.concatenate([p_ref[:, c, :] for c in range(ROW_SUB)], axis=-1)
    o_ref[...] = _rms(x_ref[...] + gate_ref[...] * peer) * g_ref[...]


def _final(x1, peer, gate2, g):
    bsz, s, d = x1.shape
    row = pl.BlockSpec((None, ROW_TILE, d), lambda b, i: (b, i, 0))
    return pl.pallas_call(
        _final_kernel,
        grid=(bsz, s // ROW_TILE),
        in_specs=[row, pl.BlockSpec((None, ROW_TILE, ROW_SUB, d // ROW_SUB), lambda b, i: (b, i, 0, 0)),
                  pl.BlockSpec((None, 1, d), lambda b, i: (b, 0, 0)), pl.BlockSpec((1, d), lambda b, i: (0, 0))],
        out_specs=row,
        out_shape=jax.ShapeDtypeStruct((bsz, s, d), jnp.float32), name="final",
    )(x1, peer, gate2, g.reshape(1, d))


def _in_proj_weights(w_in):
    o = np.cumsum([0, GLA_QK_WIDTH, GLA_QK_WIDTH, GLA_WIDTH, GLA_GATE_RANK, GLA_WIDTH, SWA_WIDTH, SWA_KV_WIDTH])
    src = np.full((PROJ_W,), IN_WIDTH, np.int32)
    src[COL_QS + _wide_cols()] = o[5] + np.arange(SWA_WIDTH)
    for dst, start, width in ((COL_QG, o[0], GLA_QK_WIDTH), (COL_KG, o[1], GLA_QK_WIDTH), (COL_VG, o[2], GLA_WIDTH),
                              (COL_ALR, o[3], GLA_GATE_RANK), (COL_RG, o[4], GLA_WIDTH),
                              (COL_KS, o[6], SWA_KV_WIDTH), (COL_VS, o[7], SWA_KV_WIDTH)):
        src[dst:dst + width] = start + np.arange(width)
    w_ext = jnp.concatenate([w_in, jnp.zeros((w_in.shape[0], 1), w_in.dtype)], axis=1)
    return w_ext[:, src].astype(jnp.bfloat16)


def _out_proj_weights(w_out):
    src = np.full((MIX_W,), MIX_WIDTH, np.int32)
    src[:GLA_WIDTH] = np.arange(GLA_WIDTH)
    src[GLA_WIDTH + _wide_cols()] = GLA_WIDTH + np.arange(SWA_WIDTH)
    w_ext = jnp.concatenate([w_out, jnp.zeros((1, w_out.shape[1]), w_out.dtype)], axis=0)
    return w_ext[src].astype(jnp.bfloat16)


def kernel(x, c, w_ada, b_ada, norm1_g, norm2_g, w_in, w_gla_alpha, b_gla_alpha, gla_norm_g,
           swa_sinks, swa_norm_g, w_out, w_peer_q, peer_keys_1, peer_keys_2, peer_u, peer_v, final_g):
    assert w_ada.shape[0] == 1, "single-layer block"
    l = 0
    mod = _ada(c, w_ada[l], b_ada[l])
    shift1, scale1, gate1, shift2, scale2, gate2 = [m[:, None, :] for m in jnp.split(mod, 6, axis=-1)]
    proj = _in_proj(x, norm1_g[l], shift1, scale1, _in_proj_weights(w_in[l]))
    o_gla = _gla(proj, w_gla_alpha[l], b_gla_alpha[l], gla_norm_g[l])
    g_wide = jnp.zeros((1, SWA_QW), jnp.float32).at[0, _wide_cols()].set(swa_norm_g[l])
    o_swa = _swa(proj, swa_sinks[l], g_wide)
    x1, h2 = _out_proj(x, o_gla, o_swa, _out_proj_weights(w_out[l]), gate1, norm2_g[l], shift2, scale2)
    peer = peer_mixer(h2, w_peer_q[l], peer_keys_1[l], peer_keys_2[l], peer_u[l], peer_v[l])
    return _final(x1, peer, gate2, final_g)
```

```python
import jax, jax.numpy as jnp
from jax import lax
import numpy as np
from jax.experimental import pallas as pl
from jax.experimental.pallas import tpu as pltpu

D_MODEL = 1024
BATCH = 4
SEQ = 8192
DEPTH = 1

GLA_HEADS = 4
GLA_DK = 64
GLA_DV = 128
GLA_GATE_RANK = 16
GLA_GATE_TEMP = 16.0
GLA_CHUNK = 64
SWA_HEADS = 8
SWA_KV_HEADS = 2
SWA_HEAD_DIM = 64
SWA_WINDOW = 128
SWA_BLOCK = 128
PEER_HEADS = 8
PEER_N_KEYS = 128
PEER_N_EXPERTS = PEER_N_KEYS * PEER_N_KEYS
PEER_QUERY_DIM = 256
PEER_TOPK = 16
RMS_EPS = 1e-6

GLA_QK_WIDTH = GLA_HEADS * GLA_DK
GLA_WIDTH = GLA_HEADS * GLA_DV
SWA_WIDTH = SWA_HEADS * SWA_HEAD_DIM
SWA_KV_WIDTH = SWA_KV_HEADS * SWA_HEAD_DIM
MIX_WIDTH = GLA_WIDTH + SWA_WIDTH
IN_WIDTH = 2 * GLA_QK_WIDTH + GLA_WIDTH + GLA_GATE_RANK + GLA_WIDTH + SWA_WIDTH + 2 * SWA_KV_WIDTH


LANES = 128
NEG_INF = float("-inf")
NT_DIMS = (((1,), (1,)), ((), ()))
ROW_TILE = 512
WIDE = 2 * SWA_HEAD_DIM
SWA_QW = SWA_HEADS * WIDE
ALR_W = LANES
COL_QS, COL_QG, COL_KG, COL_VG, COL_RG = 0, SWA_QW, SWA_QW + GLA_QK_WIDTH, SWA_QW + 2 * GLA_QK_WIDTH, SWA_QW + 2 * GLA_QK_WIDTH + GLA_WIDTH
COL_KS = COL_RG + GLA_WIDTH
COL_VS = COL_KS + SWA_KV_WIDTH
COL_ALR = COL_VS + SWA_KV_WIDTH
PROJ_W = COL_ALR + ALR_W
MIX_W = GLA_WIDTH + SWA_QW


def _wide_cols():
    grp = SWA_HEADS // SWA_KV_HEADS
    h = np.arange(SWA_WIDTH) // SWA_HEAD_DIM
    return h * WIDE + (h // grp) * SWA_HEAD_DIM + np.arange(SWA_WIDTH) % SWA_HEAD_DIM


def _rms(x):
    return x * lax.rsqrt(jnp.mean(x * x, axis=-1, keepdims=True) + RMS_EPS)


def _ada_kernel(c_ref, w_ref, b_ref, o_ref):
    c = c_ref[...]
    ca = (c * jax.nn.sigmoid(c)).astype(jnp.bfloat16)
    o_ref[...] = jnp.dot(ca, w_ref[...].astype(jnp.bfloat16), preferred_element_type=jnp.float32) + b_ref[...]


def _ada(c, w, b):
    bsz, d = c.shape
    n = w.shape[1]
    tn = 1024
    return pl.pallas_call(
        _ada_kernel,
        grid=(n // tn,),
        in_specs=[pl.BlockSpec((bsz, d), lambda j: (0, 0)), pl.BlockSpec((d, tn), lambda j: (0, j)),
                  pl.BlockSpec((1, tn), lambda j: (0, j))],
        out_specs=pl.BlockSpec((bsz, tn), lambda j: (0, j)),
        out_shape=jax.ShapeDtypeStruct((bsz, n), jnp.float32), name="ada",
    )(c, w, b.reshape(1, n))


def _in_proj_kernel(x_ref, g_ref, sh_ref, sc_ref, w_ref, o_ref):
    h = _rms(x_ref[...]) * g_ref[...] * (1.0 + sc_ref[...]) + sh_ref[...]
    o_ref[...] = jnp.dot(h.astype(jnp.bfloat16), w_ref[...], preferred_element_type=jnp.float32)


def _in_proj(x, g, shift, scale, w):
    bsz, s, d = x.shape
    n = w.shape[1]
    row = lambda width: pl.BlockSpec((None, ROW_TILE, width), lambda b, i: (b, i, 0))
    per_b = pl.BlockSpec((None, 1, d), lambda b, i: (b, 0, 0))
    return pl.pallas_call(
        _in_proj_kernel,
        grid=(bsz, s // ROW_TILE),
        in_specs=[row(d), pl.BlockSpec((1, d), lambda b, i: (0, 0)), per_b, per_b,
                  pl.BlockSpec((d, n), lambda b, i: (0, 0))],
        out_specs=row(n),
        out_shape=jax.ShapeDtypeStruct((bsz, s, n), jnp.float32),
        compiler_params=pltpu.CompilerParams(dimension_semantics=("arbitrary", "arbitrary"),
                                             vmem_limit_bytes=48 * 1024 * 1024),
        name="in_proj",
    )(x, g.reshape(1, d), shift, scale, w)


def _swa_kernel(sink_ref, q_ref, kp_ref, kc_ref, vp_ref, vc_ref, g_ref, o_ref):
    f32, bf16 = jnp.float32, jnp.bfloat16
    blk = SWA_BLOCK
    k2 = jnp.concatenate([kp_ref[...], kc_ref[...]], axis=0).astype(bf16)
    v2 = jnp.concatenate([vp_ref[...], vc_ref[...]], axis=0).astype(bf16)
    qi = lax.broadcasted_iota(jnp.int32, (blk, 2 * blk), 0)
    kj = lax.broadcasted_iota(jnp.int32, (blk, 2 * blk), 1)
    dist = qi + blk - kj
    valid = (dist >= 0) & (dist < SWA_WINDOW) & ((kj >= blk) | (pl.program_id(1) > 0))
    distf = dist.astype(f32)
    lane_kv = lax.broadcasted_iota(jnp.int32, (blk, WIDE), 1) // SWA_HEAD_DIM
    grp = SWA_HEADS // SWA_KV_HEADS
    heads = range(SWA_HEADS)
    scores = []
    for h in heads:
        q = (q_ref[:, h * WIDE:(h + 1) * WIDE] * (SWA_HEAD_DIM ** -0.5)).astype(bf16)
        scores.append(lax.dot_general(q, k2, NT_DIMS, preferred_element_type=f32))
    probs = []
    for h in heads:
        sc = scores[h] - (2.0 ** (-8.0 * (h + 1) / SWA_HEADS)) * distf
        sc = jnp.where(valid, sc, NEG_INF)
        sink = sink_ref[h]
        m = jnp.maximum(jnp.max(sc, axis=-1, keepdims=True), sink)
        p = jnp.exp(sc - m)
        probs.append((p / (jnp.sum(p, axis=-1, keepdims=True) + jnp.exp(sink - m))).astype(bf16))
    outs = [jnp.dot(probs[h], v2, preferred_element_type=f32) for h in heads]
    outs = [jnp.where(lane_kv == h // grp, outs[h], 0.0) for h in heads]
    ssq = sum(jnp.sum(o * o, axis=-1, keepdims=True) for o in outs)
    inv = lax.rsqrt(ssq * (1.0 / SWA_WIDTH) + RMS_EPS)
    for h in range(SWA_HEADS):
        o_ref[:, h * WIDE:(h + 1) * WIDE] = outs[h] * inv * g_ref[:, h * WIDE:(h + 1) * WIDE]


def _swa(proj, sinks, norm_g_wide):
    bsz, s, _ = proj.shape
    col = lambda width, off, prev: pl.BlockSpec(
        (None, SWA_BLOCK, width), lambda b, n: (b, jnp.maximum(n - 1, 0) if prev else n, off // width))
    return pl.pallas_call(
        _swa_kernel,
        grid=(bsz, s // SWA_BLOCK),
        in_specs=[pl.BlockSpec(memory_space=pltpu.SMEM), col(SWA_QW, COL_QS, False),
                  col(SWA_KV_WIDTH, COL_KS, True), col(SWA_KV_WIDTH, COL_KS, False),
                  col(SWA_KV_WIDTH, COL_VS, True), col(SWA_KV_WIDTH, COL_VS, False),
                  pl.BlockSpec((1, SWA_QW), lambda b, n: (0, 0))],
        out_specs=pl.BlockSpec((None, SWA_BLOCK, SWA_QW), lambda b, n: (b, n, 0)),
        out_shape=jax.ShapeDtypeStruct((bsz, s, SWA_QW), jnp.float32),
        compiler_params=pltpu.CompilerParams(dimension_semantics=("arbitrary", "arbitrary")),
        name="swa",
    )(sinks, proj, proj, proj, proj, proj, norm_g_wide)


def _out_proj_kernel(x_ref, og_ref, os_ref, w_ref, gate_ref, g2_ref, sh_ref, sc_ref, x1_ref, h2_ref):
    mix = jnp.concatenate([og_ref[...], os_ref[...]], axis=-1).astype(jnp.bfloat16)
    x1 = x_ref[...] + gate_ref[...] * jnp.dot(mix, w_ref[...], preferred_element_type=jnp.float32)
    x1_ref[...] = x1
    h2 = _rms(x1) * g2_ref[...] * (1.0 + sc_ref[...]) + sh_ref[...]
    for c in range(ROW_SUB):
        h2_ref[:, c, :] = h2[:, c * LANES:(c + 1) * LANES]


def _out_proj(x, o_gla, o_swa, w, gate1, g2, shift2, scale2):
    bsz, s, d = x.shape
    row = lambda width: pl.BlockSpec((None, ROW_TILE, width), lambda b, i: (b, i, 0))
    per_b = pl.BlockSpec((None, 1, d), lambda b, i: (b, 0, 0))
    shp = jax.ShapeDtypeStruct((bsz, s, d), jnp.float32)
    return pl.pallas_call(
        _out_proj_kernel,
        grid=(bsz, s // ROW_TILE),
        in_specs=[row(d), row(GLA_WIDTH), row(SWA_QW), pl.BlockSpec((MIX_W, d), lambda b, i: (0, 0)),
                  per_b, pl.BlockSpec((1, d), lambda b, i: (0, 0)), per_b, per_b],
        out_specs=[row(d), pl.BlockSpec((None, ROW_TILE, ROW_SUB, d // ROW_SUB), lambda b, i: (b, i, 0, 0))],
        out_shape=[shp, jax.ShapeDtypeStruct((bsz, s, ROW_SUB, d // ROW_SUB), jnp.float32)],
        compiler_params=pltpu.CompilerParams(dimension_semantics=("arbitrary", "arbitrary"),
                                             vmem_limit_bytes=48 * 1024 * 1024),
        name="out_proj",
    )(x, o_gla, o_swa, w, gate1, g2.reshape(1, d), shift2, scale2)


GLA_SUB = 16
GLA_BLOCK = 128
GLA_PAIR = 2 * GLA_DK
GLA_NB = 1


def _gla_kernel(q_ref, k_ref, v_ref, alr_ref, r_ref, wa_ref, ba_ref, g_ref, tri_ref, o_ref, st_ref, b_ref, q2_ref):
    f32, bf16 = jnp.float32, jnp.bfloat16

    @pl.when(pl.program_id(1) == 0)
    def _():
        st_ref[...] = jnp.zeros(st_ref.shape, f32)

    for bb in range(GLA_NB):
        z = jnp.dot(alr_ref[bb].astype(bf16), wa_ref[...], preferred_element_type=f32) + ba_ref[...]
        log_a = -(jnp.maximum(-z, 0.0) + jnp.log1p(jnp.exp(-jnp.abs(z)))) * (1.0 / GLA_GATE_TEMP)
        b_ref[bb] = jnp.dot(tri_ref[...], log_a, preferred_element_type=f32, precision=lax.Precision.HIGHEST)
        q2_ref[bb] = q_ref[bb] * (GLA_DK ** -0.5)
    lane = lax.broadcasted_iota(jnp.int32, (GLA_SUB, GLA_PAIR), 1)
    sub = lax.broadcasted_iota(jnp.int32, (GLA_SUB, GLA_PAIR), 0)
    first = lane < GLA_DK
    lane_s = lax.broadcasted_iota(jnp.int32, (GLA_DV, GLA_PAIR), 1) < GLA_DK

    chains = [(bb, hp) for bb in range(GLA_NB) for hp in range(GLA_HEADS // 2)]
    states = [st_ref[i] for i in range(len(chains))]
    for c in range(GLA_BLOCK // GLA_SUB):
        rows = pl.ds(c * GLA_SUB, GLA_SUB)
        for ci, (bb, hp) in enumerate(chains):
            cols = pl.ds(hp * GLA_PAIR, GLA_PAIR)
            state = states[ci]
            b = b_ref[bb, rows, cols]
            b_last = b_ref[bb, pl.ds(c * GLA_SUB + GLA_SUB - 1, 1), cols]
            q2, k2 = q2_ref[bb, rows, cols], k_ref[bb, rows, cols]
            qe = q2 * jnp.exp(b)
            kd = (k2 * jnp.exp(b_last - b)).astype(bf16)
            sb = state.astype(bf16)
            cols_a, cols_b = [], []
            for t in range(GLA_SUB):
                bt = b_ref[bb, pl.ds(c * GLA_SUB + t, 1), cols]
                qt = q2_ref[bb, pl.ds(c * GLA_SUB + t, 1), cols]
                p = (qt * k2) * jnp.exp(bt - b)
                keep = sub[:, :1] <= t
                cols_a.append(jnp.where(keep, jnp.sum(jnp.where(first, p, 0.0), axis=1, keepdims=True), 0.0))
                cols_b.append(jnp.where(keep, jnp.sum(jnp.where(first, 0.0, p), axis=1, keepdims=True), 0.0))
            ut = []
            for j, sc_cols in enumerate((cols_a, cols_b)):
                head = 2 * hp + j
                vcols = pl.ds(head * GLA_DV, GLA_DV)
                vh = v_ref[bb, rows, vcols]
                qm = jnp.where(first if j == 0 else ~first, qe, 0.0).astype(bf16)
                o = lax.dot_general(qm, sb, NT_DIMS, preferred_element_type=f32)
                o = o + jnp.concatenate([jnp.sum(cj * vh, axis=0, keepdims=True) for cj in sc_cols], axis=0)
                ms = jnp.mean(o * o, axis=1, keepdims=True)
                r = r_ref[bb, rows, vcols]
                o_ref[bb, rows, vcols] = (o * lax.rsqrt(ms + RMS_EPS) * g_ref[:, vcols]) * (r * jax.nn.sigmoid(r))
                ut.append(jnp.dot(vh.T.astype(bf16), kd, preferred_element_type=f32))
            states[ci] = jnp.exp(b_last) * state + jnp.where(lane_s, ut[0], ut[1])
    for ci in range(len(chains)):
        st_ref[ci] = states[ci]


def _gla_tri():
    r = np.arange(GLA_BLOCK)
    return jnp.asarray(((r[:, None] // GLA_SUB == r[None, :] // GLA_SUB) & (r[None, :] <= r[:, None])).astype(np.float32))


def _gla(proj, w_alpha, b_alpha, norm_g):
    bsz, s, _ = proj.shape
    rank = w_alpha.shape[0]
    wa = jnp.pad(w_alpha, ((0, ALR_W - rank), (0, 0))).astype(jnp.bfloat16)
    col = lambda width, off: pl.BlockSpec((GLA_NB, GLA_BLOCK, width), lambda b, i: (b, i, off // width))
    const = lambda shape: pl.BlockSpec(shape, lambda b, i: (0, 0))
    return pl.pallas_call(
        _gla_kernel,
        grid=(bsz // GLA_NB, s // GLA_BLOCK),
        in_specs=[col(GLA_QK_WIDTH, COL_QG), col(GLA_QK_WIDTH, COL_KG), col(GLA_WIDTH, COL_VG), col(ALR_W, COL_ALR),
                  col(GLA_WIDTH, COL_RG), const((ALR_W, GLA_QK_WIDTH)), const((1, GLA_QK_WIDTH)),
                  const((1, GLA_WIDTH)), const((GLA_BLOCK, GLA_BLOCK))],
        out_specs=pl.BlockSpec((GLA_NB, GLA_BLOCK, GLA_WIDTH), lambda b, i: (b, i, 0)),
        out_shape=jax.ShapeDtypeStruct((bsz, s, GLA_WIDTH), jnp.float32),
        scratch_shapes=[pltpu.VMEM((GLA_NB * GLA_HEADS // 2, GLA_DV, GLA_PAIR), jnp.float32),
                        pltpu.VMEM((GLA_NB, GLA_BLOCK, GLA_QK_WIDTH), jnp.float32),
                        pltpu.VMEM((GLA_NB, GLA_BLOCK, GLA_QK_WIDTH), jnp.float32)],
        compiler_params=pltpu.CompilerParams(dimension_semantics=("arbitrary", "arbitrary")),
        name="gla",
    )(proj, proj, proj, proj, proj, wa, b_alpha.reshape(1, -1), norm_g.reshape(1, -1), _gla_tri())


def peer_mixer(h_tiles, w_q, keys_1, keys_2, u_tab, v_tab):
    bsz, s, rs, rl = h_tiles.shape
    ht = h_tiles.reshape(bsz * s, rs, rl)
    experts, gates = _peer_route(ht, w_q, keys_1, keys_2)
    return _peer_experts(ht, experts, gates, u_tab, v_tab).reshape(bsz, s, rs, rl)


ROUTE_TT = 512


def _top16_rows(s, code, big):
    vals, codes = [], []
    for _ in range(PEER_TOPK):
        m = jnp.max(s, axis=0, keepdims=True)
        c = jnp.min(jnp.where(s == m, code, big), axis=0, keepdims=True)
        s = jnp.where(code == c, NEG_INF, s)
        vals.append(m)
        codes.append(c)
    return vals, codes


def _peer_route_kernel(h_ref, wq_ref, k1_ref, k2_ref, e_ref, g_ref, q_ref):
    half = PEER_QUERY_DIM // 2
    tt = ROUTE_TT
    h = jnp.concatenate([h_ref[:, c, :] for c in range(h_ref.shape[1])], axis=-1)
    q_ref[...] = jnp.dot(h.astype(jnp.bfloat16), wq_ref[...], preferred_element_type=jnp.float32)
    key_id = lax.broadcasted_iota(jnp.int32, (PEER_N_KEYS, 2 * tt), 0).astype(jnp.float32)
    sub8 = lax.broadcasted_iota(jnp.int32, (8, tt), 0)
    nt = (((1,), (1,)), ((), ()))

    def head_body(hd, carry):
        off = pl.multiple_of(hd * PEER_QUERY_DIM, PEER_QUERY_DIM)
        q1 = q_ref[:, pl.ds(off, half)].astype(jnp.bfloat16)
        q2 = q_ref[:, pl.ds(off + half, half)].astype(jnp.bfloat16)
        s1 = lax.dot_general(k1_ref[...], q1, nt, preferred_element_type=jnp.float32)
        s2 = lax.dot_general(k2_ref[...], q2, nt, preferred_element_type=jnp.float32)
        vals, ids = _top16_rows(jnp.concatenate([s1, s2], axis=1), key_id, float(PEER_N_KEYS))
        v = jnp.concatenate(vals, axis=0)
        ix = jnp.concatenate(ids, axis=0).astype(jnp.int32)
        v1, v2, i1, i2 = v[:, :tt], v[:, tt:], ix[:, :tt], ix[:, tt:]
        cand, code = [], []
        for a, b0 in [(0, 0), (0, 8), (1, 0), (2, 0), (3, 0), (4, 0), (5, 0), (6, 0), (7, 0)]:
            cand.append(v1[a:a + 1] + v2[b0:b0 + 8])
            flat = a * PEER_TOPK + b0 + sub8
            code.append((flat << 14) | (i1[a:a + 1] * PEER_N_KEYS + i2[b0:b0 + 8]))
        cand.append(v1[8:16] + v2[0:1])
        code.append((((8 + sub8) * PEER_TOPK) << 14) | (i1[8:16] * PEER_N_KEYS + i2[0:1]))
        sc, cc = _top16_rows(jnp.concatenate(cand, axis=0), jnp.concatenate(code, axis=0).astype(jnp.float32),
                             float(1 << 30))
        sc = jnp.concatenate(sc, axis=0)
        ex = jnp.exp(sc - sc[0:1])
        g_ref[hd] = ex / jnp.sum(ex, axis=0, keepdims=True)
        e_ref[hd] = jnp.concatenate(cc, axis=0).astype(jnp.int32) & (PEER_N_EXPERTS - 1)
        return carry

    lax.fori_loop(0, PEER_HEADS, head_body, 0)


def _peer_route(hf, w_q, keys_1, keys_2):
    t, rs, rl = hf.shape
    d = rs * rl
    qw = PEER_HEADS * PEER_QUERY_DIM
    const = lambda shape: pl.BlockSpec(shape, lambda i: tuple(0 for _ in shape))
    out_spec = pl.BlockSpec((PEER_HEADS, PEER_TOPK, ROUTE_TT), lambda i: (0, 0, i))
    e_t, g_t = pl.pallas_call(
        _peer_route_kernel,
        grid=(t // ROUTE_TT,),
        in_specs=[pl.BlockSpec((ROUTE_TT, rs, rl), lambda i: (i, 0, 0)), const((d, qw)),
                  const(keys_1.shape), const(keys_2.shape)],
        out_specs=[out_spec, out_spec],
        out_shape=[jax.ShapeDtypeStruct((PEER_HEADS, PEER_TOPK, t), jnp.int32),
                   jax.ShapeDtypeStruct((PEER_HEADS, PEER_TOPK, t), jnp.float32)],
        scratch_shapes=[pltpu.VMEM((ROUTE_TT, qw), jnp.float32)],
        compiler_params=pltpu.CompilerParams(dimension_semantics=("arbitrary",), vmem_limit_bytes=48 * 1024 * 1024),
        name="peer_route",
    )(hf, w_q.astype(jnp.bfloat16), keys_1.astype(jnp.bfloat16), keys_2.astype(jnp.bfloat16))
    return e_t, g_t


PEER_PAIRS = PEER_HEADS * PEER_TOPK
PEER_TB = 128
PEER_HALF = 64
PEER_TOK_UNROLL = 4
HALF_D = D_MODEL // 2
WORD_SUB, WORD_LANE = HALF_D // 128, 128
ROW_SUB = 2 * WORD_SUB
HI_MASK = -65536
IDX_WAYS = 8


PACK_EB = 512
TAB_PAD = PACK_EB * WORD_SUB


def _pack_kernel(x_ref, o_ref):
    i = pl.program_id(0)

    @pl.when((i == 0) | (i == pl.num_programs(0) - 1))
    def _():
        o_ref[...] = jnp.zeros(o_ref.shape, o_ref.dtype)

    @pl.when((i > 0) & (i < pl.num_programs(0) - 1))
    def _():
        x = x_ref[...].astype(jnp.bfloat16).astype(jnp.float32)
        bits = lax.bitcast_convert_type(x, jnp.int32)
        word = lax.shift_right_logical(bits[:, :HALF_D], 16) | (bits[:, HALF_D:] & HI_MASK)
        for c in range(WORD_SUB):
            o_ref[:, c, :] = word[:, c * WORD_LANE:(c + 1) * WORD_LANE]


def _pack_table(tab):
    e, d = tab.shape
    nblk = e // PACK_EB
    out = pl.pallas_call(
        _pack_kernel,
        grid=(nblk + 2,),
        in_specs=[pl.BlockSpec((PACK_EB, d), lambda i: (jnp.clip(i - 1, 0, nblk - 1), 0))],
        out_specs=pl.BlockSpec((PACK_EB, WORD_SUB, WORD_LANE), lambda i: (i, 0, 0)),
        out_shape=jax.ShapeDtypeStruct(((nblk + 2) * PACK_EB, WORD_SUB, WORD_LANE), jnp.int32),
        name="pack_table",
    )(tab)
    return out.reshape((nblk + 2) * TAB_PAD, WORD_LANE)


def _pair_tile(tab_ref, ia, ib, upper):
    wa = tab_ref[pl.ds(pl.multiple_of(ia, WORD_SUB), ROW_SUB), :]
    wb = tab_ref[pl.ds(pl.multiple_of(ib - WORD_SUB, WORD_SUB), ROW_SUB), :]
    return jnp.where(upper, wa, wb)


_ORDER = (6, 2, 4, 0, 7, 3, 5, 1)


def _fold_pairs(ps, sub):
    b1 = (sub & 2) != 0
    w = []
    for i in range(2):
        x, y = ps[2 * i], ps[2 * i + 1]
        w.append(jnp.where(b1, x + pltpu.roll(x, 2, 0), y + pltpu.roll(y, 6, 0)))
    b0 = (sub & 1) != 0
    x, y = w
    return jnp.where(b0, x + pltpu.roll(x, 1, 0), y + pltpu.roll(y, 7, 0))


def _peer_u_kernel(*refs):
    idx_refs = refs[:IDX_WAYS]
    h_ref, tab_ref, a_ref = refs[IDX_WAYS:]
    sub = lax.broadcasted_iota(jnp.int32, (ROW_SUB, WORD_LANE), 0)
    lane = lax.broadcasted_iota(jnp.int32, (ROW_SUB, WORD_LANE), 1)
    n_groups = PEER_PAIRS // IDX_WAYS
    feed = [int(i) for i in np.argsort(np.array(_ORDER))]
    a_ref[...] = jnp.zeros(a_ref.shape, a_ref.dtype)

    def folded_tiles(t):
        h = h_ref[t]
        h_lo = jnp.concatenate([h[:WORD_SUB], h[:WORD_SUB]], axis=0)
        h_hi = jnp.concatenate([h[WORD_SUB:], h[WORD_SUB:]], axis=0)
        ways = [r.at[pl.ds(t * n_groups, n_groups)] for r in idx_refs]
        tiles = []
        for g in range(n_groups):
            ps = []
            for i in range(4):
                w = _pair_tile(tab_ref, ways[feed[2 * i]][g], ways[feed[2 * i + 1]][g], sub < WORD_SUB)
                lo = lax.bitcast_convert_type(w << 16, jnp.float32)
                hi = lax.bitcast_convert_type(w & HI_MASK, jnp.float32)
                ps.append(lo * h_lo + hi * h_hi)
            tiles.append(_fold_pairs(ps, sub))
        return tuple(tiles)

    def lane_sums(t, tiles):
        here = lane == t
        for g in range(n_groups):
            a_ref[g] = jnp.where(here, jnp.sum(tiles[g], axis=1, keepdims=True), a_ref[g])

    def tok_body(t, prev):
        lane_sums(t - 1, prev)
        return folded_tiles(t)

    last = lax.fori_loop(1, PEER_TB, tok_body, folded_tiles(0))
    lane_sums(PEER_TB - 1, last)


def _split3_lanes(x, tb):
    mask = lambda v: lax.bitcast_convert_type(lax.bitcast_convert_type(v, jnp.int32) & HI_MASK, jnp.float32)
    hi = mask(x)
    mid = mask(x - hi)
    lo = x - hi - mid
    part = lax.broadcasted_iota(jnp.int32, x.shape, 1) // tb
    return jnp.where(part == 0, hi, jnp.where(part == 1, mid, jnp.where(part == 2, lo, 0.0))).astype(jnp.bfloat16)


def _peer_v_kernel(*refs):
    idx_refs = refs[:IDX_WAYS]
    a_ref, g_ref, tab_ref, sel_ref, o_ref, wrep_ref = refs[IDX_WAYS:]
    per_way = PEER_PAIRS // IDX_WAYS
    a = a_ref[...].reshape(PEER_PAIRS, PEER_TB)
    gates = jnp.concatenate([g_ref[:, m, :] for m in range(per_way)], axis=0)
    w = gates * (0.5 * a * (1.0 + lax.erf(a * (2.0 ** -0.5))))
    w_swapped = pltpu.roll(w, PEER_HALF, 1)
    low_lanes = lax.broadcasted_iota(jnp.int32, w.shape, 1) < PEER_HALF
    upper = lax.broadcasted_iota(jnp.int32, (ROW_SUB, WORD_LANE), 0) < WORD_SUB

    def one_token(tl, first):
        it = first + tl
        col = pl.multiple_of(tl * WORD_LANE, WORD_LANE)
        ways = [r.at[pl.ds(it * per_way, per_way)] for r in idx_refs]
        z = jnp.zeros((ROW_SUB, WORD_LANE), jnp.float32)
        acc_lo, acc_hi = [z, z], [z, z]
        for m in range(per_way):
            for jj in range(IDX_WAYS // 2):
                p = m * IDX_WAYS + 2 * jj
                words = _pair_tile(tab_ref, ways[2 * jj][m], ways[2 * jj + 1][m], upper)
                w2 = jnp.where(upper, wrep_ref[pl.ds(p, 1), pl.ds(col, WORD_LANE)],
                               wrep_ref[pl.ds(p + 1, 1), pl.ds(col, WORD_LANE)])
                k = jj % 2
                acc_lo[k] = acc_lo[k] + lax.bitcast_convert_type(words << 16, jnp.float32) * w2
                acc_hi[k] = acc_hi[k] + lax.bitcast_convert_type(words & HI_MASK, jnp.float32) * w2
        lo, hi = acc_lo[0] + acc_lo[1], acc_hi[0] + acc_hi[1]
        o_ref[it] = jnp.concatenate([lo[:WORD_SUB] + lo[WORD_SUB:], hi[:WORD_SUB] + hi[WORD_SUB:]], axis=0)

    for half in range(PEER_TB // PEER_HALF):
        x = jnp.where(low_lanes, w, w_swapped) if half == 0 else jnp.where(low_lanes, w_swapped, w)
        wcat = _split3_lanes(jnp.concatenate([x, x], axis=1), PEER_HALF)
        wrep_ref[...] = jnp.dot(wcat, sel_ref[...], preferred_element_type=jnp.float32)

        def tok_body(i, carry, first=half * PEER_HALF):
            for j in range(PEER_TOK_UNROLL):
                one_token(PEER_TOK_UNROLL * i + j, first)
            return carry

        lax.fori_loop(0, PEER_HALF // PEER_TOK_UNROLL, tok_body, 0)


_PEER_VMEM_LIMIT = 56 * 1024 * 1024


def _peer_experts(hf, experts, gates, u_tab, v_tab):
    t = hf.shape[0]
    nblk = t // PEER_TB
    per_way = PEER_PAIRS // IDX_WAYS
    smem_spec = pl.BlockSpec((PEER_TB * per_way,), lambda i: (i,), memory_space=pltpu.SMEM)
    tab_spec = pl.BlockSpec(memory_space=pltpu.VMEM)
    row_spec = pl.BlockSpec((PEER_TB, ROW_SUB, WORD_LANE), lambda i: (i, 0, 0))
    a_spec = pl.BlockSpec((per_way, IDX_WAYS, PEER_TB), lambda i: (i, 0, 0))
    params = pltpu.CompilerParams(dimension_semantics=("arbitrary",), vmem_limit_bytes=_PEER_VMEM_LIMIT)
    assert IDX_WAYS == PEER_HEADS and PEER_TB == LANES
    idx_ways = [(experts[j] * WORD_SUB + TAB_PAD).T.reshape(-1) for j in range(IDX_WAYS)]
    a_t = pl.pallas_call(
        _peer_u_kernel,
        grid=(nblk,),
        in_specs=[smem_spec] * IDX_WAYS + [row_spec, tab_spec],
        out_specs=a_spec,
        out_shape=jax.ShapeDtypeStruct((nblk * per_way, IDX_WAYS, PEER_TB), jnp.float32),
        compiler_params=params, name="peer_u",
    )(*idx_ways, hf, _pack_table(u_tab))
    eye = np.repeat(np.eye(PEER_HALF, dtype=np.float32), WORD_LANE, axis=1)
    sel = jnp.asarray(np.concatenate([eye, eye, eye, np.zeros_like(eye)], axis=0), jnp.bfloat16)
    return pl.pallas_call(
        _peer_v_kernel,
        grid=(nblk,),
        in_specs=[smem_spec] * IDX_WAYS + [a_spec, pl.BlockSpec((IDX_WAYS, per_way, PEER_TB), lambda i: (0, 0, i)),
                                           tab_spec, pl.BlockSpec(sel.shape, lambda i: (0, 0))],
        out_specs=row_spec,
        out_shape=jax.ShapeDtypeStruct((t, ROW_SUB, WORD_LANE), jnp.float32),
        scratch_shapes=[pltpu.VMEM((PEER_PAIRS, PEER_HALF * WORD_LANE), jnp.float32)],
        compiler_params=params, name="peer_v",
    )(*idx_ways, a_t, gates, _pack_table(v_tab), sel)


def _final_kernel(x_ref, p_ref, gate_ref, g_ref, o_ref):
    peer = jnp.concatenate([p_ref[:, c, :] for c in range(ROW_SUB)], axis=-1)
    o_ref[...] = _rms(x_ref[...] + gate_ref[...] * peer) * g_ref[...]


def _final(x1, peer, gate2, g):
    bsz, s, d = x1.shape
    row = pl.BlockSpec((None, ROW_TILE, d), lambda b, i: (b, i, 0))
    return pl.pallas_call(
        _final_kernel,
        grid=(bsz, s // ROW_TILE),
        in_specs=[row, pl.BlockSpec((None, ROW_TILE, ROW_SUB, d // ROW_SUB), lambda b, i: (b, i, 0, 0)),
                  pl.BlockSpec((None, 1, d), lambda b, i: (b, 0, 0)), pl.BlockSpec((1, d), lambda b, i: (0, 0))],
        out_specs=row,
        out_shape=jax.ShapeDtypeStruct((bsz, s, d), jnp.float32), name="final",
    )(x1, peer, gate2, g.reshape(1, d))


def _in_proj_weights(w_in):
    o = np.cumsum([0, GLA_QK_WIDTH, GLA_QK_WIDTH, GLA_WIDTH, GLA_GATE_RANK, GLA_WIDTH, SWA_WIDTH, SWA_KV_WIDTH])
    src = np.full((PROJ_W,), IN_WIDTH, np.int32)
    src[COL_QS + _wide_cols()] = o[5] + np.arange(SWA_WIDTH)
    for dst, start, width in ((COL_QG, o[0], GLA_QK_WIDTH), (COL_KG, o[1], GLA_QK_WIDTH), (COL_VG, o[2], GLA_WIDTH),
                              (COL_ALR, o[3], GLA_GATE_RANK), (COL_RG, o[4], GLA_WIDTH),
                              (COL_KS, o[6], SWA_KV_WIDTH), (COL_VS, o[7], SWA_KV_WIDTH)):
        src[dst:dst + width] = start + np.arange(width)
    w_ext = jnp.concatenate([w_in, jnp.zeros((w_in.shape[0], 1), w_in.dtype)], axis=1)
    return w_ext[:, src].astype(jnp.bfloat16)


def _out_proj_weights(w_out):
    src = np.full((MIX_W,), MIX_WIDTH, np.int32)
    src[:GLA_WIDTH] = np.arange(GLA_WIDTH)
    src[GLA_WIDTH + _wide_cols()] = GLA_WIDTH + np.arange(SWA_WIDTH)
    w_ext = jnp.concatenate([w_out, jnp.zeros((1, w_out.shape[1]), w_out.dtype)], axis=0)
    return w_ext[src].astype(jnp.bfloat16)


def kernel(x, c, w_ada, b_ada, norm1_g, norm2_g, w_in, w_gla_alpha, b_gla_alpha, gla_norm_g,
           swa_sinks, swa_norm_g, w_out, w_peer_q, peer_keys_1, peer_keys_2, peer_u, peer_v, final_g):
    assert w_ada.shape[0] == 1, "single-layer block"
    l = 0
    mod = _ada(c, w_ada[l], b_ada[l])
    shift1, scale1, gate1, shift2, scale2, gate2 = [m[:, None, :] for m in jnp.split(mod, 6, axis=-1)]
    proj = _in_proj(x, norm1_g[l], shift1, scale1, _in_proj_weights(w_in[l]))
    o_gla = _gla(proj, w_gla_alpha[l], b_gla_alpha[l], gla_norm_g[l])
    g_wide = jnp.zeros((1, SWA_QW), jnp.float32).at[0, _wide_cols()].set(swa_norm_g[l])
    o_swa = _swa(proj, swa_sinks[l], g_wide)
    x1, h2 = _out_proj(x, o_gla, o_swa, _out_proj_weights(w_out[l]), gate1, norm2_g[l], shift2, scale2)
    peer = peer_mixer(h2, w_peer_q[l], peer_keys_1[l], peer_keys_2[l], peer_u[l], peer_v[l])
    return _final(x1, peer, gate2, final_g)
```

```python
import jax, jax.numpy as jnp
from jax import lax
import numpy as np
from jax.experimental import pallas as pl
from jax.experimental.pallas import tpu as pltpu

D_MODEL = 1024
BATCH = 4
SEQ = 8192
DEPTH = 1

GLA_HEADS = 4
GLA_DK = 64
GLA_DV = 128
GLA_GATE_RANK = 16
GLA_GATE_TEMP = 16.0
GLA_CHUNK = 64
SWA_HEADS = 8
SWA_KV_HEADS = 2
SWA_HEAD_DIM = 64
SWA_WINDOW = 128
SWA_BLOCK = 128
PEER_HEADS = 8
PEER_N_KEYS = 128
PEER_N_EXPERTS = PEER_N_KEYS * PEER_N_KEYS
PEER_QUERY_DIM = 256
PEER_TOPK = 16
RMS_EPS = 1e-6

GLA_QK_WIDTH = GLA_HEADS * GLA_DK
GLA_WIDTH = GLA_HEADS * GLA_DV
SWA_WIDTH = SWA_HEADS * SWA_HEAD_DIM
SWA_KV_WIDTH = SWA_KV_HEADS * SWA_HEAD_DIM
MIX_WIDTH = GLA_WIDTH + SWA_WIDTH
IN_WIDTH = 2 * GLA_QK_WIDTH + GLA_WIDTH + GLA_GATE_RANK + GLA_WIDTH + SWA_WIDTH + 2 * SWA_KV_WIDTH


LANES = 128
NEG_INF = float("-inf")
NT_DIMS = (((1,), (1,)), ((), ()))
ROW_TILE = 512
WIDE = 2 * SWA_HEAD_DIM
SWA_QW = SWA_HEADS * WIDE
ALR_W = LANES
COL_QS, COL_QG, COL_KG, COL_VG, COL_RG = 0, SWA_QW, SWA_QW + GLA_QK_WIDTH, SWA_QW + 2 * GLA_QK_WIDTH, SWA_QW + 2 * GLA_QK_WIDTH + GLA_WIDTH
COL_KS = COL_RG + GLA_WIDTH
COL_VS = COL_KS + SWA_KV_WIDTH
COL_ALR = COL_VS + SWA_KV_WIDTH
PROJ_W = COL_ALR + ALR_W
MIX_W = GLA_WIDTH + SWA_QW


def _wide_cols():
    grp = SWA_HEADS // SWA_KV_HEADS
    h = np.arange(SWA_WIDTH) // SWA_HEAD_DIM
    return h * WIDE + (h // grp) * SWA_HEAD_DIM + np.arange(SWA_WIDTH) % SWA_HEAD_DIM


def _rms(x):
    return x * lax.rsqrt(jnp.mean(x * x, axis=-1, keepdims=True) + RMS_EPS)


def _ada_kernel(c_ref, w_ref, b_ref, o_ref):
    c = c_ref[...]
    ca = (c * jax.nn.sigmoid(c)).astype(jnp.bfloat16)
    o_ref[...] = jnp.dot(ca, w_ref[...].astype(jnp.bfloat16), preferred_element_type=jnp.float32) + b_ref[...]


def _ada(c, w, b):
    bsz, d = c.shape
    n = w.shape[1]
    tn = 1024
    return pl.pallas_call(
        _ada_kernel,
        grid=(n // tn,),
        in_specs=[pl.BlockSpec((bsz, d), lambda j: (0, 0)), pl.BlockSpec((d, tn), lambda j: (0, j)),
                  pl.BlockSpec((1, tn), lambda j: (0, j))],
        out_specs=pl.BlockSpec((bsz, tn), lambda j: (0, j)),
        out_shape=jax.ShapeDtypeStruct((bsz, n), jnp.float32), name="ada",
    )(c, w, b.reshape(1, n))


def _in_proj_kernel(x_ref, g_ref, sh_ref, sc_ref, w_ref, o_ref):
    h = _rms(x_ref[...]) * g_ref[...] * (1.0 + sc_ref[...]) + sh_ref[...]
    o_ref[...] = jnp.dot(h.astype(jnp.bfloat16), w_ref[...], preferred_element_type=jnp.float32)


def _in_proj(x, g, shift, scale, w):
    bsz, s, d = x.shape
    n = w.shape[1]
    row = lambda width: pl.BlockSpec((None, ROW_TILE, width), lambda b, i: (b, i, 0))
    per_b = pl.BlockSpec((None, 1, d), lambda b, i: (b, 0, 0))
    return pl.pallas_call(
        _in_proj_kernel,
        grid=(bsz, s // ROW_TILE),
        in_specs=[row(d), pl.BlockSpec((1, d), lambda b, i: (0, 0)), per_b, per_b,
                  pl.BlockSpec((d, n), lambda b, i: (0, 0))],
        out_specs=row(n),
        out_shape=jax.ShapeDtypeStruct((bsz, s, n), jnp.float32),
        compiler_params=pltpu.CompilerParams(dimension_semantics=("arbitrary", "arbitrary"),
                                             vmem_limit_bytes=48 * 1024 * 1024),
        name="in_proj",
    )(x, g.reshape(1, d), shift, scale, w)


def _swa_kernel(sink_ref, q_ref, kp_ref, kc_ref, vp_ref, vc_ref, g_ref, o_ref):
    f32, bf16 = jnp.float32, jnp.bfloat16
    blk = SWA_BLOCK
    k2 = jnp.concatenate([kp_ref[...], kc_ref[...]], axis=0).astype(bf16)
    v2 = jnp.concatenate([vp_ref[...], vc_ref[...]], axis=0).astype(bf16)
    qi = lax.broadcasted_iota(jnp.int32, (blk, 2 * blk), 0)
    kj = lax.broadcasted_iota(jnp.int32, (blk, 2 * blk), 1)
    dist = qi + blk - kj
    valid = (dist >= 0) & (dist < SWA_WINDOW) & ((kj >= blk) | (pl.program_id(1) > 0))
    distf = dist.astype(f32)
    lane_kv = lax.broadcasted_iota(jnp.int32, (blk, WIDE), 1) // SWA_HEAD_DIM
    grp = SWA_HEADS // SWA_KV_HEADS
    heads = range(SWA_HEADS)
    scores = []
    for h in heads:
        q = (q_ref[:, h * WIDE:(h + 1) * WIDE] * (SWA_HEAD_DIM ** -0.5)).astype(bf16)
        scores.append(lax.dot_general(q, k2, NT_DIMS, preferred_element_type=f32))
    probs = []
    for h in heads:
        sc = scores[h] - (2.0 ** (-8.0 * (h + 1) / SWA_HEADS)) * distf
        sc = jnp.where(valid, sc, NEG_INF)
        sink = sink_ref[h]
        m = jnp.maximum(jnp.max(sc, axis=-1, keepdims=True), sink)
        p = jnp.exp(sc - m)
        probs.append((p / (jnp.sum(p, axis=-1, keepdims=True) + jnp.exp(sink - m))).astype(bf16))
    outs = [jnp.dot(probs[h], v2, preferred_element_type=f32) for h in heads]
    outs = [jnp.where(lane_kv == h // grp, outs[h], 0.0) for h in heads]
    ssq = sum(jnp.sum(o * o, axis=-1, keepdims=True) for o in outs)
    inv = lax.rsqrt(ssq * (1.0 / SWA_WIDTH) + RMS_EPS)
    for h in range(SWA_HEADS):
        o_ref[:, h * WIDE:(h + 1) * WIDE] = outs[h] * inv * g_ref[:, h * WIDE:(h + 1) * WIDE]


def _swa(proj, sinks, norm_g_wide):
    bsz, s, _ = proj.shape
    col = lambda width, off, prev: pl.BlockSpec(
        (None, SWA_BLOCK, width), lambda b, n: (b, jnp.maximum(n - 1, 0) if prev else n, off // width))
    return pl.pallas_call(
        _swa_kernel,
        grid=(bsz, s // SWA_BLOCK),
        in_specs=[pl.BlockSpec(memory_space=pltpu.SMEM), col(SWA_QW, COL_QS, False),
                  col(SWA_KV_WIDTH, COL_KS, True), col(SWA_KV_WIDTH, COL_KS, False),
                  col(SWA_KV_WIDTH, COL_VS, True), col(SWA_KV_WIDTH, COL_VS, False),
                  pl.BlockSpec((1, SWA_QW), lambda b, n: (0, 0))],
        out_specs=pl.BlockSpec((None, SWA_BLOCK, SWA_QW), lambda b, n: (b, n, 0)),
        out_shape=jax.ShapeDtypeStruct((bsz, s, SWA_QW), jnp.float32),
        compiler_params=pltpu.CompilerParams(dimension_semantics=("arbitrary", "arbitrary")),
        name="swa",
    )(sinks, proj, proj, proj, proj, proj, norm_g_wide)


def _out_proj_kernel(x_ref, og_ref, os_ref, w_ref, gate_ref, g2_ref, sh_ref, sc_ref, x1_ref, h2_ref):
    mix = jnp.concatenate([og_ref[...], os_ref[...]], axis=-1).astype(jnp.bfloat16)
    x1 = x_ref[...] + gate_ref[...] * jnp.dot(mix, w_ref[...], preferred_element_type=jnp.float32)
    x1_ref[...] = x1
    h2 = _rms(x1) * g2_ref[...] * (1.0 + sc_ref[...]) + sh_ref[...]
    for c in range(ROW_SUB):
        h2_ref[:, c, :] = h2[:, c * LANES:(c + 1) * LANES]


def _out_proj(x, o_gla, o_swa, w, gate1, g2, shift2, scale2):
    bsz, s, d = x.shape
    row = lambda width: pl.BlockSpec((None, ROW_TILE, width), lambda b, i: (b, i, 0))
    per_b = pl.BlockSpec((None, 1, d), lambda b, i: (b, 0, 0))
    shp = jax.ShapeDtypeStruct((bsz, s, d), jnp.float32)
    return pl.pallas_call(
        _out_proj_kernel,
        grid=(bsz, s // ROW_TILE),
        in_specs=[row(d), row(GLA_WIDTH), row(SWA_QW), pl.BlockSpec((MIX_W, d), lambda b, i: (0, 0)),
                  per_b, pl.BlockSpec((1, d), lambda b, i: (0, 0)), per_b, per_b],
        out_specs=[row(d), pl.BlockSpec((None, ROW_TILE, ROW_SUB, d // ROW_SUB), lambda b, i: (b, i, 0, 0))],
        out_shape=[shp, jax.ShapeDtypeStruct((bsz, s, ROW_SUB, d // ROW_SUB), jnp.float32)],
        compiler_params=pltpu.CompilerParams(dimension_semantics=("arbitrary", "arbitrary"),
                                             vmem_limit_bytes=48 * 1024 * 1024),
        name="out_proj",
    )(x, o_gla, o_swa, w, gate1, g2.reshape(1, d), shift2, scale2)


GLA_SUB = 16
GLA_BLOCK = 128
GLA_PAIR = 2 * GLA_DK
GLA_NB = 1


def _gla_kernel(q_ref, k_ref, v_ref, alr_ref, r_ref, wa_ref, ba_ref, g_ref, tri_ref, o_ref, st_ref, b_ref, q2_ref):
    f32, bf16 = jnp.float32, jnp.bfloat16

    @pl.when(pl.program_id(1) == 0)
    def _():
        st_ref[...] = jnp.zeros(st_ref.shape, f32)

    for bb in range(GLA_NB):
        z = jnp.dot(alr_ref[bb].astype(bf16), wa_ref[...], preferred_element_type=f32) + ba_ref[...]
        log_a = -(jnp.maximum(-z, 0.0) + jnp.log1p(jnp.exp(-jnp.abs(z)))) * (1.0 / GLA_GATE_TEMP)
        b_ref[bb] = jnp.dot(tri_ref[...], log_a, preferred_element_type=f32, precision=lax.Precision.HIGHEST)
        q2_ref[bb] = q_ref[bb] * (GLA_DK ** -0.5)
    lane = lax.broadcasted_iota(jnp.int32, (GLA_SUB, GLA_PAIR), 1)
    sub = lax.broadcasted_iota(jnp.int32, (GLA_SUB, GLA_PAIR), 0)
    first = lane < GLA_DK
    lane_s = lax.broadcasted_iota(jnp.int32, (GLA_DV, GLA_PAIR), 1) < GLA_DK

    chains = [(bb, hp) for bb in range(GLA_NB) for hp in range(GLA_HEADS // 2)]
    states = [st_ref[i] for i in range(len(chains))]
    for c in range(GLA_BLOCK // GLA_SUB):
        rows = pl.ds(c * GLA_SUB, GLA_SUB)
        for ci, (bb, hp) in enumerate(chains):
            cols = pl.ds(hp * GLA_PAIR, GLA_PAIR)
            state = states[ci]
            b = b_ref[bb, rows, cols]
            b_last = b_ref[bb, pl.ds(c * GLA_SUB + GLA_SUB - 1, 1), cols]
            q2, k2 = q2_ref[bb, rows, cols], k_ref[bb, rows, cols]
            qe = q2 * jnp.exp(b)
            kd = (k2 * jnp.exp(b_last - b)).astype(bf16)
            sb = state.astype(bf16)
            cols_a, cols_b = [], []
            for t in range(GLA_SUB):
                bt = b_ref[bb, pl.ds(c * GLA_SUB + t, 1), cols]
                qt = q2_ref[bb, pl.ds(c * GLA_SUB + t, 1), cols]
                p = (qt * k2) * jnp.exp(bt - b)
                keep = sub[:, :1] <= t
                cols_a.append(jnp.where(keep, jnp.sum(jnp.where(first, p, 0.0), axis=1, keepdims=True), 0.0))
                cols_b.append(jnp.where(keep, jnp.sum(jnp.where(first, 0.0, p), axis=1, keepdims=True), 0.0))
            ut = []
            for j, sc_cols in enumerate((cols_a, cols_b)):
                head = 2 * hp + j
                vcols = pl.ds(head * GLA_DV, GLA_DV)
                vh = v_ref[bb, rows, vcols]
                qm = jnp.where(first if j == 0 else ~first, qe, 0.0).astype(bf16)
                o = lax.dot_general(qm, sb, NT_DIMS, preferred_element_type=f32)
                o = o + jnp.concatenate([jnp.sum(cj * vh, axis=0, keepdims=True) for cj in sc_cols], axis=0)
                ms = jnp.mean(o * o, axis=1, keepdims=True)
                r = r_ref[bb, rows, vcols]
                o_ref[bb, rows, vcols] = (o * lax.rsqrt(ms + RMS_EPS) * g_ref[:, vcols]) * (r * jax.nn.sigmoid(r))
                ut.append(jnp.dot(vh.T.astype(bf16), kd, preferred_element_type=f32))
            states[ci] = jnp.exp(b_last) * state + jnp.where(lane_s, ut[0], ut[1])
    for ci in range(len(chains)):
        st_ref[ci] = states[ci]


def _gla_tri():
    r = np.arange(GLA_BLOCK)
    return jnp.asarray(((r[:, None] // GLA_SUB == r[None, :] // GLA_SUB) & (r[None, :] <= r[:, None])).astype(np.float32))


def _gla(proj, w_alpha, b_alpha, norm_g):
    bsz, s, _ = proj.shape
    rank = w_alpha.shape[0]
    wa = jnp.pad(w_alpha, ((0, ALR_W - rank), (0, 0))).astype(jnp.bfloat16)
    col = lambda width, off: pl.BlockSpec((GLA_NB, GLA_BLOCK, width), lambda b, i: (b, i, off // width))
    const = lambda shape: pl.BlockSpec(shape, lambda b, i: (0, 0))
    return pl.pallas_call(
        _gla_kernel,
        grid=(bsz // GLA_NB, s // GLA_BLOCK),
        in_specs=[col(GLA_QK_WIDTH, COL_QG), col(GLA_QK_WIDTH, COL_KG), col(GLA_WIDTH, COL_VG), col(ALR_W, COL_ALR),
                  col(GLA_WIDTH, COL_RG), const((ALR_W, GLA_QK_WIDTH)), const((1, GLA_QK_WIDTH)),
                  const((1, GLA_WIDTH)), const((GLA_BLOCK, GLA_BLOCK))],
        out_specs=pl.BlockSpec((GLA_NB, GLA_BLOCK, GLA_WIDTH), lambda b, i: (b, i, 0)),
        out_shape=jax.ShapeDtypeStruct((bsz, s, GLA_WIDTH), jnp.float32),
        scratch_shapes=[pltpu.VMEM((GLA_NB * GLA_HEADS // 2, GLA_DV, GLA_PAIR), jnp.float32),
                        pltpu.VMEM((GLA_NB, GLA_BLOCK, GLA_QK_WIDTH), jnp.float32),
                        pltpu.VMEM((GLA_NB, GLA_BLOCK, GLA_QK_WIDTH), jnp.float32)],
        compiler_params=pltpu.CompilerParams(dimension_semantics=("arbitrary", "arbitrary")),
        name="gla",
    )(proj, proj, proj, proj, proj, wa, b_alpha.reshape(1, -1), norm_g.reshape(1, -1), _gla_tri())


def peer_mixer(h_tiles, w_q, keys_1, keys_2, u_tab, v_tab):
    bsz, s, rs, rl = h_tiles.shape
    ht = h_tiles.reshape(bsz * s, rs, rl)
    experts, gates = _peer_route(ht, w_q, keys_1, keys_2)
    return _peer_experts(ht, experts, gates, u_tab, v_tab).reshape(bsz, s, rs, rl)


ROUTE_TT = 512


def _top16_rows(s, code, big):
    vals, codes = [], []
    for _ in range(PEER_TOPK):
        m = jnp.max(s, axis=0, keepdims=True)
        c = jnp.min(jnp.where(s == m, code, big), axis=0, keepdims=True)
        s = jnp.where(code == c, NEG_INF, s)
        vals.append(m)
        codes.append(c)
    return vals, codes


def _peer_route_kernel(h_ref, wq_ref, k1_ref, k2_ref, e_ref, g_ref, q_ref):
    half = PEER_QUERY_DIM // 2
    tt = ROUTE_TT
    h = jnp.concatenate([h_ref[:, c, :] for c in range(h_ref.shape[1])], axis=-1)
    q_ref[...] = jnp.dot(h.astype(jnp.bfloat16), wq_ref[...], preferred_element_type=jnp.float32)
    key_id = lax.broadcasted_iota(jnp.int32, (PEER_N_KEYS, 2 * tt), 0).astype(jnp.float32)
    sub8 = lax.broadcasted_iota(jnp.int32, (8, tt), 0)
    nt = (((1,), (1,)), ((), ()))

    def head_body(hd, carry):
        off = pl.multiple_of(hd * PEER_QUERY_DIM, PEER_QUERY_DIM)
        q1 = q_ref[:, pl.ds(off, half)].astype(jnp.bfloat16)
        q2 = q_ref[:, pl.ds(off + half, half)].astype(jnp.bfloat16)
        s1 = lax.dot_general(k1_ref[...], q1, nt, preferred_element_type=jnp.float32)
        s2 = lax.dot_general(k2_ref[...], q2, nt, preferred_element_type=jnp.float32)
        vals, ids = _top16_rows(jnp.concatenate([s1, s2], axis=1), key_id, float(PEER_N_KEYS))
        v = jnp.concatenate(vals, axis=0)
        ix = jnp.concatenate(ids, axis=0).astype(jnp.int32)
        v1, v2, i1, i2 = v[:, :tt], v[:, tt:], ix[:, :tt], ix[:, tt:]
        cand, code = [], []
        for a, b0 in [(0, 0), (0, 8), (1, 0), (2, 0), (3, 0), (4, 0), (5, 0), (6, 0), (7, 0)]:
            cand.append(v1[a:a + 1] + v2[b0:b0 + 8])
            flat = a * PEER_TOPK + b0 + sub8
            code.append((flat << 14) | (i1[a:a + 1] * PEER_N_KEYS + i2[b0:b0 + 8]))
        cand.append(v1[8:16] + v2[0:1])
        code.append((((8 + sub8) * PEER_TOPK) << 14) | (i1[8:16] * PEER_N_KEYS + i2[0:1]))
        sc, cc = _top16_rows(jnp.concatenate(cand, axis=0), jnp.concatenate(code, axis=0).astype(jnp.float32),
                             float(1 << 30))
        sc = jnp.concatenate(sc, axis=0)
        ex = jnp.exp(sc - sc[0:1])
        g_ref[hd] = ex / jnp.sum(ex, axis=0, keepdims=True)
        e_ref[hd] = jnp.concatenate(cc, axis=0).astype(jnp.int32) & (PEER_N_EXPERTS - 1)
        return carry

    lax.fori_loop(0, PEER_HEADS, head_body, 0)


def _peer_route(hf, w_q, keys_1, keys_2):
    t, rs, rl = hf.shape
    d = rs * rl
    qw = PEER_HEADS * PEER_QUERY_DIM
    const = lambda shape: pl.BlockSpec(shape, lambda i: tuple(0 for _ in shape))
    out_spec = pl.BlockSpec((PEER_HEADS, PEER_TOPK, ROUTE_TT), lambda i: (0, 0, i))
    e_t, g_t = pl.pallas_call(
        _peer_route_kernel,
        grid=(t // ROUTE_TT,),
        in_specs=[pl.BlockSpec((ROUTE_TT, rs, rl), lambda i: (i, 0, 0)), const((d, qw)),
                  const(keys_1.shape), const(keys_2.shape)],
        out_specs=[out_spec, out_spec],
        out_shape=[jax.ShapeDtypeStruct((PEER_HEADS, PEER_TOPK, t), jnp.int32),
                   jax.ShapeDtypeStruct((PEER_HEADS, PEER_TOPK, t), jnp.float32)],
        scratch_shapes=[pltpu.VMEM((ROUTE_TT, qw), jnp.float32)],
        compiler_params=pltpu.CompilerParams(dimension_semantics=("arbitrary",), vmem_limit_bytes=48 * 1024 * 1024),
        name="peer_route",
    )(hf, w_q.astype(jnp.bfloat16), keys_1.astype(jnp.bfloat16), keys_2.astype(jnp.bfloat16))
    return e_t, g_t


PEER_PAIRS = PEER_HEADS * PEER_TOPK
PEER_TB = 128
PEER_HALF = 64
PEER_TOK_UNROLL = 4
HALF_D = D_MODEL // 2
WORD_SUB, WORD_LANE = HALF_D // 128, 128
ROW_SUB = 2 * WORD_SUB
HI_MASK = -65536
IDX_WAYS = 8


PACK_EB = 512
TAB_PAD = PACK_EB * WORD_SUB


def _pack_kernel(x_ref, o_ref):
    i = pl.program_id(0)

    @pl.when((i == 0) | (i == pl.num_programs(0) - 1))
    def _():
        o_ref[...] = jnp.zeros(o_ref.shape, o_ref.dtype)

    @pl.when((i > 0) & (i < pl.num_programs(0) - 1))
    def _():
        x = x_ref[...].astype(jnp.bfloat16).astype(jnp.float32)
        bits = lax.bitcast_convert_type(x, jnp.int32)
        word = lax.shift_right_logical(bits[:, :HALF_D], 16) | (bits[:, HALF_D:] & HI_MASK)
        for c in range(WORD_SUB):
            o_ref[:, c, :] = word[:, c * WORD_LANE:(c + 1) * WORD_LANE]


def _pack_table(tab):
    e, d = tab.shape
    nblk = e // PACK_EB
    out = pl.pallas_call(
        _pack_kernel,
        grid=(nblk + 2,),
        in_specs=[pl.BlockSpec((PACK_EB, d), lambda i: (jnp.clip(i - 1, 0, nblk - 1), 0))],
        out_specs=pl.BlockSpec((PACK_EB, WORD_SUB, WORD_LANE), lambda i: (i, 0, 0)),
        out_shape=jax.ShapeDtypeStruct(((nblk + 2) * PACK_EB, WORD_SUB, WORD_LANE), jnp.int32),
        name="pack_table",
    )(tab)
    return out.reshape((nblk + 2) * TAB_PAD, WORD_LANE)


def _pair_tile(tab_ref, ia, ib, upper):
    wa = tab_ref[pl.ds(pl.multiple_of(ia, WORD_SUB), ROW_SUB), :]
    wb = tab_ref[pl.ds(pl.multiple_of(ib - WORD_SUB, WORD_SUB), ROW_SUB), :]
    return jnp.where(upper, wa, wb)


_ORDER = (6, 2, 4, 0, 7, 3, 5, 1)


def _fold_pairs(ps, sub):
    b1 = (sub & 2) != 0
    w = []
    for i in range(2):
        x, y = ps[2 * i], ps[2 * i + 1]
        w.append(jnp.where(b1, x + pltpu.roll(x, 2, 0), y + pltpu.roll(y, 6, 0)))
    b0 = (sub & 1) != 0
    x, y = w
    return jnp.where(b0, x + pltpu.roll(x, 1, 0), y + pltpu.roll(y, 7, 0))


def _peer_u_kernel(*refs):
    idx_refs = refs[:IDX_WAYS]
    h_ref, tab_ref, a_ref = refs[IDX_WAYS:]
    sub = lax.broadcasted_iota(jnp.int32, (ROW_SUB, WORD_LANE), 0)
    lane = lax.broadcasted_iota(jnp.int32, (ROW_SUB, WORD_LANE), 1)
    n_groups = PEER_PAIRS // IDX_WAYS
    feed = [int(i) for i in np.argsort(np.array(_ORDER))]
    a_ref[...] = jnp.zeros(a_ref.shape, a_ref.dtype)

    def folded_tiles(t):
        h = h_ref[t]
        h_lo = jnp.concatenate([h[:WORD_SUB], h[:WORD_SUB]], axis=0)
        h_hi = jnp.concatenate([h[WORD_SUB:], h[WORD_SUB:]], axis=0)
        tiles = []
        for g in range(n_groups):
            ps = []
            for i in range(4):
                w = _pair_tile(tab_ref, idx_refs[feed[2 * i]][g * PEER_TB + t], idx_refs[feed[2 * i + 1]][g * PEER_TB + t],
                               sub < WORD_SUB)
                lo = lax.bitcast_convert_type(w << 16, jnp.float32)
                hi = lax.bitcast_convert_type(w & HI_MASK, jnp.float32)
                ps.append(lo * h_lo + hi * h_hi)
            tiles.append(_fold_pairs(ps, sub))
        return tuple(tiles)

    def lane_sums(t, tiles):
        here = lane == t
        for g in range(n_groups):
            a_ref[g] = jnp.where(here, jnp.sum(tiles[g], axis=1, keepdims=True), a_ref[g])

    def tok_body(t, prev):
        lane_sums(t - 1, prev)
        return folded_tiles(t)

    last = lax.fori_loop(1, PEER_TB, tok_body, folded_tiles(0))
    lane_sums(PEER_TB - 1, last)


def _split3_lanes(x, tb):
    mask = lambda v: lax.bitcast_convert_type(lax.bitcast_convert_type(v, jnp.int32) & HI_MASK, jnp.float32)
    hi = mask(x)
    mid = mask(x - hi)
    lo = x - hi - mid
    part = lax.broadcasted_iota(jnp.int32, x.shape, 1) // tb
    return jnp.where(part == 0, hi, jnp.where(part == 1, mid, jnp.where(part == 2, lo, 0.0))).astype(jnp.bfloat16)


def _peer_v_kernel(*refs):
    idx_refs = refs[:IDX_WAYS]
    a_ref, g_ref, tab_ref, sel_ref, o_ref, wrep_ref = refs[IDX_WAYS:]
    per_way = PEER_PAIRS // IDX_WAYS
    a = a_ref[...].reshape(PEER_PAIRS, PEER_TB)
    gates = jnp.concatenate([g_ref[:, m, :] for m in range(per_way)], axis=0)
    w = gates * (0.5 * a * (1.0 + lax.erf(a * (2.0 ** -0.5))))
    w_swapped = pltpu.roll(w, PEER_HALF, 1)
    low_lanes = lax.broadcasted_iota(jnp.int32, w.shape, 1) < PEER_HALF
    upper = lax.broadcasted_iota(jnp.int32, (ROW_SUB, WORD_LANE), 0) < WORD_SUB

    def one_token(tl, first):
        it = first + tl
        col = pl.multiple_of(tl * WORD_LANE, WORD_LANE)
        z = jnp.zeros((ROW_SUB, WORD_LANE), jnp.float32)
        acc_lo, acc_hi = [z, z], [z, z]
        for m in range(per_way):
            for jj in range(IDX_WAYS // 2):
                p = m * IDX_WAYS + 2 * jj
                words = _pair_tile(tab_ref, idx_refs[2 * jj][m * PEER_TB + it], idx_refs[2 * jj + 1][m * PEER_TB + it], upper)
                w2 = jnp.where(upper, wrep_ref[pl.ds(p, 1), pl.ds(col, WORD_LANE)],
                               wrep_ref[pl.ds(p + 1, 1), pl.ds(col, WORD_LANE)])
                k = jj % 2
                acc_lo[k] = acc_lo[k] + lax.bitcast_convert_type(words << 16, jnp.float32) * w2
                acc_hi[k] = acc_hi[k] + lax.bitcast_convert_type(words & HI_MASK, jnp.float32) * w2
        lo, hi = acc_lo[0] + acc_lo[1], acc_hi[0] + acc_hi[1]
        o_ref[it] = jnp.concatenate([lo[:WORD_SUB] + lo[WORD_SUB:], hi[:WORD_SUB] + hi[WORD_SUB:]], axis=0)

    for half in range(PEER_TB // PEER_HALF):
        x = jnp.where(low_lanes, w, w_swapped) if half == 0 else jnp.where(low_lanes, w_swapped, w)
        wcat = _split3_lanes(jnp.concatenate([x, x], axis=1), PEER_HALF)
        wrep_ref[...] = jnp.dot(wcat, sel_ref[...], preferred_element_type=jnp.float32)

        def tok_body(i, carry, first=half * PEER_HALF):
            for j in range(PEER_TOK_UNROLL):
                one_token(PEER_TOK_UNROLL * i + j, first)
            return carry

        lax.fori_loop(0, PEER_HALF // PEER_TOK_UNROLL, tok_body, 0)


_PEER_VMEM_LIMIT = 56 * 1024 * 1024


def _peer_experts(hf, experts, gates, u_tab, v_tab):
    t = hf.shape[0]
    nblk = t // PEER_TB
    per_way = PEER_PAIRS // IDX_WAYS
    smem_spec = pl.BlockSpec((PEER_TB * per_way,), lambda i: (i,), memory_space=pltpu.SMEM)
    tab_spec = pl.BlockSpec(memory_space=pltpu.VMEM)
    row_spec = pl.BlockSpec((PEER_TB, ROW_SUB, WORD_LANE), lambda i: (i, 0, 0))
    a_spec = pl.BlockSpec((per_way, IDX_WAYS, PEER_TB), lambda i: (i, 0, 0))
    params = pltpu.CompilerParams(dimension_semantics=("arbitrary",), vmem_limit_bytes=_PEER_VMEM_LIMIT)
    assert IDX_WAYS == PEER_HEADS and PEER_TB == LANES
    idx_ways = [(experts[j] * WORD_SUB + TAB_PAD).reshape(per_way, nblk, PEER_TB).transpose(1, 0, 2).reshape(-1)
                for j in range(IDX_WAYS)]
    a_t = pl.pallas_call(
        _peer_u_kernel,
        grid=(nblk,),
        in_specs=[smem_spec] * IDX_WAYS + [row_spec, tab_spec],
        out_specs=a_spec,
        out_shape=jax.ShapeDtypeStruct((nblk * per_way, IDX_WAYS, PEER_TB), jnp.float32),
        compiler_params=params, name="peer_u",
    )(*idx_ways, hf, _pack_table(u_tab))
    eye = np.repeat(np.eye(PEER_HALF, dtype=np.float32), WORD_LANE, axis=1)
    sel = jnp.asarray(np.concatenate([eye, eye, eye, np.zeros_like(eye)], axis=0), jnp.bfloat16)
    return pl.pallas_call(
        _peer_v_kernel,
        grid=(nblk,),
        in_specs=[smem_spec] * IDX_WAYS + [a_spec, pl.BlockSpec((IDX_WAYS, per_way, PEER_TB), lambda i: (0, 0, i)),
                                           tab_spec, pl.BlockSpec(sel.shape, lambda i: (0, 0))],
        out_specs=row_spec,
        out_shape=jax.ShapeDtypeStruct((t, ROW_SUB, WORD_LANE), jnp.float32),
        scratch_shapes=[pltpu.VMEM((PEER_PAIRS, PEER_HALF * WORD_LANE), jnp.float32)],
        compiler_params=params, name="peer_v",
    )(*idx_ways, a_t, gates, _pack_table(v_tab), sel)


def _final_kernel(x_ref, p_ref, gate_ref, g_ref, o_ref):
    peer = jnp.concatenate([p_ref[:, c, :] for c in range(ROW_SUB)], axis=-1)
    o_ref[...] = _rms(x_ref[...] + gate_ref[...] * peer) * g_ref[...]


def _final(x1, peer, gate2, g):
    bsz, s, d = x1.shape
    row = pl.BlockSpec((None, ROW_TILE, d), lambda b, i: (b, i, 0))
    return pl.pallas_call(
        _final_kernel,
        grid=(bsz, s // ROW_TILE),
        in_specs=[row, pl.BlockSpec((None, ROW_TILE, ROW_SUB, d // ROW_SUB), lambda b, i: (b, i, 0, 0)),
                  pl.BlockSpec((None, 1, d), lambda b, i: (b, 0, 0)), pl.BlockSpec((1, d), lambda b, i: (0, 0))],
        out_specs=row,
        out_shape=jax.ShapeDtypeStruct((bsz, s, d), jnp.float32), name="final",
    )(x1, peer, gate2, g.reshape(1, d))


def _in_proj_weights(w_in):
    o = np.cumsum([0, GLA_QK_WIDTH, GLA_QK_WIDTH, GLA_WIDTH, GLA_GATE_RANK, GLA_WIDTH, SWA_WIDTH, SWA_KV_WIDTH])
    src = np.full((PROJ_W,), IN_WIDTH, np.int32)
    src[COL_QS + _wide_cols()] = o[5] + np.arange(SWA_WIDTH)
    for dst, start, width in ((COL_QG, o[0], GLA_QK_WIDTH), (COL_KG, o[1], GLA_QK_WIDTH), (COL_VG, o[2], GLA_WIDTH),
                              (COL_ALR, o[3], GLA_GATE_RANK), (COL_RG, o[4], GLA_WIDTH),
                              (COL_KS, o[6], SWA_KV_WIDTH), (COL_VS, o[7], SWA_KV_WIDTH)):
        src[dst:dst + width] = start + np.arange(width)
    w_ext = jnp.concatenate([w_in, jnp.zeros((w_in.shape[0], 1), w_in.dtype)], axis=1)
    return w_ext[:, src].astype(jnp.bfloat16)


def _out_proj_weights(w_out):
    src = np.full((MIX_W,), MIX_WIDTH, np.int32)
    src[:GLA_WIDTH] = np.arange(GLA_WIDTH)
    src[GLA_WIDTH + _wide_cols()] = GLA_WIDTH + np.arange(SWA_WIDTH)
    w_ext = jnp.concatenate([w_out, jnp.zeros((1, w_out.shape[1]), w_out.dtype)], axis=0)
    return w_ext[src].astype(jnp.bfloat16)


def kernel(x, c, w_ada, b_ada, norm1_g, norm2_g, w_in, w_gla_alpha, b_gla_alpha, gla_norm_g,
           swa_sinks, swa_norm_g, w_out, w_peer_q, peer_keys_1, peer_keys_2, peer_u, peer_v, final_g):
    assert w_ada.shape[0] == 1, "single-layer block"
    l = 0
    mod = _ada(c, w_ada[l], b_ada[l])
    shift1, scale1, gate1, shift2, scale2, gate2 = [m[:, None, :] for m in jnp.split(mod, 6, axis=-1)]
    proj = _in_proj(x, norm1_g[l], shift1, scale1, _in_proj_weights(w_in[l]))
    o_gla = _gla(proj, w_gla_alpha[l], b_gla_alpha[l], gla_norm_g[l])
    g_wide = jnp.zeros((1, SWA_QW), jnp.float32).at[0, _wide_cols()].set(swa_norm_g[l])
    o_swa = _swa(proj, swa_sinks[l], g_wide)
    x1, h2 = _out_proj(x, o_gla, o_swa, _out_proj_weights(w_out[l]), gate1, norm2_g[l], shift2, scale2)
    peer = peer_mixer(h2, w_peer_q[l], peer_keys_1[l], peer_keys_2[l], peer_u[l], peer_v[l])
    return _final(x1, peer, gate2, final_g)
```

```python
import jax, jax.numpy as jnp
from jax import lax
import numpy as np
from jax.experimental import pallas as pl
from jax.experimental.pallas import tpu as pltpu

D_MODEL = 1024
BATCH = 4
SEQ = 8192
DEPTH = 1

GLA_HEADS = 4
GLA_DK = 64
GLA_DV = 128
GLA_GATE_RANK = 16
GLA_GATE_TEMP = 16.0
GLA_CHUNK = 64
SWA_HEADS = 8
SWA_KV_HEADS = 2
SWA_HEAD_DIM = 64
SWA_WINDOW = 128
SWA_BLOCK = 128
PEER_HEADS = 8
PEER_N_KEYS = 128
PEER_N_EXPERTS = PEER_N_KEYS * PEER_N_KEYS
PEER_QUERY_DIM = 256
PEER_TOPK = 16
RMS_EPS = 1e-6

GLA_QK_WIDTH = GLA_HEADS * GLA_DK
GLA_WIDTH = GLA_HEADS * GLA_DV
SWA_WIDTH = SWA_HEADS * SWA_HEAD_DIM
SWA_KV_WIDTH = SWA_KV_HEADS * SWA_HEAD_DIM
MIX_WIDTH = GLA_WIDTH + SWA_WIDTH
IN_WIDTH = 2 * GLA_QK_WIDTH + GLA_WIDTH + GLA_GATE_RANK + GLA_WIDTH + SWA_WIDTH + 2 * SWA_KV_WIDTH


LANES = 128
NEG_INF = float("-inf")
NT_DIMS = (((1,), (1,)), ((), ()))
ROW_TILE = 512
WIDE = 2 * SWA_HEAD_DIM
SWA_QW = SWA_HEADS * WIDE
ALR_W = LANES
COL_QS, COL_QG, COL_KG, COL_VG, COL_RG = 0, SWA_QW, SWA_QW + GLA_QK_WIDTH, SWA_QW + 2 * GLA_QK_WIDTH, SWA_QW + 2 * GLA_QK_WIDTH + GLA_WIDTH
COL_KS = COL_RG + GLA_WIDTH
COL_VS = COL_KS + SWA_KV_WIDTH
COL_ALR = COL_VS + SWA_KV_WIDTH
PROJ_W = COL_ALR + ALR_W
MIX_W = GLA_WIDTH + SWA_QW


def _wide_cols():
    grp = SWA_HEADS // SWA_KV_HEADS
    h = np.arange(SWA_WIDTH) // SWA_HEAD_DIM
    return h * WIDE + (h // grp) * SWA_HEAD_DIM + np.arange(SWA_WIDTH) % SWA_HEAD_DIM


def _rms(x):
    return x * lax.rsqrt(jnp.mean(x * x, axis=-1, keepdims=True) + RMS_EPS)


def _ada_kernel(c_ref, w_ref, b_ref, o_ref):
    c = c_ref[...]
    ca = (c * jax.nn.sigmoid(c)).astype(jnp.bfloat16)
    o_ref[...] = jnp.dot(ca, w_ref[...].astype(jnp.bfloat16), preferred_element_type=jnp.float32) + b_ref[...]


def _ada(c, w, b):
    bsz, d = c.shape
    n = w.shape[1]
    tn = 1024
    return pl.pallas_call(
        _ada_kernel,
        grid=(n // tn,),
        in_specs=[pl.BlockSpec((bsz, d), lambda j: (0, 0)), pl.BlockSpec((d, tn), lambda j: (0, j)),
                  pl.BlockSpec((1, tn), lambda j: (0, j))],
        out_specs=pl.BlockSpec((bsz, tn), lambda j: (0, j)),
        out_shape=jax.ShapeDtypeStruct((bsz, n), jnp.float32), name="ada",
    )(c, w, b.reshape(1, n))


def _in_proj_kernel(x_ref, g_ref, sh_ref, sc_ref, w_ref, o_ref):
    h = _rms(x_ref[...]) * g_ref[...] * (1.0 + sc_ref[...]) + sh_ref[...]
    o_ref[...] = jnp.dot(h.astype(jnp.bfloat16), w_ref[...], preferred_element_type=jnp.float32)


def _in_proj(x, g, shift, scale, w):
    bsz, s, d = x.shape
    n = w.shape[1]
    row = lambda width: pl.BlockSpec((None, ROW_TILE, width), lambda b, i: (b, i, 0))
    per_b = pl.BlockSpec((None, 1, d), lambda b, i: (b, 0, 0))
    return pl.pallas_call(
        _in_proj_kernel,
        grid=(bsz, s // ROW_TILE),
        in_specs=[row(d), pl.BlockSpec((1, d), lambda b, i: (0, 0)), per_b, per_b,
                  pl.BlockSpec((d, n), lambda b, i: (0, 0))],
        out_specs=row(n),
        out_shape=jax.ShapeDtypeStruct((bsz, s, n), jnp.float32),
        compiler_params=pltpu.CompilerParams(dimension_semantics=("arbitrary", "arbitrary"),
                                             vmem_limit_bytes=48 * 1024 * 1024),
        name="in_proj",
    )(x, g.reshape(1, d), shift, scale, w)


def _swa_kernel(sink_ref, q_ref, kp_ref, kc_ref, vp_ref, vc_ref, g_ref, o_ref):
    f32, bf16 = jnp.float32, jnp.bfloat16
    blk = SWA_BLOCK
    k2 = jnp.concatenate([kp_ref[...], kc_ref[...]], axis=0).astype(bf16)
    v2 = jnp.concatenate([vp_ref[...], vc_ref[...]], axis=0).astype(bf16)
    qi = lax.broadcasted_iota(jnp.int32, (blk, 2 * blk), 0)
    kj = lax.broadcasted_iota(jnp.int32, (blk, 2 * blk), 1)
    dist = qi + blk - kj
    valid = (dist >= 0) & (dist < SWA_WINDOW) & ((kj >= blk) | (pl.program_id(1) > 0))
    distf = dist.astype(f32)
    lane_kv = lax.broadcasted_iota(jnp.int32, (blk, WIDE), 1) // SWA_HEAD_DIM
    grp = SWA_HEADS // SWA_KV_HEADS
    heads = range(SWA_HEADS)
    scores = []
    for h in heads:
        q = (q_ref[:, h * WIDE:(h + 1) * WIDE] * (SWA_HEAD_DIM ** -0.5)).astype(bf16)
        scores.append(lax.dot_general(q, k2, NT_DIMS, preferred_element_type=f32))
    probs = []
    for h in heads:
        sc = scores[h] - (2.0 ** (-8.0 * (h + 1) / SWA_HEADS)) * distf
        sc = jnp.where(valid, sc, NEG_INF)
        sink = sink_ref[h]
        m = jnp.maximum(jnp.max(sc, axis=-1, keepdims=True), sink)
        p = jnp.exp(sc - m)
        probs.append((p / (jnp.sum(p, axis=-1, keepdims=True) + jnp.exp(sink - m))).astype(bf16))
    outs = [jnp.dot(probs[h], v2, preferred_element_type=f32) for h in heads]
    outs = [jnp.where(lane_kv == h // grp, outs[h], 0.0) for h in heads]
    ssq = sum(jnp.sum(o * o, axis=-1, keepdims=True) for o in outs)
    inv = lax.rsqrt(ssq * (1.0 / SWA_WIDTH) + RMS_EPS)
    for h in range(SWA_HEADS):
        o_ref[:, h * WIDE:(h + 1) * WIDE] = outs[h] * inv * g_ref[:, h * WIDE:(h + 1) * WIDE]


def _swa(proj, sinks, norm_g_wide):
    bsz, s, _ = proj.shape
    col = lambda width, off, prev: pl.BlockSpec(
        (None, SWA_BLOCK, width), lambda b, n: (b, jnp.maximum(n - 1, 0) if prev else n, off // width))
    return pl.pallas_call(
        _swa_kernel,
        grid=(bsz, s // SWA_BLOCK),
        in_specs=[pl.BlockSpec(memory_space=pltpu.SMEM), col(SWA_QW, COL_QS, False),
                  col(SWA_KV_WIDTH, COL_KS, True), col(SWA_KV_WIDTH, COL_KS, False),
                  col(SWA_KV_WIDTH, COL_VS, True), col(SWA_KV_WIDTH, COL_VS, False),
                  pl.BlockSpec((1, SWA_QW), lambda b, n: (0, 0))],
        out_specs=pl.BlockSpec((None, SWA_BLOCK, SWA_QW), lambda b, n: (b, n, 0)),
        out_shape=jax.ShapeDtypeStruct((bsz, s, SWA_QW), jnp.float32),
        compiler_params=pltpu.CompilerParams(dimension_semantics=("arbitrary", "arbitrary")),
        name="swa",
    )(sinks, proj, proj, proj, proj, proj, norm_g_wide)


def _out_proj_kernel(x_ref, og_ref, os_ref, w_ref, gate_ref, g2_ref, sh_ref, sc_ref, x1_ref, h2_ref):
    mix = jnp.concatenate([og_ref[...], os_ref[...]], axis=-1).astype(jnp.bfloat16)
    x1 = x_ref[...] + gate_ref[...] * jnp.dot(mix, w_ref[...], preferred_element_type=jnp.float32)
    x1_ref[...] = x1
    h2 = _rms(x1) * g2_ref[...] * (1.0 + sc_ref[...]) + sh_ref[...]
    for c in range(ROW_SUB):
        h2_ref[:, c, :] = h2[:, c * LANES:(c + 1) * LANES]


def _out_proj(x, o_gla, o_swa, w, gate1, g2, shift2, scale2):
    bsz, s, d = x.shape
    row = lambda width: pl.BlockSpec((None, ROW_TILE, width), lambda b, i: (b, i, 0))
    per_b = pl.BlockSpec((None, 1, d), lambda b, i: (b, 0, 0))
    shp = jax.ShapeDtypeStruct((bsz, s, d), jnp.float32)
    return pl.pallas_call(
        _out_proj_kernel,
        grid=(bsz, s // ROW_TILE),
        in_specs=[row(d), row(GLA_WIDTH), row(SWA_QW), pl.BlockSpec((MIX_W, d), lambda b, i: (0, 0)),
                  per_b, pl.BlockSpec((1, d), lambda b, i: (0, 0)), per_b, per_b],
        out_specs=[row(d), pl.BlockSpec((None, ROW_TILE, ROW_SUB, d // ROW_SUB), lambda b, i: (b, i, 0, 0))],
        out_shape=[shp, jax.ShapeDtypeStruct((bsz, s, ROW_SUB, d // ROW_SUB), jnp.float32)],
        compiler_params=pltpu.CompilerParams(dimension_semantics=("arbitrary", "arbitrary"),
                                             vmem_limit_bytes=48 * 1024 * 1024),
        name="out_proj",
    )(x, o_gla, o_swa, w, gate1, g2.reshape(1, d), shift2, scale2)


GLA_SUB = 16
GLA_BLOCK = 128
GLA_PAIR = 2 * GLA_DK
GLA_NB = 1


def _gla_kernel(q_ref, k_ref, v_ref, alr_ref, r_ref, wa_ref, ba_ref, g_ref, tri_ref, o_ref, st_ref, b_ref, q2_ref):
    f32, bf16 = jnp.float32, jnp.bfloat16

    @pl.when(pl.program_id(1) == 0)
    def _():
        st_ref[...] = jnp.zeros(st_ref.shape, f32)

    for bb in range(GLA_NB):
        z = jnp.dot(alr_ref[bb].astype(bf16), wa_ref[...], preferred_element_type=f32) + ba_ref[...]
        log_a = -(jnp.maximum(-z, 0.0) + jnp.log1p(jnp.exp(-jnp.abs(z)))) * (1.0 / GLA_GATE_TEMP)
        b_ref[bb] = jnp.dot(tri_ref[...], log_a, preferred_element_type=f32, precision=lax.Precision.HIGHEST)
        q2_ref[bb] = q_ref[bb] * (GLA_DK ** -0.5)
    lane = lax.broadcasted_iota(jnp.int32, (GLA_SUB, GLA_PAIR), 1)
    sub = lax.broadcasted_iota(jnp.int32, (GLA_SUB, GLA_PAIR), 0)
    first = lane < GLA_DK
    lane_s = lax.broadcasted_iota(jnp.int32, (GLA_DV, GLA_PAIR), 1) < GLA_DK

    chains = [(bb, hp) for bb in range(GLA_NB) for hp in range(GLA_HEADS // 2)]
    states = [st_ref[i] for i in range(len(chains))]
    for c in range(GLA_BLOCK // GLA_SUB):
        rows = pl.ds(c * GLA_SUB, GLA_SUB)
        for ci, (bb, hp) in enumerate(chains):
            cols = pl.ds(hp * GLA_PAIR, GLA_PAIR)
            state = states[ci]
            b = b_ref[bb, rows, cols]
            b_last = b_ref[bb, pl.ds(c * GLA_SUB + GLA_SUB - 1, 1), cols]
            q2, k2 = q2_ref[bb, rows, cols], k_ref[bb, rows, cols]
            qe = q2 * jnp.exp(b)
            kd = (k2 * jnp.exp(b_last - b)).astype(bf16)
            sb = state.astype(bf16)
            cols_a, cols_b = [], []
            for t in range(GLA_SUB):
                bt = b_ref[bb, pl.ds(c * GLA_SUB + t, 1), cols]
                qt = q2_ref[bb, pl.ds(c * GLA_SUB + t, 1), cols]
                p = (qt * k2) * jnp.exp(bt - b)
                keep = sub[:, :1] <= t
                cols_a.append(jnp.where(keep, jnp.sum(jnp.where(first, p, 0.0), axis=1, keepdims=True), 0.0))
                cols_b.append(jnp.where(keep, jnp.sum(jnp.where(first, 0.0, p), axis=1, keepdims=True), 0.0))
            ut = []
            for j, sc_cols in enumerate((cols_a, cols_b)):
                head = 2 * hp + j
                vcols = pl.ds(head * GLA_DV, GLA_DV)
                vh = v_ref[bb, rows, vcols]
                qm = jnp.where(first if j == 0 else ~first, qe, 0.0).astype(bf16)
                o = lax.dot_general(qm, sb, NT_DIMS, preferred_element_type=f32)
                o = o + jnp.concatenate([jnp.sum(cj * vh, axis=0, keepdims=True) for cj in sc_cols], axis=0)
                ms = jnp.mean(o * o, axis=1, keepdims=True)
                r = r_ref[bb, rows, vcols]
                o_ref[bb, rows, vcols] = (o * lax.rsqrt(ms + RMS_EPS) * g_ref[:, vcols]) * (r * jax.nn.sigmoid(r))
                ut.append(jnp.dot(vh.T.astype(bf16), kd, preferred_element_type=f32))
            states[ci] = jnp.exp(b_last) * state + jnp.where(lane_s, ut[0], ut[1])
    for ci in range(len(chains)):
        st_ref[ci] = states[ci]


def _gla_tri():
    r = np.arange(GLA_BLOCK)
    return jnp.asarray(((r[:, None] // GLA_SUB == r[None, :] // GLA_SUB) & (r[None, :] <= r[:, None])).astype(np.float32))


def _gla(proj, w_alpha, b_alpha, norm_g):
    bsz, s, _ = proj.shape
    rank = w_alpha.shape[0]
    wa = jnp.pad(w_alpha, ((0, ALR_W - rank), (0, 0))).astype(jnp.bfloat16)
    col = lambda width, off: pl.BlockSpec((GLA_NB, GLA_BLOCK, width), lambda b, i: (b, i, off // width))
    const = lambda shape: pl.BlockSpec(shape, lambda b, i: (0, 0))
    return pl.pallas_call(
        _gla_kernel,
        grid=(bsz // GLA_NB, s // GLA_BLOCK),
        in_specs=[col(GLA_QK_WIDTH, COL_QG), col(GLA_QK_WIDTH, COL_KG), col(GLA_WIDTH, COL_VG), col(ALR_W, COL_ALR),
                  col(GLA_WIDTH, COL_RG), const((ALR_W, GLA_QK_WIDTH)), const((1, GLA_QK_WIDTH)),
                  const((1, GLA_WIDTH)), const((GLA_BLOCK, GLA_BLOCK))],
        out_specs=pl.BlockSpec((GLA_NB, GLA_BLOCK, GLA_WIDTH), lambda b, i: (b, i, 0)),
        out_shape=jax.ShapeDtypeStruct((bsz, s, GLA_WIDTH), jnp.float32),
        scratch_shapes=[pltpu.VMEM((GLA_NB * GLA_HEADS // 2, GLA_DV, GLA_PAIR), jnp.float32),
                        pltpu.VMEM((GLA_NB, GLA_BLOCK, GLA_QK_WIDTH), jnp.float32),
                        pltpu.VMEM((GLA_NB, GLA_BLOCK, GLA_QK_WIDTH), jnp.float32)],
        compiler_params=pltpu.CompilerParams(dimension_semantics=("arbitrary", "arbitrary")),
        name="gla",
    )(proj, proj, proj, proj, proj, wa, b_alpha.reshape(1, -1), norm_g.reshape(1, -1), _gla_tri())


def peer_mixer(h_tiles, w_q, keys_1, keys_2, u_tab, v_tab):
    bsz, s, rs, rl = h_tiles.shape
    ht = h_tiles.reshape(bsz * s, rs, rl)
    experts, gates = _peer_route(ht, w_q, keys_1, keys_2)
    return _peer_experts(ht, experts, gates, u_tab, v_tab).reshape(bsz, s, rs, rl)


ROUTE_TT = 512


def _top16_rows(s, code, big):
    vals, codes = [], []
    for _ in range(PEER_TOPK):
        m = jnp.max(s, axis=0, keepdims=True)
        c = jnp.min(jnp.where(s == m, code, big), axis=0, keepdims=True)
        s = jnp.where(code == c, NEG_INF, s)
        vals.append(m)
        codes.append(c)
    return vals, codes


def _peer_route_kernel(h_ref, wq_ref, k1_ref, k2_ref, e_ref, g_ref, q_ref):
    half = PEER_QUERY_DIM // 2
    tt = ROUTE_TT
    h = jnp.concatenate([h_ref[:, c, :] for c in range(h_ref.shape[1])], axis=-1)
    q_ref[...] = jnp.dot(h.astype(jnp.bfloat16), wq_ref[...], preferred_element_type=jnp.float32)
    key_id = lax.broadcasted_iota(jnp.int32, (PEER_N_KEYS, 2 * tt), 0).astype(jnp.float32)
    sub8 = lax.broadcasted_iota(jnp.int32, (8, tt), 0)
    nt = (((1,), (1,)), ((), ()))

    def head_body(hd, carry):
        off = pl.multiple_of(hd * PEER_QUERY_DIM, PEER_QUERY_DIM)
        q1 = q_ref[:, pl.ds(off, half)].astype(jnp.bfloat16)
        q2 = q_ref[:, pl.ds(off + half, half)].astype(jnp.bfloat16)
        s1 = lax.dot_general(k1_ref[...], q1, nt, preferred_element_type=jnp.float32)
        s2 = lax.dot_general(k2_ref[...], q2, nt, preferred_element_type=jnp.float32)
        vals, ids = _top16_rows(jnp.concatenate([s1, s2], axis=1), key_id, float(PEER_N_KEYS))
        v = jnp.concatenate(vals, axis=0)
        ix = jnp.concatenate(ids, axis=0).astype(jnp.int32)
        v1, v2, i1, i2 = v[:, :tt], v[:, tt:], ix[:, :tt], ix[:, tt:]
        cand, code = [], []
        for a, b0 in [(0, 0), (0, 8), (1, 0), (2, 0), (3, 0), (4, 0), (5, 0), (6, 0), (7, 0)]:
            cand.append(v1[a:a + 1] + v2[b0:b0 + 8])
            flat = a * PEER_TOPK + b0 + sub8
            code.append((flat << 14) | (i1[a:a + 1] * PEER_N_KEYS + i2[b0:b0 + 8]))
        cand.append(v1[8:16] + v2[0:1])
        code.append((((8 + sub8) * PEER_TOPK) << 14) | (i1[8:16] * PEER_N_KEYS + i2[0:1]))
        sc, cc = _top16_rows(jnp.concatenate(cand, axis=0), jnp.concatenate(code, axis=0).astype(jnp.float32),
                             float(1 << 30))
        sc = jnp.concatenate(sc, axis=0)
        ex = jnp.exp(sc - sc[0:1])
        g_ref[hd] = ex / jnp.sum(ex, axis=0, keepdims=True)
        e_ref[hd] = jnp.concatenate(cc, axis=0).astype(jnp.int32) & (PEER_N_EXPERTS - 1)
        return carry

    lax.fori_loop(0, PEER_HEADS, head_body, 0)


def _peer_route(hf, w_q, keys_1, keys_2):
    t, rs, rl = hf.shape
    d = rs * rl
    qw = PEER_HEADS * PEER_QUERY_DIM
    const = lambda shape: pl.BlockSpec(shape, lambda i: tuple(0 for _ in shape))
    out_spec = pl.BlockSpec((PEER_HEADS, PEER_TOPK, ROUTE_TT), lambda i: (0, 0, i))
    e_t, g_t = pl.pallas_call(
        _peer_route_kernel,
        grid=(t // ROUTE_TT,),
        in_specs=[pl.BlockSpec((ROUTE_TT, rs, rl), lambda i: (i, 0, 0)), const((d, qw)),
                  const(keys_1.shape), const(keys_2.shape)],
        out_specs=[out_spec, out_spec],
        out_shape=[jax.ShapeDtypeStruct((PEER_HEADS, PEER_TOPK, t), jnp.int32),
                   jax.ShapeDtypeStruct((PEER_HEADS, PEER_TOPK, t), jnp.float32)],
        scratch_shapes=[pltpu.VMEM((ROUTE_TT, qw), jnp.float32)],
        compiler_params=pltpu.CompilerParams(dimension_semantics=("arbitrary",), vmem_limit_bytes=48 * 1024 * 1024),
        name="peer_route",
    )(hf, w_q.astype(jnp.bfloat16), keys_1.astype(jnp.bfloat16), keys_2.astype(jnp.bfloat16))
    return e_t, g_t


PEER_PAIRS = PEER_HEADS * PEER_TOPK
PEER_TB = 128
PEER_HALF = 64
PEER_TOK_UNROLL = 4
HALF_D = D_MODEL // 2
WORD_SUB, WORD_LANE = HALF_D // 128, 128
ROW_SUB = 2 * WORD_SUB
HI_MASK = -65536
IDX_WAYS = 8


PACK_EB = 512
TAB_PAD = PACK_EB * WORD_SUB


def _pack_kernel(x_ref, o_ref):
    i = pl.program_id(0)

    @pl.when((i == 0) | (i == pl.num_programs(0) - 1))
    def _():
        o_ref[...] = jnp.zeros(o_ref.shape, o_ref.dtype)

    @pl.when((i > 0) & (i < pl.num_programs(0) - 1))
    def _():
        x = x_ref[...].astype(jnp.bfloat16).astype(jnp.float32)
        bits = lax.bitcast_convert_type(x, jnp.int32)
        word = lax.shift_right_logical(bits[:, :HALF_D], 16) | (bits[:, HALF_D:] & HI_MASK)
        for c in range(WORD_SUB):
            o_ref[:, c, :] = word[:, c * WORD_LANE:(c + 1) * WORD_LANE]


def _pack_table(tab):
    e, d = tab.shape
    nblk = e // PACK_EB
    out = pl.pallas_call(
        _pack_kernel,
        grid=(nblk + 2,),
        in_specs=[pl.BlockSpec((PACK_EB, d), lambda i: (jnp.clip(i - 1, 0, nblk - 1), 0))],
        out_specs=pl.BlockSpec((PACK_EB, WORD_SUB, WORD_LANE), lambda i: (i, 0, 0)),
        out_shape=jax.ShapeDtypeStruct(((nblk + 2) * PACK_EB, WORD_SUB, WORD_LANE), jnp.int32),
        name="pack_table",
    )(tab)
    return out.reshape((nblk + 2) * TAB_PAD, WORD_LANE)


def _pair_tile(tab_ref, ia, ib, upper):
    wa = tab_ref[pl.ds(pl.multiple_of(ia, WORD_SUB), ROW_SUB), :]
    wb = tab_ref[pl.ds(pl.multiple_of(ib - WORD_SUB, WORD_SUB), ROW_SUB), :]
    return jnp.where(upper, wa, wb)


_ORDER = (6, 2, 4, 0, 7, 3, 5, 1)


def _fold_pairs(ps, sub):
    b1 = (sub & 2) != 0
    w = []
    for i in range(2):
        x, y = ps[2 * i], ps[2 * i + 1]
        w.append(jnp.where(b1, x + pltpu.roll(x, 2, 0), y + pltpu.roll(y, 6, 0)))
    b0 = (sub & 1) != 0
    x, y = w
    return jnp.where(b0, x + pltpu.roll(x, 1, 0), y + pltpu.roll(y, 7, 0))


def _peer_u_kernel(*refs):
    idx_refs = refs[:IDX_WAYS]
    h_ref, tab_ref, a_ref = refs[IDX_WAYS:]
    sub = lax.broadcasted_iota(jnp.int32, (ROW_SUB, WORD_LANE), 0)
    lane = lax.broadcasted_iota(jnp.int32, (ROW_SUB, WORD_LANE), 1)
    n_groups = PEER_PAIRS // IDX_WAYS
    feed = [int(i) for i in np.argsort(np.array(_ORDER))]
    a_ref[...] = jnp.zeros(a_ref.shape, a_ref.dtype)

    def folded_tiles(t):
        h = h_ref[t]
        h_lo = jnp.concatenate([h[:WORD_SUB], h[:WORD_SUB]], axis=0)
        h_hi = jnp.concatenate([h[WORD_SUB:], h[WORD_SUB:]], axis=0)
        tiles = []
        for g in range(n_groups):
            ps = []
            for i in range(4):
                w = _pair_tile(tab_ref, idx_refs[feed[2 * i]][g * PEER_TB + t], idx_refs[feed[2 * i + 1]][g * PEER_TB + t],
                               sub < WORD_SUB)
                lo = lax.bitcast_convert_type(w << 16, jnp.float32)
                hi = lax.bitcast_convert_type(w & HI_MASK, jnp.float32)
                ps.append(lo * h_lo + hi * h_hi)
            tiles.append(_fold_pairs(ps, sub))
        return tuple(tiles)

    def lane_sums(t, tiles):
        here = lane == t
        for g in range(n_groups):
            a_ref[g] = jnp.where(here, jnp.sum(tiles[g], axis=1, keepdims=True), a_ref[g])

    def tok_body(t, prev):
        lane_sums(t - 1, prev)
        return folded_tiles(t)

    last = lax.fori_loop(1, PEER_TB, tok_body, folded_tiles(0))
    lane_sums(PEER_TB - 1, last)


def _split3_lanes(x, tb):
    mask = lambda v: lax.bitcast_convert_type(lax.bitcast_convert_type(v, jnp.int32) & HI_MASK, jnp.float32)
    hi = mask(x)
    mid = mask(x - hi)
    lo = x - hi - mid
    part = lax.broadcasted_iota(jnp.int32, x.shape, 1) // tb
    return jnp.where(part == 0, hi, jnp.where(part == 1, mid, jnp.where(part == 2, lo, 0.0))).astype(jnp.bfloat16)


def _peer_v_kernel(*refs):
    idx_refs = refs[:IDX_WAYS]
    a_ref, g_ref, tab_ref, sel_ref, o_ref, wrep_ref = refs[IDX_WAYS:]
    per_way = PEER_PAIRS // IDX_WAYS
    a = a_ref[...].reshape(PEER_PAIRS, PEER_TB)
    gates = jnp.concatenate([g_ref[:, m, :] for m in range(per_way)], axis=0)
    w = gates * (0.5 * a * (1.0 + lax.erf(a * (2.0 ** -0.5))))
    w_swapped = pltpu.roll(w, PEER_HALF, 1)
    low_lanes = lax.broadcasted_iota(jnp.int32, w.shape, 1) < PEER_HALF
    upper = lax.broadcasted_iota(jnp.int32, (ROW_SUB, WORD_LANE), 0) < WORD_SUB

    def one_token(tl, first):
        it = first + tl
        row0 = (first // PEER_HALF) * PEER_PAIRS
        col = pl.multiple_of(tl * WORD_LANE, WORD_LANE)
        z = jnp.zeros((ROW_SUB, WORD_LANE), jnp.float32)
        acc_lo, acc_hi = [z, z], [z, z]
        for m in range(per_way):
            for jj in range(IDX_WAYS // 2):
                p = m * IDX_WAYS + 2 * jj
                words = _pair_tile(tab_ref, idx_refs[2 * jj][m * PEER_TB + it], idx_refs[2 * jj + 1][m * PEER_TB + it], upper)
                w2 = jnp.where(upper, wrep_ref[pl.ds(row0 + p, 1), pl.ds(col, WORD_LANE)],
                               wrep_ref[pl.ds(row0 + p + 1, 1), pl.ds(col, WORD_LANE)])
                k = jj % 2
                acc_lo[k] = acc_lo[k] + lax.bitcast_convert_type(words << 16, jnp.float32) * w2
                acc_hi[k] = acc_hi[k] + lax.bitcast_convert_type(words & HI_MASK, jnp.float32) * w2
        lo, hi = acc_lo[0] + acc_lo[1], acc_hi[0] + acc_hi[1]
        o_ref[it] = jnp.concatenate([lo[:WORD_SUB] + lo[WORD_SUB:], hi[:WORD_SUB] + hi[WORD_SUB:]], axis=0)

    xs = [jnp.where(low_lanes, w, w_swapped), jnp.where(low_lanes, w_swapped, w)]
    wcat = jnp.concatenate([_split3_lanes(jnp.concatenate([x, x], axis=1), PEER_HALF) for x in xs], axis=0)
    wrep_ref[...] = jnp.dot(wcat, sel_ref[...], preferred_element_type=jnp.float32)

    for half in range(PEER_TB // PEER_HALF):
        def tok_body(i, carry, first=half * PEER_HALF):
            for j in range(PEER_TOK_UNROLL):
                one_token(PEER_TOK_UNROLL * i + j, first)
            return carry

        lax.fori_loop(0, PEER_HALF // PEER_TOK_UNROLL, tok_body, 0)


_PEER_VMEM_LIMIT = 56 * 1024 * 1024


def _peer_experts(hf, experts, gates, u_tab, v_tab):
    t = hf.shape[0]
    nblk = t // PEER_TB
    per_way = PEER_PAIRS // IDX_WAYS
    smem_spec = pl.BlockSpec((PEER_TB * per_way,), lambda i: (i,), memory_space=pltpu.SMEM)
    tab_spec = pl.BlockSpec(memory_space=pltpu.VMEM)
    row_spec = pl.BlockSpec((PEER_TB, ROW_SUB, WORD_LANE), lambda i: (i, 0, 0))
    a_spec = pl.BlockSpec((per_way, IDX_WAYS, PEER_TB), lambda i: (i, 0, 0))
    params = pltpu.CompilerParams(dimension_semantics=("arbitrary",), vmem_limit_bytes=_PEER_VMEM_LIMIT)
    assert IDX_WAYS == PEER_HEADS and PEER_TB == LANES
    idx_ways = [(experts[j] * WORD_SUB + TAB_PAD).reshape(per_way, nblk, PEER_TB).transpose(1, 0, 2).reshape(-1)
                for j in range(IDX_WAYS)]
    a_t = pl.pallas_call(
        _peer_u_kernel,
        grid=(nblk,),
        in_specs=[smem_spec] * IDX_WAYS + [row_spec, tab_spec],
        out_specs=a_spec,
        out_shape=jax.ShapeDtypeStruct((nblk * per_way, IDX_WAYS, PEER_TB), jnp.float32),
        compiler_params=params, name="peer_u",
    )(*idx_ways, hf, _pack_table(u_tab))
    eye = np.repeat(np.eye(PEER_HALF, dtype=np.float32), WORD_LANE, axis=1)
    sel = jnp.asarray(np.concatenate([eye, eye, eye, np.zeros_like(eye)], axis=0), jnp.bfloat16)
    return pl.pallas_call(
        _peer_v_kernel,
        grid=(nblk,),
        in_specs=[smem_spec] * IDX_WAYS + [a_spec, pl.BlockSpec((IDX_WAYS, per_way, PEER_TB), lambda i: (0, 0, i)),
                                           tab_spec, pl.BlockSpec(sel.shape, lambda i: (0, 0))],
        out_specs=row_spec,
        out_shape=jax.ShapeDtypeStruct((t, ROW_SUB, WORD_LANE), jnp.float32),
        scratch_shapes=[pltpu.VMEM((PEER_TB // PEER_HALF * PEER_PAIRS, PEER_HALF * WORD_LANE), jnp.float32)],
        compiler_params=params, name="peer_v",
    )(*idx_ways, a_t, gates, _pack_table(v_tab), sel)


def _final_kernel(x_ref, p_ref, gate_ref, g_ref, o_ref):
    peer = jnp.concatenate([p_ref[:, c, :] for c in range(ROW_SUB)], axis=-1)
    o_ref[...] = _rms(x_ref[...] + gate_ref[...] * peer) * g_ref[...]


def _final(x1, peer, gate2, g):
    bsz, s, d = x1.shape
    row = pl.BlockSpec((None, ROW_TILE, d), lambda b, i: (b, i, 0))
    return pl.pallas_call(
        _final_kernel,
        grid=(bsz, s // ROW_TILE),
        in_specs=[row, pl.BlockSpec((None, ROW_TILE, ROW_SUB, d // ROW_SUB), lambda b, i: (b, i, 0, 0)),
                  pl.BlockSpec((None, 1, d), lambda b, i: (b, 0, 0)), pl.BlockSpec((1, d), lambda b, i: (0, 0))],
        out_specs=row,
        out_shape=jax.ShapeDtypeStruct((bsz, s, d), jnp.float32), name="final",
    )(x1, peer, gate2, g.reshape(1, d))


def _in_proj_weights(w_in):
    o = np.cumsum([0, GLA_QK_WIDTH, GLA_QK_WIDTH, GLA_WIDTH, GLA_GATE_RANK, GLA_WIDTH, SWA_WIDTH, SWA_KV_WIDTH])
    src = np.full((PROJ_W,), IN_WIDTH, np.int32)
    src[COL_QS + _wide_cols()] = o[5] + np.arange(SWA_WIDTH)
    for dst, start, width in ((COL_QG, o[0], GLA_QK_WIDTH), (COL_KG, o[1], GLA_QK_WIDTH), (COL_VG, o[2], GLA_WIDTH),
                              (COL_ALR, o[3], GLA_GATE_RANK), (COL_RG, o[4], GLA_WIDTH),
                              (COL_KS, o[6], SWA_KV_WIDTH), (COL_VS, o[7], SWA_KV_WIDTH)):
        src[dst:dst + width] = start + np.arange(width)
    w_ext = jnp.concatenate([w_in, jnp.zeros((w_in.shape[0], 1), w_in.dtype)], axis=1)
    return w_ext[:, src].astype(jnp.bfloat16)


def _out_proj_weights(w_out):
    src = np.full((MIX_W,), MIX_WIDTH, np.int32)
    src[:GLA_WIDTH] = np.arange(GLA_WIDTH)
    src[GLA_WIDTH + _wide_cols()] = GLA_WIDTH + np.arange(SWA_WIDTH)
    w_ext = jnp.concatenate([w_out, jnp.zeros((1, w_out.shape[1]), w_out.dtype)], axis=0)
    return w_ext[src].astype(jnp.bfloat16)


def kernel(x, c, w_ada, b_ada, norm1_g, norm2_g, w_in, w_gla_alpha, b_gla_alpha, gla_norm_g,
           swa_sinks, swa_norm_g, w_out, w_peer_q, peer_keys_1, peer_keys_2, peer_u, peer_v, final_g):
    assert w_ada.shape[0] == 1, "single-layer block"
    l = 0
    mod = _ada(c, w_ada[l], b_ada[l])
    shift1, scale1, gate1, shift2, scale2, gate2 = [m[:, None, :] for m in jnp.split(mod, 6, axis=-1)]
    proj = _in_proj(x, norm1_g[l], shift1, scale1, _in_proj_weights(w_in[l]))
    o_gla = _gla(proj, w_gla_alpha[l], b_gla_alpha[l], gla_norm_g[l])
    g_wide = jnp.zeros((1, SWA_QW), jnp.float32).at[0, _wide_cols()].set(swa_norm_g[l])
    o_swa = _swa(proj, swa_sinks[l], g_wide)
    x1, h2 = _out_proj(x, o_gla, o_swa, _out_proj_weights(w_out[l]), gate1, norm2_g[l], shift2, scale2)
    peer = peer_mixer(h2, w_peer_q[l], peer_keys_1[l], peer_keys_2[l], peer_u[l], peer_v[l])
    return _final(x1, peer, gate2, final_g)
```

```python
import jax, jax.numpy as jnp
from jax import lax
import numpy as np
from jax.experimental import pallas as pl
from jax.experimental.pallas import tpu as pltpu

D_MODEL = 1024
BATCH = 4
SEQ = 8192
DEPTH = 1

GLA_HEADS = 4
GLA_DK = 64
GLA_DV = 128
GLA_GATE_RANK = 16
GLA_GATE_TEMP = 16.0
GLA_CHUNK = 64
SWA_HEADS = 8
SWA_KV_HEADS = 2
SWA_HEAD_DIM = 64
SWA_WINDOW = 128
SWA_BLOCK = 128
PEER_HEADS = 8
PEER_N_KEYS = 128
PEER_N_EXPERTS = PEER_N_KEYS * PEER_N_KEYS
PEER_QUERY_DIM = 256
PEER_TOPK = 16
RMS_EPS = 1e-6

GLA_QK_WIDTH = GLA_HEADS * GLA_DK
GLA_WIDTH = GLA_HEADS * GLA_DV
SWA_WIDTH = SWA_HEADS * SWA_HEAD_DIM
SWA_KV_WIDTH = SWA_KV_HEADS * SWA_HEAD_DIM
MIX_WIDTH = GLA_WIDTH + SWA_WIDTH
IN_WIDTH = 2 * GLA_QK_WIDTH + GLA_WIDTH + GLA_GATE_RANK + GLA_WIDTH + SWA_WIDTH + 2 * SWA_KV_WIDTH


LANES = 128
NEG_INF = float("-inf")
NT_DIMS = (((1,), (1,)), ((), ()))
ROW_TILE = 512
SWA_PAIR = 2 * SWA_HEAD_DIM
ALR_W = LANES
COL_QS = 0
COL_QG = COL_QS + SWA_WIDTH
COL_KG = COL_QG + GLA_QK_WIDTH
COL_VG = COL_KG + GLA_QK_WIDTH
COL_RG = COL_VG + GLA_WIDTH
COL_KS = COL_RG + GLA_WIDTH
COL_VS = COL_KS + SWA_KV_WIDTH
COL_ALR = COL_VS + SWA_KV_WIDTH
PROJ_W = COL_ALR + ALR_W


def _rms(x):
    return x * lax.rsqrt(jnp.mean(x * x, axis=-1, keepdims=True) + RMS_EPS)


def _ada_kernel(c_ref, w_ref, b_ref, o_ref):
    c = c_ref[...]
    ca = (c * jax.nn.sigmoid(c)).astype(jnp.bfloat16)
    o_ref[...] = jnp.dot(ca, w_ref[...].astype(jnp.bfloat16), preferred_element_type=jnp.float32) + b_ref[...]


def _ada(c, w, b):
    bsz, d = c.shape
    n = w.shape[1]
    tn = 1024
    return pl.pallas_call(
        _ada_kernel,
        grid=(n // tn,),
        in_specs=[pl.BlockSpec((bsz, d), lambda j: (0, 0)), pl.BlockSpec((d, tn), lambda j: (0, j)),
                  pl.BlockSpec((1, tn), lambda j: (0, j))],
        out_specs=pl.BlockSpec((bsz, tn), lambda j: (0, j)),
        out_shape=jax.ShapeDtypeStruct((bsz, n), jnp.float32), name="ada",
    )(c, w, b.reshape(1, n))


def _in_proj_kernel(x_ref, g_ref, sh_ref, sc_ref, w_ref, o_ref):
    h = _rms(x_ref[...]) * g_ref[...] * (1.0 + sc_ref[...]) + sh_ref[...]
    o_ref[...] = jnp.dot(h.astype(jnp.bfloat16), w_ref[...], preferred_element_type=jnp.float32)


def _in_proj(x, g, shift, scale, w):
    bsz, s, d = x.shape
    n = w.shape[1]
    row = lambda width: pl.BlockSpec((None, ROW_TILE, width), lambda b, i: (b, i, 0))
    per_b = pl.BlockSpec((None, 1, d), lambda b, i: (b, 0, 0))
    return pl.pallas_call(
        _in_proj_kernel,
        grid=(bsz, s // ROW_TILE),
        in_specs=[row(d), pl.BlockSpec((1, d), lambda b, i: (0, 0)), per_b, per_b,
                  pl.BlockSpec((d, n), lambda b, i: (0, 0))],
        out_specs=row(n),
        out_shape=jax.ShapeDtypeStruct((bsz, s, n), jnp.float32),
        compiler_params=pltpu.CompilerParams(dimension_semantics=("arbitrary", "arbitrary"),
                                             vmem_limit_bytes=48 * 1024 * 1024),
        name="in_proj",
    )(x, g.reshape(1, d), shift, scale, w)


def _swa_kernel(sink_ref, q_ref, kp_ref, kc_ref, vp_ref, vc_ref, g_ref, o_ref):
    f32, bf16 = jnp.float32, jnp.bfloat16
    blk = SWA_BLOCK
    k2 = jnp.concatenate([kp_ref[...], kc_ref[...]], axis=0)
    v2 = jnp.concatenate([vp_ref[...], vc_ref[...]], axis=0)
    low = lax.broadcasted_iota(jnp.int32, k2.shape, 1) < SWA_HEAD_DIM
    k2r, v2r = pltpu.roll(k2, SWA_HEAD_DIM, 1), pltpu.roll(v2, SWA_HEAD_DIM, 1)
    kdup = [jnp.where(low, k2, k2r).astype(bf16), jnp.where(low, k2r, k2).astype(bf16)]
    vdup = [jnp.where(low, v2, v2r).astype(bf16), jnp.where(low, v2r, v2).astype(bf16)]
    qi = lax.broadcasted_iota(jnp.int32, (blk, 2 * blk), 0)
    kj = lax.broadcasted_iota(jnp.int32, (blk, 2 * blk), 1)
    dist = qi + blk - kj
    valid = (dist >= 0) & (dist < SWA_WINDOW) & ((kj >= blk) | (pl.program_id(1) > 0))
    distf = dist.astype(f32)
    first = lax.broadcasted_iota(jnp.int32, (blk, SWA_PAIR), 1) < SWA_HEAD_DIM
    grp = SWA_HEADS // SWA_KV_HEADS
    heads = range(SWA_HEADS)
    scores = []
    for h in heads:
        qp = q_ref[:, (h // 2) * SWA_PAIR:(h // 2 + 1) * SWA_PAIR] * (SWA_HEAD_DIM ** -0.5)
        q = jnp.where(first if h % 2 == 0 else ~first, qp, 0.0).astype(bf16)
        scores.append(lax.dot_general(q, kdup[h // grp], NT_DIMS, preferred_element_type=f32))
    probs = []
    for h in heads:
        sc = scores[h] - (2.0 ** (-8.0 * (h + 1) / SWA_HEADS)) * distf
        sc = jnp.where(valid, sc, NEG_INF)
        sink = sink_ref[h]
        m = jnp.maximum(jnp.max(sc, axis=-1, keepdims=True), sink)
        p = jnp.exp(sc - m)
        probs.append((p / (jnp.sum(p, axis=-1, keepdims=True) + jnp.exp(sink - m))).astype(bf16))
    outs = [jnp.dot(probs[h], vdup[h // grp], preferred_element_type=f32) for h in heads]
    pairs = [jnp.where(first, outs[2 * i], outs[2 * i + 1]) for i in range(SWA_HEADS // 2)]
    ssq = sum(jnp.sum(o * o, axis=-1, keepdims=True) for o in pairs)
    inv = lax.rsqrt(ssq * (1.0 / SWA_WIDTH) + RMS_EPS)
    for i, o in enumerate(pairs):
        o_ref[:, i * SWA_PAIR:(i + 1) * SWA_PAIR] = o * inv * g_ref[:, i * SWA_PAIR:(i + 1) * SWA_PAIR]


def _swa(proj, sinks, norm_g):
    bsz, s, _ = proj.shape
    col = lambda width, off, prev: pl.BlockSpec(
        (None, SWA_BLOCK, width), lambda b, n: (b, jnp.maximum(n - 1, 0) if prev else n, off // width))
    return pl.pallas_call(
        _swa_kernel,
        grid=(bsz, s // SWA_BLOCK),
        in_specs=[pl.BlockSpec(memory_space=pltpu.SMEM), col(SWA_WIDTH, COL_QS, False),
                  col(SWA_KV_WIDTH, COL_KS, True), col(SWA_KV_WIDTH, COL_KS, False),
                  col(SWA_KV_WIDTH, COL_VS, True), col(SWA_KV_WIDTH, COL_VS, False),
                  pl.BlockSpec((1, SWA_WIDTH), lambda b, n: (0, 0))],
        out_specs=pl.BlockSpec((None, SWA_BLOCK, SWA_WIDTH), lambda b, n: (b, n, 0)),
        out_shape=jax.ShapeDtypeStruct((bsz, s, SWA_WIDTH), jnp.float32),
        compiler_params=pltpu.CompilerParams(dimension_semantics=("arbitrary", "arbitrary")),
        name="swa",
    )(sinks, proj, proj, proj, proj, proj, norm_g.reshape(1, -1))


def _out_proj_kernel(x_ref, og_ref, os_ref, w_ref, gate_ref, g2_ref, sh_ref, sc_ref, x1_ref, h2_ref):
    mix = jnp.concatenate([og_ref[...], os_ref[...]], axis=-1).astype(jnp.bfloat16)
    x1 = x_ref[...] + gate_ref[...] * jnp.dot(mix, w_ref[...], preferred_element_type=jnp.float32)
    x1_ref[...] = x1
    h2 = _rms(x1) * g2_ref[...] * (1.0 + sc_ref[...]) + sh_ref[...]
    for c in range(ROW_SUB):
        h2_ref[:, c, :] = h2[:, c * LANES:(c + 1) * LANES]


def _out_proj(x, o_gla, o_swa, w, gate1, g2, shift2, scale2):
    bsz, s, d = x.shape
    row = lambda width: pl.BlockSpec((None, ROW_TILE, width), lambda b, i: (b, i, 0))
    per_b = pl.BlockSpec((None, 1, d), lambda b, i: (b, 0, 0))
    shp = jax.ShapeDtypeStruct((bsz, s, d), jnp.float32)
    return pl.pallas_call(
        _out_proj_kernel,
        grid=(bsz, s // ROW_TILE),
        in_specs=[row(d), row(GLA_WIDTH), row(SWA_WIDTH), pl.BlockSpec((MIX_WIDTH, d), lambda b, i: (0, 0)),
                  per_b, pl.BlockSpec((1, d), lambda b, i: (0, 0)), per_b, per_b],
        out_specs=[row(d), pl.BlockSpec((None, ROW_TILE, ROW_SUB, d // ROW_SUB), lambda b, i: (b, i, 0, 0))],
        out_shape=[shp, jax.ShapeDtypeStruct((bsz, s, ROW_SUB, d // ROW_SUB), jnp.float32)],
        compiler_params=pltpu.CompilerParams(dimension_semantics=("arbitrary", "arbitrary"),
                                             vmem_limit_bytes=48 * 1024 * 1024),
        name="out_proj",
    )(x, o_gla, o_swa, w, gate1, g2.reshape(1, d), shift2, scale2)


GLA_SUB = 16
GLA_BLOCK = 128
GLA_PAIR = 2 * GLA_DK
GLA_NB = 1


def _gla_kernel(q_ref, k_ref, v_ref, alr_ref, r_ref, wa_ref, ba_ref, g_ref, tri_ref, o_ref, st_ref, b_ref, q2_ref):
    f32, bf16 = jnp.float32, jnp.bfloat16

    @pl.when(pl.program_id(1) == 0)
    def _():
        st_ref[...] = jnp.zeros(st_ref.shape, f32)

    for bb in range(GLA_NB):
        z = jnp.dot(alr_ref[bb].astype(bf16), wa_ref[...], preferred_element_type=f32) + ba_ref[...]
        log_a = -(jnp.maximum(-z, 0.0) + jnp.log1p(jnp.exp(-jnp.abs(z)))) * (1.0 / GLA_GATE_TEMP)
        b_ref[bb] = jnp.dot(tri_ref[...], log_a, preferred_element_type=f32, precision=lax.Precision.HIGHEST)
        q2_ref[bb] = q_ref[bb] * (GLA_DK ** -0.5)
    lane = lax.broadcasted_iota(jnp.int32, (GLA_SUB, GLA_PAIR), 1)
    sub = lax.broadcasted_iota(jnp.int32, (GLA_SUB, GLA_PAIR), 0)
    first = lane < GLA_DK
    lane_s = lax.broadcasted_iota(jnp.int32, (GLA_DV, GLA_PAIR), 1) < GLA_DK

    chains = [(bb, hp) for bb in range(GLA_NB) for hp in range(GLA_HEADS // 2)]
    states = [st_ref[i] for i in range(len(chains))]
    for c in range(GLA_BLOCK // GLA_SUB):
        rows = pl.ds(c * GLA_SUB, GLA_SUB)
        for ci, (bb, hp) in enumerate(chains):
            cols = pl.ds(hp * GLA_PAIR, GLA_PAIR)
            state = states[ci]
            b = b_ref[bb, rows, cols]
            b_last = b_ref[bb, pl.ds(c * GLA_SUB + GLA_SUB - 1, 1), cols]
            q2, k2 = q2_ref[bb, rows, cols], k_ref[bb, rows, cols]
            qe = q2 * jnp.exp(b)
            kd = (k2 * jnp.exp(b_last - b)).astype(bf16)
            sb = state.astype(bf16)
            cols_a, cols_b = [], []
            for t in range(GLA_SUB):
                bt = b_ref[bb, pl.ds(c * GLA_SUB + t, 1), cols]
                qt = q2_ref[bb, pl.ds(c * GLA_SUB + t, 1), cols]
                p = (qt * k2) * jnp.exp(bt - b)
                keep = sub[:, :1] <= t
                cols_a.append(jnp.where(keep, jnp.sum(jnp.where(first, p, 0.0), axis=1, keepdims=True), 0.0))
                cols_b.append(jnp.where(keep, jnp.sum(jnp.where(first, 0.0, p), axis=1, keepdims=True), 0.0))
            ut = []
            for j, sc_cols in enumerate((cols_a, cols_b)):
                head = 2 * hp + j
                vcols = pl.ds(head * GLA_DV, GLA_DV)
                vh = v_ref[bb, rows, vcols]
                qm = jnp.where(first if j == 0 else ~first, qe, 0.0).astype(bf16)
                o = lax.dot_general(qm, sb, NT_DIMS, preferred_element_type=f32)
                o = o + jnp.concatenate([jnp.sum(cj * vh, axis=0, keepdims=True) for cj in sc_cols], axis=0)
                ms = jnp.mean(o * o, axis=1, keepdims=True)
                r = r_ref[bb, rows, vcols]
                o_ref[bb, rows, vcols] = (o * lax.rsqrt(ms + RMS_EPS) * g_ref[:, vcols]) * (r * jax.nn.sigmoid(r))
                ut.append(jnp.dot(vh.T.astype(bf16), kd, preferred_element_type=f32))
            states[ci] = jnp.exp(b_last) * state + jnp.where(lane_s, ut[0], ut[1])
    for ci in range(len(chains)):
        st_ref[ci] = states[ci]


def _gla_tri():
    r = np.arange(GLA_BLOCK)
    return jnp.asarray(((r[:, None] // GLA_SUB == r[None, :] // GLA_SUB) & (r[None, :] <= r[:, None])).astype(np.float32))


def _gla(proj, w_alpha, b_alpha, norm_g):
    bsz, s, _ = proj.shape
    rank = w_alpha.shape[0]
    wa = jnp.pad(w_alpha, ((0, ALR_W - rank), (0, 0))).astype(jnp.bfloat16)
    col = lambda width, off: pl.BlockSpec((GLA_NB, GLA_BLOCK, width), lambda b, i: (b, i, off // width))
    const = lambda shape: pl.BlockSpec(shape, lambda b, i: (0, 0))
    return pl.pallas_call(
        _gla_kernel,
        grid=(bsz // GLA_NB, s // GLA_BLOCK),
        in_specs=[col(GLA_QK_WIDTH, COL_QG), col(GLA_QK_WIDTH, COL_KG), col(GLA_WIDTH, COL_VG), col(ALR_W, COL_ALR),
                  col(GLA_WIDTH, COL_RG), const((ALR_W, GLA_QK_WIDTH)), const((1, GLA_QK_WIDTH)),
                  const((1, GLA_WIDTH)), const((GLA_BLOCK, GLA_BLOCK))],
        out_specs=pl.BlockSpec((GLA_NB, GLA_BLOCK, GLA_WIDTH), lambda b, i: (b, i, 0)),
        out_shape=jax.ShapeDtypeStruct((bsz, s, GLA_WIDTH), jnp.float32),
        scratch_shapes=[pltpu.VMEM((GLA_NB * GLA_HEADS // 2, GLA_DV, GLA_PAIR), jnp.float32),
                        pltpu.VMEM((GLA_NB, GLA_BLOCK, GLA_QK_WIDTH), jnp.float32),
                        pltpu.VMEM((GLA_NB, GLA_BLOCK, GLA_QK_WIDTH), jnp.float32)],
        compiler_params=pltpu.CompilerParams(dimension_semantics=("arbitrary", "arbitrary")),
        name="gla",
    )(proj, proj, proj, proj, proj, wa, b_alpha.reshape(1, -1), norm_g.reshape(1, -1), _gla_tri())


def peer_mixer(h_tiles, w_q, keys_1, keys_2, u_tab, v_tab):
    bsz, s, rs, rl = h_tiles.shape
    ht = h_tiles.reshape(bsz * s, rs, rl)
    experts, gates = _peer_route(ht, w_q, keys_1, keys_2)
    return _peer_experts(ht, experts, gates, u_tab, v_tab).reshape(bsz, s, rs, rl)


ROUTE_TT = 512


def _top16_rows(s, code, big):
    vals, codes = [], []
    for _ in range(PEER_TOPK):
        m = jnp.max(s, axis=0, keepdims=True)
        c = jnp.min(jnp.where(s == m, code, big), axis=0, keepdims=True)
        s = jnp.where(code == c, NEG_INF, s)
        vals.append(m)
        codes.append(c)
    return vals, codes


def _peer_route_kernel(h_ref, wq_ref, k1_ref, k2_ref, e_ref, g_ref, q_ref):
    half = PEER_QUERY_DIM // 2
    tt = ROUTE_TT
    h = jnp.concatenate([h_ref[:, c, :] for c in range(h_ref.shape[1])], axis=-1)
    q_ref[...] = jnp.dot(h.astype(jnp.bfloat16), wq_ref[...], preferred_element_type=jnp.float32)
    key_id = lax.broadcasted_iota(jnp.int32, (PEER_N_KEYS, 2 * tt), 0).astype(jnp.float32)
    sub8 = lax.broadcasted_iota(jnp.int32, (8, tt), 0)
    nt = (((1,), (1,)), ((), ()))

    def head_body(hd, carry):
        off = pl.multiple_of(hd * PEER_QUERY_DIM, PEER_QUERY_DIM)
        q1 = q_ref[:, pl.ds(off, half)].astype(jnp.bfloat16)
        q2 = q_ref[:, pl.ds(off + half, half)].astype(jnp.bfloat16)
        s1 = lax.dot_general(k1_ref[...], q1, nt, preferred_element_type=jnp.float32)
        s2 = lax.dot_general(k2_ref[...], q2, nt, preferred_element_type=jnp.float32)
        vals, ids = _top16_rows(jnp.concatenate([s1, s2], axis=1), key_id, float(PEER_N_KEYS))
        v = jnp.concatenate(vals, axis=0)
        ix = jnp.concatenate(ids, axis=0).astype(jnp.int32)
        v1, v2, i1, i2 = v[:, :tt], v[:, tt:], ix[:, :tt], ix[:, tt:]
        cand, code = [], []
        for a, b0 in [(0, 0), (0, 8), (1, 0), (2, 0), (3, 0), (4, 0), (5, 0), (6, 0), (7, 0)]:
            cand.append(v1[a:a + 1] + v2[b0:b0 + 8])
            flat = a * PEER_TOPK + b0 + sub8
            code.append((flat << 14) | (i1[a:a + 1] * PEER_N_KEYS + i2[b0:b0 + 8]))
        cand.append(v1[8:16] + v2[0:1])
        code.append((((8 + sub8) * PEER_TOPK) << 14) | (i1[8:16] * PEER_N_KEYS + i2[0:1]))
        sc, cc = _top16_rows(jnp.concatenate(cand, axis=0), jnp.concatenate(code, axis=0).astype(jnp.float32),
                             float(1 << 30))
        sc = jnp.concatenate(sc, axis=0)
        ex = jnp.exp(sc - sc[0:1])
        g_ref[hd] = ex / jnp.sum(ex, axis=0, keepdims=True)
        e_ref[hd] = jnp.concatenate(cc, axis=0).astype(jnp.int32) & (PEER_N_EXPERTS - 1)
        return carry

    lax.fori_loop(0, PEER_HEADS, head_body, 0)


def _peer_route(hf, w_q, keys_1, keys_2):
    t, rs, rl = hf.shape
    d = rs * rl
    qw = PEER_HEADS * PEER_QUERY_DIM
    const = lambda shape: pl.BlockSpec(shape, lambda i: tuple(0 for _ in shape))
    out_spec = pl.BlockSpec((PEER_HEADS, PEER_TOPK, ROUTE_TT), lambda i: (0, 0, i))
    e_t, g_t = pl.pallas_call(
        _peer_route_kernel,
        grid=(t // ROUTE_TT,),
        in_specs=[pl.BlockSpec((ROUTE_TT, rs, rl), lambda i: (i, 0, 0)), const((d, qw)),
                  const(keys_1.shape), const(keys_2.shape)],
        out_specs=[out_spec, out_spec],
        out_shape=[jax.ShapeDtypeStruct((PEER_HEADS, PEER_TOPK, t), jnp.int32),
                   jax.ShapeDtypeStruct((PEER_HEADS, PEER_TOPK, t), jnp.float32)],
        scratch_shapes=[pltpu.VMEM((ROUTE_TT, qw), jnp.float32)],
        compiler_params=pltpu.CompilerParams(dimension_semantics=("arbitrary",), vmem_limit_bytes=48 * 1024 * 1024),
        name="peer_route",
    )(hf, w_q.astype(jnp.bfloat16), keys_1.astype(jnp.bfloat16), keys_2.astype(jnp.bfloat16))
    return e_t, g_t


PEER_PAIRS = PEER_HEADS * PEER_TOPK
PEER_TB = 128
PEER_HALF = 64
PEER_TOK_UNROLL = 4
HALF_D = D_MODEL // 2
WORD_SUB, WORD_LANE = HALF_D // 128, 128
ROW_SUB = 2 * WORD_SUB
HI_MASK = -65536
IDX_WAYS = 8


PACK_EB = 512
TAB_PAD = PACK_EB * WORD_SUB


def _pack_kernel(x_ref, o_ref):
    i = pl.program_id(0)

    @pl.when((i == 0) | (i == pl.num_programs(0) - 1))
    def _():
        o_ref[...] = jnp.zeros(o_ref.shape, o_ref.dtype)

    @pl.when((i > 0) & (i < pl.num_programs(0) - 1))
    def _():
        x = x_ref[...].astype(jnp.bfloat16).astype(jnp.float32)
        bits = lax.bitcast_convert_type(x, jnp.int32)
        word = lax.shift_right_logical(bits[:, :HALF_D], 16) | (bits[:, HALF_D:] & HI_MASK)
        for c in range(WORD_SUB):
            o_ref[:, c, :] = word[:, c * WORD_LANE:(c + 1) * WORD_LANE]


def _pack_table(tab):
    e, d = tab.shape
    nblk = e // PACK_EB
    out = pl.pallas_call(
        _pack_kernel,
        grid=(nblk + 2,),
        in_specs=[pl.BlockSpec((PACK_EB, d), lambda i: (jnp.clip(i - 1, 0, nblk - 1), 0))],
        out_specs=pl.BlockSpec((PACK_EB, WORD_SUB, WORD_LANE), lambda i: (i, 0, 0)),
        out_shape=jax.ShapeDtypeStruct(((nblk + 2) * PACK_EB, WORD_SUB, WORD_LANE), jnp.int32),
        name="pack_table",
    )(tab)
    return out.reshape((nblk + 2) * TAB_PAD, WORD_LANE)


def _pair_tile(tab_ref, ia, ib, upper):
    wa = tab_ref[pl.ds(pl.multiple_of(ia, WORD_SUB), ROW_SUB), :]
    wb = tab_ref[pl.ds(pl.multiple_of(ib - WORD_SUB, WORD_SUB), ROW_SUB), :]
    return jnp.where(upper, wa, wb)


_ORDER = (6, 2, 4, 0, 7, 3, 5, 1)


def _fold_pairs(ps, sub):
    b1 = (sub & 2) != 0
    w = []
    for i in range(2):
        x, y = ps[2 * i], ps[2 * i + 1]
        w.append(jnp.where(b1, x + pltpu.roll(x, 2, 0), y + pltpu.roll(y, 6, 0)))
    b0 = (sub & 1) != 0
    x, y = w
    return jnp.where(b0, x + pltpu.roll(x, 1, 0), y + pltpu.roll(y, 7, 0))


def _peer_u_kernel(*refs):
    idx_refs = refs[:IDX_WAYS]
    h_ref, tab_ref, a_ref = refs[IDX_WAYS:]
    sub = lax.broadcasted_iota(jnp.int32, (ROW_SUB, WORD_LANE), 0)
    lane = lax.broadcasted_iota(jnp.int32, (ROW_SUB, WORD_LANE), 1)
    n_groups = PEER_PAIRS // IDX_WAYS
    feed = [int(i) for i in np.argsort(np.array(_ORDER))]
    a_ref[...] = jnp.zeros(a_ref.shape, a_ref.dtype)

    def folded_tiles(t):
        h = h_ref[t]
        h_lo = jnp.concatenate([h[:WORD_SUB], h[:WORD_SUB]], axis=0)
        h_hi = jnp.concatenate([h[WORD_SUB:], h[WORD_SUB:]], axis=0)
        tiles = []
        for g in range(n_groups):
            ps = []
            for i in range(4):
                w = _pair_tile(tab_ref, idx_refs[feed[2 * i]][g * PEER_TB + t], idx_refs[feed[2 * i + 1]][g * PEER_TB + t],
                               sub < WORD_SUB)
                lo = lax.bitcast_convert_type(w << 16, jnp.float32)
                hi = lax.bitcast_convert_type(w & HI_MASK, jnp.float32)
                ps.append(lo * h_lo + hi * h_hi)
            tiles.append(_fold_pairs(ps, sub))
        return tuple(tiles)

    def lane_sums(t, tiles):
        here = lane == t
        for g in range(n_groups):
            a_ref[g] = jnp.where(here, jnp.sum(tiles[g], axis=1, keepdims=True), a_ref[g])

    def tok_body(t, prev):
        lane_sums(t - 1, prev)
        return folded_tiles(t)

    last = lax.fori_loop(1, PEER_TB, tok_body, folded_tiles(0))
    lane_sums(PEER_TB - 1, last)


def _split3_lanes(x, tb):
    mask = lambda v: lax.bitcast_convert_type(lax.bitcast_convert_type(v, jnp.int32) & HI_MASK, jnp.float32)
    hi = mask(x)
    mid = mask(x - hi)
    lo = x - hi - mid
    part = lax.broadcasted_iota(jnp.int32, x.shape, 1) // tb
    return jnp.where(part == 0, hi, jnp.where(part == 1, mid, jnp.where(part == 2, lo, 0.0))).astype(jnp.bfloat16)


def _peer_v_kernel(*refs):
    idx_refs = refs[:IDX_WAYS]
    a_ref, g_ref, tab_ref, sel_ref, o_ref, wrep_ref = refs[IDX_WAYS:]
    per_way = PEER_PAIRS // IDX_WAYS
    a = a_ref[...].reshape(PEER_PAIRS, PEER_TB)
    gates = jnp.concatenate([g_ref[:, m, :] for m in range(per_way)], axis=0)
    w = gates * (0.5 * a * (1.0 + lax.erf(a * (2.0 ** -0.5))))
    w_swapped = pltpu.roll(w, PEER_HALF, 1)
    low_lanes = lax.broadcasted_iota(jnp.int32, w.shape, 1) < PEER_HALF
    upper = lax.broadcasted_iota(jnp.int32, (ROW_SUB, WORD_LANE), 0) < WORD_SUB

    def one_token(tl, first):
        it = first + tl
        row0 = (first // PEER_HALF) * PEER_PAIRS
        col = pl.multiple_of(tl * WORD_LANE, WORD_LANE)
        z = jnp.zeros((ROW_SUB, WORD_LANE), jnp.float32)
        acc_lo, acc_hi = [z, z], [z, z]
        for m in range(per_way):
            for jj in range(IDX_WAYS // 2):
                p = m * IDX_WAYS + 2 * jj
                words = _pair_tile(tab_ref, idx_refs[2 * jj][m * PEER_TB + it], idx_refs[2 * jj + 1][m * PEER_TB + it], upper)
                w2 = jnp.where(upper, wrep_ref[pl.ds(row0 + p, 1), pl.ds(col, WORD_LANE)],
                               wrep_ref[pl.ds(row0 + p + 1, 1), pl.ds(col, WORD_LANE)])
                k = jj % 2
                acc_lo[k] = acc_lo[k] + lax.bitcast_convert_type(words << 16, jnp.float32) * w2
                acc_hi[k] = acc_hi[k] + lax.bitcast_convert_type(words & HI_MASK, jnp.float32) * w2
        lo, hi = acc_lo[0] + acc_lo[1], acc_hi[0] + acc_hi[1]
        o_ref[it] = jnp.concatenate([lo[:WORD_SUB] + lo[WORD_SUB:], hi[:WORD_SUB] + hi[WORD_SUB:]], axis=0)

    xs = [jnp.where(low_lanes, w, w_swapped), jnp.where(low_lanes, w_swapped, w)]
    wcat = jnp.concatenate([_split3_lanes(jnp.concatenate([x, x], axis=1), PEER_HALF) for x in xs], axis=0)
    wrep_ref[...] = jnp.dot(wcat, sel_ref[...], preferred_element_type=jnp.float32)

    for half in range(PEER_TB // PEER_HALF):
        def tok_body(i, carry, first=half * PEER_HALF):
            for j in range(PEER_TOK_UNROLL):
                one_token(PEER_TOK_UNROLL * i + j, first)
            return carry

        lax.fori_loop(0, PEER_HALF // PEER_TOK_UNROLL, tok_body, 0)


_PEER_VMEM_LIMIT = 56 * 1024 * 1024


def _peer_experts(hf, experts, gates, u_tab, v_tab):
    t = hf.shape[0]
    nblk = t // PEER_TB
    per_way = PEER_PAIRS // IDX_WAYS
    smem_spec = pl.BlockSpec((PEER_TB * per_way,), lambda i: (i,), memory_space=pltpu.SMEM)
    tab_spec = pl.BlockSpec(memory_space=pltpu.VMEM)
    row_spec = pl.BlockSpec((PEER_TB, ROW_SUB, WORD_LANE), lambda i: (i, 0, 0))
    a_spec = pl.BlockSpec((per_way, IDX_WAYS, PEER_TB), lambda i: (i, 0, 0))
    params = pltpu.CompilerParams(dimension_semantics=("arbitrary",), vmem_limit_bytes=_PEER_VMEM_LIMIT)
    assert IDX_WAYS == PEER_HEADS and PEER_TB == LANES
    idx_ways = [(experts[j] * WORD_SUB + TAB_PAD).reshape(per_way, nblk, PEER_TB).transpose(1, 0, 2).reshape(-1)
                for j in range(IDX_WAYS)]
    a_t = pl.pallas_call(
        _peer_u_kernel,
        grid=(nblk,),
        in_specs=[smem_spec] * IDX_WAYS + [row_spec, tab_spec],
        out_specs=a_spec,
        out_shape=jax.ShapeDtypeStruct((nblk * per_way, IDX_WAYS, PEER_TB), jnp.float32),
        compiler_params=params, name="peer_u",
    )(*idx_ways, hf, _pack_table(u_tab))
    eye = np.repeat(np.eye(PEER_HALF, dtype=np.float32), WORD_LANE, axis=1)
    sel = jnp.asarray(np.concatenate([eye, eye, eye, np.zeros_like(eye)], axis=0), jnp.bfloat16)
    return pl.pallas_call(
        _peer_v_kernel,
        grid=(nblk,),
        in_specs=[smem_spec] * IDX_WAYS + [a_spec, pl.BlockSpec((IDX_WAYS, per_way, PEER_TB), lambda i: (0, 0, i)),
                                           tab_spec, pl.BlockSpec(sel.shape, lambda i: (0, 0))],
        out_specs=row_spec,
        out_shape=jax.ShapeDtypeStruct((t, ROW_SUB, WORD_LANE), jnp.float32),
        scratch_shapes=[pltpu.VMEM((PEER_TB // PEER_HALF * PEER_PAIRS, PEER_HALF * WORD_LANE), jnp.float32)],
        compiler_params=params, name="peer_v",
    )(*idx_ways, a_t, gates, _pack_table(v_tab), sel)


def _final_kernel(x_ref, p_ref, gate_ref, g_ref, o_ref):
    peer = jnp.concatenate([p_ref[:, c, :] for c in range(ROW_SUB)], axis=-1)
    o_ref[...] = _rms(x_ref[...] + gate_ref[...] * peer) * g_ref[...]


def _final(x1, peer, gate2, g):
    bsz, s, d = x1.shape
    row = pl.BlockSpec((None, ROW_TILE, d), lambda b, i: (b, i, 0))
    return pl.pallas_call(
        _final_kernel,
        grid=(bsz, s // ROW_TILE),
        in_specs=[row, pl.BlockSpec((None, ROW_TILE, ROW_SUB, d // ROW_SUB), lambda b, i: (b, i, 0, 0)),
                  pl.BlockSpec((None, 1, d), lambda b, i: (b, 0, 0)), pl.BlockSpec((1, d), lambda b, i: (0, 0))],
        out_specs=row,
        out_shape=jax.ShapeDtypeStruct((bsz, s, d), jnp.float32), name="final",
    )(x1, peer, gate2, g.reshape(1, d))


def _in_proj_weights(w_in):
    o = np.cumsum([0, GLA_QK_WIDTH, GLA_QK_WIDTH, GLA_WIDTH, GLA_GATE_RANK, GLA_WIDTH, SWA_WIDTH, SWA_KV_WIDTH])
    src = np.full((PROJ_W,), IN_WIDTH, np.int32)
    for dst, start, width in ((COL_QG, o[0], GLA_QK_WIDTH), (COL_KG, o[1], GLA_QK_WIDTH), (COL_VG, o[2], GLA_WIDTH),
                              (COL_ALR, o[3], GLA_GATE_RANK), (COL_RG, o[4], GLA_WIDTH), (COL_QS, o[5], SWA_WIDTH),
                              (COL_KS, o[6], SWA_KV_WIDTH), (COL_VS, o[7], SWA_KV_WIDTH)):
        src[dst:dst + width] = start + np.arange(width)
    w_ext = jnp.concatenate([w_in, jnp.zeros((w_in.shape[0], 1), w_in.dtype)], axis=1)
    return w_ext[:, src].astype(jnp.bfloat16)


def kernel(x, c, w_ada, b_ada, norm1_g, norm2_g, w_in, w_gla_alpha, b_gla_alpha, gla_norm_g,
           swa_sinks, swa_norm_g, w_out, w_peer_q, peer_keys_1, peer_keys_2, peer_u, peer_v, final_g):
    assert w_ada.shape[0] == 1, "single-layer block"
    l = 0
    mod = _ada(c, w_ada[l], b_ada[l])
    shift1, scale1, gate1, shift2, scale2, gate2 = [m[:, None, :] for m in jnp.split(mod, 6, axis=-1)]
    proj = _in_proj(x, norm1_g[l], shift1, scale1, _in_proj_weights(w_in[l]))
    o_gla = _gla(proj, w_gla_alpha[l], b_gla_alpha[l], gla_norm_g[l])
    o_swa = _swa(proj, swa_sinks[l], swa_norm_g[l])
    x1, h2 = _out_proj(x, o_gla, o_swa, w_out[l].astype(jnp.bfloat16), gate1, norm2_g[l], shift2, scale2)
    peer = peer_mixer(h2, w_peer_q[l], peer_keys_1[l], peer_keys_2[l], peer_u[l], peer_v[l])
    return _final(x1, peer, gate2, final_g)
```

```python
import jax, jax.numpy as jnp
from jax import lax
import numpy as np
from jax.experimental import pallas as pl
from jax.experimental.pallas import tpu as pltpu

D_MODEL = 1024
BATCH = 4
SEQ = 8192
DEPTH = 1

GLA_HEADS = 4
GLA_DK = 64
GLA_DV = 128
GLA_GATE_RANK = 16
GLA_GATE_TEMP = 16.0
GLA_CHUNK = 64
SWA_HEADS = 8
SWA_KV_HEADS = 2
SWA_HEAD_DIM = 64
SWA_WINDOW = 128
SWA_BLOCK = 128
PEER_HEADS = 8
PEER_N_KEYS = 128
PEER_N_EXPERTS = PEER_N_KEYS * PEER_N_KEYS
PEER_QUERY_DIM = 256
PEER_TOPK = 16
RMS_EPS = 1e-6

GLA_QK_WIDTH = GLA_HEADS * GLA_DK
GLA_WIDTH = GLA_HEADS * GLA_DV
SWA_WIDTH = SWA_HEADS * SWA_HEAD_DIM
SWA_KV_WIDTH = SWA_KV_HEADS * SWA_HEAD_DIM
MIX_WIDTH = GLA_WIDTH + SWA_WIDTH
IN_WIDTH = 2 * GLA_QK_WIDTH + GLA_WIDTH + GLA_GATE_RANK + GLA_WIDTH + SWA_WIDTH + 2 * SWA_KV_WIDTH


LANES = 128
NEG_INF = float("-inf")
NT_DIMS = (((1,), (1,)), ((), ()))
ROW_TILE = 512
SWA_PAIR = 2 * SWA_HEAD_DIM
ALR_W = LANES
COL_QS = 0
COL_QG = COL_QS + SWA_WIDTH
COL_KG = COL_QG + GLA_QK_WIDTH
COL_VG = COL_KG + GLA_QK_WIDTH
COL_RG = COL_VG + GLA_WIDTH
COL_KS = COL_RG + GLA_WIDTH
COL_VS = COL_KS + SWA_KV_WIDTH
COL_ALR = COL_VS + SWA_KV_WIDTH
PROJ_W = COL_ALR + ALR_W


def _rms(x):
    return x * lax.rsqrt(jnp.mean(x * x, axis=-1, keepdims=True) + RMS_EPS)


def _ada_kernel(c_ref, w_ref, b_ref, o_ref):
    c = c_ref[...]
    ca = (c * jax.nn.sigmoid(c)).astype(jnp.bfloat16)
    o_ref[...] = jnp.dot(ca, w_ref[...].astype(jnp.bfloat16), preferred_element_type=jnp.float32) + b_ref[...]


def _ada(c, w, b):
    bsz, d = c.shape
    n = w.shape[1]
    tn = 1024
    return pl.pallas_call(
        _ada_kernel,
        grid=(n // tn,),
        in_specs=[pl.BlockSpec((bsz, d), lambda j: (0, 0)), pl.BlockSpec((d, tn), lambda j: (0, j)),
                  pl.BlockSpec((1, tn), lambda j: (0, j))],
        out_specs=pl.BlockSpec((bsz, tn), lambda j: (0, j)),
        out_shape=jax.ShapeDtypeStruct((bsz, n), jnp.float32), name="ada",
    )(c, w, b.reshape(1, n))


def _in_proj_kernel(x_ref, g_ref, sh_ref, sc_ref, w_ref, o_ref):
    h = _rms(x_ref[...]) * g_ref[...] * (1.0 + sc_ref[...]) + sh_ref[...]
    o_ref[...] = jnp.dot(h.astype(jnp.bfloat16), w_ref[...], preferred_element_type=jnp.float32)


def _in_proj(x, g, shift, scale, w):
    bsz, s, d = x.shape
    n = w.shape[1]
    row = lambda width: pl.BlockSpec((None, ROW_TILE, width), lambda b, i: (b, i, 0))
    per_b = pl.BlockSpec((None, 1, d), lambda b, i: (b, 0, 0))
    return pl.pallas_call(
        _in_proj_kernel,
        grid=(bsz, s // ROW_TILE),
        in_specs=[row(d), pl.BlockSpec((1, d), lambda b, i: (0, 0)), per_b, per_b,
                  pl.BlockSpec((d, n), lambda b, i: (0, 0))],
        out_specs=row(n),
        out_shape=jax.ShapeDtypeStruct((bsz, s, n), jnp.float32),
        compiler_params=pltpu.CompilerParams(dimension_semantics=("arbitrary", "arbitrary"),
                                             vmem_limit_bytes=48 * 1024 * 1024),
        name="in_proj",
    )(x, g.reshape(1, d), shift, scale, w)


def _swa_kernel(sink_ref, q_ref, kp_ref, kc_ref, vp_ref, vc_ref, g_ref, o_ref):
    f32, bf16 = jnp.float32, jnp.bfloat16
    blk = SWA_BLOCK
    k2 = jnp.concatenate([kp_ref[...], kc_ref[...]], axis=0)
    v2 = jnp.concatenate([vp_ref[...], vc_ref[...]], axis=0)
    low = lax.broadcasted_iota(jnp.int32, k2.shape, 1) < SWA_HEAD_DIM
    k2r, v2r = pltpu.roll(k2, SWA_HEAD_DIM, 1), pltpu.roll(v2, SWA_HEAD_DIM, 1)
    kdup = [jnp.where(low, k2, k2r).astype(bf16), jnp.where(low, k2r, k2).astype(bf16)]
    vdup = [jnp.where(low, v2, v2r).astype(bf16), jnp.where(low, v2r, v2).astype(bf16)]
    qi = lax.broadcasted_iota(jnp.int32, (blk, 2 * blk), 0)
    kj = lax.broadcasted_iota(jnp.int32, (blk, 2 * blk), 1)
    dist = qi + blk - kj
    valid = (dist >= 0) & (dist < SWA_WINDOW) & ((kj >= blk) | (pl.program_id(1) > 0))
    distf = dist.astype(f32)
    first = lax.broadcasted_iota(jnp.int32, (blk, SWA_PAIR), 1) < SWA_HEAD_DIM
    grp = SWA_HEADS // SWA_KV_HEADS
    heads = range(SWA_HEADS)
    scores = []
    for h in heads:
        qp = q_ref[:, (h // 2) * SWA_PAIR:(h // 2 + 1) * SWA_PAIR] * (SWA_HEAD_DIM ** -0.5)
        q = jnp.where(first if h % 2 == 0 else ~first, qp, 0.0).astype(bf16)
        scores.append(lax.dot_general(q, kdup[h // grp], NT_DIMS, preferred_element_type=f32))
    probs = []
    for h in heads:
        sc = scores[h] - (2.0 ** (-8.0 * (h + 1) / SWA_HEADS)) * distf
        sc = jnp.where(valid, sc, NEG_INF)
        sink = sink_ref[h]
        m = jnp.maximum(jnp.max(sc, axis=-1, keepdims=True), sink)
        p = jnp.exp(sc - m)
        probs.append((p / (jnp.sum(p, axis=-1, keepdims=True) + jnp.exp(sink - m))).astype(bf16))
    outs = [jnp.dot(probs[h], vdup[h // grp], preferred_element_type=f32) for h in heads]
    pairs = [jnp.where(first, outs[2 * i], outs[2 * i + 1]) for i in range(SWA_HEADS // 2)]
    ssq = sum(jnp.sum(o * o, axis=-1, keepdims=True) for o in pairs)
    inv = lax.rsqrt(ssq * (1.0 / SWA_WIDTH) + RMS_EPS)
    for i, o in enumerate(pairs):
        o_ref[:, i * SWA_PAIR:(i + 1) * SWA_PAIR] = o * inv * g_ref[:, i * SWA_PAIR:(i + 1) * SWA_PAIR]


def _swa(proj, sinks, norm_g):
    bsz, s, _ = proj.shape
    col = lambda width, off, prev: pl.BlockSpec(
        (None, SWA_BLOCK, width), lambda b, n: (b, jnp.maximum(n - 1, 0) if prev else n, off // width))
    return pl.pallas_call(
        _swa_kernel,
        grid=(bsz, s // SWA_BLOCK),
        in_specs=[pl.BlockSpec(memory_space=pltpu.SMEM), col(SWA_WIDTH, COL_QS, False),
                  col(SWA_KV_WIDTH, COL_KS, True), col(SWA_KV_WIDTH, COL_KS, False),
                  col(SWA_KV_WIDTH, COL_VS, True), col(SWA_KV_WIDTH, COL_VS, False),
                  pl.BlockSpec((1, SWA_WIDTH), lambda b, n: (0, 0))],
        out_specs=pl.BlockSpec((None, SWA_BLOCK, SWA_WIDTH), lambda b, n: (b, n, 0)),
        out_shape=jax.ShapeDtypeStruct((bsz, s, SWA_WIDTH), jnp.float32),
        compiler_params=pltpu.CompilerParams(dimension_semantics=("arbitrary", "arbitrary")),
        name="swa",
    )(sinks, proj, proj, proj, proj, proj, norm_g.reshape(1, -1))


def _out_proj_kernel(x_ref, og_ref, os_ref, w_ref, gate_ref, g2_ref, sh_ref, sc_ref, x1_ref, h2_ref):
    mix = jnp.concatenate([og_ref[...], os_ref[...]], axis=-1).astype(jnp.bfloat16)
    x1 = x_ref[...] + gate_ref[...] * jnp.dot(mix, w_ref[...], preferred_element_type=jnp.float32)
    x1_ref[...] = x1
    h2 = _rms(x1) * g2_ref[...] * (1.0 + sc_ref[...]) + sh_ref[...]
    for c in range(ROW_SUB):
        h2_ref[:, c, :] = h2[:, c * LANES:(c + 1) * LANES]


def _out_proj(x, o_gla, o_swa, w, gate1, g2, shift2, scale2):
    bsz, s, d = x.shape
    row = lambda width: pl.BlockSpec((None, ROW_TILE, width), lambda b, i: (b, i, 0))
    per_b = pl.BlockSpec((None, 1, d), lambda b, i: (b, 0, 0))
    shp = jax.ShapeDtypeStruct((bsz, s, d), jnp.float32)
    return pl.pallas_call(
        _out_proj_kernel,
        grid=(bsz, s // ROW_TILE),
        in_specs=[row(d), row(GLA_WIDTH), row(SWA_WIDTH), pl.BlockSpec((MIX_WIDTH, d), lambda b, i: (0, 0)),
                  per_b, pl.BlockSpec((1, d), lambda b, i: (0, 0)), per_b, per_b],
        out_specs=[row(d), pl.BlockSpec((None, ROW_TILE, ROW_SUB, d // ROW_SUB), lambda b, i: (b, i, 0, 0))],
        out_shape=[shp, jax.ShapeDtypeStruct((bsz, s, ROW_SUB, d // ROW_SUB), jnp.float32)],
        compiler_params=pltpu.CompilerParams(dimension_semantics=("arbitrary", "arbitrary"),
                                             vmem_limit_bytes=48 * 1024 * 1024),
        name="out_proj",
    )(x, o_gla, o_swa, w, gate1, g2.reshape(1, d), shift2, scale2)


GLA_SUB = 16
GLA_BLOCK = 128
GLA_PAIR = 2 * GLA_DK
GLA_NB = 1


def _gla_kernel(q_ref, k_ref, v_ref, alr_ref, r_ref, wa_ref, ba_ref, g_ref, tri_ref, o_ref, st_ref, b_ref, q2_ref):
    f32, bf16 = jnp.float32, jnp.bfloat16

    @pl.when(pl.program_id(1) == 0)
    def _():
        st_ref[...] = jnp.zeros(st_ref.shape, f32)

    for bb in range(GLA_NB):
        z = jnp.dot(alr_ref[bb].astype(bf16), wa_ref[...], preferred_element_type=f32) + ba_ref[...]
        log_a = -(jnp.maximum(-z, 0.0) + jnp.log1p(jnp.exp(-jnp.abs(z)))) * (1.0 / GLA_GATE_TEMP)
        b_ref[bb] = jnp.dot(tri_ref[...], log_a, preferred_element_type=f32, precision=lax.Precision.HIGHEST)
        q2_ref[bb] = q_ref[bb] * (GLA_DK ** -0.5)
    lane = lax.broadcasted_iota(jnp.int32, (GLA_SUB, GLA_PAIR), 1)
    sub = lax.broadcasted_iota(jnp.int32, (GLA_SUB, GLA_PAIR), 0)
    first = lane < GLA_DK
    lane_s = lax.broadcasted_iota(jnp.int32, (GLA_DV, GLA_PAIR), 1) < GLA_DK

    chains = [(bb, hp) for bb in range(GLA_NB) for hp in range(GLA_HEADS // 2)]
    states = [st_ref[i] for i in range(len(chains))]
    for c in range(GLA_BLOCK // GLA_SUB):
        rows = pl.ds(c * GLA_SUB, GLA_SUB)
        for ci, (bb, hp) in enumerate(chains):
            cols = pl.ds(hp * GLA_PAIR, GLA_PAIR)
            state = states[ci]
            b = b_ref[bb, rows, cols]
            b_last = b_ref[bb, pl.ds(c * GLA_SUB + GLA_SUB - 1, 1), cols]
            q2, k2 = q2_ref[bb, rows, cols], k_ref[bb, rows, cols]
            qe = q2 * jnp.exp(b)
            kd = (k2 * jnp.exp(b_last - b)).astype(bf16)
            sb = state.astype(bf16)
            cols_a, cols_b = [], []
            for t in range(GLA_SUB):
                bt = b_ref[bb, pl.ds(c * GLA_SUB + t, 1), cols]
                qt = q2_ref[bb, pl.ds(c * GLA_SUB + t, 1), cols]
                p = (qt * k2) * jnp.exp(bt - b)
                keep = sub[:, :1] <= t
                cols_a.append(jnp.where(keep, jnp.sum(jnp.where(first, p, 0.0), axis=1, keepdims=True), 0.0))
                cols_b.append(jnp.where(keep, jnp.sum(jnp.where(first, 0.0, p), axis=1, keepdims=True), 0.0))
            ut = []
            for j, sc_cols in enumerate((cols_a, cols_b)):
                head = 2 * hp + j
                vcols = pl.ds(head * GLA_DV, GLA_DV)
                vh = v_ref[bb, rows, vcols]
                qm = jnp.where(first if j == 0 else ~first, qe, 0.0).astype(bf16)
                o = lax.dot_general(qm, sb, NT_DIMS, preferred_element_type=f32)
                o = o + jnp.concatenate([jnp.sum(cj * vh, axis=0, keepdims=True) for cj in sc_cols], axis=0)
                ms = jnp.mean(o * o, axis=1, keepdims=True)
                r = r_ref[bb, rows, vcols]
                o_ref[bb, rows, vcols] = (o * lax.rsqrt(ms + RMS_EPS) * g_ref[:, vcols]) * (r * jax.nn.sigmoid(r))
                ut.append(jnp.dot(vh.T.astype(bf16), kd, preferred_element_type=f32))
            states[ci] = jnp.exp(b_last) * state + jnp.where(lane_s, ut[0], ut[1])
    for ci in range(len(chains)):
        st_ref[ci] = states[ci]


def _gla_tri():
    r = np.arange(GLA_BLOCK)
    return jnp.asarray(((r[:, None] // GLA_SUB == r[None, :] // GLA_SUB) & (r[None, :] <= r[:, None])).astype(np.float32))


def _gla(proj, w_alpha, b_alpha, norm_g):
    bsz, s, _ = proj.shape
    rank = w_alpha.shape[0]
    wa = jnp.pad(w_alpha, ((0, ALR_W - rank), (0, 0))).astype(jnp.bfloat16)
    col = lambda width, off: pl.BlockSpec((GLA_NB, GLA_BLOCK, width), lambda b, i: (b, i, off // width))
    const = lambda shape: pl.BlockSpec(shape, lambda b, i: (0, 0))
    return pl.pallas_call(
        _gla_kernel,
        grid=(bsz // GLA_NB, s // GLA_BLOCK),
        in_specs=[col(GLA_QK_WIDTH, COL_QG), col(GLA_QK_WIDTH, COL_KG), col(GLA_WIDTH, COL_VG), col(ALR_W, COL_ALR),
                  col(GLA_WIDTH, COL_RG), const((ALR_W, GLA_QK_WIDTH)), const((1, GLA_QK_WIDTH)),
                  const((1, GLA_WIDTH)), const((GLA_BLOCK, GLA_BLOCK))],
        out_specs=pl.BlockSpec((GLA_NB, GLA_BLOCK, GLA_WIDTH), lambda b, i: (b, i, 0)),
        out_shape=jax.ShapeDtypeStruct((bsz, s, GLA_WIDTH), jnp.float32),
        scratch_shapes=[pltpu.VMEM((GLA_NB * GLA_HEADS // 2, GLA_DV, GLA_PAIR), jnp.float32),
                        pltpu.VMEM((GLA_NB, GLA_BLOCK, GLA_QK_WIDTH), jnp.float32),
                        pltpu.VMEM((GLA_NB, GLA_BLOCK, GLA_QK_WIDTH), jnp.float32)],
        compiler_params=pltpu.CompilerParams(dimension_semantics=("arbitrary", "arbitrary")),
        name="gla",
    )(proj, proj, proj, proj, proj, wa, b_alpha.reshape(1, -1), norm_g.reshape(1, -1), _gla_tri())


def peer_mixer(h_tiles, w_q, keys_1, keys_2, u_tab, v_tab, x1, gate2, final_g):
    bsz, s, rs, rl = h_tiles.shape
    ht = h_tiles.reshape(bsz * s, rs, rl)
    experts, gates = _peer_route(ht, w_q, keys_1, keys_2)
    return _peer_experts(ht, experts, gates, u_tab, v_tab, x1, gate2, final_g)


ROUTE_TT = 512


def _top16_rows(s, code, big):
    vals, codes = [], []
    for _ in range(PEER_TOPK):
        m = jnp.max(s, axis=0, keepdims=True)
        c = jnp.min(jnp.where(s == m, code, big), axis=0, keepdims=True)
        s = jnp.where(code == c, NEG_INF, s)
        vals.append(m)
        codes.append(c)
    return vals, codes


def _peer_route_kernel(h_ref, wq_ref, k1_ref, k2_ref, e_ref, g_ref, q_ref):
    half = PEER_QUERY_DIM // 2
    tt = ROUTE_TT
    h = jnp.concatenate([h_ref[:, c, :] for c in range(h_ref.shape[1])], axis=-1)
    q_ref[...] = jnp.dot(h.astype(jnp.bfloat16), wq_ref[...], preferred_element_type=jnp.float32)
    key_id = lax.broadcasted_iota(jnp.int32, (PEER_N_KEYS, 2 * tt), 0).astype(jnp.float32)
    sub8 = lax.broadcasted_iota(jnp.int32, (8, tt), 0)
    nt = (((1,), (1,)), ((), ()))

    def head_body(hd, carry):
        off = pl.multiple_of(hd * PEER_QUERY_DIM, PEER_QUERY_DIM)
        q1 = q_ref[:, pl.ds(off, half)].astype(jnp.bfloat16)
        q2 = q_ref[:, pl.ds(off + half, half)].astype(jnp.bfloat16)
        s1 = lax.dot_general(k1_ref[...], q1, nt, preferred_element_type=jnp.float32)
        s2 = lax.dot_general(k2_ref[...], q2, nt, preferred_element_type=jnp.float32)
        vals, ids = _top16_rows(jnp.concatenate([s1, s2], axis=1), key_id, float(PEER_N_KEYS))
        v = jnp.concatenate(vals, axis=0)
        ix = jnp.concatenate(ids, axis=0).astype(jnp.int32)
        v1, v2, i1, i2 = v[:, :tt], v[:, tt:], ix[:, :tt], ix[:, tt:]
        cand, code = [], []
        for a, b0 in [(0, 0), (0, 8), (1, 0), (2, 0), (3, 0), (4, 0), (5, 0), (6, 0), (7, 0)]:
            cand.append(v1[a:a + 1] + v2[b0:b0 + 8])
            flat = a * PEER_TOPK + b0 + sub8
            code.append((flat << 14) | (i1[a:a + 1] * PEER_N_KEYS + i2[b0:b0 + 8]))
        cand.append(v1[8:16] + v2[0:1])
        code.append((((8 + sub8) * PEER_TOPK) << 14) | (i1[8:16] * PEER_N_KEYS + i2[0:1]))
        sc, cc = _top16_rows(jnp.concatenate(cand, axis=0), jnp.concatenate(code, axis=0).astype(jnp.float32),
                             float(1 << 30))
        sc = jnp.concatenate(sc, axis=0)
        ex = jnp.exp(sc - sc[0:1])
        g_ref[hd] = ex / jnp.sum(ex, axis=0, keepdims=True)
        e_ref[hd] = jnp.concatenate(cc, axis=0).astype(jnp.int32) & (PEER_N_EXPERTS - 1)
        return carry

    lax.fori_loop(0, PEER_HEADS, head_body, 0)


def _peer_route(hf, w_q, keys_1, keys_2):
    t, rs, rl = hf.shape
    d = rs * rl
    qw = PEER_HEADS * PEER_QUERY_DIM
    const = lambda shape: pl.BlockSpec(shape, lambda i: tuple(0 for _ in shape))
    out_spec = pl.BlockSpec((PEER_HEADS, PEER_TOPK, ROUTE_TT), lambda i: (0, 0, i))
    e_t, g_t = pl.pallas_call(
        _peer_route_kernel,
        grid=(t // ROUTE_TT,),
        in_specs=[pl.BlockSpec((ROUTE_TT, rs, rl), lambda i: (i, 0, 0)), const((d, qw)),
                  const(keys_1.shape), const(keys_2.shape)],
        out_specs=[out_spec, out_spec],
        out_shape=[jax.ShapeDtypeStruct((PEER_HEADS, PEER_TOPK, t), jnp.int32),
                   jax.ShapeDtypeStruct((PEER_HEADS, PEER_TOPK, t), jnp.float32)],
        scratch_shapes=[pltpu.VMEM((ROUTE_TT, qw), jnp.float32)],
        compiler_params=pltpu.CompilerParams(dimension_semantics=("arbitrary",), vmem_limit_bytes=48 * 1024 * 1024),
        name="peer_route",
    )(hf, w_q.astype(jnp.bfloat16), keys_1.astype(jnp.bfloat16), keys_2.astype(jnp.bfloat16))
    return e_t, g_t


PEER_PAIRS = PEER_HEADS * PEER_TOPK
PEER_TB = 128
PEER_HALF = 64
PEER_TOK_UNROLL = 4
HALF_D = D_MODEL // 2
WORD_SUB, WORD_LANE = HALF_D // 128, 128
ROW_SUB = 2 * WORD_SUB
HI_MASK = -65536
IDX_WAYS = 8


PACK_EB = 512
TAB_PAD = PACK_EB * WORD_SUB


def _pack_kernel(x_ref, o_ref):
    i = pl.program_id(0)

    @pl.when((i == 0) | (i == pl.num_programs(0) - 1))
    def _():
        o_ref[...] = jnp.zeros(o_ref.shape, o_ref.dtype)

    @pl.when((i > 0) & (i < pl.num_programs(0) - 1))
    def _():
        x = x_ref[...].astype(jnp.bfloat16).astype(jnp.float32)
        bits = lax.bitcast_convert_type(x, jnp.int32)
        word = lax.shift_right_logical(bits[:, :HALF_D], 16) | (bits[:, HALF_D:] & HI_MASK)
        for c in range(WORD_SUB):
            o_ref[:, c, :] = word[:, c * WORD_LANE:(c + 1) * WORD_LANE]


def _pack_table(tab):
    e, d = tab.shape
    nblk = e // PACK_EB
    out = pl.pallas_call(
        _pack_kernel,
        grid=(nblk + 2,),
        in_specs=[pl.BlockSpec((PACK_EB, d), lambda i: (jnp.clip(i - 1, 0, nblk - 1), 0))],
        out_specs=pl.BlockSpec((PACK_EB, WORD_SUB, WORD_LANE), lambda i: (i, 0, 0)),
        out_shape=jax.ShapeDtypeStruct(((nblk + 2) * PACK_EB, WORD_SUB, WORD_LANE), jnp.int32),
        name="pack_table",
    )(tab)
    return out.reshape((nblk + 2) * TAB_PAD, WORD_LANE)


def _pair_tile(tab_ref, ia, ib, upper):
    wa = tab_ref[pl.ds(pl.multiple_of(ia, WORD_SUB), ROW_SUB), :]
    wb = tab_ref[pl.ds(pl.multiple_of(ib - WORD_SUB, WORD_SUB), ROW_SUB), :]
    return jnp.where(upper, wa, wb)


_ORDER = (6, 2, 4, 0, 7, 3, 5, 1)


def _fold_pairs(ps, sub):
    b1 = (sub & 2) != 0
    w = []
    for i in range(2):
        x, y = ps[2 * i], ps[2 * i + 1]
        w.append(jnp.where(b1, x + pltpu.roll(x, 2, 0), y + pltpu.roll(y, 6, 0)))
    b0 = (sub & 1) != 0
    x, y = w
    return jnp.where(b0, x + pltpu.roll(x, 1, 0), y + pltpu.roll(y, 7, 0))


def _peer_u_kernel(*refs):
    idx_refs = refs[:IDX_WAYS]
    h_ref, tab_ref, a_ref = refs[IDX_WAYS:]
    sub = lax.broadcasted_iota(jnp.int32, (ROW_SUB, WORD_LANE), 0)
    lane = lax.broadcasted_iota(jnp.int32, (ROW_SUB, WORD_LANE), 1)
    n_groups = PEER_PAIRS // IDX_WAYS
    feed = [int(i) for i in np.argsort(np.array(_ORDER))]
    a_ref[...] = jnp.zeros(a_ref.shape, a_ref.dtype)

    def folded_tiles(t):
        h = h_ref[t]
        h_lo = jnp.concatenate([h[:WORD_SUB], h[:WORD_SUB]], axis=0)
        h_hi = jnp.concatenate([h[WORD_SUB:], h[WORD_SUB:]], axis=0)
        tiles = []
        for g in range(n_groups):
            ps = []
            for i in range(4):
                w = _pair_tile(tab_ref, idx_refs[feed[2 * i]][g * PEER_TB + t], idx_refs[feed[2 * i + 1]][g * PEER_TB + t],
                               sub < WORD_SUB)
                lo = lax.bitcast_convert_type(w << 16, jnp.float32)
                hi = lax.bitcast_convert_type(w & HI_MASK, jnp.float32)
                ps.append(lo * h_lo + hi * h_hi)
            tiles.append(_fold_pairs(ps, sub))
        return tuple(tiles)

    def lane_sums(t, tiles):
        here = lane == t
        for g in range(n_groups):
            a_ref[g] = jnp.where(here, jnp.sum(tiles[g], axis=1, keepdims=True), a_ref[g])

    def tok_body(t, prev):
        lane_sums(t - 1, prev)
        return folded_tiles(t)

    last = lax.fori_loop(1, PEER_TB, tok_body, folded_tiles(0))
    lane_sums(PEER_TB - 1, last)


def _split3_lanes(x, tb):
    mask = lambda v: lax.bitcast_convert_type(lax.bitcast_convert_type(v, jnp.int32) & HI_MASK, jnp.float32)
    hi = mask(x)
    mid = mask(x - hi)
    lo = x - hi - mid
    part = lax.broadcasted_iota(jnp.int32, x.shape, 1) // tb
    return jnp.where(part == 0, hi, jnp.where(part == 1, mid, jnp.where(part == 2, lo, 0.0))).astype(jnp.bfloat16)


def _peer_v_kernel(*refs):
    idx_refs = refs[:IDX_WAYS]
    a_ref, g_ref, tab_ref, sel_ref, x1_ref, gate2_ref, fg_ref, o_ref, wrep_ref, peer_ref = refs[IDX_WAYS:]
    per_way = PEER_PAIRS // IDX_WAYS
    a = a_ref[...].reshape(PEER_PAIRS, PEER_TB)
    gates = jnp.concatenate([g_ref[:, m, :] for m in range(per_way)], axis=0)
    w = gates * (0.5 * a * (1.0 + lax.erf(a * (2.0 ** -0.5))))
    w_swapped = pltpu.roll(w, PEER_HALF, 1)
    low_lanes = lax.broadcasted_iota(jnp.int32, w.shape, 1) < PEER_HALF
    upper = lax.broadcasted_iota(jnp.int32, (ROW_SUB, WORD_LANE), 0) < WORD_SUB

    def one_token(tl, first):
        it = first + tl
        row0 = (first // PEER_HALF) * PEER_PAIRS
        col = pl.multiple_of(tl * WORD_LANE, WORD_LANE)
        z = jnp.zeros((ROW_SUB, WORD_LANE), jnp.float32)
        acc_lo, acc_hi = [z, z], [z, z]
        for m in range(per_way):
            for jj in range(IDX_WAYS // 2):
                p = m * IDX_WAYS + 2 * jj
                words = _pair_tile(tab_ref, idx_refs[2 * jj][m * PEER_TB + it], idx_refs[2 * jj + 1][m * PEER_TB + it], upper)
                w2 = jnp.where(upper, wrep_ref[pl.ds(row0 + p, 1), pl.ds(col, WORD_LANE)],
                               wrep_ref[pl.ds(row0 + p + 1, 1), pl.ds(col, WORD_LANE)])
                k = jj % 2
                acc_lo[k] = acc_lo[k] + lax.bitcast_convert_type(words << 16, jnp.float32) * w2
                acc_hi[k] = acc_hi[k] + lax.bitcast_convert_type(words & HI_MASK, jnp.float32) * w2
        lo, hi = acc_lo[0] + acc_lo[1], acc_hi[0] + acc_hi[1]
        peer_ref[it] = jnp.concatenate([lo[:WORD_SUB] + lo[WORD_SUB:], hi[:WORD_SUB] + hi[WORD_SUB:]], axis=0)

    xs = [jnp.where(low_lanes, w, w_swapped), jnp.where(low_lanes, w_swapped, w)]
    wcat = jnp.concatenate([_split3_lanes(jnp.concatenate([x, x], axis=1), PEER_HALF) for x in xs], axis=0)
    wrep_ref[...] = jnp.dot(wcat, sel_ref[...], preferred_element_type=jnp.float32)

    for half in range(PEER_TB // PEER_HALF):
        def tok_body(i, carry, first=half * PEER_HALF):
            for j in range(PEER_TOK_UNROLL):
                one_token(PEER_TOK_UNROLL * i + j, first)
            return carry

        lax.fori_loop(0, PEER_HALF // PEER_TOK_UNROLL, tok_body, 0)

    peer = jnp.concatenate([peer_ref[:, c, :] for c in range(ROW_SUB)], axis=-1)
    o_ref[...] = _rms(x1_ref[...] + gate2_ref[...] * peer) * fg_ref[...]


_PEER_VMEM_LIMIT = 56 * 1024 * 1024


def _peer_experts(hf, experts, gates, u_tab, v_tab, x1, gate2, final_g):
    t = hf.shape[0]
    bsz, s, d = x1.shape
    assert s % PEER_TB == 0
    nblk = t // PEER_TB
    per_way = PEER_PAIRS // IDX_WAYS
    smem_spec = pl.BlockSpec((PEER_TB * per_way,), lambda i: (i,), memory_space=pltpu.SMEM)
    tab_spec = pl.BlockSpec(memory_space=pltpu.VMEM)
    row_spec = pl.BlockSpec((PEER_TB, ROW_SUB, WORD_LANE), lambda i: (i, 0, 0))
    a_spec = pl.BlockSpec((per_way, IDX_WAYS, PEER_TB), lambda i: (i, 0, 0))
    params = pltpu.CompilerParams(dimension_semantics=("arbitrary",), vmem_limit_bytes=_PEER_VMEM_LIMIT)
    assert IDX_WAYS == PEER_HEADS and PEER_TB == LANES
    idx_ways = [(experts[j] * WORD_SUB + TAB_PAD).reshape(per_way, nblk, PEER_TB).transpose(1, 0, 2).reshape(-1)
                for j in range(IDX_WAYS)]
    a_t = pl.pallas_call(
        _peer_u_kernel,
        grid=(nblk,),
        in_specs=[smem_spec] * IDX_WAYS + [row_spec, tab_spec],
        out_specs=a_spec,
        out_shape=jax.ShapeDtypeStruct((nblk * per_way, IDX_WAYS, PEER_TB), jnp.float32),
        compiler_params=params, name="peer_u",
    )(*idx_ways, hf, _pack_table(u_tab))
    eye = np.repeat(np.eye(PEER_HALF, dtype=np.float32), WORD_LANE, axis=1)
    sel = jnp.asarray(np.concatenate([eye, eye, eye, np.zeros_like(eye)], axis=0), jnp.bfloat16)
    return pl.pallas_call(
        _peer_v_kernel,
        grid=(nblk,),
        in_specs=[smem_spec] * IDX_WAYS + [a_spec, pl.BlockSpec((IDX_WAYS, per_way, PEER_TB), lambda i: (0, 0, i)),
                                           tab_spec, pl.BlockSpec(sel.shape, lambda i: (0, 0)),
                                           pl.BlockSpec((PEER_TB, d), lambda i: (i, 0)),
                                           pl.BlockSpec((None, 1, d), lambda i: ((i * PEER_TB) // s, 0, 0)),
                                           pl.BlockSpec((1, d), lambda i: (0, 0))],
        out_specs=pl.BlockSpec((PEER_TB, d), lambda i: (i, 0)),
        out_shape=jax.ShapeDtypeStruct((t, d), jnp.float32),
        scratch_shapes=[pltpu.VMEM((PEER_TB // PEER_HALF * PEER_PAIRS, PEER_HALF * WORD_LANE), jnp.float32),
                        pltpu.VMEM((PEER_TB, ROW_SUB, WORD_LANE), jnp.float32)],
        compiler_params=params, name="peer_v",
    )(*idx_ways, a_t, gates, _pack_table(v_tab), sel, x1.reshape(t, d), gate2, final_g.reshape(1, d)).reshape(bsz, s, d)


def _in_proj_weights(w_in):
    o = np.cumsum([0, GLA_QK_WIDTH, GLA_QK_WIDTH, GLA_WIDTH, GLA_GATE_RANK, GLA_WIDTH, SWA_WIDTH, SWA_KV_WIDTH])
    src = np.full((PROJ_W,), IN_WIDTH, np.int32)
    for dst, start, width in ((COL_QG, o[0], GLA_QK_WIDTH), (COL_KG, o[1], GLA_QK_WIDTH), (COL_VG, o[2], GLA_WIDTH),
                              (COL_ALR, o[3], GLA_GATE_RANK), (COL_RG, o[4], GLA_WIDTH), (COL_QS, o[5], SWA_WIDTH),
                              (COL_KS, o[6], SWA_KV_WIDTH), (COL_VS, o[7], SWA_KV_WIDTH)):
        src[dst:dst + width] = start + np.arange(width)
    w_ext = jnp.concatenate([w_in, jnp.zeros((w_in.shape[0], 1), w_in.dtype)], axis=1)
    return w_ext[:, src].astype(jnp.bfloat16)


def kernel(x, c, w_ada, b_ada, norm1_g, norm2_g, w_in, w_gla_alpha, b_gla_alpha, gla_norm_g,
           swa_sinks, swa_norm_g, w_out, w_peer_q, peer_keys_1, peer_keys_2, peer_u, peer_v, final_g):
    assert w_ada.shape[0] == 1, "single-layer block"
    l = 0
    mod = _ada(c, w_ada[l], b_ada[l])
    shift1, scale1, gate1, shift2, scale2, gate2 = [m[:, None, :] for m in jnp.split(mod, 6, axis=-1)]
    proj = _in_proj(x, norm1_g[l], shift1, scale1, _in_proj_weights(w_in[l]))
    o_gla = _gla(proj, w_gla_alpha[l], b_gla_alpha[l], gla_norm_g[l])
    o_swa = _swa(proj, swa_sinks[l], swa_norm_g[l])
    x1, h2 = _out_proj(x, o_gla, o_swa, w_out[l].astype(jnp.bfloat16), gate1, norm2_g[l], shift2, scale2)
    return peer_mixer(h2, w_peer_q[l], peer_keys_1[l], peer_keys_2[l], peer_u[l], peer_v[l], x1, gate2, final_g)
```

```python
import jax, jax.numpy as jnp
from jax import lax
import numpy as np
from jax.experimental import pallas as pl
from jax.experimental.pallas import tpu as pltpu

D_MODEL = 1024
BATCH = 4
SEQ = 8192
DEPTH = 1

GLA_HEADS = 4
GLA_DK = 64
GLA_DV = 128
GLA_GATE_RANK = 16
GLA_GATE_TEMP = 16.0
GLA_CHUNK = 64
SWA_HEADS = 8
SWA_KV_HEADS = 2
SWA_HEAD_DIM = 64
SWA_WINDOW = 128
SWA_BLOCK = 128
PEER_HEADS = 8
PEER_N_KEYS = 128
PEER_N_EXPERTS = PEER_N_KEYS * PEER_N_KEYS
PEER_QUERY_DIM = 256
PEER_TOPK = 16
RMS_EPS = 1e-6

GLA_QK_WIDTH = GLA_HEADS * GLA_DK
GLA_WIDTH = GLA_HEADS * GLA_DV
SWA_WIDTH = SWA_HEADS * SWA_HEAD_DIM
SWA_KV_WIDTH = SWA_KV_HEADS * SWA_HEAD_DIM
MIX_WIDTH = GLA_WIDTH + SWA_WIDTH
IN_WIDTH = 2 * GLA_QK_WIDTH + GLA_WIDTH + GLA_GATE_RANK + GLA_WIDTH + SWA_WIDTH + 2 * SWA_KV_WIDTH


LANES = 128
NEG_INF = float("-inf")
NT_DIMS = (((1,), (1,)), ((), ()))
ROW_TILE = 512
SWA_PAIR = 2 * SWA_HEAD_DIM
ALR_W = LANES
COL_QS = 0
COL_QG = COL_QS + SWA_WIDTH
COL_KG = COL_QG + GLA_QK_WIDTH
COL_VG = COL_KG + GLA_QK_WIDTH
COL_RG = COL_VG + GLA_WIDTH
COL_KS = COL_RG + GLA_WIDTH
COL_VS = COL_KS + SWA_KV_WIDTH
COL_ALR = COL_VS + SWA_KV_WIDTH
PROJ_W = COL_ALR + ALR_W


def _rms(x):
    return x * lax.rsqrt(jnp.mean(x * x, axis=-1, keepdims=True) + RMS_EPS)


def _ada_kernel(c_ref, w_ref, b_ref, o_ref):
    c = c_ref[...]
    ca = (c * jax.nn.sigmoid(c)).astype(jnp.bfloat16)
    o_ref[...] = jnp.dot(ca, w_ref[...].astype(jnp.bfloat16), preferred_element_type=jnp.float32) + b_ref[...]


def _ada(c, w, b):
    bsz, d = c.shape
    n = w.shape[1]
    tn = 1024
    return pl.pallas_call(
        _ada_kernel,
        grid=(n // tn,),
        in_specs=[pl.BlockSpec((bsz, d), lambda j: (0, 0)), pl.BlockSpec((d, tn), lambda j: (0, j)),
                  pl.BlockSpec((1, tn), lambda j: (0, j))],
        out_specs=pl.BlockSpec((bsz, tn), lambda j: (0, j)),
        out_shape=jax.ShapeDtypeStruct((bsz, n), jnp.float32), name="ada",
    )(c, w, b.reshape(1, n))


def _in_proj_kernel(x_ref, g_ref, sh_ref, sc_ref, w_ref, o_ref):
    h = _rms(x_ref[...]) * g_ref[...] * (1.0 + sc_ref[...]) + sh_ref[...]
    o_ref[...] = jnp.dot(h.astype(jnp.bfloat16), w_ref[...], preferred_element_type=jnp.float32)


def _in_proj(x, g, shift, scale, w):
    bsz, s, d = x.shape
    n = w.shape[1]
    row = lambda width: pl.BlockSpec((None, ROW_TILE, width), lambda b, i: (b, i, 0))
    per_b = pl.BlockSpec((None, 1, d), lambda b, i: (b, 0, 0))
    return pl.pallas_call(
        _in_proj_kernel,
        grid=(bsz, s // ROW_TILE),
        in_specs=[row(d), pl.BlockSpec((1, d), lambda b, i: (0, 0)), per_b, per_b,
                  pl.BlockSpec((d, n), lambda b, i: (0, 0))],
        out_specs=row(n),
        out_shape=jax.ShapeDtypeStruct((bsz, s, n), jnp.float32),
        compiler_params=pltpu.CompilerParams(dimension_semantics=("arbitrary", "arbitrary"),
                                             vmem_limit_bytes=48 * 1024 * 1024),
        name="in_proj",
    )(x, g.reshape(1, d), shift, scale, w)


def _swa_kernel(sink_ref, q_ref, kp_ref, kc_ref, vp_ref, vc_ref, g_ref, o_ref):
    f32, bf16 = jnp.float32, jnp.bfloat16
    blk = SWA_BLOCK
    k2 = jnp.concatenate([kp_ref[...], kc_ref[...]], axis=0)
    v2 = jnp.concatenate([vp_ref[...], vc_ref[...]], axis=0)
    low = lax.broadcasted_iota(jnp.int32, k2.shape, 1) < SWA_HEAD_DIM
    k2r, v2r = pltpu.roll(k2, SWA_HEAD_DIM, 1), pltpu.roll(v2, SWA_HEAD_DIM, 1)
    kdup = [jnp.where(low, k2, k2r).astype(bf16), jnp.where(low, k2r, k2).astype(bf16)]
    vdup = [jnp.where(low, v2, v2r).astype(bf16), jnp.where(low, v2r, v2).astype(bf16)]
    qi = lax.broadcasted_iota(jnp.int32, (blk, 2 * blk), 0)
    kj = lax.broadcasted_iota(jnp.int32, (blk, 2 * blk), 1)
    dist = qi + blk - kj
    valid = (dist >= 0) & (dist < SWA_WINDOW) & ((kj >= blk) | (pl.program_id(1) > 0))
    distf = dist.astype(f32)
    first = lax.broadcasted_iota(jnp.int32, (blk, SWA_PAIR), 1) < SWA_HEAD_DIM
    grp = SWA_HEADS // SWA_KV_HEADS
    heads = range(SWA_HEADS)
    scores = []
    for h in heads:
        qp = q_ref[:, (h // 2) * SWA_PAIR:(h // 2 + 1) * SWA_PAIR] * (SWA_HEAD_DIM ** -0.5)
        q = jnp.where(first if h % 2 == 0 else ~first, qp, 0.0).astype(bf16)
        scores.append(lax.dot_general(q, kdup[h // grp], NT_DIMS, preferred_element_type=f32))
    probs = []
    for h in heads:
        sc = scores[h] - (2.0 ** (-8.0 * (h + 1) / SWA_HEADS)) * distf
        sc = jnp.where(valid, sc, NEG_INF)
        sink = sink_ref[h]
        m = jnp.maximum(jnp.max(sc, axis=-1, keepdims=True), sink)
        p = jnp.exp(sc - m)
        probs.append((p / (jnp.sum(p, axis=-1, keepdims=True) + jnp.exp(sink - m))).astype(bf16))
    outs = [jnp.dot(probs[h], vdup[h // grp], preferred_element_type=f32) for h in heads]
    pairs = [jnp.where(first, outs[2 * i], outs[2 * i + 1]) for i in range(SWA_HEADS // 2)]
    ssq = sum(jnp.sum(o * o, axis=-1, keepdims=True) for o in pairs)
    inv = lax.rsqrt(ssq * (1.0 / SWA_WIDTH) + RMS_EPS)
    for i, o in enumerate(pairs):
        o_ref[:, i * SWA_PAIR:(i + 1) * SWA_PAIR] = o * inv * g_ref[:, i * SWA_PAIR:(i + 1) * SWA_PAIR]


def _swa(proj, sinks, norm_g):
    bsz, s, _ = proj.shape
    col = lambda width, off, prev: pl.BlockSpec(
        (None, SWA_BLOCK, width), lambda b, n: (b, jnp.maximum(n - 1, 0) if prev else n, off // width))
    return pl.pallas_call(
        _swa_kernel,
        grid=(bsz, s // SWA_BLOCK),
        in_specs=[pl.BlockSpec(memory_space=pltpu.SMEM), col(SWA_WIDTH, COL_QS, False),
                  col(SWA_KV_WIDTH, COL_KS, True), col(SWA_KV_WIDTH, COL_KS, False),
                  col(SWA_KV_WIDTH, COL_VS, True), col(SWA_KV_WIDTH, COL_VS, False),
                  pl.BlockSpec((1, SWA_WIDTH), lambda b, n: (0, 0))],
        out_specs=pl.BlockSpec((None, SWA_BLOCK, SWA_WIDTH), lambda b, n: (b, n, 0)),
        out_shape=jax.ShapeDtypeStruct((bsz, s, SWA_WIDTH), jnp.float32),
        compiler_params=pltpu.CompilerParams(dimension_semantics=("arbitrary", "arbitrary")),
        name="swa",
    )(sinks, proj, proj, proj, proj, proj, norm_g.reshape(1, -1))


def _out_proj_kernel(x_ref, og_ref, os_ref, w_ref, gate_ref, g2_ref, sh_ref, sc_ref, x1_ref, h2_ref):
    mix = jnp.concatenate([og_ref[...], os_ref[...]], axis=-1).astype(jnp.bfloat16)
    x1 = x_ref[...] + gate_ref[...] * jnp.dot(mix, w_ref[...], preferred_element_type=jnp.float32)
    x1_ref[...] = x1
    h2 = _rms(x1) * g2_ref[...] * (1.0 + sc_ref[...]) + sh_ref[...]
    for c in range(ROW_SUB):
        h2_ref[:, c, :] = h2[:, c * LANES:(c + 1) * LANES]


def _out_proj(x, o_gla, o_swa, w, gate1, g2, shift2, scale2):
    bsz, s, d = x.shape
    row = lambda width: pl.BlockSpec((None, ROW_TILE, width), lambda b, i: (b, i, 0))
    per_b = pl.BlockSpec((None, 1, d), lambda b, i: (b, 0, 0))
    shp = jax.ShapeDtypeStruct((bsz, s, d), jnp.float32)
    return pl.pallas_call(
        _out_proj_kernel,
        grid=(bsz, s // ROW_TILE),
        in_specs=[row(d), row(GLA_WIDTH), row(SWA_WIDTH), pl.BlockSpec((MIX_WIDTH, d), lambda b, i: (0, 0)),
                  per_b, pl.BlockSpec((1, d), lambda b, i: (0, 0)), per_b, per_b],
        out_specs=[row(d), pl.BlockSpec((None, ROW_TILE, ROW_SUB, d // ROW_SUB), lambda b, i: (b, i, 0, 0))],
        out_shape=[shp, jax.ShapeDtypeStruct((bsz, s, ROW_SUB, d // ROW_SUB), jnp.float32)],
        compiler_params=pltpu.CompilerParams(dimension_semantics=("arbitrary", "arbitrary"),
                                             vmem_limit_bytes=48 * 1024 * 1024),
        name="out_proj",
    )(x, o_gla, o_swa, w, gate1, g2.reshape(1, d), shift2, scale2)


GLA_SUB = 16
GLA_BLOCK = 128
GLA_PAIR = 2 * GLA_DK
GLA_NB = 1


def _gla_kernel(q_ref, k_ref, v_ref, alr_ref, r_ref, wa_ref, ba_ref, g_ref, tri_ref, o_ref, st_ref, b_ref, q2_ref):
    f32, bf16 = jnp.float32, jnp.bfloat16

    @pl.when(pl.program_id(1) == 0)
    def _():
        st_ref[...] = jnp.zeros(st_ref.shape, f32)

    for bb in range(GLA_NB):
        z = jnp.dot(alr_ref[bb].astype(bf16), wa_ref[...], preferred_element_type=f32) + ba_ref[...]
        log_a = -(jnp.maximum(-z, 0.0) + jnp.log1p(jnp.exp(-jnp.abs(z)))) * (1.0 / GLA_GATE_TEMP)
        b_ref[bb] = jnp.dot(tri_ref[...], log_a, preferred_element_type=f32, precision=lax.Precision.HIGHEST)
        q2_ref[bb] = q_ref[bb] * (GLA_DK ** -0.5)
    lane = lax.broadcasted_iota(jnp.int32, (GLA_SUB, GLA_PAIR), 1)
    sub = lax.broadcasted_iota(jnp.int32, (GLA_SUB, GLA_PAIR), 0)
    first = lane < GLA_DK
    lane_s = lax.broadcasted_iota(jnp.int32, (GLA_DV, GLA_PAIR), 1) < GLA_DK

    chains = [(bb, hp) for bb in range(GLA_NB) for hp in range(GLA_HEADS // 2)]
    states = [st_ref[i] for i in range(len(chains))]
    for c in range(GLA_BLOCK // GLA_SUB):
        rows = pl.ds(c * GLA_SUB, GLA_SUB)
        for ci, (bb, hp) in enumerate(chains):
            cols = pl.ds(hp * GLA_PAIR, GLA_PAIR)
            state = states[ci]
            b = b_ref[bb, rows, cols]
            b_last = b_ref[bb, pl.ds(c * GLA_SUB + GLA_SUB - 1, 1), cols]
            q2, k2 = q2_ref[bb, rows, cols], k_ref[bb, rows, cols]
            qe = q2 * jnp.exp(b)
            kd = (k2 * jnp.exp(b_last - b)).astype(bf16)
            sb = state.astype(bf16)
            vcols2 = [pl.ds((2 * hp + j) * GLA_DV, GLA_DV) for j in range(2)]
            intra = [jnp.zeros((GLA_SUB, GLA_DV), f32), jnp.zeros((GLA_SUB, GLA_DV), f32)]
            for s_ in range(GLA_SUB):
                row = pl.ds(c * GLA_SUB + s_, 1)
                p = (k_ref[bb, row, cols] * q2) * jnp.exp(b - b_ref[bb, row, cols])
                keep = sub[:, :1] >= s_
                col_a = jnp.where(keep, jnp.sum(jnp.where(first, p, 0.0), axis=1, keepdims=True), 0.0)
                col_b = jnp.where(keep, jnp.sum(jnp.where(first, 0.0, p), axis=1, keepdims=True), 0.0)
                intra[0] = intra[0] + col_a * v_ref[bb, row, vcols2[0]]
                intra[1] = intra[1] + col_b * v_ref[bb, row, vcols2[1]]
            ut = []
            for j in range(2):
                vcols = vcols2[j]
                vh = v_ref[bb, rows, vcols]
                qm = jnp.where(first if j == 0 else ~first, qe, 0.0).astype(bf16)
                o = lax.dot_general(qm, sb, NT_DIMS, preferred_element_type=f32) + intra[j]
                ms = jnp.mean(o * o, axis=1, keepdims=True)
                r = r_ref[bb, rows, vcols]
                o_ref[bb, rows, vcols] = (o * lax.rsqrt(ms + RMS_EPS) * g_ref[:, vcols]) * (r * jax.nn.sigmoid(r))
                ut.append(jnp.dot(vh.T.astype(bf16), kd, preferred_element_type=f32))
            states[ci] = jnp.exp(b_last) * state + jnp.where(lane_s, ut[0], ut[1])
    for ci in range(len(chains)):
        st_ref[ci] = states[ci]


def _gla_tri():
    r = np.arange(GLA_BLOCK)
    return jnp.asarray(((r[:, None] // GLA_SUB == r[None, :] // GLA_SUB) & (r[None, :] <= r[:, None])).astype(np.float32))


def _gla(proj, w_alpha, b_alpha, norm_g):
    bsz, s, _ = proj.shape
    rank = w_alpha.shape[0]
    wa = jnp.pad(w_alpha, ((0, ALR_W - rank), (0, 0))).astype(jnp.bfloat16)
    col = lambda width, off: pl.BlockSpec((GLA_NB, GLA_BLOCK, width), lambda b, i: (b, i, off // width))
    const = lambda shape: pl.BlockSpec(shape, lambda b, i: (0, 0))
    return pl.pallas_call(
        _gla_kernel,
        grid=(bsz // GLA_NB, s // GLA_BLOCK),
        in_specs=[col(GLA_QK_WIDTH, COL_QG), col(GLA_QK_WIDTH, COL_KG), col(GLA_WIDTH, COL_VG), col(ALR_W, COL_ALR),
                  col(GLA_WIDTH, COL_RG), const((ALR_W, GLA_QK_WIDTH)), const((1, GLA_QK_WIDTH)),
                  const((1, GLA_WIDTH)), const((GLA_BLOCK, GLA_BLOCK))],
        out_specs=pl.BlockSpec((GLA_NB, GLA_BLOCK, GLA_WIDTH), lambda b, i: (b, i, 0)),
        out_shape=jax.ShapeDtypeStruct((bsz, s, GLA_WIDTH), jnp.float32),
        scratch_shapes=[pltpu.VMEM((GLA_NB * GLA_HEADS // 2, GLA_DV, GLA_PAIR), jnp.float32),
                        pltpu.VMEM((GLA_NB, GLA_BLOCK, GLA_QK_WIDTH), jnp.float32),
                        pltpu.VMEM((GLA_NB, GLA_BLOCK, GLA_QK_WIDTH), jnp.float32)],
        compiler_params=pltpu.CompilerParams(dimension_semantics=("arbitrary", "arbitrary")),
        name="gla",
    )(proj, proj, proj, proj, proj, wa, b_alpha.reshape(1, -1), norm_g.reshape(1, -1), _gla_tri())


def peer_mixer(h_tiles, w_q, keys_1, keys_2, u_tab, v_tab, x1, gate2, final_g):
    bsz, s, rs, rl = h_tiles.shape
    ht = h_tiles.reshape(bsz * s, rs, rl)
    experts, gates = _peer_route(ht, w_q, keys_1, keys_2)
    return _peer_experts(ht, experts, gates, u_tab, v_tab, x1, gate2, final_g)


ROUTE_TT = 512


def _top16_rows(s, code, big):
    vals, codes = [], []
    for _ in range(PEER_TOPK):
        m = jnp.max(s, axis=0, keepdims=True)
        c = jnp.min(jnp.where(s == m, code, big), axis=0, keepdims=True)
        s = jnp.where(code == c, NEG_INF, s)
        vals.append(m)
        codes.append(c)
    return vals, codes


def _peer_route_kernel(h_ref, wq_ref, k1_ref, k2_ref, e_ref, g_ref, q_ref):
    half = PEER_QUERY_DIM // 2
    tt = ROUTE_TT
    h = jnp.concatenate([h_ref[:, c, :] for c in range(h_ref.shape[1])], axis=-1)
    q_ref[...] = jnp.dot(h.astype(jnp.bfloat16), wq_ref[...], preferred_element_type=jnp.float32)
    key_id = lax.broadcasted_iota(jnp.int32, (PEER_N_KEYS, 2 * tt), 0).astype(jnp.float32)
    sub8 = lax.broadcasted_iota(jnp.int32, (8, tt), 0)
    nt = (((1,), (1,)), ((), ()))

    def head_body(hd, carry):
        off = pl.multiple_of(hd * PEER_QUERY_DIM, PEER_QUERY_DIM)
        q1 = q_ref[:, pl.ds(off, half)].astype(jnp.bfloat16)
        q2 = q_ref[:, pl.ds(off + half, half)].astype(jnp.bfloat16)
        s1 = lax.dot_general(k1_ref[...], q1, nt, preferred_element_type=jnp.float32)
        s2 = lax.dot_general(k2_ref[...], q2, nt, preferred_element_type=jnp.float32)
        vals, ids = _top16_rows(jnp.concatenate([s1, s2], axis=1), key_id, float(PEER_N_KEYS))
        v = jnp.concatenate(vals, axis=0)
        ix = jnp.concatenate(ids, axis=0).astype(jnp.int32)
        v1, v2, i1, i2 = v[:, :tt], v[:, tt:], ix[:, :tt], ix[:, tt:]
        cand, code = [], []
        for a, b0 in [(0, 0), (0, 8), (1, 0), (2, 0), (3, 0), (4, 0), (5, 0), (6, 0), (7, 0)]:
            cand.append(v1[a:a + 1] + v2[b0:b0 + 8])
            flat = a * PEER_TOPK + b0 + sub8
            code.append((flat << 14) | (i1[a:a + 1] * PEER_N_KEYS + i2[b0:b0 + 8]))
        cand.append(v1[8:16] + v2[0:1])
        code.append((((8 + sub8) * PEER_TOPK) << 14) | (i1[8:16] * PEER_N_KEYS + i2[0:1]))
        sc, cc = _top16_rows(jnp.concatenate(cand, axis=0), jnp.concatenate(code, axis=0).astype(jnp.float32),
                             float(1 << 30))
        sc = jnp.concatenate(sc, axis=0)
        ex = jnp.exp(sc - sc[0:1])
        g_ref[hd] = ex / jnp.sum(ex, axis=0, keepdims=True)
        e_ref[hd] = jnp.concatenate(cc, axis=0).astype(jnp.int32) & (PEER_N_EXPERTS - 1)
        return carry

    lax.fori_loop(0, PEER_HEADS, head_body, 0)


def _peer_route(hf, w_q, keys_1, keys_2):
    t, rs, rl = hf.shape
    d = rs * rl
    qw = PEER_HEADS * PEER_QUERY_DIM
    const = lambda shape: pl.BlockSpec(shape, lambda i: tuple(0 for _ in shape))
    out_spec = pl.BlockSpec((PEER_HEADS, PEER_TOPK, ROUTE_TT), lambda i: (0, 0, i))
    e_t, g_t = pl.pallas_call(
        _peer_route_kernel,
        grid=(t // ROUTE_TT,),
        in_specs=[pl.BlockSpec((ROUTE_TT, rs, rl), lambda i: (i, 0, 0)), const((d, qw)),
                  const(keys_1.shape), const(keys_2.shape)],
        out_specs=[out_spec, out_spec],
        out_shape=[jax.ShapeDtypeStruct((PEER_HEADS, PEER_TOPK, t), jnp.int32),
                   jax.ShapeDtypeStruct((PEER_HEADS, PEER_TOPK, t), jnp.float32)],
        scratch_shapes=[pltpu.VMEM((ROUTE_TT, qw), jnp.float32)],
        compiler_params=pltpu.CompilerParams(dimension_semantics=("arbitrary",), vmem_limit_bytes=48 * 1024 * 1024),
        name="peer_route",
    )(hf, w_q.astype(jnp.bfloat16), keys_1.astype(jnp.bfloat16), keys_2.astype(jnp.bfloat16))
    return e_t, g_t


PEER_PAIRS = PEER_HEADS * PEER_TOPK
PEER_TB = 128
PEER_HALF = 64
PEER_TOK_UNROLL = 4
HALF_D = D_MODEL // 2
WORD_SUB, WORD_LANE = HALF_D // 128, 128
ROW_SUB = 2 * WORD_SUB
HI_MASK = -65536
IDX_WAYS = 8


PACK_EB = 512
TAB_PAD = PACK_EB * WORD_SUB


def _pack_kernel(x_ref, o_ref):
    i = pl.program_id(0)

    @pl.when((i == 0) | (i == pl.num_programs(0) - 1))
    def _():
        o_ref[...] = jnp.zeros(o_ref.shape, o_ref.dtype)

    @pl.when((i > 0) & (i < pl.num_programs(0) - 1))
    def _():
        x = x_ref[...].astype(jnp.bfloat16).astype(jnp.float32)
        bits = lax.bitcast_convert_type(x, jnp.int32)
        word = lax.shift_right_logical(bits[:, :HALF_D], 16) | (bits[:, HALF_D:] & HI_MASK)
        for c in range(WORD_SUB):
            o_ref[:, c, :] = word[:, c * WORD_LANE:(c + 1) * WORD_LANE]


def _pack_table(tab):
    e, d = tab.shape
    nblk = e // PACK_EB
    out = pl.pallas_call(
        _pack_kernel,
        grid=(nblk + 2,),
        in_specs=[pl.BlockSpec((PACK_EB, d), lambda i: (jnp.clip(i - 1, 0, nblk - 1), 0))],
        out_specs=pl.BlockSpec((PACK_EB, WORD_SUB, WORD_LANE), lambda i: (i, 0, 0)),
        out_shape=jax.ShapeDtypeStruct(((nblk + 2) * PACK_EB, WORD_SUB, WORD_LANE), jnp.int32),
        name="pack_table",
    )(tab)
    return out.reshape((nblk + 2) * TAB_PAD, WORD_LANE)


def _pair_tile(tab_ref, ia, ib, upper):
    wa = tab_ref[pl.ds(pl.multiple_of(ia, WORD_SUB), ROW_SUB), :]
    wb = tab_ref[pl.ds(pl.multiple_of(ib - WORD_SUB, WORD_SUB), ROW_SUB), :]
    return jnp.where(upper, wa, wb)


_ORDER = (6, 2, 4, 0, 7, 3, 5, 1)


def _fold_pairs(ps, sub):
    b1 = (sub & 2) != 0
    w = []
    for i in range(2):
        x, y = ps[2 * i], ps[2 * i + 1]
        w.append(jnp.where(b1, x + pltpu.roll(x, 2, 0), y + pltpu.roll(y, 6, 0)))
    b0 = (sub & 1) != 0
    x, y = w
    return jnp.where(b0, x + pltpu.roll(x, 1, 0), y + pltpu.roll(y, 7, 0))


def _peer_u_kernel(*refs):
    idx_refs = refs[:IDX_WAYS]
    h_ref, tab_ref, a_ref = refs[IDX_WAYS:]
    sub = lax.broadcasted_iota(jnp.int32, (ROW_SUB, WORD_LANE), 0)
    lane = lax.broadcasted_iota(jnp.int32, (ROW_SUB, WORD_LANE), 1)
    n_groups = PEER_PAIRS // IDX_WAYS
    feed = [int(i) for i in np.argsort(np.array(_ORDER))]
    a_ref[...] = jnp.zeros(a_ref.shape, a_ref.dtype)

    def folded_tiles(t):
        h = h_ref[t]
        h_lo = jnp.concatenate([h[:WORD_SUB], h[:WORD_SUB]], axis=0)
        h_hi = jnp.concatenate([h[WORD_SUB:], h[WORD_SUB:]], axis=0)
        tiles = []
        for g in range(n_groups):
            ps = []
            for i in range(4):
                w = _pair_tile(tab_ref, idx_refs[feed[2 * i]][g * PEER_TB + t], idx_refs[feed[2 * i + 1]][g * PEER_TB + t],
                               sub < WORD_SUB)
                lo = lax.bitcast_convert_type(w << 16, jnp.float32)
                hi = lax.bitcast_convert_type(w & HI_MASK, jnp.float32)
                ps.append(lo * h_lo + hi * h_hi)
            tiles.append(_fold_pairs(ps, sub))
        return tuple(tiles)

    def lane_sums(t, tiles):
        here = lane == t
        for g in range(n_groups):
            a_ref[g] = jnp.where(here, jnp.sum(tiles[g], axis=1, keepdims=True), a_ref[g])

    def tok_body(t, prev):
        lane_sums(t - 1, prev)
        return folded_tiles(t)

    last = lax.fori_loop(1, PEER_TB, tok_body, folded_tiles(0))
    lane_sums(PEER_TB - 1, last)


def _split3_lanes(x, tb):
    mask = lambda v: lax.bitcast_convert_type(lax.bitcast_convert_type(v, jnp.int32) & HI_MASK, jnp.float32)
    hi = mask(x)
    mid = mask(x - hi)
    lo = x - hi - mid
    part = lax.broadcasted_iota(jnp.int32, x.shape, 1) // tb
    return jnp.where(part == 0, hi, jnp.where(part == 1, mid, jnp.where(part == 2, lo, 0.0))).astype(jnp.bfloat16)


def _peer_v_kernel(*refs):
    idx_refs = refs[:IDX_WAYS]
    a_ref, g_ref, tab_ref, sel_ref, x1_ref, gate2_ref, fg_ref, o_ref, wrep_ref, peer_ref = refs[IDX_WAYS:]
    per_way = PEER_PAIRS // IDX_WAYS
    a = a_ref[...].reshape(PEER_PAIRS, PEER_TB)
    gates = jnp.concatenate([g_ref[:, m, :] for m in range(per_way)], axis=0)
    w = gates * (0.5 * a * (1.0 + lax.erf(a * (2.0 ** -0.5))))
    w_swapped = pltpu.roll(w, PEER_HALF, 1)
    low_lanes = lax.broadcasted_iota(jnp.int32, w.shape, 1) < PEER_HALF
    upper = lax.broadcasted_iota(jnp.int32, (ROW_SUB, WORD_LANE), 0) < WORD_SUB

    def one_token(tl, first):
        it = first + tl
        row0 = (first // PEER_HALF) * PEER_PAIRS
        col = pl.multiple_of(tl * WORD_LANE, WORD_LANE)
        z = jnp.zeros((ROW_SUB, WORD_LANE), jnp.float32)
        acc_lo, acc_hi = [z, z], [z, z]
        for m in range(per_way):
            for jj in range(IDX_WAYS // 2):
                p = m * IDX_WAYS + 2 * jj
                words = _pair_tile(tab_ref, idx_refs[2 * jj][m * PEER_TB + it], idx_refs[2 * jj + 1][m * PEER_TB + it], upper)
                w2 = jnp.where(upper, wrep_ref[pl.ds(row0 + p, 1), pl.ds(col, WORD_LANE)],
                               wrep_ref[pl.ds(row0 + p + 1, 1), pl.ds(col, WORD_LANE)])
                k = jj % 2
                acc_lo[k] = acc_lo[k] + lax.bitcast_convert_type(words << 16, jnp.float32) * w2
                acc_hi[k] = acc_hi[k] + lax.bitcast_convert_type(words & HI_MASK, jnp.float32) * w2
        lo, hi = acc_lo[0] + acc_lo[1], acc_hi[0] + acc_hi[1]
        peer_ref[it] = jnp.concatenate([lo[:WORD_SUB] + lo[WORD_SUB:], hi[:WORD_SUB] + hi[WORD_SUB:]], axis=0)

    xs = [jnp.where(low_lanes, w, w_swapped), jnp.where(low_lanes, w_swapped, w)]
    wcat = jnp.concatenate([_split3_lanes(jnp.concatenate([x, x], axis=1), PEER_HALF) for x in xs], axis=0)
    wrep_ref[...] = jnp.dot(wcat, sel_ref[...], preferred_element_type=jnp.float32)

    for half in range(PEER_TB // PEER_HALF):
        def tok_body(i, carry, first=half * PEER_HALF):
            for j in range(PEER_TOK_UNROLL):
                one_token(PEER_TOK_UNROLL * i + j, first)
            return carry

        lax.fori_loop(0, PEER_HALF // PEER_TOK_UNROLL, tok_body, 0)

    peer = jnp.concatenate([peer_ref[:, c, :] for c in range(ROW_SUB)], axis=-1)
    o_ref[...] = _rms(x1_ref[...] + gate2_ref[...] * peer) * fg_ref[...]


_PEER_VMEM_LIMIT = 56 * 1024 * 1024


def _peer_experts(hf, experts, gates, u_tab, v_tab, x1, gate2, final_g):
    t = hf.shape[0]
    bsz, s, d = x1.shape
    assert s % PEER_TB == 0
    nblk = t // PEER_TB
    per_way = PEER_PAIRS // IDX_WAYS
    smem_spec = pl.BlockSpec((PEER_TB * per_way,), lambda i: (i,), memory_space=pltpu.SMEM)
    tab_spec = pl.BlockSpec(memory_space=pltpu.VMEM)
    row_spec = pl.BlockSpec((PEER_TB, ROW_SUB, WORD_LANE), lambda i: (i, 0, 0))
    a_spec = pl.BlockSpec((per_way, IDX_WAYS, PEER_TB), lambda i: (i, 0, 0))
    params = pltpu.CompilerParams(dimension_semantics=("arbitrary",), vmem_limit_bytes=_PEER_VMEM_LIMIT)
    assert IDX_WAYS == PEER_HEADS and PEER_TB == LANES
    idx_ways = [(experts[j] * WORD_SUB + TAB_PAD).reshape(per_way, nblk, PEER_TB).transpose(1, 0, 2).reshape(-1)
                for j in range(IDX_WAYS)]
    a_t = pl.pallas_call(
        _peer_u_kernel,
        grid=(nblk,),
        in_specs=[smem_spec] * IDX_WAYS + [row_spec, tab_spec],
        out_specs=a_spec,
        out_shape=jax.ShapeDtypeStruct((nblk * per_way, IDX_WAYS, PEER_TB), jnp.float32),
        compiler_params=params, name="peer_u",
    )(*idx_ways, hf, _pack_table(u_tab))
    eye = np.repeat(np.eye(PEER_HALF, dtype=np.float32), WORD_LANE, axis=1)
    sel = jnp.asarray(np.concatenate([eye, eye, eye, np.zeros_like(eye)], axis=0), jnp.bfloat16)
    return pl.pallas_call(
        _peer_v_kernel,
        grid=(nblk,),
        in_specs=[smem_spec] * IDX_WAYS + [a_spec, pl.BlockSpec((IDX_WAYS, per_way, PEER_TB), lambda i: (0, 0, i)),
                                           tab_spec, pl.BlockSpec(sel.shape, lambda i: (0, 0)),
                                           pl.BlockSpec((PEER_TB, d), lambda i: (i, 0)),
                                           pl.BlockSpec((None, 1, d), lambda i: ((i * PEER_TB) // s, 0, 0)),
                                           pl.BlockSpec((1, d), lambda i: (0, 0))],
        out_specs=pl.BlockSpec((PEER_TB, d), lambda i: (i, 0)),
        out_shape=jax.ShapeDtypeStruct((t, d), jnp.float32),
        scratch_shapes=[pltpu.VMEM((PEER_TB // PEER_HALF * PEER_PAIRS, PEER_HALF * WORD_LANE), jnp.float32),
                        pltpu.VMEM((PEER_TB, ROW_SUB, WORD_LANE), jnp.float32)],
        compiler_params=params, name="peer_v",
    )(*idx_ways, a_t, gates, _pack_table(v_tab), sel, x1.reshape(t, d), gate2, final_g.reshape(1, d)).reshape(bsz, s, d)


def _in_proj_weights(w_in):
    o = np.cumsum([0, GLA_QK_WIDTH, GLA_QK_WIDTH, GLA_WIDTH, GLA_GATE_RANK, GLA_WIDTH, SWA_WIDTH, SWA_KV_WIDTH])
    src = np.full((PROJ_W,), IN_WIDTH, np.int32)
    for dst, start, width in ((COL_QG, o[0], GLA_QK_WIDTH), (COL_KG, o[1], GLA_QK_WIDTH), (COL_VG, o[2], GLA_WIDTH),
                              (COL_ALR, o[3], GLA_GATE_RANK), (COL_RG, o[4], GLA_WIDTH), (COL_QS, o[5], SWA_WIDTH),
                              (COL_KS, o[6], SWA_KV_WIDTH), (COL_VS, o[7], SWA_KV_WIDTH)):
        src[dst:dst + width] = start + np.arange(width)
    w_ext = jnp.concatenate([w_in, jnp.zeros((w_in.shape[0], 1), w_in.dtype)], axis=1)
    return w_ext[:, src].astype(jnp.bfloat16)


def kernel(x, c, w_ada, b_ada, norm1_g, norm2_g, w_in, w_gla_alpha, b_gla_alpha, gla_norm_g,
           swa_sinks, swa_norm_g, w_out, w_peer_q, peer_keys_1, peer_keys_2, peer_u, peer_v, final_g):
    assert w_ada.shape[0] == 1, "single-layer block"
    l = 0
    mod = _ada(c, w_ada[l], b_ada[l])
    shift1, scale1, gate1, shift2, scale2, gate2 = [m[:, None, :] for m in jnp.split(mod, 6, axis=-1)]
    proj = _in_proj(x, norm1_g[l], shift1, scale1, _in_proj_weights(w_in[l]))
    o_gla = _gla(proj, w_gla_alpha[l], b_gla_alpha[l], gla_norm_g[l])
    o_swa = _swa(proj, swa_sinks[l], swa_norm_g[l])
    x1, h2 = _out_proj(x, o_gla, o_swa, w_out[l].astype(jnp.bfloat16), gate1, norm2_g[l], shift2, scale2)
    return peer_mixer(h2, w_peer_q[l], peer_keys_1[l], peer_keys_2[l], peer_u[l], peer_v[l], x1, gate2, final_g)
```

```python
import jax, jax.numpy as jnp
from jax import lax
import numpy as np
from jax.experimental import pallas as pl
from jax.experimental.pallas import tpu as pltpu

D_MODEL = 1024
BATCH = 4
SEQ = 8192
DEPTH = 1

GLA_HEADS = 4
GLA_DK = 64
GLA_DV = 128
GLA_GATE_RANK = 16
GLA_GATE_TEMP = 16.0
GLA_CHUNK = 64
SWA_HEADS = 8
SWA_KV_HEADS = 2
SWA_HEAD_DIM = 64
SWA_WINDOW = 128
SWA_BLOCK = 128
PEER_HEADS = 8
PEER_N_KEYS = 128
PEER_N_EXPERTS = PEER_N_KEYS * PEER_N_KEYS
PEER_QUERY_DIM = 256
PEER_TOPK = 16
RMS_EPS = 1e-6

GLA_QK_WIDTH = GLA_HEADS * GLA_DK
GLA_WIDTH = GLA_HEADS * GLA_DV
SWA_WIDTH = SWA_HEADS * SWA_HEAD_DIM
SWA_KV_WIDTH = SWA_KV_HEADS * SWA_HEAD_DIM
MIX_WIDTH = GLA_WIDTH + SWA_WIDTH
IN_WIDTH = 2 * GLA_QK_WIDTH + GLA_WIDTH + GLA_GATE_RANK + GLA_WIDTH + SWA_WIDTH + 2 * SWA_KV_WIDTH


LANES = 128
NEG_INF = float("-inf")
NT_DIMS = (((1,), (1,)), ((), ()))
ROW_TILE = 512
SWA_PAIR = 2 * SWA_HEAD_DIM
ALR_W = LANES
COL_QS = 0
COL_QG = COL_QS + SWA_WIDTH
COL_KG = COL_QG + GLA_QK_WIDTH
COL_VG = COL_KG + GLA_QK_WIDTH
COL_RG = COL_VG + GLA_WIDTH
COL_KS = COL_RG + GLA_WIDTH
COL_VS = COL_KS + SWA_KV_WIDTH
COL_ALR = COL_VS + SWA_KV_WIDTH
PROJ_W = COL_ALR + ALR_W


def _rms(x):
    return x * lax.rsqrt(jnp.mean(x * x, axis=-1, keepdims=True) + RMS_EPS)


def _ada_kernel(c_ref, w_ref, b_ref, o_ref):
    c = c_ref[...]
    ca = (c * jax.nn.sigmoid(c)).astype(jnp.bfloat16)
    o_ref[...] = jnp.dot(ca, w_ref[...].astype(jnp.bfloat16), preferred_element_type=jnp.float32) + b_ref[...]


def _ada(c, w, b):
    bsz, d = c.shape
    n = w.shape[1]
    tn = 1024
    return pl.pallas_call(
        _ada_kernel,
        grid=(n // tn,),
        in_specs=[pl.BlockSpec((bsz, d), lambda j: (0, 0)), pl.BlockSpec((d, tn), lambda j: (0, j)),
                  pl.BlockSpec((1, tn), lambda j: (0, j))],
        out_specs=pl.BlockSpec((bsz, tn), lambda j: (0, j)),
        out_shape=jax.ShapeDtypeStruct((bsz, n), jnp.float32), name="ada",
    )(c, w, b.reshape(1, n))


def _in_proj_kernel(x_ref, g_ref, sh_ref, sc_ref, w_ref, o_ref):
    h = _rms(x_ref[...]) * g_ref[...] * (1.0 + sc_ref[...]) + sh_ref[...]
    o_ref[...] = jnp.dot(h.astype(jnp.bfloat16), w_ref[...], preferred_element_type=jnp.float32)


def _in_proj(x, g, shift, scale, w):
    bsz, s, d = x.shape
    n = w.shape[1]
    row = lambda width: pl.BlockSpec((None, ROW_TILE, width), lambda b, i: (b, i, 0))
    per_b = pl.BlockSpec((None, 1, d), lambda b, i: (b, 0, 0))
    return pl.pallas_call(
        _in_proj_kernel,
        grid=(bsz, s // ROW_TILE),
        in_specs=[row(d), pl.BlockSpec((1, d), lambda b, i: (0, 0)), per_b, per_b,
                  pl.BlockSpec((d, n), lambda b, i: (0, 0))],
        out_specs=row(n),
        out_shape=jax.ShapeDtypeStruct((bsz, s, n), jnp.float32),
        compiler_params=pltpu.CompilerParams(dimension_semantics=("arbitrary", "arbitrary"),
                                             vmem_limit_bytes=48 * 1024 * 1024),
        name="in_proj",
    )(x, g.reshape(1, d), shift, scale, w)


def _swa_kernel(sink_ref, q_ref, kp_ref, kc_ref, vp_ref, vc_ref, g_ref, o_ref):
    f32, bf16 = jnp.float32, jnp.bfloat16
    blk = SWA_BLOCK
    k2 = jnp.concatenate([kp_ref[...], kc_ref[...]], axis=0)
    v2 = jnp.concatenate([vp_ref[...], vc_ref[...]], axis=0)
    low = lax.broadcasted_iota(jnp.int32, k2.shape, 1) < SWA_HEAD_DIM
    k2r, v2r = pltpu.roll(k2, SWA_HEAD_DIM, 1), pltpu.roll(v2, SWA_HEAD_DIM, 1)
    kdup = [jnp.where(low, k2, k2r).astype(bf16), jnp.where(low, k2r, k2).astype(bf16)]
    vdup = [jnp.where(low, v2, v2r).astype(bf16), jnp.where(low, v2r, v2).astype(bf16)]
    qi = lax.broadcasted_iota(jnp.int32, (blk, 2 * blk), 0)
    kj = lax.broadcasted_iota(jnp.int32, (blk, 2 * blk), 1)
    dist = qi + blk - kj
    valid = (dist >= 0) & (dist < SWA_WINDOW) & ((kj >= blk) | (pl.program_id(1) > 0))
    distf = dist.astype(f32)
    first = lax.broadcasted_iota(jnp.int32, (blk, SWA_PAIR), 1) < SWA_HEAD_DIM
    grp = SWA_HEADS // SWA_KV_HEADS
    heads = range(SWA_HEADS)
    scores = []
    for h in heads:
        qp = q_ref[:, (h // 2) * SWA_PAIR:(h // 2 + 1) * SWA_PAIR] * (SWA_HEAD_DIM ** -0.5)
        q = jnp.where(first if h % 2 == 0 else ~first, qp, 0.0).astype(bf16)
        scores.append(lax.dot_general(q, kdup[h // grp], NT_DIMS, preferred_element_type=f32))
    probs = []
    for h in heads:
        sc = scores[h] - (2.0 ** (-8.0 * (h + 1) / SWA_HEADS)) * distf
        sc = jnp.where(valid, sc, NEG_INF)
        sink = sink_ref[h]
        m = jnp.maximum(jnp.max(sc, axis=-1, keepdims=True), sink)
        p = jnp.exp(sc - m)
        probs.append((p / (jnp.sum(p, axis=-1, keepdims=True) + jnp.exp(sink - m))).astype(bf16))
    outs = [jnp.dot(probs[h], vdup[h // grp], preferred_element_type=f32) for h in heads]
    pairs = [jnp.where(first, outs[2 * i], outs[2 * i + 1]) for i in range(SWA_HEADS // 2)]
    ssq = sum(jnp.sum(o * o, axis=-1, keepdims=True) for o in pairs)
    inv = lax.rsqrt(ssq * (1.0 / SWA_WIDTH) + RMS_EPS)
    for i, o in enumerate(pairs):
        o_ref[:, i * SWA_PAIR:(i + 1) * SWA_PAIR] = o * inv * g_ref[:, i * SWA_PAIR:(i + 1) * SWA_PAIR]


def _swa(proj, sinks, norm_g):
    bsz, s, _ = proj.shape
    col = lambda width, off, prev: pl.BlockSpec(
        (None, SWA_BLOCK, width), lambda b, n: (b, jnp.maximum(n - 1, 0) if prev else n, off // width))
    return pl.pallas_call(
        _swa_kernel,
        grid=(bsz, s // SWA_BLOCK),
        in_specs=[pl.BlockSpec(memory_space=pltpu.SMEM), col(SWA_WIDTH, COL_QS, False),
                  col(SWA_KV_WIDTH, COL_KS, True), col(SWA_KV_WIDTH, COL_KS, False),
                  col(SWA_KV_WIDTH, COL_VS, True), col(SWA_KV_WIDTH, COL_VS, False),
                  pl.BlockSpec((1, SWA_WIDTH), lambda b, n: (0, 0))],
        out_specs=pl.BlockSpec((None, SWA_BLOCK, SWA_WIDTH), lambda b, n: (b, n, 0)),
        out_shape=jax.ShapeDtypeStruct((bsz, s, SWA_WIDTH), jnp.float32),
        compiler_params=pltpu.CompilerParams(dimension_semantics=("arbitrary", "arbitrary")),
        name="swa",
    )(sinks, proj, proj, proj, proj, proj, norm_g.reshape(1, -1))


def _out_proj_kernel(x_ref, og_ref, os_ref, w_ref, gate_ref, g2_ref, sh_ref, sc_ref, x1_ref, h2_ref):
    mix = jnp.concatenate([og_ref[...], os_ref[...]], axis=-1).astype(jnp.bfloat16)
    x1 = x_ref[...] + gate_ref[...] * jnp.dot(mix, w_ref[...], preferred_element_type=jnp.float32)
    x1_ref[...] = x1
    h2 = _rms(x1) * g2_ref[...] * (1.0 + sc_ref[...]) + sh_ref[...]
    for c in range(ROW_SUB):
        h2_ref[:, c, :] = h2[:, c * LANES:(c + 1) * LANES]


def _out_proj(x, o_gla, o_swa, w, gate1, g2, shift2, scale2):
    bsz, s, d = x.shape
    row = lambda width: pl.BlockSpec((None, ROW_TILE, width), lambda b, i: (b, i, 0))
    per_b = pl.BlockSpec((None, 1, d), lambda b, i: (b, 0, 0))
    shp = jax.ShapeDtypeStruct((bsz, s, d), jnp.float32)
    return pl.pallas_call(
        _out_proj_kernel,
        grid=(bsz, s // ROW_TILE),
        in_specs=[row(d), row(GLA_WIDTH), row(SWA_WIDTH), pl.BlockSpec((MIX_WIDTH, d), lambda b, i: (0, 0)),
                  per_b, pl.BlockSpec((1, d), lambda b, i: (0, 0)), per_b, per_b],
        out_specs=[row(d), pl.BlockSpec((None, ROW_TILE, ROW_SUB, d // ROW_SUB), lambda b, i: (b, i, 0, 0))],
        out_shape=[shp, jax.ShapeDtypeStruct((bsz, s, ROW_SUB, d // ROW_SUB), jnp.float32)],
        compiler_params=pltpu.CompilerParams(dimension_semantics=("arbitrary", "arbitrary"),
                                             vmem_limit_bytes=48 * 1024 * 1024),
        name="out_proj",
    )(x, o_gla, o_swa, w, gate1, g2.reshape(1, d), shift2, scale2)


GLA_SUB = 16
GLA_BLOCK = 128
GLA_PAIR = 2 * GLA_DK
GLA_NB = 1


def _gla_kernel(q_ref, k_ref, v_ref, alr_ref, r_ref, wa_ref, ba_ref, g_ref, tri_ref, o_ref, st_ref, b_ref, q2_ref):
    f32, bf16 = jnp.float32, jnp.bfloat16

    @pl.when(pl.program_id(1) == 0)
    def _():
        st_ref[...] = jnp.zeros(st_ref.shape, f32)

    for bb in range(GLA_NB):
        z = jnp.dot(alr_ref[bb].astype(bf16), wa_ref[...], preferred_element_type=f32) + ba_ref[...]
        log_a = -(jnp.maximum(-z, 0.0) + jnp.log1p(jnp.exp(-jnp.abs(z)))) * (1.0 / GLA_GATE_TEMP)
        b_ref[bb] = jnp.dot(tri_ref[...], log_a, preferred_element_type=f32, precision=lax.Precision.HIGHEST)
        q2_ref[bb] = q_ref[bb] * (GLA_DK ** -0.5)
    lane = lax.broadcasted_iota(jnp.int32, (GLA_SUB, GLA_PAIR), 1)
    sub = lax.broadcasted_iota(jnp.int32, (GLA_SUB, GLA_PAIR), 0)
    first = lane < GLA_DK
    lane_s = lax.broadcasted_iota(jnp.int32, (GLA_DV, GLA_PAIR), 1) < GLA_DK

    chains = [(bb, hp) for bb in range(GLA_NB) for hp in range(GLA_HEADS // 2)]
    states = [st_ref[i] for i in range(len(chains))]
    for c in range(GLA_BLOCK // GLA_SUB):
        rows = pl.ds(c * GLA_SUB, GLA_SUB)
        for ci, (bb, hp) in enumerate(chains):
            cols = pl.ds(hp * GLA_PAIR, GLA_PAIR)
            state = states[ci]
            b = b_ref[bb, rows, cols]
            b_last = b_ref[bb, pl.ds(c * GLA_SUB + GLA_SUB - 1, 1), cols]
            q2, k2 = q2_ref[bb, rows, cols], k_ref[bb, rows, cols]
            qe = q2 * jnp.exp(b)
            kd = (k2 * jnp.exp(b_last - b)).astype(bf16)
            sb = state.astype(bf16)
            vcols2 = [pl.ds((2 * hp + j) * GLA_DV, GLA_DV) for j in range(2)]
            intra = [jnp.zeros((GLA_SUB, GLA_DV), f32), jnp.zeros((GLA_SUB, GLA_DV), f32)]
            for s_ in range(GLA_SUB):
                row = pl.ds(c * GLA_SUB + s_, 1)
                p = (k_ref[bb, row, cols] * q2) * jnp.exp(b - b_ref[bb, row, cols])
                keep = sub[:, :1] >= s_
                col_a = jnp.where(keep, jnp.sum(jnp.where(first, p, 0.0), axis=1, keepdims=True), 0.0)
                col_b = jnp.where(keep, jnp.sum(jnp.where(first, 0.0, p), axis=1, keepdims=True), 0.0)
                intra[0] = intra[0] + col_a * v_ref[bb, row, vcols2[0]]
                intra[1] = intra[1] + col_b * v_ref[bb, row, vcols2[1]]
            ut = []
            for j in range(2):
                vcols = vcols2[j]
                vh = v_ref[bb, rows, vcols]
                qm = jnp.where(first if j == 0 else ~first, qe, 0.0).astype(bf16)
                o = lax.dot_general(qm, sb, NT_DIMS, preferred_element_type=f32) + intra[j]
                ms = jnp.mean(o * o, axis=1, keepdims=True)
                r = r_ref[bb, rows, vcols]
                o_ref[bb, rows, vcols] = (o * lax.rsqrt(ms + RMS_EPS) * g_ref[:, vcols]) * (r * jax.nn.sigmoid(r))
                ut.append(jnp.dot(vh.T.astype(bf16), kd, preferred_element_type=f32))
            states[ci] = jnp.exp(b_last) * state + jnp.where(lane_s, ut[0], ut[1])
    for ci in range(len(chains)):
        st_ref[ci] = states[ci]


def _gla_tri():
    r = np.arange(GLA_BLOCK)
    return jnp.asarray(((r[:, None] // GLA_SUB == r[None, :] // GLA_SUB) & (r[None, :] <= r[:, None])).astype(np.float32))


def _gla(proj, w_alpha, b_alpha, norm_g):
    bsz, s, _ = proj.shape
    rank = w_alpha.shape[0]
    wa = jnp.pad(w_alpha, ((0, ALR_W - rank), (0, 0))).astype(jnp.bfloat16)
    col = lambda width, off: pl.BlockSpec((GLA_NB, GLA_BLOCK, width), lambda b, i: (b, i, off // width))
    const = lambda shape: pl.BlockSpec(shape, lambda b, i: (0, 0))
    return pl.pallas_call(
        _gla_kernel,
        grid=(bsz // GLA_NB, s // GLA_BLOCK),
        in_specs=[col(GLA_QK_WIDTH, COL_QG), col(GLA_QK_WIDTH, COL_KG), col(GLA_WIDTH, COL_VG), col(ALR_W, COL_ALR),
                  col(GLA_WIDTH, COL_RG), const((ALR_W, GLA_QK_WIDTH)), const((1, GLA_QK_WIDTH)),
                  const((1, GLA_WIDTH)), const((GLA_BLOCK, GLA_BLOCK))],
        out_specs=pl.BlockSpec((GLA_NB, GLA_BLOCK, GLA_WIDTH), lambda b, i: (b, i, 0)),
        out_shape=jax.ShapeDtypeStruct((bsz, s, GLA_WIDTH), jnp.float32),
        scratch_shapes=[pltpu.VMEM((GLA_NB * GLA_HEADS // 2, GLA_DV, GLA_PAIR), jnp.float32),
                        pltpu.VMEM((GLA_NB, GLA_BLOCK, GLA_QK_WIDTH), jnp.float32),
                        pltpu.VMEM((GLA_NB, GLA_BLOCK, GLA_QK_WIDTH), jnp.float32)],
        compiler_params=pltpu.CompilerParams(dimension_semantics=("arbitrary", "arbitrary")),
        name="gla",
    )(proj, proj, proj, proj, proj, wa, b_alpha.reshape(1, -1), norm_g.reshape(1, -1), _gla_tri())


def peer_mixer(h_tiles, w_q, keys_1, keys_2, u_tab, v_tab, x1, gate2, final_g):
    bsz, s, rs, rl = h_tiles.shape
    ht = h_tiles.reshape(bsz * s, rs, rl)
    experts, gates = _peer_route(ht, w_q, keys_1, keys_2)
    return _peer_experts(ht, experts, gates, u_tab, v_tab, x1, gate2, final_g)


ROUTE_TT = 1024


def _top16_rows(s, code, big):
    vals, codes = [], []
    for _ in range(PEER_TOPK):
        m = jnp.max(s, axis=0, keepdims=True)
        c = jnp.min(jnp.where(s == m, code, big), axis=0, keepdims=True)
        s = jnp.where(code == c, NEG_INF, s)
        vals.append(m)
        codes.append(c)
    return vals, codes


def _peer_route_kernel(h_ref, wq_ref, k1_ref, k2_ref, e_ref, g_ref, q_ref):
    half = PEER_QUERY_DIM // 2
    tt = ROUTE_TT
    h = jnp.concatenate([h_ref[:, c, :] for c in range(h_ref.shape[1])], axis=-1)
    q_ref[...] = jnp.dot(h.astype(jnp.bfloat16), wq_ref[...], preferred_element_type=jnp.float32)
    key_id = lax.broadcasted_iota(jnp.int32, (PEER_N_KEYS, 2 * tt), 0).astype(jnp.float32)
    sub8 = lax.broadcasted_iota(jnp.int32, (8, tt), 0)
    nt = (((1,), (1,)), ((), ()))

    def head_body(hd, carry):
        off = pl.multiple_of(hd * PEER_QUERY_DIM, PEER_QUERY_DIM)
        q1 = q_ref[:, pl.ds(off, half)].astype(jnp.bfloat16)
        q2 = q_ref[:, pl.ds(off + half, half)].astype(jnp.bfloat16)
        s1 = lax.dot_general(k1_ref[...], q1, nt, preferred_element_type=jnp.float32)
        s2 = lax.dot_general(k2_ref[...], q2, nt, preferred_element_type=jnp.float32)
        vals, ids = _top16_rows(jnp.concatenate([s1, s2], axis=1), key_id, float(PEER_N_KEYS))
        v = jnp.concatenate(vals, axis=0)
        ix = jnp.concatenate(ids, axis=0).astype(jnp.int32)
        v1, v2, i1, i2 = v[:, :tt], v[:, tt:], ix[:, :tt], ix[:, tt:]
        cand, code = [], []
        for a, b0 in [(0, 0), (0, 8), (1, 0), (2, 0), (3, 0), (4, 0), (5, 0), (6, 0), (7, 0)]:
            cand.append(v1[a:a + 1] + v2[b0:b0 + 8])
            flat = a * PEER_TOPK + b0 + sub8
            code.append((flat << 14) | (i1[a:a + 1] * PEER_N_KEYS + i2[b0:b0 + 8]))
        cand.append(v1[8:16] + v2[0:1])
        code.append((((8 + sub8) * PEER_TOPK) << 14) | (i1[8:16] * PEER_N_KEYS + i2[0:1]))
        sc, cc = _top16_rows(jnp.concatenate(cand, axis=0), jnp.concatenate(code, axis=0).astype(jnp.float32),
                             float(1 << 30))
        sc = jnp.concatenate(sc, axis=0)
        ex = jnp.exp(sc - sc[0:1])
        g_ref[hd] = ex / jnp.sum(ex, axis=0, keepdims=True)
        e_ref[hd] = jnp.concatenate(cc, axis=0).astype(jnp.int32) & (PEER_N_EXPERTS - 1)
        return carry

    lax.fori_loop(0, PEER_HEADS, head_body, 0)


def _peer_route(hf, w_q, keys_1, keys_2):
    t, rs, rl = hf.shape
    d = rs * rl
    qw = PEER_HEADS * PEER_QUERY_DIM
    const = lambda shape: pl.BlockSpec(shape, lambda i: tuple(0 for _ in shape))
    out_spec = pl.BlockSpec((PEER_HEADS, PEER_TOPK, ROUTE_TT), lambda i: (0, 0, i))
    e_t, g_t = pl.pallas_call(
        _peer_route_kernel,
        grid=(t // ROUTE_TT,),
        in_specs=[pl.BlockSpec((ROUTE_TT, rs, rl), lambda i: (i, 0, 0)), const((d, qw)),
                  const(keys_1.shape), const(keys_2.shape)],
        out_specs=[out_spec, out_spec],
        out_shape=[jax.ShapeDtypeStruct((PEER_HEADS, PEER_TOPK, t), jnp.int32),
                   jax.ShapeDtypeStruct((PEER_HEADS, PEER_TOPK, t), jnp.float32)],
        scratch_shapes=[pltpu.VMEM((ROUTE_TT, qw), jnp.float32)],
        compiler_params=pltpu.CompilerParams(dimension_semantics=("arbitrary",), vmem_limit_bytes=48 * 1024 * 1024),
        name="peer_route",
    )(hf, w_q.astype(jnp.bfloat16), keys_1.astype(jnp.bfloat16), keys_2.astype(jnp.bfloat16))
    return e_t, g_t


PEER_PAIRS = PEER_HEADS * PEER_TOPK
PEER_TB = 128
PEER_HALF = 64
PEER_TOK_UNROLL = 4
HALF_D = D_MODEL // 2
WORD_SUB, WORD_LANE = HALF_D // 128, 128
ROW_SUB = 2 * WORD_SUB
HI_MASK = -65536
IDX_WAYS = 8


PACK_EB = 512
TAB_PAD = PACK_EB * WORD_SUB


def _pack_kernel(x_ref, o_ref):
    i = pl.program_id(0)

    @pl.when((i == 0) | (i == pl.num_programs(0) - 1))
    def _():
        o_ref[...] = jnp.zeros(o_ref.shape, o_ref.dtype)

    @pl.when((i > 0) & (i < pl.num_programs(0) - 1))
    def _():
        x = x_ref[...].astype(jnp.bfloat16).astype(jnp.float32)
        bits = lax.bitcast_convert_type(x, jnp.int32)
        word = lax.shift_right_logical(bits[:, :HALF_D], 16) | (bits[:, HALF_D:] & HI_MASK)
        for c in range(WORD_SUB):
            o_ref[:, c, :] = word[:, c * WORD_LANE:(c + 1) * WORD_LANE]


def _pack_table(tab):
    e, d = tab.shape
    nblk = e // PACK_EB
    out = pl.pallas_call(
        _pack_kernel,
        grid=(nblk + 2,),
        in_specs=[pl.BlockSpec((PACK_EB, d), lambda i: (jnp.clip(i - 1, 0, nblk - 1), 0))],
        out_specs=pl.BlockSpec((PACK_EB, WORD_SUB, WORD_LANE), lambda i: (i, 0, 0)),
        out_shape=jax.ShapeDtypeStruct(((nblk + 2) * PACK_EB, WORD_SUB, WORD_LANE), jnp.int32),
        name="pack_table",
    )(tab)
    return out.reshape((nblk + 2) * TAB_PAD, WORD_LANE)


def _pair_tile(tab_ref, ia, ib, upper):
    wa = tab_ref[pl.ds(pl.multiple_of(ia, WORD_SUB), ROW_SUB), :]
    wb = tab_ref[pl.ds(pl.multiple_of(ib - WORD_SUB, WORD_SUB), ROW_SUB), :]
    return jnp.where(upper, wa, wb)


_ORDER = (6, 2, 4, 0, 7, 3, 5, 1)


def _fold_pairs(ps, sub):
    b1 = (sub & 2) != 0
    w = []
    for i in range(2):
        x, y = ps[2 * i], ps[2 * i + 1]
        w.append(jnp.where(b1, x + pltpu.roll(x, 2, 0), y + pltpu.roll(y, 6, 0)))
    b0 = (sub & 1) != 0
    x, y = w
    return jnp.where(b0, x + pltpu.roll(x, 1, 0), y + pltpu.roll(y, 7, 0))


def _peer_u_kernel(*refs):
    idx_refs = refs[:IDX_WAYS]
    h_ref, tab_ref, a_ref = refs[IDX_WAYS:]
    sub = lax.broadcasted_iota(jnp.int32, (ROW_SUB, WORD_LANE), 0)
    lane = lax.broadcasted_iota(jnp.int32, (ROW_SUB, WORD_LANE), 1)
    n_groups = PEER_PAIRS // IDX_WAYS
    feed = [int(i) for i in np.argsort(np.array(_ORDER))]
    a_ref[...] = jnp.zeros(a_ref.shape, a_ref.dtype)

    def folded_tiles(t):
        h = h_ref[t]
        h_lo = jnp.concatenate([h[:WORD_SUB], h[:WORD_SUB]], axis=0)
        h_hi = jnp.concatenate([h[WORD_SUB:], h[WORD_SUB:]], axis=0)
        tiles = []
        for g in range(n_groups):
            ps = []
            for i in range(4):
                w = _pair_tile(tab_ref, idx_refs[feed[2 * i]][g * PEER_TB + t], idx_refs[feed[2 * i + 1]][g * PEER_TB + t],
                               sub < WORD_SUB)
                lo = lax.bitcast_convert_type(w << 16, jnp.float32)
                hi = lax.bitcast_convert_type(w & HI_MASK, jnp.float32)
                ps.append(lo * h_lo + hi * h_hi)
            tiles.append(_fold_pairs(ps, sub))
        return tuple(tiles)

    def lane_sums(t, tiles):
        here = lane == t
        for g in range(n_groups):
            a_ref[g] = jnp.where(here, jnp.sum(tiles[g], axis=1, keepdims=True), a_ref[g])

    def tok_body(t, prev):
        lane_sums(t - 1, prev)
        return folded_tiles(t)

    last = lax.fori_loop(1, PEER_TB, tok_body, folded_tiles(0))
    lane_sums(PEER_TB - 1, last)


def _split3_lanes(x, tb):
    mask = lambda v: lax.bitcast_convert_type(lax.bitcast_convert_type(v, jnp.int32) & HI_MASK, jnp.float32)
    hi = mask(x)
    mid = mask(x - hi)
    lo = x - hi - mid
    part = lax.broadcasted_iota(jnp.int32, x.shape, 1) // tb
    return jnp.where(part == 0, hi, jnp.where(part == 1, mid, jnp.where(part == 2, lo, 0.0))).astype(jnp.bfloat16)


def _peer_v_kernel(*refs):
    idx_refs = refs[:IDX_WAYS]
    a_ref, g_ref, tab_ref, sel_ref, x1_ref, gate2_ref, fg_ref, o_ref, wrep_ref, peer_ref = refs[IDX_WAYS:]
    per_way = PEER_PAIRS // IDX_WAYS
    a = a_ref[...].reshape(PEER_PAIRS, PEER_TB)
    gates = jnp.concatenate([g_ref[:, m, :] for m in range(per_way)], axis=0)
    w = gates * (0.5 * a * (1.0 + lax.erf(a * (2.0 ** -0.5))))
    w_swapped = pltpu.roll(w, PEER_HALF, 1)
    low_lanes = lax.broadcasted_iota(jnp.int32, w.shape, 1) < PEER_HALF
    upper = lax.broadcasted_iota(jnp.int32, (ROW_SUB, WORD_LANE), 0) < WORD_SUB

    def one_token(tl, first):
        it = first + tl
        row0 = (first // PEER_HALF) * PEER_PAIRS
        col = pl.multiple_of(tl * WORD_LANE, WORD_LANE)
        z = jnp.zeros((ROW_SUB, WORD_LANE), jnp.float32)
        acc_lo, acc_hi = [z, z], [z, z]
        for m in range(per_way):
            for jj in range(IDX_WAYS // 2):
                p = m * IDX_WAYS + 2 * jj
                words = _pair_tile(tab_ref, idx_refs[2 * jj][m * PEER_TB + it], idx_refs[2 * jj + 1][m * PEER_TB + it], upper)
                w2 = jnp.where(upper, wrep_ref[pl.ds(row0 + p, 1), pl.ds(col, WORD_LANE)],
                               wrep_ref[pl.ds(row0 + p + 1, 1), pl.ds(col, WORD_LANE)])
                k = jj % 2
                acc_lo[k] = acc_lo[k] + lax.bitcast_convert_type(words << 16, jnp.float32) * w2
                acc_hi[k] = acc_hi[k] + lax.bitcast_convert_type(words & HI_MASK, jnp.float32) * w2
        lo, hi = acc_lo[0] + acc_lo[1], acc_hi[0] + acc_hi[1]
        peer_ref[it] = jnp.concatenate([lo[:WORD_SUB] + lo[WORD_SUB:], hi[:WORD_SUB] + hi[WORD_SUB:]], axis=0)

    xs = [jnp.where(low_lanes, w, w_swapped), jnp.where(low_lanes, w_swapped, w)]
    wcat = jnp.concatenate([_split3_lanes(jnp.concatenate([x, x], axis=1), PEER_HALF) for x in xs], axis=0)
    wrep_ref[...] = jnp.dot(wcat, sel_ref[...], preferred_element_type=jnp.float32)

    for half in range(PEER_TB // PEER_HALF):
        def tok_body(i, carry, first=half * PEER_HALF):
            for j in range(PEER_TOK_UNROLL):
                one_token(PEER_TOK_UNROLL * i + j, first)
            return carry

        lax.fori_loop(0, PEER_HALF // PEER_TOK_UNROLL, tok_body, 0)

    peer = jnp.concatenate([peer_ref[:, c, :] for c in range(ROW_SUB)], axis=-1)
    o_ref[...] = _rms(x1_ref[...] + gate2_ref[...] * peer) * fg_ref[...]


_PEER_VMEM_LIMIT = 56 * 1024 * 1024


def _peer_experts(hf, experts, gates, u_tab, v_tab, x1, gate2, final_g):
    t = hf.shape[0]
    bsz, s, d = x1.shape
    assert s % PEER_TB == 0
    nblk = t // PEER_TB
    per_way = PEER_PAIRS // IDX_WAYS
    smem_spec = pl.BlockSpec((PEER_TB * per_way,), lambda i: (i,), memory_space=pltpu.SMEM)
    tab_spec = pl.BlockSpec(memory_space=pltpu.VMEM)
    row_spec = pl.BlockSpec((PEER_TB, ROW_SUB, WORD_LANE), lambda i: (i, 0, 0))
    a_spec = pl.BlockSpec((per_way, IDX_WAYS, PEER_TB), lambda i: (i, 0, 0))
    params = pltpu.CompilerParams(dimension_semantics=("arbitrary",), vmem_limit_bytes=_PEER_VMEM_LIMIT)
    assert IDX_WAYS == PEER_HEADS and PEER_TB == LANES
    idx_ways = [(experts[j] * WORD_SUB + TAB_PAD).reshape(per_way, nblk, PEER_TB).transpose(1, 0, 2).reshape(-1)
                for j in range(IDX_WAYS)]
    a_t = pl.pallas_call(
        _peer_u_kernel,
        grid=(nblk,),
        in_specs=[smem_spec] * IDX_WAYS + [row_spec, tab_spec],
        out_specs=a_spec,
        out_shape=jax.ShapeDtypeStruct((nblk * per_way, IDX_WAYS, PEER_TB), jnp.float32),
        compiler_params=params, name="peer_u",
    )(*idx_ways, hf, _pack_table(u_tab))
    eye = np.repeat(np.eye(PEER_HALF, dtype=np.float32), WORD_LANE, axis=1)
    sel = jnp.asarray(np.concatenate([eye, eye, eye, np.zeros_like(eye)], axis=0), jnp.bfloat16)
    return pl.pallas_call(
        _peer_v_kernel,
        grid=(nblk,),
        in_specs=[smem_spec] * IDX_WAYS + [a_spec, pl.BlockSpec((IDX_WAYS, per_way, PEER_TB), lambda i: (0, 0, i)),
                                           tab_spec, pl.BlockSpec(sel.shape, lambda i: (0, 0)),
                                           pl.BlockSpec((PEER_TB, d), lambda i: (i, 0)),
                                           pl.BlockSpec((None, 1, d), lambda i: ((i * PEER_TB) // s, 0, 0)),
                                           pl.BlockSpec((1, d), lambda i: (0, 0))],
        out_specs=pl.BlockSpec((PEER_TB, d), lambda i: (i, 0)),
        out_shape=jax.ShapeDtypeStruct((t, d), jnp.float32),
        scratch_shapes=[pltpu.VMEM((PEER_TB // PEER_HALF * PEER_PAIRS, PEER_HALF * WORD_LANE), jnp.float32),
                        pltpu.VMEM((PEER_TB, ROW_SUB, WORD_LANE), jnp.float32)],
        compiler_params=params, name="peer_v",
    )(*idx_ways, a_t, gates, _pack_table(v_tab), sel, x1.reshape(t, d), gate2, final_g.reshape(1, d)).reshape(bsz, s, d)


def _in_proj_weights(w_in):
    o = np.cumsum([0, GLA_QK_WIDTH, GLA_QK_WIDTH, GLA_WIDTH, GLA_GATE_RANK, GLA_WIDTH, SWA_WIDTH, SWA_KV_WIDTH])
    src = np.full((PROJ_W,), IN_WIDTH, np.int32)
    for dst, start, width in ((COL_QG, o[0], GLA_QK_WIDTH), (COL_KG, o[1], GLA_QK_WIDTH), (COL_VG, o[2], GLA_WIDTH),
                              (COL_ALR, o[3], GLA_GATE_RANK), (COL_RG, o[4], GLA_WIDTH), (COL_QS, o[5], SWA_WIDTH),
                              (COL_KS, o[6], SWA_KV_WIDTH), (COL_VS, o[7], SWA_KV_WIDTH)):
        src[dst:dst + width] = start + np.arange(width)
    w_ext = jnp.concatenate([w_in, jnp.zeros((w_in.shape[0], 1), w_in.dtype)], axis=1)
    return w_ext[:, src].astype(jnp.bfloat16)


def kernel(x, c, w_ada, b_ada, norm1_g, norm2_g, w_in, w_gla_alpha, b_gla_alpha, gla_norm_g,
           swa_sinks, swa_norm_g, w_out, w_peer_q, peer_keys_1, peer_keys_2, peer_u, peer_v, final_g):
    assert w_ada.shape[0] == 1, "single-layer block"
    l = 0
    mod = _ada(c, w_ada[l], b_ada[l])
    shift1, scale1, gate1, shift2, scale2, gate2 = [m[:, None, :] for m in jnp.split(mod, 6, axis=-1)]
    proj = _in_proj(x, norm1_g[l], shift1, scale1, _in_proj_weights(w_in[l]))
    o_gla = _gla(proj, w_gla_alpha[l], b_gla_alpha[l], gla_norm_g[l])
    o_swa = _swa(proj, swa_sinks[l], swa_norm_g[l])
    x1, h2 = _out_proj(x, o_gla, o_swa, w_out[l].astype(jnp.bfloat16), gate1, norm2_g[l], shift2, scale2)
    return peer_mixer(h2, w_peer_q[l], peer_keys_1[l], peer_keys_2[l], peer_u[l], peer_v[l], x1, gate2, final_g)
```

```python
import jax, jax.numpy as jnp
from jax import lax
import numpy as np
from jax.experimental import pallas as pl
from jax.experimental.pallas import tpu as pltpu

D_MODEL = 1024

GLA_HEADS = 4
GLA_DK = 64
GLA_DV = 128
GLA_GATE_RANK = 16
GLA_GATE_TEMP = 16.0
SWA_HEADS = 8
SWA_KV_HEADS = 2
SWA_HEAD_DIM = 64
SWA_WINDOW = 128
SWA_BLOCK = 128
PEER_HEADS = 8
PEER_N_KEYS = 128
PEER_N_EXPERTS = PEER_N_KEYS * PEER_N_KEYS
PEER_QUERY_DIM = 256
PEER_TOPK = 16
RMS_EPS = 1e-6

GLA_QK_WIDTH = GLA_HEADS * GLA_DK
GLA_WIDTH = GLA_HEADS * GLA_DV
SWA_WIDTH = SWA_HEADS * SWA_HEAD_DIM
SWA_KV_WIDTH = SWA_KV_HEADS * SWA_HEAD_DIM
MIX_WIDTH = GLA_WIDTH + SWA_WIDTH
IN_WIDTH = 2 * GLA_QK_WIDTH + GLA_WIDTH + GLA_GATE_RANK + GLA_WIDTH + SWA_WIDTH + 2 * SWA_KV_WIDTH


LANES, SUBLANES = 128, 8
MIB = 1024 * 1024
PROJ_VMEM_LIMIT = 48 * MIB
PEER_VMEM_LIMIT = 56 * MIB
NEG_INF = float("-inf")
NT_DIMS = (((1,), (1,)), ((), ()))
ROW_TILE = 512
SWA_PAIR = 2 * SWA_HEAD_DIM
ALR_W = LANES
COL_QS = 0
COL_QG = COL_QS + SWA_WIDTH
COL_KG = COL_QG + GLA_QK_WIDTH
COL_VG = COL_KG + GLA_QK_WIDTH
COL_RG = COL_VG + GLA_WIDTH
COL_KS = COL_RG + GLA_WIDTH
COL_VS = COL_KS + SWA_KV_WIDTH
COL_ALR = COL_VS + SWA_KV_WIDTH
PROJ_W = COL_ALR + ALR_W


def _rms(x):
    return x * lax.rsqrt(jnp.mean(x * x, axis=-1, keepdims=True) + RMS_EPS)


def _ada_kernel(c_ref, w_ref, b_ref, o_ref):
    c = c_ref[...]
    ca = (c * jax.nn.sigmoid(c)).astype(jnp.bfloat16)
    o_ref[...] = jnp.dot(ca, w_ref[...].astype(jnp.bfloat16), preferred_element_type=jnp.float32) + b_ref[...]


def _ada(c, w, b):
    bsz, d = c.shape
    n = w.shape[1]
    tn = D_MODEL
    return pl.pallas_call(
        _ada_kernel,
        grid=(n // tn,),
        in_specs=[pl.BlockSpec((bsz, d), lambda j: (0, 0)), pl.BlockSpec((d, tn), lambda j: (0, j)),
                  pl.BlockSpec((1, tn), lambda j: (0, j))],
        out_specs=pl.BlockSpec((bsz, tn), lambda j: (0, j)),
        out_shape=jax.ShapeDtypeStruct((bsz, n), jnp.float32), name="ada",
    )(c, w, b.reshape(1, n))


def _in_proj_kernel(x_ref, g_ref, sh_ref, sc_ref, w_ref, o_ref):
    h = _rms(x_ref[...]) * g_ref[...] * (1.0 + sc_ref[...]) + sh_ref[...]
    o_ref[...] = jnp.dot(h.astype(jnp.bfloat16), w_ref[...], preferred_element_type=jnp.float32)


def _in_proj(x, g, shift, scale, w):
    bsz, s, d = x.shape
    n = w.shape[1]
    row = lambda width: pl.BlockSpec((None, ROW_TILE, width), lambda b, i: (b, i, 0))
    per_b = pl.BlockSpec((None, 1, d), lambda b, i: (b, 0, 0))
    return pl.pallas_call(
        _in_proj_kernel,
        grid=(bsz, s // ROW_TILE),
        in_specs=[row(d), pl.BlockSpec((1, d), lambda b, i: (0, 0)), per_b, per_b,
                  pl.BlockSpec((d, n), lambda b, i: (0, 0))],
        out_specs=row(n),
        out_shape=jax.ShapeDtypeStruct((bsz, s, n), jnp.float32),
        compiler_params=pltpu.CompilerParams(dimension_semantics=("arbitrary", "arbitrary"),
                                             vmem_limit_bytes=PROJ_VMEM_LIMIT),
        name="in_proj",
    )(x, g.reshape(1, d), shift, scale, w)


def _swa_kernel(sink_ref, q_ref, kp_ref, kc_ref, vp_ref, vc_ref, g_ref, o_ref):
    f32, bf16 = jnp.float32, jnp.bfloat16
    blk = SWA_BLOCK
    k2 = jnp.concatenate([kp_ref[...], kc_ref[...]], axis=0)
    v2 = jnp.concatenate([vp_ref[...], vc_ref[...]], axis=0)
    low = lax.broadcasted_iota(jnp.int32, k2.shape, 1) < SWA_HEAD_DIM
    k2r, v2r = pltpu.roll(k2, SWA_HEAD_DIM, 1), pltpu.roll(v2, SWA_HEAD_DIM, 1)
    kdup = [jnp.where(low, k2, k2r).astype(bf16), jnp.where(low, k2r, k2).astype(bf16)]
    vdup = [jnp.where(low, v2, v2r).astype(bf16), jnp.where(low, v2r, v2).astype(bf16)]
    qi = lax.broadcasted_iota(jnp.int32, (blk, 2 * blk), 0)
    kj = lax.broadcasted_iota(jnp.int32, (blk, 2 * blk), 1)
    dist = qi + blk - kj
    valid = (dist >= 0) & (dist < SWA_WINDOW) & ((kj >= blk) | (pl.program_id(1) > 0))
    distf = dist.astype(f32)
    first = lax.broadcasted_iota(jnp.int32, (blk, SWA_PAIR), 1) < SWA_HEAD_DIM
    grp = SWA_HEADS // SWA_KV_HEADS
    heads = range(SWA_HEADS)
    scores = []
    for h in heads:
        qp = q_ref[:, (h // 2) * SWA_PAIR:(h // 2 + 1) * SWA_PAIR] * (SWA_HEAD_DIM ** -0.5)
        q = jnp.where(first if h % 2 == 0 else ~first, qp, 0.0).astype(bf16)
        scores.append(lax.dot_general(q, kdup[h // grp], NT_DIMS, preferred_element_type=f32))
    probs = []
    for h in heads:
        sc = scores[h] - (2.0 ** (-8.0 * (h + 1) / SWA_HEADS)) * distf
        sc = jnp.where(valid, sc, NEG_INF)
        sink = sink_ref[h]
        m = jnp.maximum(jnp.max(sc, axis=-1, keepdims=True), sink)
        p = jnp.exp(sc - m)
        probs.append((p / (jnp.sum(p, axis=-1, keepdims=True) + jnp.exp(sink - m))).astype(bf16))
    outs = [jnp.dot(probs[h], vdup[h // grp], preferred_element_type=f32) for h in heads]
    pairs = [jnp.where(first, outs[2 * i], outs[2 * i + 1]) for i in range(SWA_HEADS // 2)]
    ssq = sum(jnp.sum(o * o, axis=-1, keepdims=True) for o in pairs)
    inv = lax.rsqrt(ssq * (1.0 / SWA_WIDTH) + RMS_EPS)
    for i, o in enumerate(pairs):
        o_ref[:, i * SWA_PAIR:(i + 1) * SWA_PAIR] = o * inv * g_ref[:, i * SWA_PAIR:(i + 1) * SWA_PAIR]


def _swa(proj, sinks, norm_g):
    bsz, s, _ = proj.shape
    col = lambda width, off, prev: pl.BlockSpec(
        (None, SWA_BLOCK, width), lambda b, n: (b, jnp.maximum(n - 1, 0) if prev else n, off // width))
    return pl.pallas_call(
        _swa_kernel,
        grid=(bsz, s // SWA_BLOCK),
        in_specs=[pl.BlockSpec(memory_space=pltpu.SMEM), col(SWA_WIDTH, COL_QS, False),
                  col(SWA_KV_WIDTH, COL_KS, True), col(SWA_KV_WIDTH, COL_KS, False),
                  col(SWA_KV_WIDTH, COL_VS, True), col(SWA_KV_WIDTH, COL_VS, False),
                  pl.BlockSpec((1, SWA_WIDTH), lambda b, n: (0, 0))],
        out_specs=pl.BlockSpec((None, SWA_BLOCK, SWA_WIDTH), lambda b, n: (b, n, 0)),
        out_shape=jax.ShapeDtypeStruct((bsz, s, SWA_WIDTH), jnp.float32),
        compiler_params=pltpu.CompilerParams(dimension_semantics=("arbitrary", "arbitrary")),
        name="swa",
    )(sinks, proj, proj, proj, proj, proj, norm_g.reshape(1, -1))


def _out_proj_kernel(x_ref, og_ref, os_ref, w_ref, gate_ref, g2_ref, sh_ref, sc_ref, x1_ref, h2_ref):
    mix = jnp.concatenate([og_ref[...], os_ref[...]], axis=-1).astype(jnp.bfloat16)
    x1 = x_ref[...] + gate_ref[...] * jnp.dot(mix, w_ref[...], preferred_element_type=jnp.float32)
    x1_ref[...] = x1
    h2 = _rms(x1) * g2_ref[...] * (1.0 + sc_ref[...]) + sh_ref[...]
    for c in range(ROW_SUB):
        h2_ref[:, c, :] = h2[:, c * LANES:(c + 1) * LANES]


def _out_proj(x, o_gla, o_swa, w, gate1, g2, shift2, scale2):
    bsz, s, d = x.shape
    row = lambda width: pl.BlockSpec((None, ROW_TILE, width), lambda b, i: (b, i, 0))
    per_b = pl.BlockSpec((None, 1, d), lambda b, i: (b, 0, 0))
    shp = jax.ShapeDtypeStruct((bsz, s, d), jnp.float32)
    return pl.pallas_call(
        _out_proj_kernel,
        grid=(bsz, s // ROW_TILE),
        in_specs=[row(d), row(GLA_WIDTH), row(SWA_WIDTH), pl.BlockSpec((MIX_WIDTH, d), lambda b, i: (0, 0)),
                  per_b, pl.BlockSpec((1, d), lambda b, i: (0, 0)), per_b, per_b],
        out_specs=[row(d), pl.BlockSpec((None, ROW_TILE, ROW_SUB, d // ROW_SUB), lambda b, i: (b, i, 0, 0))],
        out_shape=[shp, jax.ShapeDtypeStruct((bsz, s, ROW_SUB, d // ROW_SUB), jnp.float32)],
        compiler_params=pltpu.CompilerParams(dimension_semantics=("arbitrary", "arbitrary"),
                                             vmem_limit_bytes=PROJ_VMEM_LIMIT),
        name="out_proj",
    )(x, o_gla, o_swa, w, gate1, g2.reshape(1, d), shift2, scale2)


GLA_SUB = 16
GLA_BLOCK = 128
GLA_PAIR = 2 * GLA_DK
GLA_NB = 1


def _gla_kernel(q_ref, k_ref, v_ref, alr_ref, r_ref, wa_ref, ba_ref, g_ref, tri_ref, o_ref, st_ref, b_ref, q2_ref):
    f32, bf16 = jnp.float32, jnp.bfloat16

    @pl.when(pl.program_id(1) == 0)
    def _():
        st_ref[...] = jnp.zeros(st_ref.shape, f32)

    for bb in range(GLA_NB):
        z = jnp.dot(alr_ref[bb].astype(bf16), wa_ref[...], preferred_element_type=f32) + ba_ref[...]
        log_a = -(jnp.maximum(-z, 0.0) + jnp.log1p(jnp.exp(-jnp.abs(z)))) * (1.0 / GLA_GATE_TEMP)
        b_ref[bb] = jnp.dot(tri_ref[...], log_a, preferred_element_type=f32, precision=lax.Precision.HIGHEST)
        q2_ref[bb] = q_ref[bb] * (GLA_DK ** -0.5)
    lane = lax.broadcasted_iota(jnp.int32, (GLA_SUB, GLA_PAIR), 1)
    sub = lax.broadcasted_iota(jnp.int32, (GLA_SUB, GLA_PAIR), 0)
    first = lane < GLA_DK
    lane_s = lax.broadcasted_iota(jnp.int32, (GLA_DV, GLA_PAIR), 1) < GLA_DK

    chains = [(bb, hp) for bb in range(GLA_NB) for hp in range(GLA_HEADS // 2)]
    states = [st_ref[i] for i in range(len(chains))]
    for c in range(GLA_BLOCK // GLA_SUB):
        rows = pl.ds(c * GLA_SUB, GLA_SUB)
        for ci, (bb, hp) in enumerate(chains):
            cols = pl.ds(hp * GLA_PAIR, GLA_PAIR)
            state = states[ci]
            b = b_ref[bb, rows, cols]
            b_last = b_ref[bb, pl.ds(c * GLA_SUB + GLA_SUB - 1, 1), cols]
            q2, k2 = q2_ref[bb, rows, cols], k_ref[bb, rows, cols]
            qe = q2 * jnp.exp(b)
            kd = (k2 * jnp.exp(b_last - b)).astype(bf16)
            sb = state.astype(bf16)
            vcols2 = [pl.ds((2 * hp + j) * GLA_DV, GLA_DV) for j in range(2)]
            intra = [jnp.zeros((GLA_SUB, GLA_DV), f32), jnp.zeros((GLA_SUB, GLA_DV), f32)]
            for s_ in range(GLA_SUB):
                row = pl.ds(c * GLA_SUB + s_, 1)
                p = (k_ref[bb, row, cols] * q2) * jnp.exp(b - b_ref[bb, row, cols])
                keep = sub[:, :1] >= s_
                col_a = jnp.where(keep, jnp.sum(jnp.where(first, p, 0.0), axis=1, keepdims=True), 0.0)
                col_b = jnp.where(keep, jnp.sum(jnp.where(first, 0.0, p), axis=1, keepdims=True), 0.0)
                intra[0] = intra[0] + col_a * v_ref[bb, row, vcols2[0]]
                intra[1] = intra[1] + col_b * v_ref[bb, row, vcols2[1]]
            ut = []
            for j in range(2):
                vcols = vcols2[j]
                vh = v_ref[bb, rows, vcols]
                qm = jnp.where(first if j == 0 else ~first, qe, 0.0).astype(bf16)
                o = lax.dot_general(qm, sb, NT_DIMS, preferred_element_type=f32) + intra[j]
                ms = jnp.mean(o * o, axis=1, keepdims=True)
                r = r_ref[bb, rows, vcols]
                o_ref[bb, rows, vcols] = (o * lax.rsqrt(ms + RMS_EPS) * g_ref[:, vcols]) * (r * jax.nn.sigmoid(r))
                ut.append(jnp.dot(vh.T.astype(bf16), kd, preferred_element_type=f32))
            states[ci] = jnp.exp(b_last) * state + jnp.where(lane_s, ut[0], ut[1])
    for ci in range(len(chains)):
        st_ref[ci] = states[ci]


def _gla_tri():
    r = np.arange(GLA_BLOCK)
    return jnp.asarray(((r[:, None] // GLA_SUB == r[None, :] // GLA_SUB) & (r[None, :] <= r[:, None])).astype(np.float32))


def _gla(proj, w_alpha, b_alpha, norm_g):
    bsz, s, _ = proj.shape
    rank = w_alpha.shape[0]
    wa = jnp.pad(w_alpha, ((0, ALR_W - rank), (0, 0))).astype(jnp.bfloat16)
    col = lambda width, off: pl.BlockSpec((GLA_NB, GLA_BLOCK, width), lambda b, i: (b, i, off // width))
    const = lambda shape: pl.BlockSpec(shape, lambda b, i: (0, 0))
    return pl.pallas_call(
        _gla_kernel,
        grid=(bsz // GLA_NB, s // GLA_BLOCK),
        in_specs=[col(GLA_QK_WIDTH, COL_QG), col(GLA_QK_WIDTH, COL_KG), col(GLA_WIDTH, COL_VG), col(ALR_W, COL_ALR),
                  col(GLA_WIDTH, COL_RG), const((ALR_W, GLA_QK_WIDTH)), const((1, GLA_QK_WIDTH)),
                  const((1, GLA_WIDTH)), const((GLA_BLOCK, GLA_BLOCK))],
        out_specs=pl.BlockSpec((GLA_NB, GLA_BLOCK, GLA_WIDTH), lambda b, i: (b, i, 0)),
        out_shape=jax.ShapeDtypeStruct((bsz, s, GLA_WIDTH), jnp.float32),
        scratch_shapes=[pltpu.VMEM((GLA_NB * GLA_HEADS // 2, GLA_DV, GLA_PAIR), jnp.float32),
                        pltpu.VMEM((GLA_NB, GLA_BLOCK, GLA_QK_WIDTH), jnp.float32),
                        pltpu.VMEM((GLA_NB, GLA_BLOCK, GLA_QK_WIDTH), jnp.float32)],
        compiler_params=pltpu.CompilerParams(dimension_semantics=("arbitrary", "arbitrary")),
        name="gla",
    )(proj, proj, proj, proj, proj, wa, b_alpha.reshape(1, -1), norm_g.reshape(1, -1), _gla_tri())


def peer_mixer(h_tiles, w_q, keys_1, keys_2, u_tab, v_tab, x1, gate2, final_g):
    bsz, s, rs, rl = h_tiles.shape
    ht = h_tiles.reshape(bsz * s, rs, rl)
    experts, gates = _peer_route(ht, w_q, keys_1, keys_2)
    return _peer_experts(ht, experts, gates, u_tab, v_tab, x1, gate2, final_g)


ROUTE_TT = 1024


def _top16_rows(s, code, big):
    vals, codes = [], []
    for _ in range(PEER_TOPK):
        m = jnp.max(s, axis=0, keepdims=True)
        c = jnp.min(jnp.where(s == m, code, big), axis=0, keepdims=True)
        s = jnp.where(code == c, NEG_INF, s)
        vals.append(m)
        codes.append(c)
    return vals, codes


def _peer_route_kernel(h_ref, wq_ref, k1_ref, k2_ref, e_ref, g_ref, q_ref):
    half = PEER_QUERY_DIM // 2
    tt = ROUTE_TT
    h = jnp.concatenate([h_ref[:, c, :] for c in range(h_ref.shape[1])], axis=-1)
    q_ref[...] = jnp.dot(h.astype(jnp.bfloat16), wq_ref[...], preferred_element_type=jnp.float32)
    key_id = lax.broadcasted_iota(jnp.int32, (PEER_N_KEYS, 2 * tt), 0).astype(jnp.float32)
    sub8 = lax.broadcasted_iota(jnp.int32, (SUBLANES, tt), 0)
    id_bits = PEER_N_EXPERTS.bit_length() - 1

    def head_body(hd, carry):
        off = pl.multiple_of(hd * PEER_QUERY_DIM, PEER_QUERY_DIM)
        q1 = q_ref[:, pl.ds(off, half)].astype(jnp.bfloat16)
        q2 = q_ref[:, pl.ds(off + half, half)].astype(jnp.bfloat16)
        s1 = lax.dot_general(k1_ref[...], q1, NT_DIMS, preferred_element_type=jnp.float32)
        s2 = lax.dot_general(k2_ref[...], q2, NT_DIMS, preferred_element_type=jnp.float32)
        vals, ids = _top16_rows(jnp.concatenate([s1, s2], axis=1), key_id, float(PEER_N_KEYS))
        v = jnp.concatenate(vals, axis=0)
        ix = jnp.concatenate(ids, axis=0).astype(jnp.int32)
        v1, v2, i1, i2 = v[:, :tt], v[:, tt:], ix[:, :tt], ix[:, tt:]
        cand, code = [], []
        for a, b0 in [(0, 0), (0, 8), (1, 0), (2, 0), (3, 0), (4, 0), (5, 0), (6, 0), (7, 0)]:
            cand.append(v1[a:a + 1] + v2[b0:b0 + 8])
            flat = a * PEER_TOPK + b0 + sub8
            code.append((flat << id_bits) | (i1[a:a + 1] * PEER_N_KEYS + i2[b0:b0 + 8]))
        cand.append(v1[8:16] + v2[0:1])
        code.append((((8 + sub8) * PEER_TOPK) << id_bits) | (i1[8:16] * PEER_N_KEYS + i2[0:1]))
        sc, cc = _top16_rows(jnp.concatenate(cand, axis=0), jnp.concatenate(code, axis=0).astype(jnp.float32),
                             float(1 << 30))
        sc = jnp.concatenate(sc, axis=0)
        ex = jnp.exp(sc - sc[0:1])
        g_ref[hd] = ex / jnp.sum(ex, axis=0, keepdims=True)
        e_ref[hd] = jnp.concatenate(cc, axis=0).astype(jnp.int32) & (PEER_N_EXPERTS - 1)
        return carry

    lax.fori_loop(0, PEER_HEADS, head_body, 0)


def _peer_route(hf, w_q, keys_1, keys_2):
    t, rs, rl = hf.shape
    d = rs * rl
    qw = PEER_HEADS * PEER_QUERY_DIM
    const = lambda shape: pl.BlockSpec(shape, lambda i: tuple(0 for _ in shape))
    out_spec = pl.BlockSpec((PEER_HEADS, PEER_TOPK, ROUTE_TT), lambda i: (0, 0, i))
    e_t, g_t = pl.pallas_call(
        _peer_route_kernel,
        grid=(t // ROUTE_TT,),
        in_specs=[pl.BlockSpec((ROUTE_TT, rs, rl), lambda i: (i, 0, 0)), const((d, qw)),
                  const(keys_1.shape), const(keys_2.shape)],
        out_specs=[out_spec, out_spec],
        out_shape=[jax.ShapeDtypeStruct((PEER_HEADS, PEER_TOPK, t), jnp.int32),
                   jax.ShapeDtypeStruct((PEER_HEADS, PEER_TOPK, t), jnp.float32)],
        scratch_shapes=[pltpu.VMEM((ROUTE_TT, qw), jnp.float32)],
        compiler_params=pltpu.CompilerParams(dimension_semantics=("arbitrary",), vmem_limit_bytes=PROJ_VMEM_LIMIT),
        name="peer_route",
    )(hf, w_q.astype(jnp.bfloat16), keys_1.astype(jnp.bfloat16), keys_2.astype(jnp.bfloat16))
    return e_t, g_t


PEER_PAIRS = PEER_HEADS * PEER_TOPK
PEER_TB = 128
PEER_HALF = 64
PEER_TOK_UNROLL = 4
HALF_D = D_MODEL // 2
WORD_SUB, WORD_LANE = HALF_D // 128, 128
ROW_SUB = 2 * WORD_SUB
HI_MASK = -65536
IDX_WAYS = 8


PACK_EB = 512
TAB_PAD = PACK_EB * WORD_SUB


def _pack_kernel(x_ref, o_ref):
    i = pl.program_id(0)

    @pl.when((i == 0) | (i == pl.num_programs(0) - 1))
    def _():
        o_ref[...] = jnp.zeros(o_ref.shape, o_ref.dtype)

    @pl.when((i > 0) & (i < pl.num_programs(0) - 1))
    def _():
        x = x_ref[...].astype(jnp.bfloat16).astype(jnp.float32)
        bits = lax.bitcast_convert_type(x, jnp.int32)
        word = lax.shift_right_logical(bits[:, :HALF_D], 16) | (bits[:, HALF_D:] & HI_MASK)
        for c in range(WORD_SUB):
            o_ref[:, c, :] = word[:, c * WORD_LANE:(c + 1) * WORD_LANE]


def _pack_table(tab):
    e, d = tab.shape
    nblk = e // PACK_EB
    out = pl.pallas_call(
        _pack_kernel,
        grid=(nblk + 2,),
        in_specs=[pl.BlockSpec((PACK_EB, d), lambda i: (jnp.clip(i - 1, 0, nblk - 1), 0))],
        out_specs=pl.BlockSpec((PACK_EB, WORD_SUB, WORD_LANE), lambda i: (i, 0, 0)),
        out_shape=jax.ShapeDtypeStruct(((nblk + 2) * PACK_EB, WORD_SUB, WORD_LANE), jnp.int32),
        name="pack_table",
    )(tab)
    return out.reshape((nblk + 2) * TAB_PAD, WORD_LANE)


def _pair_tile(tab_ref, ia, ib, upper):
    wa = tab_ref[pl.ds(pl.multiple_of(ia, WORD_SUB), ROW_SUB), :]
    wb = tab_ref[pl.ds(pl.multiple_of(ib - WORD_SUB, WORD_SUB), ROW_SUB), :]
    return jnp.where(upper, wa, wb)


_ORDER = (6, 2, 4, 0, 7, 3, 5, 1)


def _fold_pairs(ps, sub):
    b1 = (sub & 2) != 0
    w = []
    for i in range(2):
        x, y = ps[2 * i], ps[2 * i + 1]
        w.append(jnp.where(b1, x + pltpu.roll(x, 2, 0), y + pltpu.roll(y, 6, 0)))
    b0 = (sub & 1) != 0
    x, y = w
    return jnp.where(b0, x + pltpu.roll(x, 1, 0), y + pltpu.roll(y, 7, 0))


def _peer_u_kernel(*refs):
    idx_refs = refs[:IDX_WAYS]
    h_ref, tab_ref, a_ref = refs[IDX_WAYS:]
    sub = lax.broadcasted_iota(jnp.int32, (ROW_SUB, WORD_LANE), 0)
    lane = lax.broadcasted_iota(jnp.int32, (ROW_SUB, WORD_LANE), 1)
    n_groups = PEER_PAIRS // IDX_WAYS
    feed = [int(i) for i in np.argsort(np.array(_ORDER))]
    a_ref[...] = jnp.zeros(a_ref.shape, a_ref.dtype)

    def folded_tiles(t):
        h = h_ref[t]
        h_lo = jnp.concatenate([h[:WORD_SUB], h[:WORD_SUB]], axis=0)
        h_hi = jnp.concatenate([h[WORD_SUB:], h[WORD_SUB:]], axis=0)
        tiles = []
        for g in range(n_groups):
            ps = []
            for i in range(4):
                w = _pair_tile(tab_ref, idx_refs[feed[2 * i]][g * PEER_TB + t], idx_refs[feed[2 * i + 1]][g * PEER_TB + t],
                               sub < WORD_SUB)
                lo = lax.bitcast_convert_type(w << 16, jnp.float32)
                hi = lax.bitcast_convert_type(w & HI_MASK, jnp.float32)
                ps.append(lo * h_lo + hi * h_hi)
            tiles.append(_fold_pairs(ps, sub))
        return tuple(tiles)

    def lane_sums(t, tiles):
        here = lane == t
        for g in range(n_groups):
            a_ref[g] = jnp.where(here, jnp.sum(tiles[g], axis=1, keepdims=True), a_ref[g])

    def tok_body(t, prev):
        lane_sums(t - 1, prev)
        return folded_tiles(t)

    last = lax.fori_loop(1, PEER_TB, tok_body, folded_tiles(0))
    lane_sums(PEER_TB - 1, last)


def _split3_lanes(x, tb):
    mask = lambda v: lax.bitcast_convert_type(lax.bitcast_convert_type(v, jnp.int32) & HI_MASK, jnp.float32)
    hi = mask(x)
    mid = mask(x - hi)
    lo = x - hi - mid
    part = lax.broadcasted_iota(jnp.int32, x.shape, 1) // tb
    return jnp.where(part == 0, hi, jnp.where(part == 1, mid, jnp.where(part == 2, lo, 0.0))).astype(jnp.bfloat16)


def _peer_v_kernel(*refs):
    idx_refs = refs[:IDX_WAYS]
    a_ref, g_ref, tab_ref, sel_ref, x1_ref, gate2_ref, fg_ref, o_ref, wrep_ref, peer_ref = refs[IDX_WAYS:]
    per_way = PEER_PAIRS // IDX_WAYS
    a = a_ref[...].reshape(PEER_PAIRS, PEER_TB)
    gates = jnp.concatenate([g_ref[:, m, :] for m in range(per_way)], axis=0)
    w = gates * (0.5 * a * (1.0 + lax.erf(a * (2.0 ** -0.5))))
    w_swapped = pltpu.roll(w, PEER_HALF, 1)
    low_lanes = lax.broadcasted_iota(jnp.int32, w.shape, 1) < PEER_HALF
    upper = lax.broadcasted_iota(jnp.int32, (ROW_SUB, WORD_LANE), 0) < WORD_SUB

    def one_token(tl, first):
        it = first + tl
        row0 = (first // PEER_HALF) * PEER_PAIRS
        col = pl.multiple_of(tl * WORD_LANE, WORD_LANE)
        z = jnp.zeros((ROW_SUB, WORD_LANE), jnp.float32)
        acc_lo, acc_hi = [z, z], [z, z]
        for m in range(per_way):
            for jj in range(IDX_WAYS // 2):
                p = m * IDX_WAYS + 2 * jj
                words = _pair_tile(tab_ref, idx_refs[2 * jj][m * PEER_TB + it], idx_refs[2 * jj + 1][m * PEER_TB + it], upper)
                w2 = jnp.where(upper, wrep_ref[pl.ds(row0 + p, 1), pl.ds(col, WORD_LANE)],
                               wrep_ref[pl.ds(row0 + p + 1, 1), pl.ds(col, WORD_LANE)])
                k = jj % 2
                acc_lo[k] = acc_lo[k] + lax.bitcast_convert_type(words << 16, jnp.float32) * w2
                acc_hi[k] = acc_hi[k] + lax.bitcast_convert_type(words & HI_MASK, jnp.float32) * w2
        lo, hi = acc_lo[0] + acc_lo[1], acc_hi[0] + acc_hi[1]
        peer_ref[it] = jnp.concatenate([lo[:WORD_SUB] + lo[WORD_SUB:], hi[:WORD_SUB] + hi[WORD_SUB:]], axis=0)

    xs = [jnp.where(low_lanes, w, w_swapped), jnp.where(low_lanes, w_swapped, w)]
    wcat = jnp.concatenate([_split3_lanes(jnp.concatenate([x, x], axis=1), PEER_HALF) for x in xs], axis=0)
    wrep_ref[...] = jnp.dot(wcat, sel_ref[...], preferred_element_type=jnp.float32)

    for half in range(PEER_TB // PEER_HALF):
        def tok_body(i, carry, first=half * PEER_HALF):
            for j in range(PEER_TOK_UNROLL):
                one_token(PEER_TOK_UNROLL * i + j, first)
            return carry

        lax.fori_loop(0, PEER_HALF // PEER_TOK_UNROLL, tok_body, 0)

    peer = jnp.concatenate([peer_ref[:, c, :] for c in range(ROW_SUB)], axis=-1)
    o_ref[...] = _rms(x1_ref[...] + gate2_ref[...] * peer) * fg_ref[...]


def _peer_experts(hf, experts, gates, u_tab, v_tab, x1, gate2, final_g):
    t = hf.shape[0]
    bsz, s, d = x1.shape
    assert s % PEER_TB == 0
    nblk = t // PEER_TB
    per_way = PEER_PAIRS // IDX_WAYS
    smem_spec = pl.BlockSpec((PEER_TB * per_way,), lambda i: (i,), memory_space=pltpu.SMEM)
    tab_spec = pl.BlockSpec(memory_space=pltpu.VMEM)
    row_spec = pl.BlockSpec((PEER_TB, ROW_SUB, WORD_LANE), lambda i: (i, 0, 0))
    a_spec = pl.BlockSpec((per_way, IDX_WAYS, PEER_TB), lambda i: (i, 0, 0))
    params = pltpu.CompilerParams(dimension_semantics=("arbitrary",), vmem_limit_bytes=PEER_VMEM_LIMIT)
    assert IDX_WAYS == PEER_HEADS and PEER_TB == LANES
    idx_ways = [(experts[j] * WORD_SUB + TAB_PAD).reshape(per_way, nblk, PEER_TB).transpose(1, 0, 2).reshape(-1)
                for j in range(IDX_WAYS)]
    a_t = pl.pallas_call(
        _peer_u_kernel,
        grid=(nblk,),
        in_specs=[smem_spec] * IDX_WAYS + [row_spec, tab_spec],
        out_specs=a_spec,
        out_shape=jax.ShapeDtypeStruct((nblk * per_way, IDX_WAYS, PEER_TB), jnp.float32),
        compiler_params=params, name="peer_u",
    )(*idx_ways, hf, _pack_table(u_tab))
    eye = np.repeat(np.eye(PEER_HALF, dtype=np.float32), WORD_LANE, axis=1)
    sel = jnp.asarray(np.concatenate([eye, eye, eye, np.zeros_like(eye)], axis=0), jnp.bfloat16)
    return pl.pallas_call(
        _peer_v_kernel,
        grid=(nblk,),
        in_specs=[smem_spec] * IDX_WAYS + [a_spec, pl.BlockSpec((IDX_WAYS, per_way, PEER_TB), lambda i: (0, 0, i)),
                                           tab_spec, pl.BlockSpec(sel.shape, lambda i: (0, 0)),
                                           pl.BlockSpec((PEER_TB, d), lambda i: (i, 0)),
                                           pl.BlockSpec((None, 1, d), lambda i: ((i * PEER_TB) // s, 0, 0)),
                                           pl.BlockSpec((1, d), lambda i: (0, 0))],
        out_specs=pl.BlockSpec((PEER_TB, d), lambda i: (i, 0)),
        out_shape=jax.ShapeDtypeStruct((t, d), jnp.float32),
        scratch_shapes=[pltpu.VMEM((PEER_TB // PEER_HALF * PEER_PAIRS, PEER_HALF * WORD_LANE), jnp.float32),
                        pltpu.VMEM((PEER_TB, ROW_SUB, WORD_LANE), jnp.float32)],
        compiler_params=params, name="peer_v",
    )(*idx_ways, a_t, gates, _pack_table(v_tab), sel, x1.reshape(t, d), gate2, final_g.reshape(1, d)).reshape(bsz, s, d)


def _in_proj_weights(w_in):
    o = np.cumsum([0, GLA_QK_WIDTH, GLA_QK_WIDTH, GLA_WIDTH, GLA_GATE_RANK, GLA_WIDTH, SWA_WIDTH, SWA_KV_WIDTH])
    src = np.full((PROJ_W,), IN_WIDTH, np.int32)
    for dst, start, width in ((COL_QG, o[0], GLA_QK_WIDTH), (COL_KG, o[1], GLA_QK_WIDTH), (COL_VG, o[2], GLA_WIDTH),
                              (COL_ALR, o[3], GLA_GATE_RANK), (COL_RG, o[4], GLA_WIDTH), (COL_QS, o[5], SWA_WIDTH),
                              (COL_KS, o[6], SWA_KV_WIDTH), (COL_VS, o[7], SWA_KV_WIDTH)):
        src[dst:dst + width] = start + np.arange(width)
    w_ext = jnp.concatenate([w_in, jnp.zeros((w_in.shape[0], 1), w_in.dtype)], axis=1)
    return w_ext[:, src].astype(jnp.bfloat16)


def kernel(x, c, w_ada, b_ada, norm1_g, norm2_g, w_in, w_gla_alpha, b_gla_alpha, gla_norm_g,
           swa_sinks, swa_norm_g, w_out, w_peer_q, peer_keys_1, peer_keys_2, peer_u, peer_v, final_g):
    assert w_ada.shape[0] == 1, "single-layer block"
    l = 0
    mod = _ada(c, w_ada[l], b_ada[l])
    shift1, scale1, gate1, shift2, scale2, gate2 = [m[:, None, :] for m in jnp.split(mod, 6, axis=-1)]
    proj = _in_proj(x, norm1_g[l], shift1, scale1, _in_proj_weights(w_in[l]))
    o_gla = _gla(proj, w_gla_alpha[l], b_gla_alpha[l], gla_norm_g[l])
    o_swa = _swa(proj, swa_sinks[l], swa_norm_g[l])
    x1, h2 = _out_proj(x, o_gla, o_swa, w_out[l].astype(jnp.bfloat16), gate1, norm2_g[l], shift2, scale2)
    return peer_mixer(h2, w_peer_q[l], peer_keys_1[l], peer_keys_2[l], peer_u[l], peer_v[l], x1, gate2, final_g)
```

```python
import jax, jax.numpy as jnp
from jax import lax
import numpy as np
from jax.experimental import pallas as pl
from jax.experimental.pallas import tpu as pltpu

D_MODEL = 1024

GLA_HEADS = 4
GLA_DK = 64
GLA_DV = 128
GLA_GATE_RANK = 16
GLA_GATE_TEMP = 16.0
SWA_HEADS = 8
SWA_KV_HEADS = 2
SWA_HEAD_DIM = 64
SWA_WINDOW = 128
SWA_BLOCK = 128
PEER_HEADS = 8
PEER_N_KEYS = 128
PEER_N_EXPERTS = PEER_N_KEYS * PEER_N_KEYS
PEER_QUERY_DIM = 256
PEER_TOPK = 16
RMS_EPS = 1e-6

GLA_QK_WIDTH = GLA_HEADS * GLA_DK
GLA_WIDTH = GLA_HEADS * GLA_DV
SWA_WIDTH = SWA_HEADS * SWA_HEAD_DIM
SWA_KV_WIDTH = SWA_KV_HEADS * SWA_HEAD_DIM
MIX_WIDTH = GLA_WIDTH + SWA_WIDTH
IN_WIDTH = 2 * GLA_QK_WIDTH + GLA_WIDTH + GLA_GATE_RANK + GLA_WIDTH + SWA_WIDTH + 2 * SWA_KV_WIDTH


LANES, SUBLANES = 128, 8
MIB = 1024 * 1024
PROJ_VMEM_LIMIT = 48 * MIB
PEER_VMEM_LIMIT = 56 * MIB
NEG_INF = float("-inf")
NT_DIMS = (((1,), (1,)), ((), ()))
ROW_TILE = 1024
SWA_PAIR = 2 * SWA_HEAD_DIM
ALR_W = LANES
COL_QS = 0
COL_QG = COL_QS + SWA_WIDTH
COL_KG = COL_QG + GLA_QK_WIDTH
COL_VG = COL_KG + GLA_QK_WIDTH
COL_RG = COL_VG + GLA_WIDTH
COL_KS = COL_RG + GLA_WIDTH
COL_VS = COL_KS + SWA_KV_WIDTH
COL_ALR = COL_VS + SWA_KV_WIDTH
PROJ_W = COL_ALR + ALR_W


def _rms(x):
    return x * lax.rsqrt(jnp.mean(x * x, axis=-1, keepdims=True) + RMS_EPS)


def _ada_kernel(c_ref, w_ref, b_ref, o_ref):
    c = c_ref[...]
    ca = (c * jax.nn.sigmoid(c)).astype(jnp.bfloat16)
    o_ref[...] = jnp.dot(ca, w_ref[...].astype(jnp.bfloat16), preferred_element_type=jnp.float32) + b_ref[...]


def _ada(c, w, b):
    bsz, d = c.shape
    n = w.shape[1]
    tn = D_MODEL
    return pl.pallas_call(
        _ada_kernel,
        grid=(n // tn,),
        in_specs=[pl.BlockSpec((bsz, d), lambda j: (0, 0)), pl.BlockSpec((d, tn), lambda j: (0, j)),
                  pl.BlockSpec((1, tn), lambda j: (0, j))],
        out_specs=pl.BlockSpec((bsz, tn), lambda j: (0, j)),
        out_shape=jax.ShapeDtypeStruct((bsz, n), jnp.float32), name="ada",
    )(c, w, b.reshape(1, n))


def _in_proj_kernel(x_ref, g_ref, sh_ref, sc_ref, w_ref, o_ref):
    h = _rms(x_ref[...]) * g_ref[...] * (1.0 + sc_ref[...]) + sh_ref[...]
    o_ref[...] = jnp.dot(h.astype(jnp.bfloat16), w_ref[...], preferred_element_type=jnp.float32)


def _in_proj(x, g, shift, scale, w):
    bsz, s, d = x.shape
    n = w.shape[1]
    row = lambda width: pl.BlockSpec((None, ROW_TILE, width), lambda b, i: (b, i, 0))
    per_b = pl.BlockSpec((None, 1, d), lambda b, i: (b, 0, 0))
    return pl.pallas_call(
        _in_proj_kernel,
        grid=(bsz, s // ROW_TILE),
        in_specs=[row(d), pl.BlockSpec((1, d), lambda b, i: (0, 0)), per_b, per_b,
                  pl.BlockSpec((d, n), lambda b, i: (0, 0))],
        out_specs=row(n),
        out_shape=jax.ShapeDtypeStruct((bsz, s, n), jnp.float32),
        compiler_params=pltpu.CompilerParams(dimension_semantics=("arbitrary", "arbitrary"),
                                             vmem_limit_bytes=PROJ_VMEM_LIMIT),
        name="in_proj",
    )(x, g.reshape(1, d), shift, scale, w)


def _swa_kernel(sink_ref, q_ref, kp_ref, kc_ref, vp_ref, vc_ref, g_ref, o_ref):
    f32, bf16 = jnp.float32, jnp.bfloat16
    blk = SWA_BLOCK
    k2 = jnp.concatenate([kp_ref[...], kc_ref[...]], axis=0)
    v2 = jnp.concatenate([vp_ref[...], vc_ref[...]], axis=0)
    low = lax.broadcasted_iota(jnp.int32, k2.shape, 1) < SWA_HEAD_DIM
    k2r, v2r = pltpu.roll(k2, SWA_HEAD_DIM, 1), pltpu.roll(v2, SWA_HEAD_DIM, 1)
    kdup = [jnp.where(low, k2, k2r).astype(bf16), jnp.where(low, k2r, k2).astype(bf16)]
    vdup = [jnp.where(low, v2, v2r).astype(bf16), jnp.where(low, v2r, v2).astype(bf16)]
    qi = lax.broadcasted_iota(jnp.int32, (blk, 2 * blk), 0)
    kj = lax.broadcasted_iota(jnp.int32, (blk, 2 * blk), 1)
    dist = qi + blk - kj
    valid = (dist >= 0) & (dist < SWA_WINDOW) & ((kj >= blk) | (pl.program_id(1) > 0))
    distf = dist.astype(f32)
    first = lax.broadcasted_iota(jnp.int32, (blk, SWA_PAIR), 1) < SWA_HEAD_DIM
    grp = SWA_HEADS // SWA_KV_HEADS
    heads = range(SWA_HEADS)
    scores = []
    for h in heads:
        qp = q_ref[:, (h // 2) * SWA_PAIR:(h // 2 + 1) * SWA_PAIR] * (SWA_HEAD_DIM ** -0.5)
        q = jnp.where(first if h % 2 == 0 else ~first, qp, 0.0).astype(bf16)
        scores.append(lax.dot_general(q, kdup[h // grp], NT_DIMS, preferred_element_type=f32))
    probs = []
    for h in heads:
        sc = scores[h] - (2.0 ** (-8.0 * (h + 1) / SWA_HEADS)) * distf
        sc = jnp.where(valid, sc, NEG_INF)
        sink = sink_ref[h]
        m = jnp.maximum(jnp.max(sc, axis=-1, keepdims=True), sink)
        p = jnp.exp(sc - m)
        probs.append((p / (jnp.sum(p, axis=-1, keepdims=True) + jnp.exp(sink - m))).astype(bf16))
    outs = [jnp.dot(probs[h], vdup[h // grp], preferred_element_type=f32) for h in heads]
    pairs = [jnp.where(first, outs[2 * i], outs[2 * i + 1]) for i in range(SWA_HEADS // 2)]
    ssq = sum(jnp.sum(o * o, axis=-1, keepdims=True) for o in pairs)
    inv = lax.rsqrt(ssq * (1.0 / SWA_WIDTH) + RMS_EPS)
    for i, o in enumerate(pairs):
        o_ref[:, i * SWA_PAIR:(i + 1) * SWA_PAIR] = o * inv * g_ref[:, i * SWA_PAIR:(i + 1) * SWA_PAIR]


def _swa(proj, sinks, norm_g):
    bsz, s, _ = proj.shape
    col = lambda width, off, prev: pl.BlockSpec(
        (None, SWA_BLOCK, width), lambda b, n: (b, jnp.maximum(n - 1, 0) if prev else n, off // width))
    return pl.pallas_call(
        _swa_kernel,
        grid=(bsz, s // SWA_BLOCK),
        in_specs=[pl.BlockSpec(memory_space=pltpu.SMEM), col(SWA_WIDTH, COL_QS, False),
                  col(SWA_KV_WIDTH, COL_KS, True), col(SWA_KV_WIDTH, COL_KS, False),
                  col(SWA_KV_WIDTH, COL_VS, True), col(SWA_KV_WIDTH, COL_VS, False),
                  pl.BlockSpec((1, SWA_WIDTH), lambda b, n: (0, 0))],
        out_specs=pl.BlockSpec((None, SWA_BLOCK, SWA_WIDTH), lambda b, n: (b, n, 0)),
        out_shape=jax.ShapeDtypeStruct((bsz, s, SWA_WIDTH), jnp.float32),
        compiler_params=pltpu.CompilerParams(dimension_semantics=("arbitrary", "arbitrary")),
        name="swa",
    )(sinks, proj, proj, proj, proj, proj, norm_g.reshape(1, -1))


def _out_proj_kernel(x_ref, og_ref, os_ref, w_ref, gate_ref, g2_ref, sh_ref, sc_ref, x1_ref, h2_ref):
    mix = jnp.concatenate([og_ref[...], os_ref[...]], axis=-1).astype(jnp.bfloat16)
    x1 = x_ref[...] + gate_ref[...] * jnp.dot(mix, w_ref[...], preferred_element_type=jnp.float32)
    x1_ref[...] = x1
    h2 = _rms(x1) * g2_ref[...] * (1.0 + sc_ref[...]) + sh_ref[...]
    for c in range(ROW_SUB):
        h2_ref[:, c, :] = h2[:, c * LANES:(c + 1) * LANES]


def _out_proj(x, o_gla, o_swa, w, gate1, g2, shift2, scale2):
    bsz, s, d = x.shape
    row = lambda width: pl.BlockSpec((None, ROW_TILE, width), lambda b, i: (b, i, 0))
    per_b = pl.BlockSpec((None, 1, d), lambda b, i: (b, 0, 0))
    shp = jax.ShapeDtypeStruct((bsz, s, d), jnp.float32)
    return pl.pallas_call(
        _out_proj_kernel,
        grid=(bsz, s // ROW_TILE),
        in_specs=[row(d), row(GLA_WIDTH), row(SWA_WIDTH), pl.BlockSpec((MIX_WIDTH, d), lambda b, i: (0, 0)),
                  per_b, pl.BlockSpec((1, d), lambda b, i: (0, 0)), per_b, per_b],
        out_specs=[row(d), pl.BlockSpec((None, ROW_TILE, ROW_SUB, d // ROW_SUB), lambda b, i: (b, i, 0, 0))],
        out_shape=[shp, jax.ShapeDtypeStruct((bsz, s, ROW_SUB, d // ROW_SUB), jnp.float32)],
        compiler_params=pltpu.CompilerParams(dimension_semantics=("arbitrary", "arbitrary"),
                                             vmem_limit_bytes=PROJ_VMEM_LIMIT),
        name="out_proj",
    )(x, o_gla, o_swa, w, gate1, g2.reshape(1, d), shift2, scale2)


GLA_SUB = 16
GLA_BLOCK = 256
GLA_PAIR = 2 * GLA_DK
GLA_NB = 1


def _gla_kernel(q_ref, k_ref, v_ref, alr_ref, r_ref, wa_ref, ba_ref, g_ref, tri_ref, o_ref, st_ref, b_ref, q2_ref):
    f32, bf16 = jnp.float32, jnp.bfloat16

    @pl.when(pl.program_id(1) == 0)
    def _():
        st_ref[...] = jnp.zeros(st_ref.shape, f32)

    for bb in range(GLA_NB):
        z = jnp.dot(alr_ref[bb].astype(bf16), wa_ref[...], preferred_element_type=f32) + ba_ref[...]
        log_a = -(jnp.maximum(-z, 0.0) + jnp.log1p(jnp.exp(-jnp.abs(z)))) * (1.0 / GLA_GATE_TEMP)
        b_ref[bb] = jnp.dot(tri_ref[...], log_a, preferred_element_type=f32, precision=lax.Precision.HIGHEST)
        q2_ref[bb] = q_ref[bb] * (GLA_DK ** -0.5)
    lane = lax.broadcasted_iota(jnp.int32, (GLA_SUB, GLA_PAIR), 1)
    sub = lax.broadcasted_iota(jnp.int32, (GLA_SUB, GLA_PAIR), 0)
    first = lane < GLA_DK
    lane_s = lax.broadcasted_iota(jnp.int32, (GLA_DV, GLA_PAIR), 1) < GLA_DK

    chains = [(bb, hp) for bb in range(GLA_NB) for hp in range(GLA_HEADS // 2)]
    states = [st_ref[i] for i in range(len(chains))]
    for c in range(GLA_BLOCK // GLA_SUB):
        rows = pl.ds(c * GLA_SUB, GLA_SUB)
        for ci, (bb, hp) in enumerate(chains):
            cols = pl.ds(hp * GLA_PAIR, GLA_PAIR)
            state = states[ci]
            b = b_ref[bb, rows, cols]
            b_last = b_ref[bb, pl.ds(c * GLA_SUB + GLA_SUB - 1, 1), cols]
            q2, k2 = q2_ref[bb, rows, cols], k_ref[bb, rows, cols]
            qe = q2 * jnp.exp(b)
            kd = (k2 * jnp.exp(b_last - b)).astype(bf16)
            sb = state.astype(bf16)
            vcols2 = [pl.ds((2 * hp + j) * GLA_DV, GLA_DV) for j in range(2)]
            intra = [jnp.zeros((GLA_SUB, GLA_DV), f32), jnp.zeros((GLA_SUB, GLA_DV), f32)]
            for s_ in range(GLA_SUB):
                row = pl.ds(c * GLA_SUB + s_, 1)
                p = (k_ref[bb, row, cols] * q2) * jnp.exp(b - b_ref[bb, row, cols])
                keep = sub[:, :1] >= s_
                col_a = jnp.where(keep, jnp.sum(jnp.where(first, p, 0.0), axis=1, keepdims=True), 0.0)
                col_b = jnp.where(keep, jnp.sum(jnp.where(first, 0.0, p), axis=1, keepdims=True), 0.0)
                intra[0] = intra[0] + col_a * v_ref[bb, row, vcols2[0]]
                intra[1] = intra[1] + col_b * v_ref[bb, row, vcols2[1]]
            ut = []
            for j in range(2):
                vcols = vcols2[j]
                vh = v_ref[bb, rows, vcols]
                qm = jnp.where(first if j == 0 else ~first, qe, 0.0).astype(bf16)
                o = lax.dot_general(qm, sb, NT_DIMS, preferred_element_type=f32) + intra[j]
                ms = jnp.mean(o * o, axis=1, keepdims=True)
                r = r_ref[bb, rows, vcols]
                o_ref[bb, rows, vcols] = (o * lax.rsqrt(ms + RMS_EPS) * g_ref[:, vcols]) * (r * jax.nn.sigmoid(r))
                ut.append(jnp.dot(vh.T.astype(bf16), kd, preferred_element_type=f32))
            states[ci] = jnp.exp(b_last) * state + jnp.where(lane_s, ut[0], ut[1])
    for ci in range(len(chains)):
        st_ref[ci] = states[ci]


def _gla_tri():
    r = np.arange(GLA_BLOCK)
    return jnp.asarray(((r[:, None] // GLA_SUB == r[None, :] // GLA_SUB) & (r[None, :] <= r[:, None])).astype(np.float32))


def _gla(proj, w_alpha, b_alpha, norm_g):
    bsz, s, _ = proj.shape
    rank = w_alpha.shape[0]
    wa = jnp.pad(w_alpha, ((0, ALR_W - rank), (0, 0))).astype(jnp.bfloat16)
    col = lambda width, off: pl.BlockSpec((GLA_NB, GLA_BLOCK, width), lambda b, i: (b, i, off // width))
    const = lambda shape: pl.BlockSpec(shape, lambda b, i: (0, 0))
    return pl.pallas_call(
        _gla_kernel,
        grid=(bsz // GLA_NB, s // GLA_BLOCK),
        in_specs=[col(GLA_QK_WIDTH, COL_QG), col(GLA_QK_WIDTH, COL_KG), col(GLA_WIDTH, COL_VG), col(ALR_W, COL_ALR),
                  col(GLA_WIDTH, COL_RG), const((ALR_W, GLA_QK_WIDTH)), const((1, GLA_QK_WIDTH)),
                  const((1, GLA_WIDTH)), const((GLA_BLOCK, GLA_BLOCK))],
        out_specs=pl.BlockSpec((GLA_NB, GLA_BLOCK, GLA_WIDTH), lambda b, i: (b, i, 0)),
        out_shape=jax.ShapeDtypeStruct((bsz, s, GLA_WIDTH), jnp.float32),
        scratch_shapes=[pltpu.VMEM((GLA_NB * GLA_HEADS // 2, GLA_DV, GLA_PAIR), jnp.float32),
                        pltpu.VMEM((GLA_NB, GLA_BLOCK, GLA_QK_WIDTH), jnp.float32),
                        pltpu.VMEM((GLA_NB, GLA_BLOCK, GLA_QK_WIDTH), jnp.float32)],
        compiler_params=pltpu.CompilerParams(dimension_semantics=("arbitrary", "arbitrary")),
        name="gla",
    )(proj, proj, proj, proj, proj, wa, b_alpha.reshape(1, -1), norm_g.reshape(1, -1), _gla_tri())


def peer_mixer(h_tiles, w_q, keys_1, keys_2, u_tab, v_tab, x1, gate2, final_g):
    bsz, s, rs, rl = h_tiles.shape
    ht = h_tiles.reshape(bsz * s, rs, rl)
    experts, gates = _peer_route(ht, w_q, keys_1, keys_2)
    return _peer_experts(ht, experts, gates, u_tab, v_tab, x1, gate2, final_g)


ROUTE_TT = 1024


def _top16_rows(s, code, big):
    vals, codes = [], []
    for _ in range(PEER_TOPK):
        m = jnp.max(s, axis=0, keepdims=True)
        c = jnp.min(jnp.where(s == m, code, big), axis=0, keepdims=True)
        s = jnp.where(code == c, NEG_INF, s)
        vals.append(m)
        codes.append(c)
    return vals, codes


def _peer_route_kernel(h_ref, wq_ref, k1_ref, k2_ref, e_ref, g_ref, q_ref):
    half = PEER_QUERY_DIM // 2
    tt = ROUTE_TT
    h = jnp.concatenate([h_ref[:, c, :] for c in range(h_ref.shape[1])], axis=-1)
    q_ref[...] = jnp.dot(h.astype(jnp.bfloat16), wq_ref[...], preferred_element_type=jnp.float32)
    key_id = lax.broadcasted_iota(jnp.int32, (PEER_N_KEYS, 2 * tt), 0).astype(jnp.float32)
    sub8 = lax.broadcasted_iota(jnp.int32, (SUBLANES, tt), 0)
    id_bits = PEER_N_EXPERTS.bit_length() - 1

    def head_body(hd, carry):
        off = pl.multiple_of(hd * PEER_QUERY_DIM, PEER_QUERY_DIM)
        q1 = q_ref[:, pl.ds(off, half)].astype(jnp.bfloat16)
        q2 = q_ref[:, pl.ds(off + half, half)].astype(jnp.bfloat16)
        s1 = lax.dot_general(k1_ref[...], q1, NT_DIMS, preferred_element_type=jnp.float32)
        s2 = lax.dot_general(k2_ref[...], q2, NT_DIMS, preferred_element_type=jnp.float32)
        vals, ids = _top16_rows(jnp.concatenate([s1, s2], axis=1), key_id, float(PEER_N_KEYS))
        v = jnp.concatenate(vals, axis=0)
        ix = jnp.concatenate(ids, axis=0).astype(jnp.int32)
        v1, v2, i1, i2 = v[:, :tt], v[:, tt:], ix[:, :tt], ix[:, tt:]
        cand, code = [], []
        for a, b0 in [(0, 0), (0, 8), (1, 0), (2, 0), (3, 0), (4, 0), (5, 0), (6, 0), (7, 0)]:
            cand.append(v1[a:a + 1] + v2[b0:b0 + 8])
            flat = a * PEER_TOPK + b0 + sub8
            code.append((flat << id_bits) | (i1[a:a + 1] * PEER_N_KEYS + i2[b0:b0 + 8]))
        cand.append(v1[8:16] + v2[0:1])
        code.append((((8 + sub8) * PEER_TOPK) << id_bits) | (i1[8:16] * PEER_N_KEYS + i2[0:1]))
        sc, cc = _top16_rows(jnp.concatenate(cand, axis=0), jnp.concatenate(code, axis=0).astype(jnp.float32),
                             float(1 << 30))
        sc = jnp.concatenate(sc, axis=0)
        ex = jnp.exp(sc - sc[0:1])
        g_ref[hd] = ex / jnp.sum(ex, axis=0, keepdims=True)
        e_ref[hd] = jnp.concatenate(cc, axis=0).astype(jnp.int32) & (PEER_N_EXPERTS - 1)
        return carry

    lax.fori_loop(0, PEER_HEADS, head_body, 0)


def _peer_route(hf, w_q, keys_1, keys_2):
    t, rs, rl = hf.shape
    d = rs * rl
    qw = PEER_HEADS * PEER_QUERY_DIM
    const = lambda shape: pl.BlockSpec(shape, lambda i: tuple(0 for _ in shape))
    out_spec = pl.BlockSpec((PEER_HEADS, PEER_TOPK, ROUTE_TT), lambda i: (0, 0, i))
    e_t, g_t = pl.pallas_call(
        _peer_route_kernel,
        grid=(t // ROUTE_TT,),
        in_specs=[pl.BlockSpec((ROUTE_TT, rs, rl), lambda i: (i, 0, 0)), const((d, qw)),
                  const(keys_1.shape), const(keys_2.shape)],
        out_specs=[out_spec, out_spec],
        out_shape=[jax.ShapeDtypeStruct((PEER_HEADS, PEER_TOPK, t), jnp.int32),
                   jax.ShapeDtypeStruct((PEER_HEADS, PEER_TOPK, t), jnp.float32)],
        scratch_shapes=[pltpu.VMEM((ROUTE_TT, qw), jnp.float32)],
        compiler_params=pltpu.CompilerParams(dimension_semantics=("arbitrary",), vmem_limit_bytes=PROJ_VMEM_LIMIT),
        name="peer_route",
    )(hf, w_q.astype(jnp.bfloat16), keys_1.astype(jnp.bfloat16), keys_2.astype(jnp.bfloat16))
    return e_t, g_t


PEER_PAIRS = PEER_HEADS * PEER_TOPK
PEER_TB = 128
PEER_HALF = 64
PEER_TOK_UNROLL = 4
HALF_D = D_MODEL // 2
WORD_SUB, WORD_LANE = HALF_D // 128, 128
ROW_SUB = 2 * WORD_SUB
HI_MASK = -65536
IDX_WAYS = 8


PACK_EB = 512
TAB_PAD = PACK_EB * WORD_SUB


def _pack_kernel(x_ref, o_ref):
    i = pl.program_id(0)

    @pl.when((i == 0) | (i == pl.num_programs(0) - 1))
    def _():
        o_ref[...] = jnp.zeros(o_ref.shape, o_ref.dtype)

    @pl.when((i > 0) & (i < pl.num_programs(0) - 1))
    def _():
        x = x_ref[...].astype(jnp.bfloat16).astype(jnp.float32)
        bits = lax.bitcast_convert_type(x, jnp.int32)
        word = lax.shift_right_logical(bits[:, :HALF_D], 16) | (bits[:, HALF_D:] & HI_MASK)
        for c in range(WORD_SUB):
            o_ref[:, c, :] = word[:, c * WORD_LANE:(c + 1) * WORD_LANE]


def _pack_table(tab):
    e, d = tab.shape
    nblk = e // PACK_EB
    out = pl.pallas_call(
        _pack_kernel,
        grid=(nblk + 2,),
        in_specs=[pl.BlockSpec((PACK_EB, d), lambda i: (jnp.clip(i - 1, 0, nblk - 1), 0))],
        out_specs=pl.BlockSpec((PACK_EB, WORD_SUB, WORD_LANE), lambda i: (i, 0, 0)),
        out_shape=jax.ShapeDtypeStruct(((nblk + 2) * PACK_EB, WORD_SUB, WORD_LANE), jnp.int32),
        name="pack_table",
    )(tab)
    return out.reshape((nblk + 2) * TAB_PAD, WORD_LANE)


def _pair_tile(tab_ref, ia, ib, upper):
    wa = tab_ref[pl.ds(pl.multiple_of(ia, WORD_SUB), ROW_SUB), :]
    wb = tab_ref[pl.ds(pl.multiple_of(ib - WORD_SUB, WORD_SUB), ROW_SUB), :]
    return jnp.where(upper, wa, wb)


_ORDER = (6, 2, 4, 0, 7, 3, 5, 1)


def _fold_pairs(ps, sub):
    b1 = (sub & 2) != 0
    w = []
    for i in range(2):
        x, y = ps[2 * i], ps[2 * i + 1]
        w.append(jnp.where(b1, x + pltpu.roll(x, 2, 0), y + pltpu.roll(y, 6, 0)))
    b0 = (sub & 1) != 0
    x, y = w
    return jnp.where(b0, x + pltpu.roll(x, 1, 0), y + pltpu.roll(y, 7, 0))


def _peer_u_kernel(*refs):
    idx_refs = refs[:IDX_WAYS]
    h_ref, tab_ref, a_ref = refs[IDX_WAYS:]
    sub = lax.broadcasted_iota(jnp.int32, (ROW_SUB, WORD_LANE), 0)
    lane = lax.broadcasted_iota(jnp.int32, (ROW_SUB, WORD_LANE), 1)
    n_groups = PEER_PAIRS // IDX_WAYS
    feed = [int(i) for i in np.argsort(np.array(_ORDER))]
    a_ref[...] = jnp.zeros(a_ref.shape, a_ref.dtype)

    def folded_tiles(t):
        h = h_ref[t]
        h_lo = jnp.concatenate([h[:WORD_SUB], h[:WORD_SUB]], axis=0)
        h_hi = jnp.concatenate([h[WORD_SUB:], h[WORD_SUB:]], axis=0)
        tiles = []
        for g in range(n_groups):
            ps = []
            for i in range(4):
                w = _pair_tile(tab_ref, idx_refs[feed[2 * i]][g * PEER_TB + t], idx_refs[feed[2 * i + 1]][g * PEER_TB + t],
                               sub < WORD_SUB)
                lo = lax.bitcast_convert_type(w << 16, jnp.float32)
                hi = lax.bitcast_convert_type(w & HI_MASK, jnp.float32)
                ps.append(lo * h_lo + hi * h_hi)
            tiles.append(_fold_pairs(ps, sub))
        return tuple(tiles)

    def lane_sums(t, tiles):
        here = lane == t
        for g in range(n_groups):
            a_ref[g] = jnp.where(here, jnp.sum(tiles[g], axis=1, keepdims=True), a_ref[g])

    def tok_body(t, prev):
        lane_sums(t - 1, prev)
        return folded_tiles(t)

    last = lax.fori_loop(1, PEER_TB, tok_body, folded_tiles(0))
    lane_sums(PEER_TB - 1, last)


def _split3_lanes(x, tb):
    mask = lambda v: lax.bitcast_convert_type(lax.bitcast_convert_type(v, jnp.int32) & HI_MASK, jnp.float32)
    hi = mask(x)
    mid = mask(x - hi)
    lo = x - hi - mid
    part = lax.broadcasted_iota(jnp.int32, x.shape, 1) // tb
    return jnp.where(part == 0, hi, jnp.where(part == 1, mid, jnp.where(part == 2, lo, 0.0))).astype(jnp.bfloat16)


def _peer_v_kernel(*refs):
    idx_refs = refs[:IDX_WAYS]
    a_ref, g_ref, tab_ref, sel_ref, x1_ref, gate2_ref, fg_ref, o_ref, wrep_ref, peer_ref = refs[IDX_WAYS:]
    per_way = PEER_PAIRS // IDX_WAYS
    a = a_ref[...].reshape(PEER_PAIRS, PEER_TB)
    gates = jnp.concatenate([g_ref[:, m, :] for m in range(per_way)], axis=0)
    w = gates * (0.5 * a * (1.0 + lax.erf(a * (2.0 ** -0.5))))
    w_swapped = pltpu.roll(w, PEER_HALF, 1)
    low_lanes = lax.broadcasted_iota(jnp.int32, w.shape, 1) < PEER_HALF
    upper = lax.broadcasted_iota(jnp.int32, (ROW_SUB, WORD_LANE), 0) < WORD_SUB

    def one_token(tl, first):
        it = first + tl
        row0 = (first // PEER_HALF) * PEER_PAIRS
        col = pl.multiple_of(tl * WORD_LANE, WORD_LANE)
        z = jnp.zeros((ROW_SUB, WORD_LANE), jnp.float32)
        acc_lo, acc_hi = [z, z], [z, z]
        for m in range(per_way):
            for jj in range(IDX_WAYS // 2):
                p = m * IDX_WAYS + 2 * jj
                words = _pair_tile(tab_ref, idx_refs[2 * jj][m * PEER_TB + it], idx_refs[2 * jj + 1][m * PEER_TB + it], upper)
                w2 = jnp.where(upper, wrep_ref[pl.ds(row0 + p, 1), pl.ds(col, WORD_LANE)],
                               wrep_ref[pl.ds(row0 + p + 1, 1), pl.ds(col, WORD_LANE)])
                k = jj % 2
                acc_lo[k] = acc_lo[k] + lax.bitcast_convert_type(words << 16, jnp.float32) * w2
                acc_hi[k] = acc_hi[k] + lax.bitcast_convert_type(words & HI_MASK, jnp.float32) * w2
        lo, hi = acc_lo[0] + acc_lo[1], acc_hi[0] + acc_hi[1]
        peer_ref[it] = jnp.concatenate([lo[:WORD_SUB] + lo[WORD_SUB:], hi[:WORD_SUB] + hi[WORD_SUB:]], axis=0)

    xs = [jnp.where(low_lanes, w, w_swapped), jnp.where(low_lanes, w_swapped, w)]
    wcat = jnp.concatenate([_split3_lanes(jnp.concatenate([x, x], axis=1), PEER_HALF) for x in xs], axis=0)
    wrep_ref[...] = jnp.dot(wcat, sel_ref[...], preferred_element_type=jnp.float32)

    for half in range(PEER_TB // PEER_HALF):
        def tok_body(i, carry, first=half * PEER_HALF):
            for j in range(PEER_TOK_UNROLL):
                one_token(PEER_TOK_UNROLL * i + j, first)
            return carry

        lax.fori_loop(0, PEER_HALF // PEER_TOK_UNROLL, tok_body, 0)

    peer = jnp.concatenate([peer_ref[:, c, :] for c in range(ROW_SUB)], axis=-1)
    o_ref[...] = _rms(x1_ref[...] + gate2_ref[...] * peer) * fg_ref[...]


def _peer_experts(hf, experts, gates, u_tab, v_tab, x1, gate2, final_g):
    t = hf.shape[0]
    bsz, s, d = x1.shape
    assert s % PEER_TB == 0
    nblk = t // PEER_TB
    per_way = PEER_PAIRS // IDX_WAYS
    smem_spec = pl.BlockSpec((PEER_TB * per_way,), lambda i: (i,), memory_space=pltpu.SMEM)
    tab_spec = pl.BlockSpec(memory_space=pltpu.VMEM)
    row_spec = pl.BlockSpec((PEER_TB, ROW_SUB, WORD_LANE), lambda i: (i, 0, 0))
    a_spec = pl.BlockSpec((per_way, IDX_WAYS, PEER_TB), lambda i: (i, 0, 0))
    params = pltpu.CompilerParams(dimension_semantics=("arbitrary",), vmem_limit_bytes=PEER_VMEM_LIMIT)
    assert IDX_WAYS == PEER_HEADS and PEER_TB == LANES
    idx_ways = [(experts[j] * WORD_SUB + TAB_PAD).reshape(per_way, nblk, PEER_TB).transpose(1, 0, 2).reshape(-1)
                for j in range(IDX_WAYS)]
    a_t = pl.pallas_call(
        _peer_u_kernel,
        grid=(nblk,),
        in_specs=[smem_spec] * IDX_WAYS + [row_spec, tab_spec],
        out_specs=a_spec,
        out_shape=jax.ShapeDtypeStruct((nblk * per_way, IDX_WAYS, PEER_TB), jnp.float32),
        compiler_params=params, name="peer_u",
    )(*idx_ways, hf, _pack_table(u_tab))
    eye = np.repeat(np.eye(PEER_HALF, dtype=np.float32), WORD_LANE, axis=1)
    sel = jnp.asarray(np.concatenate([eye, eye, eye, np.zeros_like(eye)], axis=0), jnp.bfloat16)
    return pl.pallas_call(
        _peer_v_kernel,
        grid=(nblk,),
        in_specs=[smem_spec] * IDX_WAYS + [a_spec, pl.BlockSpec((IDX_WAYS, per_way, PEER_TB), lambda i: (0, 0, i)),
                                           tab_spec, pl.BlockSpec(sel.shape, lambda i: (0, 0)),
                                           pl.BlockSpec((PEER_TB, d), lambda i: (i, 0)),
                                           pl.BlockSpec((None, 1, d), lambda i: ((i * PEER_TB) // s, 0, 0)),
                                           pl.BlockSpec((1, d), lambda i: (0, 0))],
        out_specs=pl.BlockSpec((PEER_TB, d), lambda i: (i, 0)),
        out_shape=jax.ShapeDtypeStruct((t, d), jnp.float32),
        scratch_shapes=[pltpu.VMEM((PEER_TB // PEER_HALF * PEER_PAIRS, PEER_HALF * WORD_LANE), jnp.float32),
                        pltpu.VMEM((PEER_TB, ROW_SUB, WORD_LANE), jnp.float32)],
        compiler_params=params, name="peer_v",
    )(*idx_ways, a_t, gates, _pack_table(v_tab), sel, x1.reshape(t, d), gate2, final_g.reshape(1, d)).reshape(bsz, s, d)


def _in_proj_weights(w_in):
    o = np.cumsum([0, GLA_QK_WIDTH, GLA_QK_WIDTH, GLA_WIDTH, GLA_GATE_RANK, GLA_WIDTH, SWA_WIDTH, SWA_KV_WIDTH])
    src = np.full((PROJ_W,), IN_WIDTH, np.int32)
    for dst, start, width in ((COL_QG, o[0], GLA_QK_WIDTH), (COL_KG, o[1], GLA_QK_WIDTH), (COL_VG, o[2], GLA_WIDTH),
                              (COL_ALR, o[3], GLA_GATE_RANK), (COL_RG, o[4], GLA_WIDTH), (COL_QS, o[5], SWA_WIDTH),
                              (COL_KS, o[6], SWA_KV_WIDTH), (COL_VS, o[7], SWA_KV_WIDTH)):
        src[dst:dst + width] = start + np.arange(width)
    w_ext = jnp.concatenate([w_in, jnp.zeros((w_in.shape[0], 1), w_in.dtype)], axis=1)
    return w_ext[:, src].astype(jnp.bfloat16)


def kernel(x, c, w_ada, b_ada, norm1_g, norm2_g, w_in, w_gla_alpha, b_gla_alpha, gla_norm_g,
           swa_sinks, swa_norm_g, w_out, w_peer_q, peer_keys_1, peer_keys_2, peer_u, peer_v, final_g):
    assert w_ada.shape[0] == 1, "single-layer block"
    l = 0
    mod = _ada(c, w_ada[l], b_ada[l])
    shift1, scale1, gate1, shift2, scale2, gate2 = [m[:, None, :] for m in jnp.split(mod, 6, axis=-1)]
    proj = _in_proj(x, norm1_g[l], shift1, scale1, _in_proj_weights(w_in[l]))
    o_gla = _gla(proj, w_gla_alpha[l], b_gla_alpha[l], gla_norm_g[l])
    o_swa = _swa(proj, swa_sinks[l], swa_norm_g[l])
    x1, h2 = _out_proj(x, o_gla, o_swa, w_out[l].astype(jnp.bfloat16), gate1, norm2_g[l], shift2, scale2)
    return peer_mixer(h2, w_peer_q[l], peer_keys_1[l], peer_keys_2[l], peer_u[l], peer_v[l], x1, gate2, final_g)
```

```python
import jax, jax.numpy as jnp
from jax import lax
import numpy as np
from jax.experimental import pallas as pl
from jax.experimental.pallas import tpu as pltpu

D_MODEL = 1024

GLA_HEADS = 4
GLA_DK = 64
GLA_DV = 128
GLA_GATE_RANK = 16
GLA_GATE_TEMP = 16.0
SWA_HEADS = 8
SWA_KV_HEADS = 2
SWA_HEAD_DIM = 64
SWA_WINDOW = 128
SWA_BLOCK = 128
PEER_HEADS = 8
PEER_N_KEYS = 128
PEER_N_EXPERTS = PEER_N_KEYS * PEER_N_KEYS
PEER_QUERY_DIM = 256
PEER_TOPK = 16
RMS_EPS = 1e-6

GLA_QK_WIDTH = GLA_HEADS * GLA_DK
GLA_WIDTH = GLA_HEADS * GLA_DV
SWA_WIDTH = SWA_HEADS * SWA_HEAD_DIM
SWA_KV_WIDTH = SWA_KV_HEADS * SWA_HEAD_DIM
MIX_WIDTH = GLA_WIDTH + SWA_WIDTH
IN_WIDTH = 2 * GLA_QK_WIDTH + GLA_WIDTH + GLA_GATE_RANK + GLA_WIDTH + SWA_WIDTH + 2 * SWA_KV_WIDTH


LANES, SUBLANES = 128, 8
MIB = 1024 * 1024
PROJ_VMEM_LIMIT = 48 * MIB
PEER_VMEM_LIMIT = 56 * MIB
NEG_INF = float("-inf")
NT_DIMS = (((1,), (1,)), ((), ()))
ROW_TILE = 1024
SWA_PAIR = 2 * SWA_HEAD_DIM
ALR_W = LANES
COL_QS = 0
COL_QG = COL_QS + SWA_WIDTH
COL_KG = COL_QG + GLA_QK_WIDTH
COL_VG = COL_KG + GLA_QK_WIDTH
COL_RG = COL_VG + GLA_WIDTH
COL_KS = COL_RG + GLA_WIDTH
COL_VS = COL_KS + SWA_KV_WIDTH
COL_ALR = COL_VS + SWA_KV_WIDTH
PROJ_W = COL_ALR + ALR_W


def _rms(x):
    return x * lax.rsqrt(jnp.mean(x * x, axis=-1, keepdims=True) + RMS_EPS)


def _ada_kernel(c_ref, w_ref, b_ref, o_ref):
    c = c_ref[...]
    ca = (c * jax.nn.sigmoid(c)).astype(jnp.bfloat16)
    o_ref[...] = jnp.dot(ca, w_ref[...].astype(jnp.bfloat16), preferred_element_type=jnp.float32) + b_ref[...]


def _ada(c, w, b):
    bsz, d = c.shape
    n = w.shape[1]
    tn = D_MODEL
    return pl.pallas_call(
        _ada_kernel,
        grid=(n // tn,),
        in_specs=[pl.BlockSpec((bsz, d), lambda j: (0, 0)), pl.BlockSpec((d, tn), lambda j: (0, j)),
                  pl.BlockSpec((1, tn), lambda j: (0, j))],
        out_specs=pl.BlockSpec((bsz, tn), lambda j: (0, j)),
        out_shape=jax.ShapeDtypeStruct((bsz, n), jnp.float32), name="ada",
    )(c, w, b.reshape(1, n))


def _in_proj_kernel(x_ref, g_ref, sh_ref, sc_ref, w_ref, o_ref):
    h = _rms(x_ref[...]) * g_ref[...] * (1.0 + sc_ref[...]) + sh_ref[...]
    o_ref[...] = jnp.dot(h.astype(jnp.bfloat16), w_ref[...], preferred_element_type=jnp.float32)


def _in_proj(x, g, shift, scale, w):
    bsz, s, d = x.shape
    n = w.shape[1]
    row = lambda width: pl.BlockSpec((None, ROW_TILE, width), lambda b, i: (b, i, 0))
    per_b = pl.BlockSpec((None, 1, d), lambda b, i: (b, 0, 0))
    return pl.pallas_call(
        _in_proj_kernel,
        grid=(bsz, s // ROW_TILE),
        in_specs=[row(d), pl.BlockSpec((1, d), lambda b, i: (0, 0)), per_b, per_b,
                  pl.BlockSpec((d, n), lambda b, i: (0, 0))],
        out_specs=row(n),
        out_shape=jax.ShapeDtypeStruct((bsz, s, n), jnp.float32),
        compiler_params=pltpu.CompilerParams(dimension_semantics=("arbitrary", "arbitrary"),
                                             vmem_limit_bytes=PROJ_VMEM_LIMIT),
        name="in_proj",
    )(x, g.reshape(1, d), shift, scale, w)


def _swa_kernel(sink_ref, q_ref, kp_ref, kc_ref, vp_ref, vc_ref, g_ref, o_ref):
    f32, bf16 = jnp.float32, jnp.bfloat16
    blk = SWA_BLOCK
    k2 = jnp.concatenate([kp_ref[...], kc_ref[...]], axis=0)
    v2 = jnp.concatenate([vp_ref[...], vc_ref[...]], axis=0)
    low = lax.broadcasted_iota(jnp.int32, k2.shape, 1) < SWA_HEAD_DIM
    k2r, v2r = pltpu.roll(k2, SWA_HEAD_DIM, 1), pltpu.roll(v2, SWA_HEAD_DIM, 1)
    kdup = [jnp.where(low, k2, k2r).astype(bf16), jnp.where(low, k2r, k2).astype(bf16)]
    vdup = [jnp.where(low, v2, v2r).astype(bf16), jnp.where(low, v2r, v2).astype(bf16)]
    qi = lax.broadcasted_iota(jnp.int32, (blk, 2 * blk), 0)
    kj = lax.broadcasted_iota(jnp.int32, (blk, 2 * blk), 1)
    dist = qi + blk - kj
    valid = (dist >= 0) & (dist < SWA_WINDOW) & ((kj >= blk) | (pl.program_id(1) > 0))
    distf = dist.astype(f32)
    first = lax.broadcasted_iota(jnp.int32, (blk, SWA_PAIR), 1) < SWA_HEAD_DIM
    grp = SWA_HEADS // SWA_KV_HEADS
    heads = range(SWA_HEADS)
    scores = []
    for h in heads:
        qp = q_ref[:, (h // 2) * SWA_PAIR:(h // 2 + 1) * SWA_PAIR] * (SWA_HEAD_DIM ** -0.5)
        q = jnp.where(first if h % 2 == 0 else ~first, qp, 0.0).astype(bf16)
        scores.append(lax.dot_general(q, kdup[h // grp], NT_DIMS, preferred_element_type=f32))
    probs = []
    for h in heads:
        sc = scores[h] - (2.0 ** (-8.0 * (h + 1) / SWA_HEADS)) * distf
        sc = jnp.where(valid, sc, NEG_INF)
        sink = sink_ref[h]
        m = jnp.maximum(jnp.max(sc, axis=-1, keepdims=True), sink)
        p = jnp.exp(sc - m)
        probs.append((p / (jnp.sum(p, axis=-1, keepdims=True) + jnp.exp(sink - m))).astype(bf16))
    outs = [jnp.dot(probs[h], vdup[h // grp], preferred_element_type=f32) for h in heads]
    pairs = [jnp.where(first, outs[2 * i], outs[2 * i + 1]) for i in range(SWA_HEADS // 2)]
    ssq = sum(jnp.sum(o * o, axis=-1, keepdims=True) for o in pairs)
    inv = lax.rsqrt(ssq * (1.0 / SWA_WIDTH) + RMS_EPS)
    for i, o in enumerate(pairs):
        o_ref[:, i * SWA_PAIR:(i + 1) * SWA_PAIR] = o * inv * g_ref[:, i * SWA_PAIR:(i + 1) * SWA_PAIR]


def _swa(proj, sinks, norm_g):
    bsz, s, _ = proj.shape
    col = lambda width, off, prev: pl.BlockSpec(
        (None, SWA_BLOCK, width), lambda b, n: (b, jnp.maximum(n - 1, 0) if prev else n, off // width))
    return pl.pallas_call(
        _swa_kernel,
        grid=(bsz, s // SWA_BLOCK),
        in_specs=[pl.BlockSpec(memory_space=pltpu.SMEM), col(SWA_WIDTH, COL_QS, False),
                  col(SWA_KV_WIDTH, COL_KS, True), col(SWA_KV_WIDTH, COL_KS, False),
                  col(SWA_KV_WIDTH, COL_VS, True), col(SWA_KV_WIDTH, COL_VS, False),
                  pl.BlockSpec((1, SWA_WIDTH), lambda b, n: (0, 0))],
        out_specs=pl.BlockSpec((None, SWA_BLOCK, SWA_WIDTH), lambda b, n: (b, n, 0)),
        out_shape=jax.ShapeDtypeStruct((bsz, s, SWA_WIDTH), jnp.float32),
        compiler_params=pltpu.CompilerParams(dimension_semantics=("arbitrary", "arbitrary")),
        name="swa",
    )(sinks, proj, proj, proj, proj, proj, norm_g.reshape(1, -1))


def _out_proj_kernel(x_ref, og_ref, os_ref, w_ref, gate_ref, g2_ref, sh_ref, sc_ref, x1_ref, h2_ref):
    mix = jnp.concatenate([og_ref[...], os_ref[...]], axis=-1).astype(jnp.bfloat16)
    x1 = x_ref[...] + gate_ref[...] * jnp.dot(mix, w_ref[...], preferred_element_type=jnp.float32)
    x1_ref[...] = x1
    h2 = _rms(x1) * g2_ref[...] * (1.0 + sc_ref[...]) + sh_ref[...]
    for c in range(ROW_SUB):
        h2_ref[:, c, :] = h2[:, c * LANES:(c + 1) * LANES]


def _out_proj(x, o_gla, o_swa, w, gate1, g2, shift2, scale2):
    bsz, s, d = x.shape
    row = lambda width: pl.BlockSpec((None, ROW_TILE, width), lambda b, i: (b, i, 0))
    per_b = pl.BlockSpec((None, 1, d), lambda b, i: (b, 0, 0))
    shp = jax.ShapeDtypeStruct((bsz, s, d), jnp.float32)
    return pl.pallas_call(
        _out_proj_kernel,
        grid=(bsz, s // ROW_TILE),
        in_specs=[row(d), row(GLA_WIDTH), row(SWA_WIDTH), pl.BlockSpec((MIX_WIDTH, d), lambda b, i: (0, 0)),
                  per_b, pl.BlockSpec((1, d), lambda b, i: (0, 0)), per_b, per_b],
        out_specs=[row(d), pl.BlockSpec((None, ROW_TILE, ROW_SUB, d // ROW_SUB), lambda b, i: (b, i, 0, 0))],
        out_shape=[shp, jax.ShapeDtypeStruct((bsz, s, ROW_SUB, d // ROW_SUB), jnp.float32)],
        compiler_params=pltpu.CompilerParams(dimension_semantics=("arbitrary", "arbitrary"),
                                             vmem_limit_bytes=PROJ_VMEM_LIMIT),
        name="out_proj",
    )(x, o_gla, o_swa, w, gate1, g2.reshape(1, d), shift2, scale2)


GLA_SUB = 16
GLA_BLOCK = 256
GLA_PAIR = 2 * GLA_DK
GLA_NB = 1


def _gla_kernel(q_ref, k_ref, v_ref, alr_ref, r_ref, wa_ref, ba_ref, g_ref, tri_ref, o_ref, st_ref, b_ref, q2_ref):
    f32, bf16 = jnp.float32, jnp.bfloat16

    @pl.when(pl.program_id(1) == 0)
    def _():
        st_ref[...] = jnp.zeros(st_ref.shape, f32)

    for bb in range(GLA_NB):
        z = jnp.dot(alr_ref[bb].astype(bf16), wa_ref[...], preferred_element_type=f32) + ba_ref[...]
        log_a = -(jnp.maximum(-z, 0.0) + jnp.log1p(jnp.exp(-jnp.abs(z)))) * (1.0 / GLA_GATE_TEMP)
        b_ref[bb] = jnp.dot(tri_ref[...], log_a, preferred_element_type=f32, precision=lax.Precision.HIGHEST)
        q2_ref[bb] = q_ref[bb] * (GLA_DK ** -0.5)
    lane = lax.broadcasted_iota(jnp.int32, (GLA_SUB, GLA_PAIR), 1)
    sub = lax.broadcasted_iota(jnp.int32, (GLA_SUB, GLA_PAIR), 0)
    first = lane < GLA_DK
    lane_s = lax.broadcasted_iota(jnp.int32, (GLA_DV, GLA_PAIR), 1) < GLA_DK

    chains = [(bb, hp) for bb in range(GLA_NB) for hp in range(GLA_HEADS // 2)]
    states = [st_ref[i] for i in range(len(chains))]
    for c in range(GLA_BLOCK // GLA_SUB):
        rows = pl.ds(c * GLA_SUB, GLA_SUB)
        for ci, (bb, hp) in enumerate(chains):
            cols = pl.ds(hp * GLA_PAIR, GLA_PAIR)
            state = states[ci]
            b = b_ref[bb, rows, cols]
            b_last = b_ref[bb, pl.ds(c * GLA_SUB + GLA_SUB - 1, 1), cols]
            q2, k2 = q2_ref[bb, rows, cols], k_ref[bb, rows, cols]
            qe = q2 * jnp.exp(b)
            kd = (k2 * jnp.exp(b_last - b)).astype(bf16)
            sb = state.astype(bf16)
            vcols2 = [pl.ds((2 * hp + j) * GLA_DV, GLA_DV) for j in range(2)]
            intra = [jnp.zeros((GLA_SUB, GLA_DV), f32), jnp.zeros((GLA_SUB, GLA_DV), f32)]
            for s_ in range(GLA_SUB):
                row = pl.ds(c * GLA_SUB + s_, 1)
                p = (k_ref[bb, row, cols] * q2) * jnp.exp(b - b_ref[bb, row, cols])
                keep = sub[:, :1] >= s_
                col_a = jnp.where(keep, jnp.sum(jnp.where(first, p, 0.0), axis=1, keepdims=True), 0.0)
                col_b = jnp.where(keep, jnp.sum(jnp.where(first, 0.0, p), axis=1, keepdims=True), 0.0)
                intra[0] = intra[0] + col_a * v_ref[bb, row, vcols2[0]]
                intra[1] = intra[1] + col_b * v_ref[bb, row, vcols2[1]]
            ut = []
            for j in range(2):
                vcols = vcols2[j]
                vh = v_ref[bb, rows, vcols]
                qm = jnp.where(first if j == 0 else ~first, qe, 0.0).astype(bf16)
                o = lax.dot_general(qm, sb, NT_DIMS, preferred_element_type=f32) + intra[j]
                ms = jnp.mean(o * o, axis=1, keepdims=True)
                r = r_ref[bb, rows, vcols]
                o_ref[bb, rows, vcols] = (o * lax.rsqrt(ms + RMS_EPS) * g_ref[:, vcols]) * (r * jax.nn.sigmoid(r))
                ut.append(jnp.dot(vh.T.astype(bf16), kd, preferred_element_type=f32))
            states[ci] = jnp.exp(b_last) * state + jnp.where(lane_s, ut[0], ut[1])
    for ci in range(len(chains)):
        st_ref[ci] = states[ci]


def _gla_tri():
    r = np.arange(GLA_BLOCK)
    return jnp.asarray(((r[:, None] // GLA_SUB == r[None, :] // GLA_SUB) & (r[None, :] <= r[:, None])).astype(np.float32))


def _gla(proj, w_alpha, b_alpha, norm_g):
    bsz, s, _ = proj.shape
    rank = w_alpha.shape[0]
    wa = jnp.pad(w_alpha, ((0, ALR_W - rank), (0, 0))).astype(jnp.bfloat16)
    col = lambda width, off: pl.BlockSpec((GLA_NB, GLA_BLOCK, width), lambda b, i: (b, i, off // width))
    const = lambda shape: pl.BlockSpec(shape, lambda b, i: (0, 0))
    return pl.pallas_call(
        _gla_kernel,
        grid=(bsz // GLA_NB, s // GLA_BLOCK),
        in_specs=[col(GLA_QK_WIDTH, COL_QG), col(GLA_QK_WIDTH, COL_KG), col(GLA_WIDTH, COL_VG), col(ALR_W, COL_ALR),
                  col(GLA_WIDTH, COL_RG), const((ALR_W, GLA_QK_WIDTH)), const((1, GLA_QK_WIDTH)),
                  const((1, GLA_WIDTH)), const((GLA_BLOCK, GLA_BLOCK))],
        out_specs=pl.BlockSpec((GLA_NB, GLA_BLOCK, GLA_WIDTH), lambda b, i: (b, i, 0)),
        out_shape=jax.ShapeDtypeStruct((bsz, s, GLA_WIDTH), jnp.float32),
        scratch_shapes=[pltpu.VMEM((GLA_NB * GLA_HEADS // 2, GLA_DV, GLA_PAIR), jnp.float32),
                        pltpu.VMEM((GLA_NB, GLA_BLOCK, GLA_QK_WIDTH), jnp.float32),
                        pltpu.VMEM((GLA_NB, GLA_BLOCK, GLA_QK_WIDTH), jnp.float32)],
        compiler_params=pltpu.CompilerParams(dimension_semantics=("arbitrary", "arbitrary")),
        name="gla",
    )(proj, proj, proj, proj, proj, wa, b_alpha.reshape(1, -1), norm_g.reshape(1, -1), _gla_tri())


def peer_mixer(h_tiles, w_q, keys_1, keys_2, u_tab, v_tab, x1, gate2, final_g):
    bsz, s, rs, rl = h_tiles.shape
    ht = h_tiles.reshape(bsz * s, rs, rl)
    experts, gates = _peer_route(ht, w_q, keys_1, keys_2)
    return _peer_experts(ht, experts, gates, u_tab, v_tab, x1, gate2, final_g)


ROUTE_TT = 1024


def _top16_rows(s, code, big):
    vals, codes = [], []
    for _ in range(PEER_TOPK):
        m = jnp.max(s, axis=0, keepdims=True)
        c = jnp.min(jnp.where(s == m, code, big), axis=0, keepdims=True)
        s = jnp.where(code == c, NEG_INF, s)
        vals.append(m)
        codes.append(c)
    return vals, codes


def _peer_route_kernel(h_ref, wq_ref, k1_ref, k2_ref, e_ref, g_ref, q_ref):
    half = PEER_QUERY_DIM // 2
    tt = ROUTE_TT
    h = jnp.concatenate([h_ref[:, c, :] for c in range(h_ref.shape[1])], axis=-1)
    q_ref[...] = jnp.dot(h.astype(jnp.bfloat16), wq_ref[...], preferred_element_type=jnp.float32)
    key_id = lax.broadcasted_iota(jnp.int32, (PEER_N_KEYS, 2 * tt), 0).astype(jnp.float32)
    sub8 = lax.broadcasted_iota(jnp.int32, (SUBLANES, tt), 0)
    id_bits = PEER_N_EXPERTS.bit_length() - 1

    def head_body(hd, carry):
        off = pl.multiple_of(hd * PEER_QUERY_DIM, PEER_QUERY_DIM)
        q1 = q_ref[:, pl.ds(off, half)].astype(jnp.bfloat16)
        q2 = q_ref[:, pl.ds(off + half, half)].astype(jnp.bfloat16)
        s1 = lax.dot_general(k1_ref[...], q1, NT_DIMS, preferred_element_type=jnp.float32)
        s2 = lax.dot_general(k2_ref[...], q2, NT_DIMS, preferred_element_type=jnp.float32)
        vals, ids = _top16_rows(jnp.concatenate([s1, s2], axis=1), key_id, float(PEER_N_KEYS))
        v = jnp.concatenate(vals, axis=0)
        ix = jnp.concatenate(ids, axis=0).astype(jnp.int32)
        v1, v2, i1, i2 = v[:, :tt], v[:, tt:], ix[:, :tt], ix[:, tt:]
        cand, code = [], []
        for a, b0 in [(0, 0), (0, 8), (1, 0), (2, 0), (3, 0), (4, 0), (5, 0), (6, 0), (7, 0)]:
            cand.append(v1[a:a + 1] + v2[b0:b0 + 8])
            flat = a * PEER_TOPK + b0 + sub8
            code.append((flat << id_bits) | (i1[a:a + 1] * PEER_N_KEYS + i2[b0:b0 + 8]))
        cand.append(v1[8:16] + v2[0:1])
        code.append((((8 + sub8) * PEER_TOPK) << id_bits) | (i1[8:16] * PEER_N_KEYS + i2[0:1]))
        sc, cc = _top16_rows(jnp.concatenate(cand, axis=0), jnp.concatenate(code, axis=0).astype(jnp.float32),
                             float(1 << 30))
        sc = jnp.concatenate(sc, axis=0)
        ex = jnp.exp(sc - sc[0:1])
        g_ref[hd] = ex / jnp.sum(ex, axis=0, keepdims=True)
        e_ref[hd] = jnp.concatenate(cc, axis=0).astype(jnp.int32) & (PEER_N_EXPERTS - 1)
        return carry

    lax.fori_loop(0, PEER_HEADS, head_body, 0)


def _peer_route(hf, w_q, keys_1, keys_2):
    t, rs, rl = hf.shape
    d = rs * rl
    qw = PEER_HEADS * PEER_QUERY_DIM
    const = lambda shape: pl.BlockSpec(shape, lambda i: tuple(0 for _ in shape))
    out_spec = pl.BlockSpec((PEER_HEADS, PEER_TOPK, ROUTE_TT), lambda i: (0, 0, i))
    e_t, g_t = pl.pallas_call(
        _peer_route_kernel,
        grid=(t // ROUTE_TT,),
        in_specs=[pl.BlockSpec((ROUTE_TT, rs, rl), lambda i: (i, 0, 0)), const((d, qw)),
                  const(keys_1.shape), const(keys_2.shape)],
        out_specs=[out_spec, out_spec],
        out_shape=[jax.ShapeDtypeStruct((PEER_HEADS, PEER_TOPK, t), jnp.int32),
                   jax.ShapeDtypeStruct((PEER_HEADS, PEER_TOPK, t), jnp.float32)],
        scratch_shapes=[pltpu.VMEM((ROUTE_TT, qw), jnp.float32)],
        compiler_params=pltpu.CompilerParams(dimension_semantics=("arbitrary",), vmem_limit_bytes=PROJ_VMEM_LIMIT),
        name="peer_route",
    )(hf, w_q.astype(jnp.bfloat16), keys_1.astype(jnp.bfloat16), keys_2.astype(jnp.bfloat16))
    return e_t, g_t


PEER_PAIRS = PEER_HEADS * PEER_TOPK
PEER_TB = 128
PEER_HALF = 64
PEER_TOK_UNROLL = 4
HALF_D = D_MODEL // 2
WORD_SUB, WORD_LANE = HALF_D // 128, 128
ROW_SUB = 2 * WORD_SUB
HI_MASK = -65536
IDX_WAYS = 8


PACK_EB = 512
TAB_PAD = PACK_EB * WORD_SUB


def _pack_kernel(x_ref, o_ref):
    i = pl.program_id(0)

    @pl.when((i == 0) | (i == pl.num_programs(0) - 1))
    def _():
        o_ref[...] = jnp.zeros(o_ref.shape, o_ref.dtype)

    @pl.when((i > 0) & (i < pl.num_programs(0) - 1))
    def _():
        x = x_ref[...].astype(jnp.bfloat16).astype(jnp.float32)
        bits = lax.bitcast_convert_type(x, jnp.int32)
        word = lax.shift_right_logical(bits[:, :HALF_D], 16) | (bits[:, HALF_D:] & HI_MASK)
        for c in range(WORD_SUB):
            o_ref[:, c, :] = word[:, c * WORD_LANE:(c + 1) * WORD_LANE]


def _pack_table(tab):
    e, d = tab.shape
    nblk = e // PACK_EB
    out = pl.pallas_call(
        _pack_kernel,
        grid=(nblk + 2,),
        in_specs=[pl.BlockSpec((PACK_EB, d), lambda i: (jnp.clip(i - 1, 0, nblk - 1), 0))],
        out_specs=pl.BlockSpec((PACK_EB, WORD_SUB, WORD_LANE), lambda i: (i, 0, 0)),
        out_shape=jax.ShapeDtypeStruct(((nblk + 2) * PACK_EB, WORD_SUB, WORD_LANE), jnp.int32),
        name="pack_table",
    )(tab)
    return out.reshape((nblk + 2) * TAB_PAD, WORD_LANE)


def _pair_tile(tab_ref, ia, ib, upper):
    wa = tab_ref[pl.ds(pl.multiple_of(ia, WORD_SUB), ROW_SUB), :]
    wb = tab_ref[pl.ds(pl.multiple_of(ib - WORD_SUB, WORD_SUB), ROW_SUB), :]
    return jnp.where(upper, wa, wb)


_ORDER = (6, 2, 4, 0, 7, 3, 5, 1)


def _fold_pairs(ps, sub):
    b1 = (sub & 2) != 0
    w = []
    for i in range(2):
        x, y = ps[2 * i], ps[2 * i + 1]
        w.append(jnp.where(b1, x + pltpu.roll(x, 2, 0), y + pltpu.roll(y, 6, 0)))
    b0 = (sub & 1) != 0
    x, y = w
    return jnp.where(b0, x + pltpu.roll(x, 1, 0), y + pltpu.roll(y, 7, 0))


def _peer_u_kernel(*refs):
    idx_refs = refs[:IDX_WAYS]
    h_ref, tab_ref, a_ref = refs[IDX_WAYS:]
    sub = lax.broadcasted_iota(jnp.int32, (ROW_SUB, WORD_LANE), 0)
    lane = lax.broadcasted_iota(jnp.int32, (ROW_SUB, WORD_LANE), 1)
    n_groups = PEER_PAIRS // IDX_WAYS
    feed = [int(i) for i in np.argsort(np.array(_ORDER))]
    a_ref[...] = jnp.zeros(a_ref.shape, a_ref.dtype)

    def folded_tiles(t):
        h = h_ref[t]
        hb = lax.bitcast_convert_type(h.astype(jnp.bfloat16).astype(jnp.float32), jnp.int32)
        hw = lax.shift_right_logical(hb[:WORD_SUB], 16) | (hb[WORD_SUB:] & HI_MASK)
        hp = pltpu.bitcast(jnp.concatenate([hw, hw], axis=0), jnp.bfloat16)
        tiles = []
        for g in range(n_groups):
            ps = []
            for i in range(4):
                w = _pair_tile(tab_ref, idx_refs[feed[2 * i]][g * PEER_TB + t], idx_refs[feed[2 * i + 1]][g * PEER_TB + t],
                               sub < WORD_SUB)
                pw = pltpu.bitcast(pltpu.bitcast(w, jnp.bfloat16) * hp, jnp.int32)
                ps.append(lax.bitcast_convert_type(pw << 16, jnp.float32)
                          + lax.bitcast_convert_type(pw & HI_MASK, jnp.float32))
            tiles.append(_fold_pairs(ps, sub))
        return tuple(tiles)

    def lane_sums(t, tiles):
        here = lane == t
        for g in range(n_groups):
            a_ref[g] = jnp.where(here, jnp.sum(tiles[g], axis=1, keepdims=True), a_ref[g])

    def tok_body(t, prev):
        lane_sums(t - 1, prev)
        return folded_tiles(t)

    last = lax.fori_loop(1, PEER_TB, tok_body, folded_tiles(0))
    lane_sums(PEER_TB - 1, last)


def _split3_lanes(x, tb):
    mask = lambda v: lax.bitcast_convert_type(lax.bitcast_convert_type(v, jnp.int32) & HI_MASK, jnp.float32)
    hi = mask(x)
    mid = mask(x - hi)
    lo = x - hi - mid
    part = lax.broadcasted_iota(jnp.int32, x.shape, 1) // tb
    return jnp.where(part == 0, hi, jnp.where(part == 1, mid, jnp.where(part == 2, lo, 0.0))).astype(jnp.bfloat16)


def _peer_v_kernel(*refs):
    idx_refs = refs[:IDX_WAYS]
    a_ref, g_ref, tab_ref, sel_ref, x1_ref, gate2_ref, fg_ref, o_ref, wrep_ref, peer_ref = refs[IDX_WAYS:]
    per_way = PEER_PAIRS // IDX_WAYS
    a = a_ref[...].reshape(PEER_PAIRS, PEER_TB)
    gates = jnp.concatenate([g_ref[:, m, :] for m in range(per_way)], axis=0)
    w = gates * (0.5 * a * (1.0 + lax.erf(a * (2.0 ** -0.5))))
    w_swapped = pltpu.roll(w, PEER_HALF, 1)
    low_lanes = lax.broadcasted_iota(jnp.int32, w.shape, 1) < PEER_HALF
    upper = lax.broadcasted_iota(jnp.int32, (ROW_SUB, WORD_LANE), 0) < WORD_SUB

    def one_token(tl, first):
        it = first + tl
        row0 = (first // PEER_HALF) * PEER_PAIRS
        col = pl.multiple_of(tl * WORD_LANE, WORD_LANE)
        z = jnp.zeros((ROW_SUB, WORD_LANE), jnp.float32)
        acc_lo, acc_hi = [z, z], [z, z]
        for m in range(per_way):
            for jj in range(IDX_WAYS // 2):
                p = m * IDX_WAYS + 2 * jj
                words = _pair_tile(tab_ref, idx_refs[2 * jj][m * PEER_TB + it], idx_refs[2 * jj + 1][m * PEER_TB + it], upper)
                w2 = jnp.where(upper, wrep_ref[pl.ds(row0 + p, 1), pl.ds(col, WORD_LANE)],
                               wrep_ref[pl.ds(row0 + p + 1, 1), pl.ds(col, WORD_LANE)])
                k = jj % 2
                acc_lo[k] = acc_lo[k] + lax.bitcast_convert_type(words << 16, jnp.float32) * w2
                acc_hi[k] = acc_hi[k] + lax.bitcast_convert_type(words & HI_MASK, jnp.float32) * w2
        lo, hi = acc_lo[0] + acc_lo[1], acc_hi[0] + acc_hi[1]
        peer_ref[it] = jnp.concatenate([lo[:WORD_SUB] + lo[WORD_SUB:], hi[:WORD_SUB] + hi[WORD_SUB:]], axis=0)

    xs = [jnp.where(low_lanes, w, w_swapped), jnp.where(low_lanes, w_swapped, w)]
    wcat = jnp.concatenate([_split3_lanes(jnp.concatenate([x, x], axis=1), PEER_HALF) for x in xs], axis=0)
    wrep_ref[...] = jnp.dot(wcat, sel_ref[...], preferred_element_type=jnp.float32)

    for half in range(PEER_TB // PEER_HALF):
        def tok_body(i, carry, first=half * PEER_HALF):
            for j in range(PEER_TOK_UNROLL):
                one_token(PEER_TOK_UNROLL * i + j, first)
            return carry

        lax.fori_loop(0, PEER_HALF // PEER_TOK_UNROLL, tok_body, 0)

    peer = jnp.concatenate([peer_ref[:, c, :] for c in range(ROW_SUB)], axis=-1)
    o_ref[...] = _rms(x1_ref[...] + gate2_ref[...] * peer) * fg_ref[...]


def _peer_experts(hf, experts, gates, u_tab, v_tab, x1, gate2, final_g):
    t = hf.shape[0]
    bsz, s, d = x1.shape
    assert s % PEER_TB == 0
    nblk = t // PEER_TB
    per_way = PEER_PAIRS // IDX_WAYS
    smem_spec = pl.BlockSpec((PEER_TB * per_way,), lambda i: (i,), memory_space=pltpu.SMEM)
    tab_spec = pl.BlockSpec(memory_space=pltpu.VMEM)
    row_spec = pl.BlockSpec((PEER_TB, ROW_SUB, WORD_LANE), lambda i: (i, 0, 0))
    a_spec = pl.BlockSpec((per_way, IDX_WAYS, PEER_TB), lambda i: (i, 0, 0))
    params = pltpu.CompilerParams(dimension_semantics=("arbitrary",), vmem_limit_bytes=PEER_VMEM_LIMIT)
    assert IDX_WAYS == PEER_HEADS and PEER_TB == LANES
    idx_ways = [(experts[j] * WORD_SUB + TAB_PAD).reshape(per_way, nblk, PEER_TB).transpose(1, 0, 2).reshape(-1)
                for j in range(IDX_WAYS)]
    a_t = pl.pallas_call(
        _peer_u_kernel,
        grid=(nblk,),
        in_specs=[smem_spec] * IDX_WAYS + [row_spec, tab_spec],
        out_specs=a_spec,
        out_shape=jax.ShapeDtypeStruct((nblk * per_way, IDX_WAYS, PEER_TB), jnp.float32),
        compiler_params=params, name="peer_u",
    )(*idx_ways, hf, _pack_table(u_tab))
    eye = np.repeat(np.eye(PEER_HALF, dtype=np.float32), WORD_LANE, axis=1)
    sel = jnp.asarray(np.concatenate([eye, eye, eye, np.zeros_like(eye)], axis=0), jnp.bfloat16)
    return pl.pallas_call(
        _peer_v_kernel,
        grid=(nblk,),
        in_specs=[smem_spec] * IDX_WAYS + [a_spec, pl.BlockSpec((IDX_WAYS, per_way, PEER_TB), lambda i: (0, 0, i)),
                                           tab_spec, pl.BlockSpec(sel.shape, lambda i: (0, 0)),
                                           pl.BlockSpec((PEER_TB, d), lambda i: (i, 0)),
                                           pl.BlockSpec((None, 1, d), lambda i: ((i * PEER_TB) // s, 0, 0)),
                                           pl.BlockSpec((1, d), lambda i: (0, 0))],
        out_specs=pl.BlockSpec((PEER_TB, d), lambda i: (i, 0)),
        out_shape=jax.ShapeDtypeStruct((t, d), jnp.float32),
        scratch_shapes=[pltpu.VMEM((PEER_TB // PEER_HALF * PEER_PAIRS, PEER_HALF * WORD_LANE), jnp.float32),
                        pltpu.VMEM((PEER_TB, ROW_SUB, WORD_LANE), jnp.float32)],
        compiler_params=params, name="peer_v",
    )(*idx_ways, a_t, gates, _pack_table(v_tab), sel, x1.reshape(t, d), gate2, final_g.reshape(1, d)).reshape(bsz, s, d)


def _in_proj_weights(w_in):
    o = np.cumsum([0, GLA_QK_WIDTH, GLA_QK_WIDTH, GLA_WIDTH, GLA_GATE_RANK, GLA_WIDTH, SWA_WIDTH, SWA_KV_WIDTH])
    src = np.full((PROJ_W,), IN_WIDTH, np.int32)
    for dst, start, width in ((COL_QG, o[0], GLA_QK_WIDTH), (COL_KG, o[1], GLA_QK_WIDTH), (COL_VG, o[2], GLA_WIDTH),
                              (COL_ALR, o[3], GLA_GATE_RANK), (COL_RG, o[4], GLA_WIDTH), (COL_QS, o[5], SWA_WIDTH),
                              (COL_KS, o[6], SWA_KV_WIDTH), (COL_VS, o[7], SWA_KV_WIDTH)):
        src[dst:dst + width] = start + np.arange(width)
    w_ext = jnp.concatenate([w_in, jnp.zeros((w_in.shape[0], 1), w_in.dtype)], axis=1)
    return w_ext[:, src].astype(jnp.bfloat16)


def kernel(x, c, w_ada, b_ada, norm1_g, norm2_g, w_in, w_gla_alpha, b_gla_alpha, gla_norm_g,
           swa_sinks, swa_norm_g, w_out, w_peer_q, peer_keys_1, peer_keys_2, peer_u, peer_v, final_g):
    assert w_ada.shape[0] == 1, "single-layer block"
    l = 0
    mod = _ada(c, w_ada[l], b_ada[l])
    shift1, scale1, gate1, shift2, scale2, gate2 = [m[:, None, :] for m in jnp.split(mod, 6, axis=-1)]
    proj = _in_proj(x, norm1_g[l], shift1, scale1, _in_proj_weights(w_in[l]))
    o_gla = _gla(proj, w_gla_alpha[l], b_gla_alpha[l], gla_norm_g[l])
    o_swa = _swa(proj, swa_sinks[l], swa_norm_g[l])
    x1, h2 = _out_proj(x, o_gla, o_swa, w_out[l].astype(jnp.bfloat16), gate1, norm2_g[l], shift2, scale2)
    return peer_mixer(h2, w_peer_q[l], peer_keys_1[l], peer_keys_2[l], peer_u[l], peer_v[l], x1, gate2, final_g)
```
